```python
import math
import jax, jax.numpy as jnp
from jax import lax
import numpy as np

D_MODEL = 1024
BATCH = 8
SEQ = 2048
DEPTH = 2

N_MIXERS = 2
N_POOL_LAYERS = (DEPTH + 1) // 2
N_ATTN_LAYERS = DEPTH // 2
EPS = 1e-6
POOL_WINDOWS = (2, 4, 8, 16)
POOL_GROUP_DIM = D_MODEL // len(POOL_WINDOWS)
DILATED_GROUPS = ((128, 1), (512, 4), (2048, 16))
N_ATT_GROUPS = len(DILATED_GROUPS)
ATT_HEADS = 16
HEAD_DIM = D_MODEL // ATT_HEADS
ATT_BLOCK = 128
N_BUCKETS = 32
REL_MAX_DIST = 2048
NEG_INF = -1e30
N_EXPERTS = 32
TOP_K = 4
D_EXPERT = D_MODEL
SWIGLU_LIMIT = 7.0
SWIGLU_ALPHA = 1.702
MOE_BLOCK = 128

kernel_name = "hybrid_pool_dilated_attn_moe"


def rmsnorm(x, g):
    xf = x.astype(jnp.float32)
    y = xf * lax.rsqrt(jnp.mean(xf * xf, axis=-1, keepdims=True) + EPS)
    return (y * g.astype(jnp.float32)).astype(x.dtype)


def pool_mixer(h, w_groups, scale):
    B, S, D = h.shape
    hf = h.astype(jnp.float32)
    c = jnp.cumsum(hf, axis=1)
    pos = jnp.arange(S)
    outs = []
    for gi, w in enumerate(POOL_WINDOWS):
        sl = slice(gi * POOL_GROUP_DIM, (gi + 1) * POOL_GROUP_DIM)
        cg = c[..., sl]
        c_lag = jnp.pad(cg, ((0, 0), (w, 0), (0, 0)))[:, :S]
        cnt = jnp.minimum(pos + 1, w).astype(jnp.float32)[None, :, None]
        outs.append((cg - c_lag) / cnt - hf[..., sl])
    p = jnp.stack(outs, axis=2).astype(h.dtype)
    y = jnp.einsum('bsgc,gcd->bsgd', p, w_groups).reshape(B, S, D)
    return y * scale


def t5_bucket(n):
    n = jnp.maximum(n, 0)
    max_exact = N_BUCKETS // 2
    nf = jnp.maximum(n, 1).astype(jnp.float32)
    large = max_exact + (jnp.log(nf / max_exact) / math.log(REL_MAX_DIST / max_exact)
                         * (N_BUCKETS - max_exact)).astype(jnp.int32)
    large = jnp.minimum(large, N_BUCKETS - 1)
    return jnp.where(n < max_exact, n, large)


def rms_head(t, g):
    tf = t.astype(jnp.float32)
    y = tf * lax.rsqrt(jnp.mean(tf * tf, axis=-1, keepdims=True) + EPS)
    return (y * g.astype(jnp.float32)).astype(t.dtype)


def dilated_group_attn(q, k, v, window, dil, bias_tab):
    B, S, H, Dh = q.shape
    L = S // dil
    nb = -(-L // ATT_BLOCK)
    Lp = nb * ATT_BLOCK
    steps = window // dil
    Z = B * dil

    def sub(t):
        t = jnp.swapaxes(t.reshape(B, L, dil, H, Dh), 1, 2).reshape(Z, L, H, Dh)
        return jnp.pad(t, ((0, 0), (0, Lp - L), (0, 0), (0, 0)))

    def band(t):
        tp = jnp.pad(t, ((0, 0), (ATT_BLOCK, 0), (0, 0), (0, 0))).reshape(Z, nb + 1, ATT_BLOCK, H, Dh)
        return jnp.concatenate([tp[:, :-1], tp[:, 1:]], axis=2)

    qb = sub(q).reshape(Z, nb, ATT_BLOCK, H, Dh)
    kb = band(sub(k))
    vb = band(sub(v))
    s = jnp.einsum('znqhc,znkhc->znhqk', qb, kb,
                   preferred_element_type=jnp.float32) * (Dh ** -0.5)
    qi = jnp.arange(ATT_BLOCK)[:, None]
    kj = jnp.arange(2 * ATT_BLOCK)[None, :]
    delta = qi + ATT_BLOCK - kj
    bias = bias_tab[t5_bucket(delta * dil)].transpose(2, 0, 1)
    key_pos = jnp.arange(nb)[:, None] * ATT_BLOCK - ATT_BLOCK + kj
    valid = ((delta >= 0) & (delta <= steps))[None] & (key_pos >= 0)[:, None, :]
    s = jnp.where(valid[None, :, None], s + bias.astype(jnp.float32)[None, None], NEG_INF)
    m = jnp.max(s, axis=-1, keepdims=True)
    p = jnp.exp(s - m)
    den = jnp.sum(p, axis=-1, keepdims=True)
    o = jnp.einsum('znhqk,znkhc->znqhc', p / den, vb.astype(jnp.float32))
    lse = (m[..., 0] + jnp.log(den[..., 0])).transpose(0, 1, 3, 2)

    def unsub(t):
        rest = t.shape[3:]
        t = t.reshape((Z, Lp) + rest)[:, :L].reshape((B, dil, L) + rest)
        return jnp.swapaxes(t, 1, 2).reshape((B, S) + rest)

    return unsub(o), unsub(lse)


def dilated_attention(h, w_qkv, q_gain, k_gain, w_o, rel_bias):
    B, S, D = h.shape
    qkv = (h @ w_qkv).reshape(B, S, 3, N_ATT_GROUPS, ATT_HEADS, HEAD_DIM)
    q = rms_head(qkv[:, :, 0], q_gain)
    k = rms_head(qkv[:, :, 1], k_gain)
    v = qkv[:, :, 2]
    outs, lses = [], []
    for g, (win, dil) in enumerate(DILATED_GROUPS):
        o_g, l_g = dilated_group_attn(q[:, :, g], k[:, :, g], v[:, :, g], win, dil,
                                      rel_bias[:, g * ATT_HEADS:(g + 1) * ATT_HEADS])
        outs.append(o_g)
        lses.append(l_g)
    alpha = jax.nn.softmax(jnp.stack(lses, axis=0), axis=0)
    o = jnp.einsum('gbsh,gbshc->bshc', alpha, jnp.stack(outs, axis=0))
    return o.reshape(B, S, ATT_HEADS * HEAD_DIM).astype(h.dtype) @ w_o


def moe(h, w_r, b_r, w_gu, b_gu, w_dn, b_dn):
    B, S, D = h.shape
    N = B * S
    t = h.reshape(N, D)
    logits = (t @ w_r + b_r).astype(jnp.float32)
    top_v, top_i = lax.top_k(logits, TOP_K)
    gates = jax.nn.softmax(top_v, axis=-1)
    NK = N * TOP_K
    e_flat = top_i.reshape(NK)
    order = jnp.argsort(e_flat)
    e_sorted = e_flat[order]
    tok_sorted = order // TOP_K
    g_sorted = gates.reshape(NK)[order]
    counts = jnp.zeros((N_EXPERTS,), jnp.int32).at[e_flat].add(1)
    padded = ((counts + MOE_BLOCK - 1) // MOE_BLOCK) * MOE_BLOCK
    start = jnp.cumsum(counts) - counts
    pend = jnp.cumsum(padded)
    pstart = pend - padded
    dest = pstart[e_sorted] + (jnp.arange(NK) - start[e_sorted])
    P = ((NK + N_EXPERTS * (MOE_BLOCK - 1) + MOE_BLOCK - 1) // MOE_BLOCK) * MOE_BLOCK
    NB = P // MOE_BLOCK
    x_pad = jnp.zeros((P, D), t.dtype).at[dest].set(t[tok_sorted])
    block_start = jnp.arange(NB) * MOE_BLOCK
    block_e = jnp.minimum(jnp.sum(block_start[:, None] >= pend[None, :], axis=1),
                          N_EXPERTS - 1).astype(jnp.int32)

    def expert_block(args):
        xb, e = args
        gu = xb @ w_gu[e] + b_gu[e]
        gate = jnp.minimum(gu[:, :D_EXPERT], SWIGLU_LIMIT)
        up = jnp.clip(gu[:, D_EXPERT:], -SWIGLU_LIMIT, SWIGLU_LIMIT)
        glu = gate * jax.nn.sigmoid(SWIGLU_ALPHA * gate)
        return ((up + 1) * glu) @ w_dn[e] + b_dn[e]

    y_pad = lax.map(expert_block, (x_pad.reshape(NB, MOE_BLOCK, D), block_e)).reshape(P, D)
    y = y_pad[dest] * g_sorted[:, None].astype(t.dtype)
    out = jax.ops.segment_sum(y, tok_sorted, num_segments=N)
    return out.reshape(B, S, D)


def setup_inputs(seed: int = 0) -> dict:
    key = jax.random.key(seed)
    ks = jax.random.split(key, 16)
    f32 = jnp.float32
    D, E, F = D_MODEL, N_EXPERTS, D_EXPERT
    QKV = 3 * N_ATT_GROUPS * ATT_HEADS * HEAD_DIM
    nrm = lambda k, s: jax.random.normal(k, s, f32)
    return {
        "x": nrm(ks[0], (BATCH, SEQ, D)),
        "norm_mix_g": 1.0 + 0.02 * nrm(ks[1], (DEPTH, D)),
        "norm_ffn_g": 1.0 + 0.02 * nrm(ks[2], (DEPTH, D)),
        "pool_w": nrm(ks[3], (N_POOL_LAYERS, len(POOL_WINDOWS), POOL_GROUP_DIM, POOL_GROUP_DIM)) * POOL_GROUP_DIM ** -0.5,
        "pool_scale": 1.0 + 0.02 * nrm(ks[4], (N_POOL_LAYERS, D)),
        "attn_w_qkv": nrm(ks[5], (N_ATTN_LAYERS, D, QKV)) * D ** -0.5,
        "attn_q_gain": 1.0 + 0.02 * nrm(ks[6], (N_ATTN_LAYERS, HEAD_DIM)),
        "attn_k_gain": 1.0 + 0.02 * nrm(ks[7], (N_ATTN_LAYERS, HEAD_DIM)),
        "attn_w_o": nrm(ks[8], (N_ATTN_LAYERS, ATT_HEADS * HEAD_DIM, D)) * (ATT_HEADS * HEAD_DIM) ** -0.5,
        "rel_bias": 0.2 * nrm(ks[9], (N_BUCKETS, N_ATT_GROUPS * ATT_HEADS)),
        "moe_w_router": nrm(ks[10], (DEPTH, D, E)) * D ** -0.5,
        "moe_b_router": 0.01 * nrm(ks[11], (DEPTH, E)),
        "moe_w_gate_up": nrm(ks[12], (DEPTH, E, D, 2 * F)) * D ** -0.5,
        "moe_b_gate_up": 0.01 * nrm(ks[13], (DEPTH, E, 2 * F)),
        "moe_w_down": nrm(ks[14], (DEPTH, E, F, D)) * F ** -0.5,
        "moe_b_down": 0.01 * nrm(ks[15], (DEPTH, E, D)),
    }


def reference(x, norm_mix_g, norm_ffn_g, pool_w, pool_scale, attn_w_qkv, attn_q_gain,
              attn_k_gain, attn_w_o, rel_bias, moe_w_router, moe_b_router, moe_w_gate_up,
              moe_b_gate_up, moe_w_down, moe_b_down):
    for i in range(DEPTH):
        h = rmsnorm(x, norm_mix_g[i])
        j = i // N_MIXERS
        if i % N_MIXERS == 0:
            x = x + pool_mixer(h, pool_w[j], pool_scale[j])
        else:
            x = x + dilated_attention(h, attn_w_qkv[j], attn_q_gain[j], attn_k_gain[j],
                                      attn_w_o[j], rel_bias)
        h = rmsnorm(x, norm_ffn_g[i])
        x = x + moe(h, moe_w_router[i], moe_b_router[i], moe_w_gate_up[i],
                    moe_b_gate_up[i], moe_w_down[i], moe_b_down[i])
    return x
```

```python
import functools
import math

import numpy as np
import jax
import jax.numpy as jnp
from jax import lax
from jax.experimental import pallas as pl
from jax.experimental.pallas import tpu as pltpu

F32 = jnp.float32
BF16 = jnp.bfloat16
I32 = jnp.int32

EPS = 1e-6
POOL_WINDOWS = (2, 4, 8, 16)
POOL_HALO = 16
DILATED_GROUPS = ((128, 1), (512, 4), (2048, 16))
ATT_HEADS = 16
HEAD_DIM = 64
ATT_BLOCK = 128
N_BUCKETS = 32
REL_MAX_DIST = 2048
NEG_INF = -1e30
N_EXPERTS = 32
TOP_K = 4
SWIGLU_LIMIT = 7.0
SWIGLU_ALPHA = 1.702

V7X_VMEM_LIMIT_BYTES = 56 * 1024 * 1024
MOE_TILE = 256
ROW_TILE = 256
ROUTER_TILE = 512
POOL_TILE = 512
QKV_ROWS = 512
OUT_TILE = 256


def _rms(xf, g):
    ms = jnp.mean(xf * xf, axis=-1, keepdims=True)
    return xf * lax.rsqrt(ms + EPS) * g


def _params(sem, vmem=None):
    return pltpu.CompilerParams(dimension_semantics=sem,
                                vmem_limit_bytes=vmem or V7X_VMEM_LIMIT_BYTES)


def _pool_kernel(x_ref, halo_ref, g_ref, w_ref, sc_ref, o_ref, *, ts, dg):
    i = pl.program_id(1)
    x = x_ref[0]
    g = g_ref[...]
    h = _rms(x, g)
    hh = _rms(halo_ref[0], g)
    hh = jnp.where(i == 0, 0.0, hh)
    full = jnp.concatenate([hh, h], axis=0)
    pos = i * ts + lax.broadcasted_iota(I32, (ts, 1), 0)
    outs = []
    for gi, w in enumerate(POOL_WINDOWS):
        s = full[:, gi * dg:(gi + 1) * dg]
        sh = 1
        while sh < w:
            s = s + pltpu.roll(s, sh, 0)
            sh *= 2
        s = s[POOL_HALO:]
        cnt = jnp.minimum(pos + 1, w).astype(F32)
        p = s / cnt - h[:, gi * dg:(gi + 1) * dg]
        outs.append(jnp.dot(p.astype(BF16), w_ref[gi].astype(BF16),
                            preferred_element_type=F32))
    y = jnp.concatenate(outs, axis=1) * sc_ref[...]
    o_ref[0] = x + y


def _pool_layer(x, g, w_groups, scale):
    B, S, D = x.shape
    ts = POOL_TILE
    dg = D // len(POOL_WINDOWS)
    hb = ts // POOL_HALO
    return pl.pallas_call(
        functools.partial(_pool_kernel, ts=ts, dg=dg),
        grid=(B, S // ts),
        in_specs=[
            pl.BlockSpec((1, ts, D), lambda b, i: (b, i, 0)),
            pl.BlockSpec((1, POOL_HALO, D), lambda b, i: (b, jnp.maximum(i * hb - 1, 0), 0)),
            pl.BlockSpec((1, D), lambda b, i: (0, 0)),
            pl.BlockSpec((len(POOL_WINDOWS), dg, dg), lambda b, i: (0, 0, 0)),
            pl.BlockSpec((1, D), lambda b, i: (0, 0)),
        ],
        out_specs=pl.BlockSpec((1, ts, D), lambda b, i: (b, i, 0)),
        out_shape=jax.ShapeDtypeStruct((B, S, D), F32),
        compiler_params=_params(("parallel", "parallel")),
        name="pool_mixer",
    )(x, x, g.reshape(1, D), w_groups, scale.reshape(1, D))


def _split3_dot(a, b):
    a_hi = a.astype(BF16)
    a_lo = (a - a_hi.astype(F32)).astype(BF16)
    b_hi = b.astype(BF16)
    b_lo = (b - b_hi.astype(F32)).astype(BF16)
    d = functools.partial(jnp.dot, preferred_element_type=F32)
    return d(a_hi, b_hi) + (d(a_hi, b_lo) + d(a_lo, b_hi))


def _router_kernel(x_ref, g_ref, wr_ref, br_ref, e_ref, gate_ref, rank_ref, cnt_ref,
                   tri_ref, carry_ref, *, tt):
    i = pl.program_id(0)
    E = N_EXPERTS

    @pl.when(i == 0)
    def _():
        r = lax.broadcasted_iota(I32, (tt, tt), 0)
        c = lax.broadcasted_iota(I32, (tt, tt), 1)
        tri_ref[...] = jnp.where(c < r, 1.0, 0.0).astype(BF16)
        carry_ref[...] = jnp.zeros_like(carry_ref)

    h = _rms(x_ref[...], g_ref[...])
    logits = _split3_dot(h, wr_ref[...]) + br_ref[...]
    lane = lax.broadcasted_iota(I32, (tt, E), 1).astype(F32)
    lane_k = lax.broadcasted_iota(I32, (tt, TOP_K), 1)
    l = logits
    vals, sels = [], []
    e_out = jnp.zeros((tt, TOP_K), F32)
    for k in range(TOP_K):
        m = jnp.max(l, axis=-1, keepdims=True)
        idx = jnp.min(jnp.where(l == m, lane, float(E)), axis=-1, keepdims=True)
        sel = lane == idx
        vals.append(m)
        sels.append(sel)
        e_out = jnp.where(lane_k == k, idx, e_out)
        l = jnp.where(sel, -jnp.inf, l)
    ex = [jnp.exp(v - vals[0]) for v in vals]
    den = ex[0] + ex[1] + ex[2] + ex[3]
    multi = jnp.zeros((tt, E), F32)
    for sel in sels:
        multi = multi + jnp.where(sel, 1.0, 0.0)
    before = jnp.dot(tri_ref[...], multi.astype(BF16), preferred_element_type=F32)
    before = before + carry_ref[...]
    g_out = jnp.zeros((tt, TOP_K), F32)
    r_out = jnp.zeros((tt, TOP_K), F32)
    for k in range(TOP_K):
        g_out = jnp.where(lane_k == k, ex[k] / den, g_out)
        rk = jnp.sum(jnp.where(sels[k], before, 0.0), axis=-1, keepdims=True)
        r_out = jnp.where(lane_k == k, rk, r_out)
    e_ref[...] = e_out.astype(I32)
    gate_ref[...] = g_out
    rank_ref[...] = r_out.astype(I32)
    carry_ref[...] = carry_ref[...] + jnp.sum(multi, axis=0, keepdims=True)
    cnt_ref[...] = carry_ref[...]


def _router(x, g, w_r, b_r):
    N, D = x.shape
    tt = ROUTER_TILE
    E = N_EXPERTS
    return pl.pallas_call(
        functools.partial(_router_kernel, tt=tt),
        grid=(N // tt,),
        in_specs=[
            pl.BlockSpec((tt, D), lambda i: (i, 0)),
            pl.BlockSpec((1, D), lambda i: (0, 0)),
            pl.BlockSpec((D, E), lambda i: (0, 0)),
            pl.BlockSpec((1, E), lambda i: (0, 0)),
        ],
        out_specs=[
            pl.BlockSpec((tt, TOP_K), lambda i: (i, 0)),
            pl.BlockSpec((tt, TOP_K), lambda i: (i, 0)),
            pl.BlockSpec((tt, TOP_K), lambda i: (i, 0)),
            pl.BlockSpec((1, E), lambda i: (0, 0)),
        ],
        out_shape=[
            jax.ShapeDtypeStruct((N, TOP_K), I32),
            jax.ShapeDtypeStruct((N, TOP_K), F32),
            jax.ShapeDtypeStruct((N, TOP_K), I32),
            jax.ShapeDtypeStruct((1, E), F32),
        ],
        scratch_shapes=[pltpu.VMEM((tt, tt), BF16), pltpu.VMEM((1, E), F32)],
        compiler_params=_params(("arbitrary",)),
        name="moe_router",
    )(x, g.reshape(1, D), w_r, b_r.reshape(1, E))


def _row_copy_wait(src, dst, sem, rows, times):
    for _ in range(times):
        pltpu.make_async_copy(src, dst.at[pl.ds(0, rows)], sem).wait()


def _dispatch_kernel(dest_ref, x_ref, g_ref, xpad_ref, hbuf, sem, *, tt):
    hbuf[...] = _rms(x_ref[...], g_ref[...])

    def issue(t, carry):
        for k in range(TOP_K):
            d = dest_ref[t * TOP_K + k]
            pltpu.make_async_copy(hbuf.at[pl.ds(t, 1)], xpad_ref.at[pl.ds(d, 1)], sem).start()
        return carry

    lax.fori_loop(0, tt, issue, 0, unroll=8)
    _row_copy_wait(hbuf, xpad_ref, sem, tt, TOP_K)


def _dispatch(x, g, dest, P):
    N, D = x.shape
    tt = ROW_TILE
    return pl.pallas_call(
        functools.partial(_dispatch_kernel, tt=tt),
        grid=(N // tt,),
        in_specs=[
            pl.BlockSpec((tt * TOP_K,), lambda i: (i,), memory_space=pltpu.SMEM),
            pl.BlockSpec((tt, D), lambda i: (i, 0)),
            pl.BlockSpec((1, D), lambda i: (0, 0)),
        ],
        out_specs=pl.BlockSpec(memory_space=pl.ANY),
        out_shape=jax.ShapeDtypeStruct((P, D), F32),
        scratch_shapes=[pltpu.VMEM((tt, D), F32), pltpu.SemaphoreType.DMA(())],
        compiler_params=_params(("arbitrary",)),
        name="moe_dispatch",
    )(dest, x, g.reshape(1, D))


def _expert_kernel(te_ref, nu_ref, x_ref, wgu_ref, bgu_ref, wdn_ref, bdn_ref, o_ref,
                   wgu_bf, wdn_bf, *, F):
    i = pl.program_id(0)

    @pl.when(i < nu_ref[0])
    def _():
        prev = te_ref[jnp.maximum(i - 1, 0)]

        @pl.when((i == 0) | (te_ref[i] != prev))
        def _():
            wgu_bf[...] = wgu_ref[0].astype(BF16)
            wdn_bf[...] = wdn_ref[0].astype(BF16)

        x = x_ref[...].astype(BF16)
        gu = jnp.dot(x, wgu_bf[...], preferred_element_type=F32) + bgu_ref[0]
        gate = jnp.minimum(gu[:, :F], SWIGLU_LIMIT)
        up = jnp.clip(gu[:, F:], -SWIGLU_LIMIT, SWIGLU_LIMIT)
        glu = gate * jax.nn.sigmoid(SWIGLU_ALPHA * gate)
        a = ((up + 1.0) * glu).astype(BF16)
        o_ref[...] = jnp.dot(a, wdn_bf[...], preferred_element_type=F32) + bdn_ref[0]


def _experts(tile_e, n_used, x_pad, w_gu, b_gu, w_dn, b_dn, layer):
    P, D = x_pad.shape
    _, E, _, F2 = w_gu.shape
    F = F2 // 2
    tm = MOE_TILE
    row = lambda i, te, nu: (jnp.minimum(i, nu[0] - 1), 0)
    exp4 = lambda i, te, nu: (layer, te[i], 0, 0)
    exp3 = lambda i, te, nu: (te[i], 0, 0)
    grid_spec = pltpu.PrefetchScalarGridSpec(
        num_scalar_prefetch=2,
        grid=(P // tm,),
        in_specs=[
            pl.BlockSpec((tm, D), row),
            pl.BlockSpec((None, 1, D, F2), exp4),
            pl.BlockSpec((1, 1, F2), exp3),
            pl.BlockSpec((None, 1, F, D), exp4),
            pl.BlockSpec((1, 1, D), exp3),
        ],
        out_specs=pl.BlockSpec((tm, D), row),
        scratch_shapes=[pltpu.VMEM((D, F2), BF16), pltpu.VMEM((F, D), BF16)],
    )
    return pl.pallas_call(
        functools.partial(_expert_kernel, F=F),
        grid_spec=grid_spec,
        out_shape=jax.ShapeDtypeStruct((P, D), F32),
        compiler_params=_params(("arbitrary",)),
        name="moe_experts",
    )(tile_e, n_used, x_pad, w_gu, b_gu.reshape(E, 1, F2), w_dn, b_dn.reshape(E, 1, D))


def _combine_kernel(dest_ref, x_ref, gate_ref, ypad_ref, o_ref, ybuf, sem, *, tt):
    def issue(t, carry):
        for k in range(TOP_K):
            d = dest_ref[t * TOP_K + k]
            pltpu.make_async_copy(ypad_ref.at[pl.ds(d, 1)], ybuf.at[k, pl.ds(t, 1)], sem).start()
        return carry

    lax.fori_loop(0, tt, issue, 0, unroll=8)
    for k in range(TOP_K):
        pltpu.make_async_copy(ypad_ref.at[pl.ds(0, tt)], ybuf.at[k], sem).wait()
    gate = gate_ref[...]
    acc = x_ref[...]
    for k in range(TOP_K):
        acc = acc + ybuf[k] * gate[:, k:k + 1]
    o_ref[...] = acc


def _combine(x, gate, dest, y_pad):
    N, D = x.shape
    tt = ROW_TILE
    return pl.pallas_call(
        functools.partial(_combine_kernel, tt=tt),
        grid=(N // tt,),
        in_specs=[
            pl.BlockSpec((tt * TOP_K,), lambda i: (i,), memory_space=pltpu.SMEM),
            pl.BlockSpec((tt, D), lambda i: (i, 0)),
            pl.BlockSpec((tt, TOP_K), lambda i: (i, 0)),
            pl.BlockSpec(memory_space=pl.ANY),
        ],
        out_specs=pl.BlockSpec((tt, D), lambda i: (i, 0)),
        out_shape=jax.ShapeDtypeStruct((N, D), F32),
        scratch_shapes=[pltpu.VMEM((TOP_K, tt, D), F32), pltpu.SemaphoreType.DMA(())],
        compiler_params=_params(("arbitrary",)),
        name="moe_combine",
    )(dest, x, gate, y_pad)


def _moe_layer(x, g, w_r, b_r, w_gu, b_gu, w_dn, b_dn, layer):
    N, D = x.shape
    E = N_EXPERTS
    tm = MOE_TILE
    e_idx, gate, rank, counts = _router(x, g, w_r, b_r)
    counts = counts[0].astype(I32)
    padded = ((counts + tm - 1) // tm) * tm
    pend = jnp.cumsum(padded)
    pstart = pend - padded
    dest = (pstart[e_idx] + rank).reshape(-1)
    P = ((N * TOP_K + E * (tm - 1) + tm - 1) // tm) * tm
    nt = P // tm
    tiles = jnp.arange(nt, dtype=I32)
    tile_e = jnp.minimum(jnp.sum(tiles[:, None] * tm >= pend[None, :], axis=1), E - 1).astype(I32)
    n_used = (pend[-1] // tm).astype(I32)
    tile_e = jnp.where(tiles < n_used, tile_e, tile_e[n_used - 1])
    x_pad = _dispatch(x, g, dest, P)
    y_pad = _experts(tile_e, n_used.reshape(1), x_pad, w_gu, b_gu, w_dn, b_dn, layer)
    return _combine(x, gate, dest, y_pad)


def _bucket_maps():
    qi = np.arange(ATT_BLOCK)[:, None]
    kj = np.arange(2 * ATT_BLOCK)[None, :]
    delta = qi + ATT_BLOCK - kj
    buckets, valids = [], []
    max_exact = N_BUCKETS // 2
    for win, dil in DILATED_GROUPS:
        n = np.maximum(delta * dil, 0)
        nf = np.maximum(n, 1).astype(np.float32)
        large = max_exact + (np.log(nf / np.float32(max_exact)) / np.float32(math.log(REL_MAX_DIST / max_exact))
                             * np.float32(N_BUCKETS - max_exact)).astype(np.int32)
        large = np.minimum(large, N_BUCKETS - 1)
        buckets.append(np.where(n < max_exact, n, large).astype(np.int32))
        valids.append(((delta >= 0) & (delta <= win // dil)).astype(np.int32))
    return np.stack(buckets), np.stack(valids)


def _bias_kernel(tab_ref, bm_ref, valid_ref, o_ref):
    gh = pl.program_id(0)
    bm = bm_ref[0]
    acc = jnp.zeros(bm.shape, F32)
    for b in range(N_BUCKETS):
        acc = jnp.where(bm == b, tab_ref[b, gh], acc)
    band = jnp.where(valid_ref[0] > 0, acc, NEG_INF)
    col = lax.broadcasted_iota(I32, bm.shape, 1)
    o_ref[0, 0] = band
    o_ref[0, 1] = jnp.where(col >= ATT_BLOCK, band, NEG_INF)


def _bias_tables(rel_bias):
    bm, valid = _bucket_maps()
    GH = len(DILATED_GROUPS) * ATT_HEADS
    blk = (1, ATT_BLOCK, 2 * ATT_BLOCK)
    return pl.pallas_call(
        _bias_kernel,
        grid=(GH,),
        in_specs=[
            pl.BlockSpec(memory_space=pltpu.SMEM),
            pl.BlockSpec(blk, lambda i: (i // ATT_HEADS, 0, 0)),
            pl.BlockSpec(blk, lambda i: (i // ATT_HEADS, 0, 0)),
        ],
        out_specs=pl.BlockSpec((1, 2, ATT_BLOCK, 2 * ATT_BLOCK), lambda i: (i, 0, 0, 0)),
        out_shape=jax.ShapeDtypeStruct((GH, 2, ATT_BLOCK, 2 * ATT_BLOCK), F32),
        compiler_params=_params(("arbitrary",)),
        name="attn_bias",
    )(rel_bias, jnp.asarray(bm), jnp.asarray(valid))


def _qkv_kernel(x_ref, g_ref, wq_ref, wk_ref, wv_ref, qg_ref, kg_ref, q_ref, k_ref, v_ref,
                w_bf, bd_ref, *, R, tl, D):
    first = (pl.program_id(0) == 0) & (pl.program_id(1) == 0) & (pl.program_id(2) == 0)

    @pl.when(first)
    def _():
        w_bf[0] = wq_ref[...].astype(BF16)
        w_bf[1] = wk_ref[...].astype(BF16)
        w_bf[2] = wv_ref[...].astype(BF16)
        r = lax.broadcasted_iota(I32, bd_ref.shape, 0)
        c = lax.broadcasted_iota(I32, bd_ref.shape, 1)
        hd_shift = HEAD_DIM.bit_length() - 1
        same_head = lax.shift_right_logical(r, hd_shift) == lax.shift_right_logical(c, hd_shift)
        bd_ref[...] = jnp.where(same_head, 1.0, 0.0).astype(BF16)

    xs = [x_ref[0, :, j * D:(j + 1) * D] for j in range(R)]
    x = xs[0] if R == 1 else jnp.concatenate(xs, axis=0)
    h = _rms(x, g_ref[...]).astype(BF16)
    nb = bd_ref.shape[0]

    def head_norm(y, gain):
        parts = []
        for c0 in range(0, y.shape[1], nb):
            yc = y[:, c0:c0 + nb]
            ssq = jnp.dot((yc * yc).astype(BF16), bd_ref[...], preferred_element_type=F32)
            parts.append(yc * lax.rsqrt(ssq * (1.0 / HEAD_DIM) + EPS))
        return jnp.concatenate(parts, axis=1) * gain

    q = head_norm(jnp.dot(h, w_bf[0], preferred_element_type=F32), qg_ref[...])
    q = q * (HEAD_DIM ** -0.5)
    k = head_norm(jnp.dot(h, w_bf[1], preferred_element_type=F32), kg_ref[...])
    v = jnp.dot(h, w_bf[2], preferred_element_type=F32)
    for j in range(R):
        q_ref[0, j] = q[j * tl:(j + 1) * tl].astype(BF16)
        k_ref[0, j] = k[j * tl:(j + 1) * tl].astype(BF16)
        v_ref[0, j] = v[j * tl:(j + 1) * tl].astype(BF16)


def _qkv_group(x, g, w_qkv, q_gain, k_gain, gi, dil, layer):
    B, S, D = x.shape
    HD = ATT_HEADS * HEAD_DIM
    G = len(DILATED_GROUPS)
    L = S // dil
    tl = min(L, QKV_ROWS)
    R = min(dil, QKV_ROWS // tl)
    xv = x.reshape(B, L, dil * D)
    wspec = lambda s: pl.BlockSpec((None, D, HD), lambda b, r, l: (layer, 0, s * G + gi))
    ospec = pl.BlockSpec((1, R, tl, HD), lambda b, r, l: (b, r, l, 0))
    oshape = jax.ShapeDtypeStruct((B, dil, L, HD), BF16)
    return pl.pallas_call(
        functools.partial(_qkv_kernel, R=R, tl=tl, D=D),
        grid=(B, dil // R, L // tl),
        in_specs=[
            pl.BlockSpec((1, tl, R * D), lambda b, r, l: (b, l, r)),
            pl.BlockSpec((1, D), lambda b, r, l: (0, 0)),
            wspec(0), wspec(1), wspec(2),
            pl.BlockSpec((1, HD), lambda b, r, l: (0, 0)),
            pl.BlockSpec((1, HD), lambda b, r, l: (0, 0)),
        ],
        out_specs=[ospec, ospec, ospec],
        out_shape=[oshape, oshape, oshape],
        scratch_shapes=[pltpu.VMEM((3, D, HD), BF16), pltpu.VMEM((256, 256), BF16)],
        compiler_params=_params(("arbitrary", "arbitrary", "arbitrary")),
        name=f"attn_qkv_g{gi}",
    )(xv, g.reshape(1, D), w_qkv, w_qkv, w_qkv,
      jnp.tile(q_gain, ATT_HEADS).reshape(1, HD), jnp.tile(k_gain, ATT_HEADS).reshape(1, HD))


def _attn_kernel(q_ref, kc_ref, kp_ref, vc_ref, vp_ref, bias_ref, o_ref, l_ref):
    n = pl.program_id(2)
    first = jnp.where(n == 0, 1, 0)
    blk = ATT_BLOCK
    lo = lax.broadcasted_iota(I32, (blk, 2 * HEAD_DIM), 1) < HEAD_DIM
    lane_row = lax.broadcasted_iota(I32, (1, 2 * HEAD_DIM), 1)
    half_mask = [jnp.where(lane_row < HEAD_DIM, 1.0, 0.0).astype(BF16),
                 jnp.where(lane_row < HEAD_DIM, 0.0, 1.0).astype(BF16)]
    nt = (((1,), (1,)), ((), ()))
    outs, lses = [], []
    for pr in range(ATT_HEADS // 2):
        cs = slice(pr * 2 * HEAD_DIM, (pr + 1) * 2 * HEAD_DIM)
        q2 = q_ref[0, 0, :, cs]
        k2 = jnp.concatenate([kp_ref[0, 0, :, cs], kc_ref[0, 0, :, cs]], axis=0)
        v2 = jnp.concatenate([vp_ref[0, 0, :, cs], vc_ref[0, 0, :, cs]], axis=0)
        acc = jnp.zeros((blk, 2 * HEAD_DIM), F32)
        inv = jnp.zeros((blk, 2 * HEAD_DIM), F32)
        lse = jnp.zeros((blk, 2 * HEAD_DIM), F32)
        for half in range(2):
            keep = lo if half == 0 else ~lo
            qh = q2 * half_mask[half]
            s = lax.dot_general(qh, k2, nt, preferred_element_type=F32)
            s = s + bias_ref[2 * pr + half, first]
            m = jnp.max(s, axis=-1, keepdims=True)
            p = jnp.exp(s - m)
            den = jnp.sum(p, axis=-1, keepdims=True)
            vh = v2 * half_mask[half]
            acc = acc + jnp.dot(p.astype(BF16), vh, preferred_element_type=F32)
            inv = jnp.where(keep, 1.0 / den, inv)
            lse = jnp.where(keep, m + jnp.log(den), lse)
        outs.append(acc * inv)
        lses.append(lse)
    o_ref[0] = jnp.concatenate(outs, axis=1)
    l_ref[0] = jnp.concatenate(lses, axis=1)


def _attn_group(q, k, v, bias, gi):
    B, dil, L, HD = q.shape
    blk = ATT_BLOCK
    nb = L // blk
    cur = pl.BlockSpec((1, 1, blk, HD), lambda b, r, n: (b, r, n, 0))
    prev = pl.BlockSpec((1, 1, blk, HD), lambda b, r, n: (b, r, jnp.maximum(n - 1, 0), 0))
    ospec = pl.BlockSpec((1, blk, HD), lambda b, r, n: (b, n, r))
    oshape = jax.ShapeDtypeStruct((B, L, dil * HD), F32)
    o, l = pl.pallas_call(
        _attn_kernel,
        grid=(B, dil, nb),
        in_specs=[cur, cur, prev, cur, prev,
                  pl.BlockSpec((ATT_HEADS, 2, blk, 2 * blk), lambda b, r, n: (gi, 0, 0, 0))],
        out_specs=[ospec, ospec],
        out_shape=[oshape, oshape],
        compiler_params=_params(("parallel", "parallel", "arbitrary")),
        name=f"attn_core_g{gi}",
    )(q, k, k, v, v, bias)
    return o.reshape(B, L * dil, HD), l.reshape(B, L * dil, HD)


def _attn_out_kernel(x_ref, o0, o1, o2, l0, l1, l2, wo_ref, out_ref, wo_bf):
    @pl.when(pl.program_id(0) == 0)
    def _():
        wo_bf[...] = wo_ref[...].astype(BF16)

    la, lb, lc = l0[...], l1[...], l2[...]
    m = jnp.maximum(jnp.maximum(la, lb), lc)
    ea, eb, ec = jnp.exp(la - m), jnp.exp(lb - m), jnp.exp(lc - m)
    att = (ea * o0[...] + eb * o1[...] + ec * o2[...]) / (ea + eb + ec)
    out_ref[...] = x_ref[...] + jnp.dot(att.astype(BF16), wo_bf[...], preferred_element_type=F32)


def _attn_out(x, os, ls, w_o):
    N, D = x.shape
    HD = w_o.shape[0]
    tt = OUT_TILE
    row = lambda w: pl.BlockSpec((tt, w), lambda i: (i, 0))
    return pl.pallas_call(
        _attn_out_kernel,
        grid=(N // tt,),
        in_specs=[row(D)] + [row(HD)] * 6 + [pl.BlockSpec((HD, D), lambda i: (0, 0))],
        out_specs=row(D),
        out_shape=jax.ShapeDtypeStruct((N, D), F32),
        scratch_shapes=[pltpu.VMEM((HD, D), BF16)],
        compiler_params=_params(("arbitrary",)),
        name="attn_out",
    )(x, *os, *ls, w_o)


def _attention_layer(x, g, w_qkv, q_gain, k_gain, w_o, rel_bias, layer):
    B, S, D = x.shape
    HD = ATT_HEADS * HEAD_DIM
    bias = _bias_tables(rel_bias)
    os, ls = [], []
    for gi, (_, dil) in enumerate(DILATED_GROUPS):
        q, k, v = _qkv_group(x, g, w_qkv, q_gain, k_gain, gi, dil, layer)
        o, l = _attn_group(q, k, v, bias, gi)
        os.append(o.reshape(B * S, HD))
        ls.append(l.reshape(B * S, HD))
    return _attn_out(x.reshape(B * S, D), os, ls, w_o).reshape(B, S, D)


def kernel(x, norm_mix_g, norm_ffn_g, pool_w, pool_scale, attn_w_qkv, attn_q_gain, attn_k_gain,
           attn_w_o, rel_bias, moe_w_router, moe_b_router, moe_w_gate_up, moe_b_gate_up,
           moe_w_down, moe_b_down):
    B, S, D = x.shape
    depth = norm_mix_g.shape[0]
    for i in range(depth):
        j = i // 2
        if i % 2 == 0:
            x = _pool_layer(x, norm_mix_g[i], pool_w[j], pool_scale[j])
        else:
            x = _attention_layer(x, norm_mix_g[i], attn_w_qkv, attn_q_gain[j], attn_k_gain[j],
                                 attn_w_o[j], rel_bias, j)
        x = _moe_layer(x.reshape(B * S, D), norm_ffn_g[i], moe_w_router[i], moe_b_router[i],
                       moe_w_gate_up, moe_b_gate_up[i], moe_w_down, moe_b_down[i],
                       i).reshape(B, S, D)
    return x
```

```python
import functools
import math

import numpy as np
import jax
import jax.numpy as jnp
from jax import lax
from jax.experimental import pallas as pl
from jax.experimental.pallas import tpu as pltpu

F32 = jnp.float32
BF16 = jnp.bfloat16
I32 = jnp.int32

EPS = 1e-6
POOL_WINDOWS = (2, 4, 8, 16)
POOL_HALO = 16
DILATED_GROUPS = ((128, 1), (512, 4), (2048, 16))
ATT_HEADS = 16
HEAD_DIM = 64
ATT_BLOCK = 128
N_BUCKETS = 32
REL_MAX_DIST = 2048
NEG_INF = -1e30
N_EXPERTS = 32
TOP_K = 4
SWIGLU_LIMIT = 7.0
SWIGLU_ALPHA = 1.702

V7X_VMEM_LIMIT_BYTES = 56 * 1024 * 1024
MOE_TILE = 256
ROW_TILE = 256
ROUTER_TILE = 512
POOL_TILE = 512
QKV_ROWS = 512
OUT_TILE = 256


def _rms(xf, g):
    ms = jnp.mean(xf * xf, axis=-1, keepdims=True)
    return xf * lax.rsqrt(ms + EPS) * g


def _params(sem, vmem=None):
    return pltpu.CompilerParams(dimension_semantics=sem,
                                vmem_limit_bytes=vmem or V7X_VMEM_LIMIT_BYTES)


def _pool_kernel(x_ref, halo_ref, g_ref, w_ref, sc_ref, o_ref, *, ts, dg):
    i = pl.program_id(1)
    x = x_ref[0]
    g = g_ref[...]
    h = _rms(x, g)
    hh = _rms(halo_ref[0], g)
    hh = jnp.where(i == 0, 0.0, hh)
    full = jnp.concatenate([hh, h], axis=0)
    pos = i * ts + lax.broadcasted_iota(I32, (ts, 1), 0)
    outs = []
    for gi, w in enumerate(POOL_WINDOWS):
        s = full[:, gi * dg:(gi + 1) * dg]
        sh = 1
        while sh < w:
            s = s + pltpu.roll(s, sh, 0)
            sh *= 2
        s = s[POOL_HALO:]
        cnt = jnp.minimum(pos + 1, w).astype(F32)
        p = s / cnt - h[:, gi * dg:(gi + 1) * dg]
        outs.append(jnp.dot(p.astype(BF16), w_ref[gi].astype(BF16),
                            preferred_element_type=F32))
    y = jnp.concatenate(outs, axis=1) * sc_ref[...]
    o_ref[0] = x + y


def _pool_layer(x, g, w_groups, scale):
    B, S, D = x.shape
    ts = POOL_TILE
    dg = D // len(POOL_WINDOWS)
    hb = ts // POOL_HALO
    return pl.pallas_call(
        functools.partial(_pool_kernel, ts=ts, dg=dg),
        grid=(B, S // ts),
        in_specs=[
            pl.BlockSpec((1, ts, D), lambda b, i: (b, i, 0)),
            pl.BlockSpec((1, POOL_HALO, D), lambda b, i: (b, jnp.maximum(i * hb - 1, 0), 0)),
            pl.BlockSpec((1, D), lambda b, i: (0, 0)),
            pl.BlockSpec((len(POOL_WINDOWS), dg, dg), lambda b, i: (0, 0, 0)),
            pl.BlockSpec((1, D), lambda b, i: (0, 0)),
        ],
        out_specs=pl.BlockSpec((1, ts, D), lambda b, i: (b, i, 0)),
        out_shape=jax.ShapeDtypeStruct((B, S, D), F32),
        compiler_params=_params(("parallel", "parallel")),
        name="pool_mixer",
    )(x, x, g.reshape(1, D), w_groups, scale.reshape(1, D))


def _split3_dot(a, b):
    a_hi = a.astype(BF16)
    a_lo = (a - a_hi.astype(F32)).astype(BF16)
    b_hi = b.astype(BF16)
    b_lo = (b - b_hi.astype(F32)).astype(BF16)
    d = functools.partial(jnp.dot, preferred_element_type=F32)
    return d(a_hi, b_hi) + (d(a_hi, b_lo) + d(a_lo, b_hi))


def _router_kernel(x_ref, g_ref, wr_ref, br_ref, e_ref, gate_ref, rank_ref, cnt_ref,
                   tri_ref, carry_ref, *, tt):
    i = pl.program_id(0)
    E = N_EXPERTS

    @pl.when(i == 0)
    def _():
        r = lax.broadcasted_iota(I32, (tt, tt), 0)
        c = lax.broadcasted_iota(I32, (tt, tt), 1)
        tri_ref[...] = jnp.where(c < r, 1.0, 0.0).astype(BF16)
        carry_ref[...] = jnp.zeros_like(carry_ref)

    h = _rms(x_ref[...], g_ref[...])
    logits = _split3_dot(h, wr_ref[...]) + br_ref[...]
    lane = lax.broadcasted_iota(I32, (tt, E), 1).astype(F32)
    lane_k = lax.broadcasted_iota(I32, (tt, TOP_K), 1)
    l = logits
    vals, sels = [], []
    e_out = jnp.zeros((tt, TOP_K), F32)
    for k in range(TOP_K):
        m = jnp.max(l, axis=-1, keepdims=True)
        idx = jnp.min(jnp.where(l == m, lane, float(E)), axis=-1, keepdims=True)
        sel = lane == idx
        vals.append(m)
        sels.append(sel)
        e_out = jnp.where(lane_k == k, idx, e_out)
        l = jnp.where(sel, -jnp.inf, l)
    ex = [jnp.exp(v - vals[0]) for v in vals]
    den = ex[0] + ex[1] + ex[2] + ex[3]
    multi = jnp.zeros((tt, E), F32)
    for sel in sels:
        multi = multi + jnp.where(sel, 1.0, 0.0)
    before = jnp.dot(tri_ref[...], multi.astype(BF16), preferred_element_type=F32)
    before = before + carry_ref[...]
    g_out = jnp.zeros((tt, TOP_K), F32)
    r_out = jnp.zeros((tt, TOP_K), F32)
    for k in range(TOP_K):
        g_out = jnp.where(lane_k == k, ex[k] / den, g_out)
        rk = jnp.sum(jnp.where(sels[k], before, 0.0), axis=-1, keepdims=True)
        r_out = jnp.where(lane_k == k, rk, r_out)
    e_ref[...] = e_out.astype(I32)
    gate_ref[...] = g_out
    rank_ref[...] = r_out.astype(I32)
    carry_ref[...] = carry_ref[...] + jnp.sum(multi, axis=0, keepdims=True)
    cnt_ref[...] = carry_ref[...]


def _router(x, g, w_r, b_r):
    N, D = x.shape
    tt = ROUTER_TILE
    E = N_EXPERTS
    return pl.pallas_call(
        functools.partial(_router_kernel, tt=tt),
        grid=(N // tt,),
        in_specs=[
            pl.BlockSpec((tt, D), lambda i: (i, 0)),
            pl.BlockSpec((1, D), lambda i: (0, 0)),
            pl.BlockSpec((D, E), lambda i: (0, 0)),
            pl.BlockSpec((1, E), lambda i: (0, 0)),
        ],
        out_specs=[
            pl.BlockSpec((tt, TOP_K), lambda i: (i, 0)),
            pl.BlockSpec((tt, TOP_K), lambda i: (i, 0)),
            pl.BlockSpec((tt, TOP_K), lambda i: (i, 0)),
            pl.BlockSpec((1, E), lambda i: (0, 0)),
        ],
        out_shape=[
            jax.ShapeDtypeStruct((N, TOP_K), I32),
            jax.ShapeDtypeStruct((N, TOP_K), F32),
            jax.ShapeDtypeStruct((N, TOP_K), I32),
            jax.ShapeDtypeStruct((1, E), F32),
        ],
        scratch_shapes=[pltpu.VMEM((tt, tt), BF16), pltpu.VMEM((1, E), F32)],
        compiler_params=_params(("arbitrary",)),
        name="moe_router",
    )(x, g.reshape(1, D), w_r, b_r.reshape(1, E))


def _row_copy_wait(src, dst, sem, rows, times):
    for _ in range(times):
        pltpu.make_async_copy(src, dst.at[pl.ds(0, rows)], sem).wait()


def _dispatch_kernel(pend_ref, nu_ref, dest_ref, x_ref, g_ref, xpad_ref, hbuf, zbuf, sems, zsem,
                     *, tt, tm, nt):
    i = pl.program_id(0)
    slot = i % 2

    @pl.when(i == 0)
    def _():
        zbuf[...] = jnp.zeros_like(zbuf)

        def seg_copy(e):
            start = pl.multiple_of(jnp.maximum(pend_ref[e] - tm, 0), tm)
            return pltpu.make_async_copy(zbuf, xpad_ref.at[pl.ds(start, tm)], zsem)

        def tail_copy(j):
            return pltpu.make_async_copy(zbuf, xpad_ref.at[pl.ds((nt - 1 - j) * tm, tm)], zsem)

        for e in range(N_EXPERTS):
            seg_copy(e).start()
        for j in range(N_EXPERTS):
            pl.when(nt - 1 - j >= nu_ref[0])(tail_copy(j).start)
        for e in range(N_EXPERTS):
            seg_copy(e).wait()
        for j in range(N_EXPERTS):
            pl.when(nt - 1 - j >= nu_ref[0])(tail_copy(j).wait)

    hb = hbuf.at[slot]
    hb[...] = _rms(x_ref[...], g_ref[...])

    def issue(t, carry):
        for k in range(TOP_K):
            d = dest_ref[t * TOP_K + k]
            pltpu.make_async_copy(hb.at[pl.ds(t, 1)], xpad_ref.at[pl.ds(d, 1)],
                                  sems.at[slot]).start(priority=k % 2)
        return carry

    lax.fori_loop(0, tt, issue, 0, unroll=8)

    @pl.when(i > 0)
    def _():
        _row_copy_wait(hbuf.at[1 - slot], xpad_ref, sems.at[1 - slot], tt, TOP_K)

    @pl.when(i == pl.num_programs(0) - 1)
    def _():
        _row_copy_wait(hb, xpad_ref, sems.at[slot], tt, TOP_K)


def _dispatch(x, g, dest, pend, n_used, P):
    N, D = x.shape
    tt = ROW_TILE
    tm = MOE_TILE
    grid_spec = pltpu.PrefetchScalarGridSpec(
        num_scalar_prefetch=2,
        grid=(N // tt,),
        in_specs=[
            pl.BlockSpec((tt * TOP_K,), lambda i, pe, nu: (i,), memory_space=pltpu.SMEM),
            pl.BlockSpec((tt, D), lambda i, pe, nu: (i, 0)),
            pl.BlockSpec((1, D), lambda i, pe, nu: (0, 0)),
        ],
        out_specs=pl.BlockSpec(memory_space=pl.ANY),
        scratch_shapes=[pltpu.VMEM((2, tt, D), F32), pltpu.VMEM((tm, D), F32),
                        pltpu.SemaphoreType.DMA((2,)), pltpu.SemaphoreType.DMA(())],
    )
    return pl.pallas_call(
        functools.partial(_dispatch_kernel, tt=tt, tm=tm, nt=P // tm),
        grid_spec=grid_spec,
        out_shape=jax.ShapeDtypeStruct((P, D), F32),
        compiler_params=_params(("arbitrary",)),
        name="moe_dispatch",
    )(pend, n_used, dest, x, g.reshape(1, D))


def _expert_kernel(te_ref, nu_ref, x_ref, wgu_ref, bgu_ref, wdn_ref, bdn_ref, o_ref,
                   wgu_bf, wdn_bf, *, F):
    i = pl.program_id(0)

    @pl.when(i < nu_ref[0])
    def _():
        prev = te_ref[jnp.maximum(i - 1, 0)]

        @pl.when((i == 0) | (te_ref[i] != prev))
        def _():
            wgu_bf[...] = wgu_ref[0].astype(BF16)
            wdn_bf[...] = wdn_ref[0].astype(BF16)

        x = x_ref[...].astype(BF16)
        gu = jnp.dot(x, wgu_bf[...], preferred_element_type=F32) + bgu_ref[0]
        gate = jnp.minimum(gu[:, :F], SWIGLU_LIMIT)
        up = jnp.clip(gu[:, F:], -SWIGLU_LIMIT, SWIGLU_LIMIT)
        glu = gate * jax.nn.sigmoid(SWIGLU_ALPHA * gate)
        a = ((up + 1.0) * glu).astype(BF16)
        o_ref[...] = jnp.dot(a, wdn_bf[...], preferred_element_type=F32) + bdn_ref[0]

    @pl.when(i >= nu_ref[0])
    def _():
        o_ref[...] = jnp.zeros_like(o_ref)


def _experts(tile_e, n_used, x_pad, w_gu, b_gu, w_dn, b_dn, layer):
    P, D = x_pad.shape
    _, E, _, F2 = w_gu.shape
    F = F2 // 2
    tm = MOE_TILE
    row = lambda i, te, nu: (jnp.minimum(i, nu[0] - 1), 0)
    exp4 = lambda i, te, nu: (layer, te[i], 0, 0)
    exp3 = lambda i, te, nu: (te[i], 0, 0)
    grid_spec = pltpu.PrefetchScalarGridSpec(
        num_scalar_prefetch=2,
        grid=(P // tm,),
        in_specs=[
            pl.BlockSpec((tm, D), row),
            pl.BlockSpec((None, 1, D, F2), exp4),
            pl.BlockSpec((1, 1, F2), exp3),
            pl.BlockSpec((None, 1, F, D), exp4),
            pl.BlockSpec((1, 1, D), exp3),
        ],
        out_specs=pl.BlockSpec((tm, D), lambda i, te, nu: (i, 0)),
        scratch_shapes=[pltpu.VMEM((D, F2), BF16), pltpu.VMEM((F, D), BF16)],
    )
    return pl.pallas_call(
        functools.partial(_expert_kernel, F=F),
        grid_spec=grid_spec,
        out_shape=jax.ShapeDtypeStruct((P, D), F32),
        compiler_params=_params(("arbitrary",)),
        name="moe_experts",
    )(tile_e, n_used, x_pad, w_gu, b_gu.reshape(E, 1, F2), w_dn, b_dn.reshape(E, 1, D))


def _combine_kernel(dest_ref, dnext_ref, x_ref, gate_ref, ypad_ref, o_ref, ybuf, sems, *, tt):
    i = pl.program_id(0)
    slot = i % 2

    def gather(idx_ref, s):
        def issue(t, carry):
            for k in range(TOP_K):
                d = idx_ref[t * TOP_K + k]
                pltpu.make_async_copy(ypad_ref.at[pl.ds(d, 1)], ybuf.at[s, k, pl.ds(t, 1)],
                                      sems.at[s]).start(priority=k % 2)
            return carry

        lax.fori_loop(0, tt, issue, 0, unroll=8)

    @pl.when(i == 0)
    def _():
        gather(dest_ref, slot)

    @pl.when(i + 1 < pl.num_programs(0))
    def _():
        gather(dnext_ref, 1 - slot)

    for k in range(TOP_K):
        pltpu.make_async_copy(ypad_ref.at[pl.ds(0, tt)], ybuf.at[slot, k], sems.at[slot]).wait()
    gate = gate_ref[...]
    acc = x_ref[...]
    for k in range(TOP_K):
        acc = acc + ybuf[slot, k] * gate[:, k:k + 1]
    o_ref[...] = acc


def _combine(x, gate, dest, y_pad):
    N, D = x.shape
    tt = ROW_TILE
    last = N // tt - 1
    return pl.pallas_call(
        functools.partial(_combine_kernel, tt=tt),
        grid=(N // tt,),
        in_specs=[
            pl.BlockSpec((tt * TOP_K,), lambda i: (i,), memory_space=pltpu.SMEM),
            pl.BlockSpec((tt * TOP_K,), lambda i: (jnp.minimum(i + 1, last),), memory_space=pltpu.SMEM),
            pl.BlockSpec((tt, D), lambda i: (i, 0)),
            pl.BlockSpec((tt, TOP_K), lambda i: (i, 0)),
            pl.BlockSpec(memory_space=pl.ANY),
        ],
        out_specs=pl.BlockSpec((tt, D), lambda i: (i, 0)),
        out_shape=jax.ShapeDtypeStruct((N, D), F32),
        scratch_shapes=[pltpu.VMEM((2, TOP_K, tt, D), F32), pltpu.SemaphoreType.DMA((2,))],
        compiler_params=_params(("arbitrary",)),
        name="moe_combine",
    )(dest, dest, x, gate, y_pad)


def _moe_layer(x, g, w_r, b_r, w_gu, b_gu, w_dn, b_dn, layer):
    N, D = x.shape
    E = N_EXPERTS
    tm = MOE_TILE
    e_idx, gate, rank, counts = _router(x, g, w_r, b_r)
    counts = counts[0].astype(I32)
    padded = ((counts + tm - 1) // tm) * tm
    pend = jnp.cumsum(padded)
    pstart = pend - padded
    dest = (pstart[e_idx] + rank).reshape(-1)
    P = ((N * TOP_K + E * (tm - 1) + tm - 1) // tm) * tm
    nt = P // tm
    tiles = jnp.arange(nt, dtype=I32)
    tile_e = jnp.minimum(jnp.sum(tiles[:, None] * tm >= pend[None, :], axis=1), E - 1).astype(I32)
    n_used = (pend[-1] // tm).astype(I32)
    tile_e = jnp.where(tiles < n_used, tile_e, tile_e[n_used - 1])
    x_pad = _dispatch(x, g, dest, pend.astype(I32), n_used.reshape(1), P)
    y_pad = _experts(tile_e, n_used.reshape(1), x_pad, w_gu, b_gu, w_dn, b_dn, layer)
    return _combine(x, gate, dest, y_pad)


def _bucket_maps():
    qi = np.arange(ATT_BLOCK)[:, None]
    kj = np.arange(2 * ATT_BLOCK)[None, :]
    delta = qi + ATT_BLOCK - kj
    buckets, valids = [], []
    max_exact = N_BUCKETS // 2
    for win, dil in DILATED_GROUPS:
        n = np.maximum(delta * dil, 0)
        nf = np.maximum(n, 1).astype(np.float32)
        large = max_exact + (np.log(nf / np.float32(max_exact)) / np.float32(math.log(REL_MAX_DIST / max_exact))
                             * np.float32(N_BUCKETS - max_exact)).astype(np.int32)
        large = np.minimum(large, N_BUCKETS - 1)
        buckets.append(np.where(n < max_exact, n, large).astype(np.int32))
        valids.append(((delta >= 0) & (delta <= win // dil)).astype(np.int32))
    return np.stack(buckets), np.stack(valids)


def _bias_kernel(tab_ref, bm_ref, valid_ref, o_ref):
    gh = pl.program_id(0)
    bm = bm_ref[0]
    acc = jnp.zeros(bm.shape, F32)
    for b in range(N_BUCKETS):
        acc = jnp.where(bm == b, tab_ref[b, gh], acc)
    band = jnp.where(valid_ref[0] > 0, acc, NEG_INF)
    col = lax.broadcasted_iota(I32, bm.shape, 1)
    o_ref[0, 0] = band
    o_ref[0, 1] = jnp.where(col >= ATT_BLOCK, band, NEG_INF)


def _bias_tables(rel_bias):
    bm, valid = _bucket_maps()
    GH = len(DILATED_GROUPS) * ATT_HEADS
    blk = (1, ATT_BLOCK, 2 * ATT_BLOCK)
    return pl.pallas_call(
        _bias_kernel,
        grid=(GH,),
        in_specs=[
            pl.BlockSpec(memory_space=pltpu.SMEM),
            pl.BlockSpec(blk, lambda i: (i // ATT_HEADS, 0, 0)),
            pl.BlockSpec(blk, lambda i: (i // ATT_HEADS, 0, 0)),
        ],
        out_specs=pl.BlockSpec((1, 2, ATT_BLOCK, 2 * ATT_BLOCK), lambda i: (i, 0, 0, 0)),
        out_shape=jax.ShapeDtypeStruct((GH, 2, ATT_BLOCK, 2 * ATT_BLOCK), F32),
        compiler_params=_params(("arbitrary",)),
        name="attn_bias",
    )(rel_bias, jnp.asarray(bm), jnp.asarray(valid))


def _qkv_kernel(x_ref, g_ref, wq_ref, wk_ref, wv_ref, qg_ref, kg_ref, q_ref, k_ref, v_ref,
                w_bf, bd_ref, *, R, tl, D):
    first = (pl.program_id(0) == 0) & (pl.program_id(1) == 0) & (pl.program_id(2) == 0)

    @pl.when(first)
    def _():
        w_bf[0] = wq_ref[...].astype(BF16)
        w_bf[1] = wk_ref[...].astype(BF16)
        w_bf[2] = wv_ref[...].astype(BF16)
        r = lax.broadcasted_iota(I32, bd_ref.shape, 0)
        c = lax.broadcasted_iota(I32, bd_ref.shape, 1)
        hd_shift = HEAD_DIM.bit_length() - 1
        same_head = lax.shift_right_logical(r, hd_shift) == lax.shift_right_logical(c, hd_shift)
        bd_ref[...] = jnp.where(same_head, 1.0, 0.0).astype(BF16)

    xs = [x_ref[0, :, j * D:(j + 1) * D] for j in range(R)]
    x = xs[0] if R == 1 else jnp.concatenate(xs, axis=0)
    h = _rms(x, g_ref[...]).astype(BF16)
    nb = bd_ref.shape[0]

    def head_norm(y, gain):
        parts = []
        for c0 in range(0, y.shape[1], nb):
            yc = y[:, c0:c0 + nb]
            ssq = jnp.dot((yc * yc).astype(BF16), bd_ref[...], preferred_element_type=F32)
            parts.append(yc * lax.rsqrt(ssq * (1.0 / HEAD_DIM) + EPS))
        return jnp.concatenate(parts, axis=1) * gain

    q = head_norm(jnp.dot(h, w_bf[0], preferred_element_type=F32), qg_ref[...])
    q = q * (HEAD_DIM ** -0.5)
    k = head_norm(jnp.dot(h, w_bf[1], preferred_element_type=F32), kg_ref[...])
    v = jnp.dot(h, w_bf[2], preferred_element_type=F32)
    for j in range(R):
        q_ref[0, j] = q[j * tl:(j + 1) * tl].astype(BF16)
        k_ref[0, j] = k[j * tl:(j + 1) * tl].astype(BF16)
        v_ref[0, j] = v[j * tl:(j + 1) * tl].astype(BF16)


def _qkv_group(x, g, w_qkv, q_gain, k_gain, gi, dil, layer):
    B, S, D = x.shape
    HD = ATT_HEADS * HEAD_DIM
    G = len(DILATED_GROUPS)
    L = S // dil
    tl = min(L, QKV_ROWS)
    R = min(dil, QKV_ROWS // tl)
    xv = x.reshape(B, L, dil * D)
    wspec = lambda s: pl.BlockSpec((None, D, HD), lambda b, r, l: (layer, 0, s * G + gi))
    ospec = pl.BlockSpec((1, R, tl, HD), lambda b, r, l: (b, r, l, 0))
    oshape = jax.ShapeDtypeStruct((B, dil, L, HD), BF16)
    return pl.pallas_call(
        functools.partial(_qkv_kernel, R=R, tl=tl, D=D),
        grid=(B, dil // R, L // tl),
        in_specs=[
            pl.BlockSpec((1, tl, R * D), lambda b, r, l: (b, l, r)),
            pl.BlockSpec((1, D), lambda b, r, l: (0, 0)),
            wspec(0), wspec(1), wspec(2),
            pl.BlockSpec((1, HD), lambda b, r, l: (0, 0)),
            pl.BlockSpec((1, HD), lambda b, r, l: (0, 0)),
        ],
        out_specs=[ospec, ospec, ospec],
        out_shape=[oshape, oshape, oshape],
        scratch_shapes=[pltpu.VMEM((3, D, HD), BF16), pltpu.VMEM((256, 256), BF16)],
        compiler_params=_params(("arbitrary", "arbitrary", "arbitrary")),
        name=f"attn_qkv_g{gi}",
    )(xv, g.reshape(1, D), w_qkv, w_qkv, w_qkv,
      jnp.tile(q_gain, ATT_HEADS).reshape(1, HD), jnp.tile(k_gain, ATT_HEADS).reshape(1, HD))


def _attn_kernel(q_ref, kc_ref, kp_ref, vc_ref, vp_ref, bias_ref, o_ref, l_ref):
    n = pl.program_id(2)
    first = jnp.where(n == 0, 1, 0)
    blk = ATT_BLOCK
    lo = lax.broadcasted_iota(I32, (blk, 2 * HEAD_DIM), 1) < HEAD_DIM
    lane_row = lax.broadcasted_iota(I32, (1, 2 * HEAD_DIM), 1)
    half_mask = [jnp.where(lane_row < HEAD_DIM, 1.0, 0.0).astype(BF16),
                 jnp.where(lane_row < HEAD_DIM, 0.0, 1.0).astype(BF16)]
    nt = (((1,), (1,)), ((), ()))
    outs, lses = [], []
    for pr in range(ATT_HEADS // 2):
        cs = slice(pr * 2 * HEAD_DIM, (pr + 1) * 2 * HEAD_DIM)
        q2 = q_ref[0, 0, :, cs]
        k2 = jnp.concatenate([kp_ref[0, 0, :, cs], kc_ref[0, 0, :, cs]], axis=0)
        v2 = jnp.concatenate([vp_ref[0, 0, :, cs], vc_ref[0, 0, :, cs]], axis=0)
        acc = jnp.zeros((blk, 2 * HEAD_DIM), F32)
        inv = jnp.zeros((blk, 2 * HEAD_DIM), F32)
        lse = jnp.zeros((blk, 2 * HEAD_DIM), F32)
        for half in range(2):
            keep = lo if half == 0 else ~lo
            qh = q2 * half_mask[half]
            s = lax.dot_general(qh, k2, nt, preferred_element_type=F32)
            s = s + bias_ref[2 * pr + half, first]
            m = jnp.max(s, axis=-1, keepdims=True)
            p = jnp.exp(s - m)
            den = jnp.sum(p, axis=-1, keepdims=True)
            vh = v2 * half_mask[half]
            acc = acc + jnp.dot(p.astype(BF16), vh, preferred_element_type=F32)
            inv = jnp.where(keep, 1.0 / den, inv)
            lse = jnp.where(keep, m + jnp.log(den), lse)
        outs.append(acc * inv)
        lses.append(lse)
    o_ref[0] = jnp.concatenate(outs, axis=1)
    l_ref[0] = jnp.concatenate(lses, axis=1)


def _attn_group(q, k, v, bias, gi):
    B, dil, L, HD = q.shape
    blk = ATT_BLOCK
    nb = L // blk
    cur = pl.BlockSpec((1, 1, blk, HD), lambda b, r, n: (b, r, n, 0))
    prev = pl.BlockSpec((1, 1, blk, HD), lambda b, r, n: (b, r, jnp.maximum(n - 1, 0), 0))
    ospec = pl.BlockSpec((1, blk, HD), lambda b, r, n: (b, n, r))
    oshape = jax.ShapeDtypeStruct((B, L, dil * HD), F32)
    o, l = pl.pallas_call(
        _attn_kernel,
        grid=(B, dil, nb),
        in_specs=[cur, cur, prev, cur, prev,
                  pl.BlockSpec((ATT_HEADS, 2, blk, 2 * blk), lambda b, r, n: (gi, 0, 0, 0))],
        out_specs=[ospec, ospec],
        out_shape=[oshape, oshape],
        compiler_params=_params(("parallel", "parallel", "arbitrary")),
        name=f"attn_core_g{gi}",
    )(q, k, k, v, v, bias)
    return o.reshape(B, L * dil, HD), l.reshape(B, L * dil, HD)


def _attn_out_kernel(x_ref, o0, o1, o2, l0, l1, l2, wo_ref, out_ref, wo_bf):
    @pl.when(pl.program_id(0) == 0)
    def _():
        wo_bf[...] = wo_ref[...].astype(BF16)

    la, lb, lc = l0[...], l1[...], l2[...]
    m = jnp.maximum(jnp.maximum(la, lb), lc)
    ea, eb, ec = jnp.exp(la - m), jnp.exp(lb - m), jnp.exp(lc - m)
    att = (ea * o0[...] + eb * o1[...] + ec * o2[...]) / (ea + eb + ec)
    out_ref[...] = x_ref[...] + jnp.dot(att.astype(BF16), wo_bf[...], preferred_element_type=F32)


def _attn_out(x, os, ls, w_o):
    N, D = x.shape
    HD = w_o.shape[0]
    tt = OUT_TILE
    row = lambda w: pl.BlockSpec((tt, w), lambda i: (i, 0))
    return pl.pallas_call(
        _attn_out_kernel,
        grid=(N // tt,),
        in_specs=[row(D)] + [row(HD)] * 6 + [pl.BlockSpec((HD, D), lambda i: (0, 0))],
        out_specs=row(D),
        out_shape=jax.ShapeDtypeStruct((N, D), F32),
        scratch_shapes=[pltpu.VMEM((HD, D), BF16)],
        compiler_params=_params(("arbitrary",)),
        name="attn_out",
    )(x, *os, *ls, w_o)


def _attention_layer(x, g, w_qkv, q_gain, k_gain, w_o, rel_bias, layer):
    B, S, D = x.shape
    HD = ATT_HEADS * HEAD_DIM
    bias = _bias_tables(rel_bias)
    os, ls = [], []
    for gi, (_, dil) in enumerate(DILATED_GROUPS):
        q, k, v = _qkv_group(x, g, w_qkv, q_gain, k_gain, gi, dil, layer)
        o, l = _attn_group(q, k, v, bias, gi)
        os.append(o.reshape(B * S, HD))
        ls.append(l.reshape(B * S, HD))
    return _attn_out(x.reshape(B * S, D), os, ls, w_o).reshape(B, S, D)


def kernel(x, norm_mix_g, norm_ffn_g, pool_w, pool_scale, attn_w_qkv, attn_q_gain, attn_k_gain,
           attn_w_o, rel_bias, moe_w_router, moe_b_router, moe_w_gate_up, moe_b_gate_up,
           moe_w_down, moe_b_down):
    B, S, D = x.shape
    depth = norm_mix_g.shape[0]
    for i in range(depth):
        j = i // 2
        if i % 2 == 0:
            x = _pool_layer(x, norm_mix_g[i], pool_w[j], pool_scale[j])
        else:
            x = _attention_layer(x, norm_mix_g[i], attn_w_qkv, attn_q_gain[j], attn_k_gain[j],
                                 attn_w_o[j], rel_bias, j)
        x = _moe_layer(x.reshape(B * S, D), norm_ffn_g[i], moe_w_router[i], moe_b_router[i],
                       moe_w_gate_up, moe_b_gate_up[i], moe_w_down, moe_b_down[i],
                       i).reshape(B, S, D)
    return x
```

```python
import functools
import math

import numpy as np
import jax
import jax.numpy as jnp
from jax import lax
from jax.experimental import pallas as pl
from jax.experimental.pallas import tpu as pltpu

F32 = jnp.float32
BF16 = jnp.bfloat16
I32 = jnp.int32

EPS = 1e-6
POOL_WINDOWS = (2, 4, 8, 16)
POOL_HALO = 16
DILATED_GROUPS = ((128, 1), (512, 4), (2048, 16))
ATT_HEADS = 16
HEAD_DIM = 64
ATT_BLOCK = 128
N_BUCKETS = 32
REL_MAX_DIST = 2048
NEG_INF = -1e30
N_EXPERTS = 32
TOP_K = 4
SWIGLU_LIMIT = 7.0
SWIGLU_ALPHA = 1.702

V7X_VMEM_LIMIT_BYTES = 56 * 1024 * 1024
MOE_TILE = 256
ROW_TILE = 256
ROUTER_TILE = 512
POOL_TILE = 512
QKV_ROWS = 512
OUT_TILE = 256


def _rms(xf, g):
    ms = jnp.mean(xf * xf, axis=-1, keepdims=True)
    return xf * lax.rsqrt(ms + EPS) * g


def _params(sem, vmem=None):
    return pltpu.CompilerParams(dimension_semantics=sem,
                                vmem_limit_bytes=vmem or V7X_VMEM_LIMIT_BYTES)


def _pool_kernel(x_ref, halo_ref, g_ref, w_ref, sc_ref, o_ref, *, ts, dg):
    i = pl.program_id(1)
    x = x_ref[0]
    g = g_ref[...]
    h = _rms(x, g)
    hh = _rms(halo_ref[0], g)
    hh = jnp.where(i == 0, 0.0, hh)
    full = jnp.concatenate([hh, h], axis=0)
    pos = i * ts + lax.broadcasted_iota(I32, (ts, 1), 0)
    outs = []
    for gi, w in enumerate(POOL_WINDOWS):
        s = full[:, gi * dg:(gi + 1) * dg]
        sh = 1
        while sh < w:
            s = s + pltpu.roll(s, sh, 0)
            sh *= 2
        s = s[POOL_HALO:]
        cnt = jnp.minimum(pos + 1, w).astype(F32)
        p = s / cnt - h[:, gi * dg:(gi + 1) * dg]
        outs.append(jnp.dot(p.astype(BF16), w_ref[gi].astype(BF16),
                            preferred_element_type=F32))
    y = jnp.concatenate(outs, axis=1) * sc_ref[...]
    o_ref[0] = x + y


def _pool_layer(x, g, w_groups, scale):
    B, S, D = x.shape
    ts = POOL_TILE
    dg = D // len(POOL_WINDOWS)
    hb = ts // POOL_HALO
    return pl.pallas_call(
        functools.partial(_pool_kernel, ts=ts, dg=dg),
        grid=(B, S // ts),
        in_specs=[
            pl.BlockSpec((1, ts, D), lambda b, i: (b, i, 0)),
            pl.BlockSpec((1, POOL_HALO, D), lambda b, i: (b, jnp.maximum(i * hb - 1, 0), 0)),
            pl.BlockSpec((1, D), lambda b, i: (0, 0)),
            pl.BlockSpec((len(POOL_WINDOWS), dg, dg), lambda b, i: (0, 0, 0)),
            pl.BlockSpec((1, D), lambda b, i: (0, 0)),
        ],
        out_specs=pl.BlockSpec((1, ts, D), lambda b, i: (b, i, 0)),
        out_shape=jax.ShapeDtypeStruct((B, S, D), F32),
        compiler_params=_params(("parallel", "parallel")),
        name="pool_mixer",
    )(x, x, g.reshape(1, D), w_groups, scale.reshape(1, D))


def _split3_dot(a, b):
    a_hi = a.astype(BF16)
    a_lo = (a - a_hi.astype(F32)).astype(BF16)
    b_hi = b.astype(BF16)
    b_lo = (b - b_hi.astype(F32)).astype(BF16)
    d = functools.partial(jnp.dot, preferred_element_type=F32)
    return d(a_hi, b_hi) + (d(a_hi, b_lo) + d(a_lo, b_hi))


def _router_kernel(x_ref, g_ref, wr_ref, br_ref, e_ref, gate_ref, rank_ref, cnt_ref,
                   tri_ref, carry_ref, *, tt):
    i = pl.program_id(0)
    E = N_EXPERTS

    @pl.when(i == 0)
    def _():
        r = lax.broadcasted_iota(I32, (tt, tt), 0)
        c = lax.broadcasted_iota(I32, (tt, tt), 1)
        tri_ref[...] = jnp.where(c < r, 1.0, 0.0).astype(BF16)
        carry_ref[...] = jnp.zeros_like(carry_ref)

    h = _rms(x_ref[...], g_ref[...])
    logits = _split3_dot(h, wr_ref[...]) + br_ref[...]
    lane = lax.broadcasted_iota(I32, (tt, E), 1).astype(F32)
    lane_k = lax.broadcasted_iota(I32, (tt, TOP_K), 1)
    l = logits
    vals, sels = [], []
    e_out = jnp.zeros((tt, TOP_K), F32)
    for k in range(TOP_K):
        m = jnp.max(l, axis=-1, keepdims=True)
        idx = jnp.min(jnp.where(l == m, lane, float(E)), axis=-1, keepdims=True)
        sel = lane == idx
        vals.append(m)
        sels.append(sel)
        e_out = jnp.where(lane_k == k, idx, e_out)
        l = jnp.where(sel, -jnp.inf, l)
    ex = [jnp.exp(v - vals[0]) for v in vals]
    den = ex[0] + ex[1] + ex[2] + ex[3]
    multi = jnp.zeros((tt, E), F32)
    for sel in sels:
        multi = multi + jnp.where(sel, 1.0, 0.0)
    before = jnp.dot(tri_ref[...], multi.astype(BF16), preferred_element_type=F32)
    before = before + carry_ref[...]
    g_out = jnp.zeros((tt, TOP_K), F32)
    r_out = jnp.zeros((tt, TOP_K), F32)
    for k in range(TOP_K):
        g_out = jnp.where(lane_k == k, ex[k] / den, g_out)
        rk = jnp.sum(jnp.where(sels[k], before, 0.0), axis=-1, keepdims=True)
        r_out = jnp.where(lane_k == k, rk, r_out)
    e_ref[...] = e_out.astype(I32)
    gate_ref[...] = g_out
    rank_ref[...] = r_out.astype(I32)
    carry_ref[...] = carry_ref[...] + jnp.sum(multi, axis=0, keepdims=True)
    cnt_ref[...] = carry_ref[...]


def _router(x, g, w_r, b_r):
    N, D = x.shape
    tt = ROUTER_TILE
    E = N_EXPERTS
    return pl.pallas_call(
        functools.partial(_router_kernel, tt=tt),
        grid=(N // tt,),
        in_specs=[
            pl.BlockSpec((tt, D), lambda i: (i, 0)),
            pl.BlockSpec((1, D), lambda i: (0, 0)),
            pl.BlockSpec((D, E), lambda i: (0, 0)),
            pl.BlockSpec((1, E), lambda i: (0, 0)),
        ],
        out_specs=[
            pl.BlockSpec((tt, TOP_K), lambda i: (i, 0)),
            pl.BlockSpec((tt, TOP_K), lambda i: (i, 0)),
            pl.BlockSpec((tt, TOP_K), lambda i: (i, 0)),
            pl.BlockSpec((1, E), lambda i: (0, 0)),
        ],
        out_shape=[
            jax.ShapeDtypeStruct((N, TOP_K), I32),
            jax.ShapeDtypeStruct((N, TOP_K), F32),
            jax.ShapeDtypeStruct((N, TOP_K), I32),
            jax.ShapeDtypeStruct((1, E), F32),
        ],
        scratch_shapes=[pltpu.VMEM((tt, tt), BF16), pltpu.VMEM((1, E), F32)],
        compiler_params=_params(("arbitrary",)),
        name="moe_router",
    )(x, g.reshape(1, D), w_r, b_r.reshape(1, E))


def _row_copy_wait(src, dst, sem, rows, times):
    for _ in range(times):
        pltpu.make_async_copy(src, dst.at[pl.ds(0, rows)], sem).wait()


def _dispatch_kernel(pend_ref, nu_ref, dest_ref, x_ref, g_ref, xpad_ref, hbuf, zbuf, sems, zsem,
                     *, tt, tm, nt):
    i = pl.program_id(0)
    slot = i % 2

    @pl.when(i == 0)
    def _():
        zbuf[...] = jnp.zeros_like(zbuf)

        def seg_copy(e):
            start = pl.multiple_of(jnp.maximum(pend_ref[e] - tm, 0), tm)
            return pltpu.make_async_copy(zbuf, xpad_ref.at[pl.ds(start, tm)], zsem)

        def tail_copy(j):
            return pltpu.make_async_copy(zbuf, xpad_ref.at[pl.ds((nt - 1 - j) * tm, tm)], zsem)

        for e in range(N_EXPERTS):
            seg_copy(e).start()
        for j in range(N_EXPERTS):
            pl.when(nt - 1 - j >= nu_ref[0])(tail_copy(j).start)
        for e in range(N_EXPERTS):
            seg_copy(e).wait()
        for j in range(N_EXPERTS):
            pl.when(nt - 1 - j >= nu_ref[0])(tail_copy(j).wait)

    hb = hbuf.at[slot]
    hb[...] = _rms(x_ref[...], g_ref[...])

    def issue(t, carry):
        for k in range(TOP_K):
            d = dest_ref[t * TOP_K + k]
            pltpu.make_async_copy(hb.at[pl.ds(t, 1)], xpad_ref.at[pl.ds(d, 1)],
                                  sems.at[slot]).start(priority=k % 2)
        return carry

    lax.fori_loop(0, tt, issue, 0, unroll=8)

    @pl.when(i > 0)
    def _():
        _row_copy_wait(hbuf.at[1 - slot], xpad_ref, sems.at[1 - slot], tt, TOP_K)

    @pl.when(i == pl.num_programs(0) - 1)
    def _():
        _row_copy_wait(hb, xpad_ref, sems.at[slot], tt, TOP_K)


def _dispatch(x, g, dest, pend, n_used, P):
    N, D = x.shape
    tt = ROW_TILE
    tm = MOE_TILE
    grid_spec = pltpu.PrefetchScalarGridSpec(
        num_scalar_prefetch=2,
        grid=(N // tt,),
        in_specs=[
            pl.BlockSpec((tt * TOP_K,), lambda i, pe, nu: (i,), memory_space=pltpu.SMEM),
            pl.BlockSpec((tt, D), lambda i, pe, nu: (i, 0)),
            pl.BlockSpec((1, D), lambda i, pe, nu: (0, 0)),
        ],
        out_specs=pl.BlockSpec(memory_space=pl.ANY),
        scratch_shapes=[pltpu.VMEM((2, tt, D), F32), pltpu.VMEM((tm, D), F32),
                        pltpu.SemaphoreType.DMA((2,)), pltpu.SemaphoreType.DMA(())],
    )
    return pl.pallas_call(
        functools.partial(_dispatch_kernel, tt=tt, tm=tm, nt=P // tm),
        grid_spec=grid_spec,
        out_shape=jax.ShapeDtypeStruct((P, D), F32),
        compiler_params=_params(("arbitrary",)),
        name="moe_dispatch",
    )(pend, n_used, dest, x, g.reshape(1, D))


def _expert_kernel(te_ref, nu_ref, x_ref, wgu_ref, bgu_ref, wdn_ref, bdn_ref, o_ref,
                   wgu_bf, wdn_bf, *, F):
    i = pl.program_id(0)

    @pl.when(i < nu_ref[0])
    def _():
        prev = te_ref[jnp.maximum(i - 1, 0)]

        @pl.when((i == 0) | (te_ref[i] != prev))
        def _():
            wgu_bf[...] = wgu_ref[0].astype(BF16)
            wdn_bf[...] = wdn_ref[0].astype(BF16)

        x = x_ref[...].astype(BF16)
        gu = jnp.dot(x, wgu_bf[...], preferred_element_type=F32) + bgu_ref[0]
        gate = jnp.minimum(gu[:, :F], SWIGLU_LIMIT)
        up = jnp.clip(gu[:, F:], -SWIGLU_LIMIT, SWIGLU_LIMIT)
        glu = gate * jax.nn.sigmoid(SWIGLU_ALPHA * gate)
        a = ((up + 1.0) * glu).astype(BF16)
        o_ref[...] = jnp.dot(a, wdn_bf[...], preferred_element_type=F32) + bdn_ref[0]

    @pl.when(i >= nu_ref[0])
    def _():
        o_ref[...] = jnp.zeros_like(o_ref)


def _experts(tile_e, n_used, x_pad, w_gu, b_gu, w_dn, b_dn, layer):
    P, D = x_pad.shape
    _, E, _, F2 = w_gu.shape
    F = F2 // 2
    tm = MOE_TILE
    row = lambda i, te, nu: (jnp.minimum(i, nu[0] - 1), 0)
    exp4 = lambda i, te, nu: (layer, te[i], 0, 0)
    exp3 = lambda i, te, nu: (te[i], 0, 0)
    grid_spec = pltpu.PrefetchScalarGridSpec(
        num_scalar_prefetch=2,
        grid=(P // tm,),
        in_specs=[
            pl.BlockSpec((tm, D), row),
            pl.BlockSpec((None, 1, D, F2), exp4),
            pl.BlockSpec((1, 1, F2), exp3),
            pl.BlockSpec((None, 1, F, D), exp4),
            pl.BlockSpec((1, 1, D), exp3),
        ],
        out_specs=pl.BlockSpec((tm, D), lambda i, te, nu: (i, 0)),
        scratch_shapes=[pltpu.VMEM((D, F2), BF16), pltpu.VMEM((F, D), BF16)],
    )
    return pl.pallas_call(
        functools.partial(_expert_kernel, F=F),
        grid_spec=grid_spec,
        out_shape=jax.ShapeDtypeStruct((P, D), F32),
        compiler_params=_params(("arbitrary",)),
        name="moe_experts",
    )(tile_e, n_used, x_pad, w_gu, b_gu.reshape(E, 1, F2), w_dn, b_dn.reshape(E, 1, D))


def _combine_kernel(dest_ref, dnext_ref, x_ref, gate_ref, ypad_ref, o_ref, ybuf, sems, *, tt):
    i = pl.program_id(0)
    slot = i % 2

    def gather(idx_ref, s):
        def issue(t, carry):
            for k in range(TOP_K):
                d = idx_ref[t * TOP_K + k]
                pltpu.make_async_copy(ypad_ref.at[pl.ds(d, 1)], ybuf.at[s, k, pl.ds(t, 1)],
                                      sems.at[s]).start(priority=k % 2)
            return carry

        lax.fori_loop(0, tt, issue, 0, unroll=8)

    @pl.when(i == 0)
    def _():
        gather(dest_ref, slot)

    @pl.when(i + 1 < pl.num_programs(0))
    def _():
        gather(dnext_ref, 1 - slot)

    for k in range(TOP_K):
        pltpu.make_async_copy(ypad_ref.at[pl.ds(0, tt)], ybuf.at[slot, k], sems.at[slot]).wait()
    gate = gate_ref[...]
    acc = x_ref[...]
    for k in range(TOP_K):
        acc = acc + ybuf[slot, k] * gate[:, k:k + 1]
    o_ref[...] = acc


def _combine(x, gate, dest, y_pad):
    N, D = x.shape
    tt = ROW_TILE
    last = N // tt - 1
    return pl.pallas_call(
        functools.partial(_combine_kernel, tt=tt),
        grid=(N // tt,),
        in_specs=[
            pl.BlockSpec((tt * TOP_K,), lambda i: (i,), memory_space=pltpu.SMEM),
            pl.BlockSpec((tt * TOP_K,), lambda i: (jnp.minimum(i + 1, last),), memory_space=pltpu.SMEM),
            pl.BlockSpec((tt, D), lambda i: (i, 0)),
            pl.BlockSpec((tt, TOP_K), lambda i: (i, 0)),
            pl.BlockSpec(memory_space=pl.ANY),
        ],
        out_specs=pl.BlockSpec((tt, D), lambda i: (i, 0)),
        out_shape=jax.ShapeDtypeStruct((N, D), F32),
        scratch_shapes=[pltpu.VMEM((2, TOP_K, tt, D), F32), pltpu.SemaphoreType.DMA((2,))],
        compiler_params=_params(("arbitrary",)),
        name="moe_combine",
    )(dest, dest, x, gate, y_pad)


def _moe_layer(x, g, w_r, b_r, w_gu, b_gu, w_dn, b_dn, layer):
    N, D = x.shape
    E = N_EXPERTS
    tm = MOE_TILE
    e_idx, gate, rank, counts = _router(x, g, w_r, b_r)
    counts = counts[0].astype(I32)
    padded = ((counts + tm - 1) // tm) * tm
    pend = jnp.cumsum(padded)
    pstart = pend - padded
    dest = (pstart[e_idx] + rank).reshape(-1)
    P = ((N * TOP_K + E * (tm - 1) + tm - 1) // tm) * tm
    nt = P // tm
    tiles = jnp.arange(nt, dtype=I32)
    tile_e = jnp.minimum(jnp.sum(tiles[:, None] * tm >= pend[None, :], axis=1), E - 1).astype(I32)
    n_used = (pend[-1] // tm).astype(I32)
    tile_e = jnp.where(tiles < n_used, tile_e, tile_e[n_used - 1])
    x_pad = _dispatch(x, g, dest, pend.astype(I32), n_used.reshape(1), P)
    y_pad = _experts(tile_e, n_used.reshape(1), x_pad, w_gu, b_gu, w_dn, b_dn, layer)
    return _combine(x, gate, dest, y_pad)


def _bucket_maps():
    qi = np.arange(ATT_BLOCK)[None, :]
    kj = np.arange(2 * ATT_BLOCK)[:, None]
    delta = qi + ATT_BLOCK - kj
    buckets, valids = [], []
    max_exact = N_BUCKETS // 2
    for win, dil in DILATED_GROUPS:
        n = np.maximum(delta * dil, 0)
        nf = np.maximum(n, 1).astype(np.float32)
        large = max_exact + (np.log(nf / np.float32(max_exact)) / np.float32(math.log(REL_MAX_DIST / max_exact))
                             * np.float32(N_BUCKETS - max_exact)).astype(np.int32)
        large = np.minimum(large, N_BUCKETS - 1)
        buckets.append(np.where(n < max_exact, n, large).astype(np.int32))
        valids.append(((delta >= 0) & (delta <= win // dil)).astype(np.int32))
    return np.stack(buckets), np.stack(valids)


def _bias_kernel(tab_ref, bm_ref, valid_ref, o_ref):
    gh = pl.program_id(0)
    bm = bm_ref[0]
    acc = jnp.zeros(bm.shape, F32)
    for b in range(N_BUCKETS):
        acc = jnp.where(bm == b, tab_ref[b, gh], acc)
    band = jnp.where(valid_ref[0] > 0, acc, NEG_INF)
    key = lax.broadcasted_iota(I32, bm.shape, 0)
    o_ref[0, 0] = band
    o_ref[0, 1] = jnp.where(key >= ATT_BLOCK, band, NEG_INF)


def _bias_tables(rel_bias):
    bm, valid = _bucket_maps()
    GH = len(DILATED_GROUPS) * ATT_HEADS
    blk = (1, 2 * ATT_BLOCK, ATT_BLOCK)
    return pl.pallas_call(
        _bias_kernel,
        grid=(GH,),
        in_specs=[
            pl.BlockSpec(memory_space=pltpu.SMEM),
            pl.BlockSpec(blk, lambda i: (i // ATT_HEADS, 0, 0)),
            pl.BlockSpec(blk, lambda i: (i // ATT_HEADS, 0, 0)),
        ],
        out_specs=pl.BlockSpec((1, 2, 2 * ATT_BLOCK, ATT_BLOCK), lambda i: (i, 0, 0, 0)),
        out_shape=jax.ShapeDtypeStruct((GH, 2, 2 * ATT_BLOCK, ATT_BLOCK), F32),
        compiler_params=_params(("arbitrary",)),
        name="attn_bias",
    )(rel_bias, jnp.asarray(bm), jnp.asarray(valid))


def _cast_kernel(w_ref, o_ref):
    o_ref[...] = w_ref[...].astype(BF16)


def _to_bf16(w, layer, col_block):
    _, rows, cols = w.shape
    return pl.pallas_call(
        _cast_kernel,
        grid=(cols // col_block,),
        in_specs=[pl.BlockSpec((None, rows, col_block), lambda j: (layer, 0, j))],
        out_specs=pl.BlockSpec((rows, col_block), lambda j: (0, j)),
        out_shape=jax.ShapeDtypeStruct((rows, cols), BF16),
        compiler_params=_params(("parallel",)),
        name="cast_bf16",
    )(w)


def _qkv_kernel(x_ref, g_ref, wq_ref, wk_ref, wv_ref, qg_ref, kg_ref, qt_ref, k_ref, vt_ref,
                slab_ref, bd_ref, *, dil, R, nl):
    rc = pl.program_id(2)
    D = x_ref.shape[2]
    lanes = slab_ref.shape[2]
    first = (pl.program_id(0) == 0) & (pl.program_id(1) == 0) & (rc == 0)

    @pl.when(first)
    def _():
        r = lax.broadcasted_iota(I32, bd_ref.shape, 0)
        c = lax.broadcasted_iota(I32, bd_ref.shape, 1)
        hd_shift = HEAD_DIM.bit_length() - 1
        same_head = lax.shift_right_logical(r, hd_shift) == lax.shift_right_logical(c, hd_shift)
        bd_ref[...] = jnp.where(same_head, 1.0, 0.0).astype(BF16)

    if dil == 1:
        x = x_ref[0]
    else:
        @pl.when(rc == 0)
        def _():
            for c in range(D // lanes):
                slab_ref[c] = x_ref[0, :, c * lanes:(c + 1) * lanes]

        pieces = []
        for j in range(R):
            r = rc * R + j
            cols = [slab_ref[c, pl.ds(r, nl, stride=dil), :] for c in range(D // lanes)]
            pieces.append(jnp.concatenate(cols, axis=1))
        x = jnp.concatenate(pieces, axis=0)
    h = _rms(x, g_ref[...]).astype(BF16)
    nb = bd_ref.shape[0]

    def head_norm(y, gain):
        parts = []
        for c0 in range(0, y.shape[1], nb):
            yc = y[:, c0:c0 + nb]
            ssq = jnp.dot((yc * yc).astype(BF16), bd_ref[...], preferred_element_type=F32)
            parts.append(yc * lax.rsqrt(ssq * (1.0 / HEAD_DIM) + EPS))
        return jnp.concatenate(parts, axis=1) * gain

    q = head_norm(jnp.dot(h, wq_ref[...], preferred_element_type=F32), qg_ref[...])
    q = q * (HEAD_DIM ** -0.5)
    k = head_norm(jnp.dot(h, wk_ref[...], preferred_element_type=F32), kg_ref[...])
    v = jnp.dot(h, wv_ref[...], preferred_element_type=F32)
    blk = ATT_BLOCK
    for j in range(R):
        k_ref[0, j] = k[j * nl:(j + 1) * nl].astype(BF16)
        for c in range(nl // blk):
            rows = slice(j * nl + c * blk, j * nl + (c + 1) * blk)
            for hp in range(q.shape[1] // blk):
                cs = slice(hp * blk, (hp + 1) * blk)
                qt_ref[0, j, cs, c * blk:(c + 1) * blk] = q[rows, cs].T.astype(BF16)
                vt_ref[0, j, cs, c * blk:(c + 1) * blk] = v[rows, cs].T.astype(BF16)


def _qkv_group(x, g, w_bf, q_gain, k_gain, gi, dil):
    B, S, D = x.shape
    HD = ATT_HEADS * HEAD_DIM
    G = len(DILATED_GROUPS)
    L = S // dil
    R = min(dil, QKV_ROWS // ATT_BLOCK)
    nl = QKV_ROWS // R
    lanes = 128
    wspec = lambda s: pl.BlockSpec((D, HD), lambda b, l, r: (0, s * G + gi))
    tspec = pl.BlockSpec((1, R, HD, nl), lambda b, l, r: (b, r, 0, l))
    tshape = jax.ShapeDtypeStruct((B, dil, HD, L), BF16)
    return pl.pallas_call(
        functools.partial(_qkv_kernel, dil=dil, R=R, nl=nl),
        grid=(B, L // nl, dil // R),
        in_specs=[
            pl.BlockSpec((1, nl * dil, D), lambda b, l, r: (b, l, 0)),
            pl.BlockSpec((1, D), lambda b, l, r: (0, 0)),
            wspec(0), wspec(1), wspec(2),
            pl.BlockSpec((1, HD), lambda b, l, r: (0, 0)),
            pl.BlockSpec((1, HD), lambda b, l, r: (0, 0)),
        ],
        out_specs=[tspec, pl.BlockSpec((1, R, nl, HD), lambda b, l, r: (b, r, l, 0)), tspec],
        out_shape=[tshape, jax.ShapeDtypeStruct((B, dil, L, HD), BF16), tshape],
        scratch_shapes=[pltpu.VMEM((D // lanes, nl * dil if dil > 1 else 8, lanes), F32),
                        pltpu.VMEM((256, 256), BF16)],
        compiler_params=_params(("arbitrary", "arbitrary", "arbitrary")),
        name=f"attn_qkv_g{gi}",
    )(x, g.reshape(1, D), w_bf, w_bf, w_bf,
      jnp.tile(q_gain, ATT_HEADS).reshape(1, HD), jnp.tile(k_gain, ATT_HEADS).reshape(1, HD))


def _attn_kernel(qt_ref, kc_ref, kp_ref, vtc_ref, vtp_ref, bias_ref, o_ref, l_ref, s_ref, p_ref):
    n = pl.program_id(2)
    first = jnp.where(n == 0, 1, 0)
    blk = ATT_BLOCK
    pair = 2 * HEAD_DIM
    nk = s_ref.shape[1]
    k0 = 2 * blk - nk
    feat = lax.broadcasted_iota(I32, (pair, blk), 0)
    head_rows = [jnp.where(feat < HEAD_DIM, 1.0, 0.0).astype(BF16),
                 jnp.where(feat < HEAD_DIM, 0.0, 1.0).astype(BF16)]
    for pr in range(ATT_HEADS // 2):
        cs = slice(pr * pair, (pr + 1) * pair)
        qt2 = qt_ref[0, 0, cs, :]
        if k0:
            k2 = kc_ref[0, 0, :, cs]
        else:
            k2 = jnp.concatenate([kp_ref[0, 0, :, cs], kc_ref[0, 0, :, cs]], axis=0)
        for half in range(2):
            qth = qt2 * head_rows[half]
            s_ref[2 * pr + half] = jnp.dot(k2, qth, preferred_element_type=F32)
    lse_row = lax.broadcasted_iota(I32, (blk, blk), 0)
    lse_t = jnp.zeros((blk, blk), F32)
    invs = []
    for head in range(ATT_HEADS):
        s = s_ref[head] + bias_ref[head, first, k0:, :]
        m = jnp.max(s, axis=0, keepdims=True)
        p = jnp.exp(s - m)
        den = jnp.sum(p, axis=0, keepdims=True)
        p_ref[head] = p.astype(BF16)
        invs.append(1.0 / den)
        lse_t = jnp.where(lse_row == head, m + jnp.log(den), lse_t)
    for pr in range(ATT_HEADS // 2):
        cs = slice(pr * pair, (pr + 1) * pair)
        if k0:
            vt2 = vtc_ref[0, 0, cs, :]
        else:
            vt2 = jnp.concatenate([vtp_ref[0, 0, cs, :], vtc_ref[0, 0, cs, :]], axis=1)
        halves = []
        for half in range(2):
            head = 2 * pr + half
            vth = vt2[half * HEAD_DIM:(half + 1) * HEAD_DIM]
            ot = jnp.dot(vth, p_ref[head], preferred_element_type=F32)
            halves.append(ot * invs[head])
        o_ref[0, 0, :, cs] = jnp.concatenate(halves, axis=0).T
    l_ref[0, 0] = lse_t.T


def _attn_group(qt, k, vt, bias, gi):
    B, dil, L, HD = k.shape
    blk = ATT_BLOCK
    nb = L // blk
    cur = pl.BlockSpec((1, 1, blk, HD), lambda b, r, n: (b, r, n, 0))
    prev = pl.BlockSpec((1, 1, blk, HD), lambda b, r, n: (b, r, jnp.maximum(n - 1, 0), 0))
    tcur = pl.BlockSpec((1, 1, HD, blk), lambda b, r, n: (b, r, 0, n))
    tprev = pl.BlockSpec((1, 1, HD, blk), lambda b, r, n: (b, r, 0, jnp.maximum(n - 1, 0)))
    return pl.pallas_call(
        _attn_kernel,
        grid=(B, dil, nb),
        in_specs=[tcur, cur, prev, tcur, tprev,
                  pl.BlockSpec((ATT_HEADS, 2, 2 * blk, blk), lambda b, r, n: (gi, 0, 0, 0))],
        out_specs=[cur, pl.BlockSpec((1, 1, blk, blk), lambda b, r, n: (b, r, n, 0))],
        out_shape=[jax.ShapeDtypeStruct((B, dil, L, HD), F32),
                   jax.ShapeDtypeStruct((B, dil, L, blk), F32)],
        scratch_shapes=[pltpu.VMEM((ATT_HEADS, blk if nb == 1 else 2 * blk, blk), F32),
                        pltpu.VMEM((ATT_HEADS, blk if nb == 1 else 2 * blk, blk), BF16)],
        compiler_params=_params(("parallel", "parallel", "arbitrary")),
        name=f"attn_core_g{gi}",
    )(qt, k, k, vt, vt, bias)


def _attn_out_kernel(x_ref, o0, o1, o2, l0, l1, l2, wo_ref, out_ref, oslab, lslab, ex_ref, *, T):
    lanes = oslab.shape[3]
    HD = o0.shape[3]

    @pl.when((pl.program_id(0) == 0) & (pl.program_id(1) == 0))
    def _():
        h = lax.broadcasted_iota(I32, ex_ref.shape, 0)
        c = lax.broadcasted_iota(I32, ex_ref.shape, 1)
        owner = lax.shift_right_logical(c, HEAD_DIM.bit_length() - 1)
        ex_ref[...] = jnp.where(h == owner, 1.0, 0.0).astype(BF16)

    def token_order(o_ref, l_ref, dil, s):
        if dil == 1:
            return o_ref[0, 0], l_ref[0, 0]
        n = T // dil
        for r in range(dil):
            blk = o_ref[0, r]
            for c in range(HD // lanes):
                oslab[s, c, pl.ds(r, n, stride=dil), :] = blk[:, c * lanes:(c + 1) * lanes]
            lslab[s, pl.ds(r, n, stride=dil), :] = l_ref[0, r]
        o = jnp.concatenate([oslab[s, c] for c in range(HD // lanes)], axis=1)
        return o, lslab[s]

    groups = [token_order(o_ref, l_ref, dil, s)
              for s, ((_, dil), o_ref, l_ref) in enumerate(zip(DILATED_GROUPS, (o0, o1, o2), (l0, l1, l2)))]
    m = jnp.maximum(jnp.maximum(groups[0][1], groups[1][1]), groups[2][1])
    es = [jnp.exp(l - m) for _, l in groups]
    inv = 1.0 / (es[0] + es[1] + es[2])
    att = jnp.zeros((T, HD), F32)
    for (o, _), e in zip(groups, es):
        a = e * inv
        a_hi = a.astype(BF16)
        a_lo = (a - a_hi.astype(F32)).astype(BF16)
        wide = (jnp.dot(a_hi, ex_ref[...], preferred_element_type=F32)
                + jnp.dot(a_lo, ex_ref[...], preferred_element_type=F32))
        att = att + wide * o
    out_ref[0] = x_ref[0] + jnp.dot(att.astype(BF16), wo_ref[...], preferred_element_type=F32)


def _attn_out(x, os, ls, wo_bf):
    B, S, D = x.shape
    HD = wo_bf.shape[0]
    T = OUT_TILE
    lanes = 128
    ospec = lambda dil, w: pl.BlockSpec((1, dil, T // dil, w), lambda b, i: (b, 0, i, 0))
    dils = [dil for _, dil in DILATED_GROUPS]
    xspec = pl.BlockSpec((1, T, D), lambda b, i: (b, i, 0))
    return pl.pallas_call(
        functools.partial(_attn_out_kernel, T=T),
        grid=(B, S // T),
        in_specs=[xspec] + [ospec(d, HD) for d in dils] + [ospec(d, ATT_BLOCK) for d in dils]
                 + [pl.BlockSpec((HD, D), lambda b, i: (0, 0))],
        out_specs=xspec,
        out_shape=jax.ShapeDtypeStruct((B, S, D), F32),
        scratch_shapes=[pltpu.VMEM((len(dils), HD // lanes, T, lanes), F32),
                        pltpu.VMEM((len(dils), T, ATT_BLOCK), F32),
                        pltpu.VMEM((ATT_BLOCK, HD), BF16)],
        compiler_params=_params(("arbitrary", "arbitrary")),
        name="attn_out",
    )(x, *os, *ls, wo_bf)


def _attention_layer(x, g, w_qkv, q_gain, k_gain, w_o, rel_bias, layer):
    HD = ATT_HEADS * HEAD_DIM
    bias = _bias_tables(rel_bias)
    wqkv_bf = _to_bf16(w_qkv, layer, HD)
    wo_bf = _to_bf16(w_o, layer, w_o.shape[2])
    os, ls = [], []
    for gi, (_, dil) in enumerate(DILATED_GROUPS):
        qt, k, vt = _qkv_group(x, g, wqkv_bf, q_gain, k_gain, gi, dil)
        o, l = _attn_group(qt, k, vt, bias, gi)
        os.append(o)
        ls.append(l)
    return _attn_out(x, os, ls, wo_bf)


def kernel(x, norm_mix_g, norm_ffn_g, pool_w, pool_scale, attn_w_qkv, attn_q_gain, attn_k_gain,
           attn_w_o, rel_bias, moe_w_router, moe_b_router, moe_w_gate_up, moe_b_gate_up,
           moe_w_down, moe_b_down):
    B, S, D = x.shape
    depth = norm_mix_g.shape[0]
    for i in range(depth):
        j = i // 2
        if i % 2 == 0:
            x = _pool_layer(x, norm_mix_g[i], pool_w[j], pool_scale[j])
        else:
            x = _attention_layer(x, norm_mix_g[i], attn_w_qkv, attn_q_gain[j], attn_k_gain[j],
                                 attn_w_o, rel_bias, j)
        x = _moe_layer(x.reshape(B * S, D), norm_ffn_g[i], moe_w_router[i], moe_b_router[i],
                       moe_w_gate_up, moe_b_gate_up[i], moe_w_down, moe_b_down[i],
                       i).reshape(B, S, D)
    return x
```

```python
import functools
import math

import numpy as np
import jax
import jax.numpy as jnp
from jax import lax
from jax.experimental import pallas as pl
from jax.experimental.pallas import tpu as pltpu

F32 = jnp.float32
BF16 = jnp.bfloat16
I32 = jnp.int32

EPS = 1e-6
POOL_WINDOWS = (2, 4, 8, 16)
POOL_HALO = 16
DILATED_GROUPS = ((128, 1), (512, 4), (2048, 16))
ATT_HEADS = 16
HEAD_DIM = 64
ATT_BLOCK = 128
N_BUCKETS = 32
REL_MAX_DIST = 2048
NEG_INF = -1e30
N_EXPERTS = 32
TOP_K = 4
SWIGLU_LIMIT = 7.0
SWIGLU_ALPHA = 1.702

V7X_VMEM_LIMIT_BYTES = 56 * 1024 * 1024
MOE_TILE = 512
ROW_TILE = 256
ROUTER_TILE = 512
POOL_TILE = 512
QKV_ROWS = 512
OUT_TILE = 256


def _rms(xf, g):
    ms = jnp.mean(xf * xf, axis=-1, keepdims=True)
    return xf * lax.rsqrt(ms + EPS) * g


def _params(sem, vmem=None):
    return pltpu.CompilerParams(dimension_semantics=sem,
                                vmem_limit_bytes=vmem or V7X_VMEM_LIMIT_BYTES)


def _pool_kernel(x_ref, halo_ref, g_ref, w_ref, sc_ref, o_ref, *, ts, dg):
    i = pl.program_id(1)
    x = x_ref[0]
    g = g_ref[...]
    h = _rms(x, g)
    hh = _rms(halo_ref[0], g)
    hh = jnp.where(i == 0, 0.0, hh)
    full = jnp.concatenate([hh, h], axis=0)
    pos = i * ts + lax.broadcasted_iota(I32, (ts, 1), 0)
    outs = []
    for gi, w in enumerate(POOL_WINDOWS):
        s = full[:, gi * dg:(gi + 1) * dg]
        sh = 1
        while sh < w:
            s = s + pltpu.roll(s, sh, 0)
            sh *= 2
        s = s[POOL_HALO:]
        cnt = jnp.minimum(pos + 1, w).astype(F32)
        p = s / cnt - h[:, gi * dg:(gi + 1) * dg]
        outs.append(jnp.dot(p.astype(BF16), w_ref[gi].astype(BF16),
                            preferred_element_type=F32))
    y = jnp.concatenate(outs, axis=1) * sc_ref[...]
    o_ref[0] = x + y


def _pool_layer(x, g, w_groups, scale):
    B, S, D = x.shape
    ts = POOL_TILE
    dg = D // len(POOL_WINDOWS)
    hb = ts // POOL_HALO
    return pl.pallas_call(
        functools.partial(_pool_kernel, ts=ts, dg=dg),
        grid=(B, S // ts),
        in_specs=[
            pl.BlockSpec((1, ts, D), lambda b, i: (b, i, 0)),
            pl.BlockSpec((1, POOL_HALO, D), lambda b, i: (b, jnp.maximum(i * hb - 1, 0), 0)),
            pl.BlockSpec((1, D), lambda b, i: (0, 0)),
            pl.BlockSpec((len(POOL_WINDOWS), dg, dg), lambda b, i: (0, 0, 0)),
            pl.BlockSpec((1, D), lambda b, i: (0, 0)),
        ],
        out_specs=pl.BlockSpec((1, ts, D), lambda b, i: (b, i, 0)),
        out_shape=jax.ShapeDtypeStruct((B, S, D), F32),
        compiler_params=_params(("parallel", "parallel")),
        name="pool_mixer",
    )(x, x, g.reshape(1, D), w_groups, scale.reshape(1, D))


def _split3_dot(a, b):
    a_hi = a.astype(BF16)
    a_lo = (a - a_hi.astype(F32)).astype(BF16)
    b_hi = b.astype(BF16)
    b_lo = (b - b_hi.astype(F32)).astype(BF16)
    d = functools.partial(jnp.dot, preferred_element_type=F32)
    return d(a_hi, b_hi) + (d(a_hi, b_lo) + d(a_lo, b_hi))


def _router_kernel(x_ref, g_ref, wr_ref, br_ref, e_ref, gate_ref, rank_ref, cnt_ref,
                   tri_ref, carry_ref, *, tt):
    i = pl.program_id(0)
    E = N_EXPERTS

    @pl.when(i == 0)
    def _():
        r = lax.broadcasted_iota(I32, (tt, tt), 0)
        c = lax.broadcasted_iota(I32, (tt, tt), 1)
        tri_ref[...] = jnp.where(c < r, 1.0, 0.0).astype(BF16)
        carry_ref[...] = jnp.zeros_like(carry_ref)

    h = _rms(x_ref[...], g_ref[...])
    logits = _split3_dot(h, wr_ref[...]) + br_ref[...]
    lane = lax.broadcasted_iota(I32, (tt, E), 1).astype(F32)
    lane_k = lax.broadcasted_iota(I32, (tt, TOP_K), 1)
    l = logits
    vals, sels = [], []
    e_out = jnp.zeros((tt, TOP_K), F32)
    for k in range(TOP_K):
        m = jnp.max(l, axis=-1, keepdims=True)
        idx = jnp.min(jnp.where(l == m, lane, float(E)), axis=-1, keepdims=True)
        sel = lane == idx
        vals.append(m)
        sels.append(sel)
        e_out = jnp.where(lane_k == k, idx, e_out)
        l = jnp.where(sel, -jnp.inf, l)
    ex = [jnp.exp(v - vals[0]) for v in vals]
    den = ex[0] + ex[1] + ex[2] + ex[3]
    multi = jnp.zeros((tt, E), F32)
    for sel in sels:
        multi = multi + jnp.where(sel, 1.0, 0.0)
    before = jnp.dot(tri_ref[...], multi.astype(BF16), preferred_element_type=F32)
    before = before + carry_ref[...]
    g_out = jnp.zeros((tt, TOP_K), F32)
    r_out = jnp.zeros((tt, TOP_K), F32)
    for k in range(TOP_K):
        g_out = jnp.where(lane_k == k, ex[k] / den, g_out)
        rk = jnp.sum(jnp.where(sels[k], before, 0.0), axis=-1, keepdims=True)
        r_out = jnp.where(lane_k == k, rk, r_out)
    e_ref[...] = e_out.astype(I32)
    gate_ref[...] = g_out
    rank_ref[...] = r_out.astype(I32)
    carry_ref[...] = carry_ref[...] + jnp.sum(multi, axis=0, keepdims=True)
    cnt_ref[...] = carry_ref[...]


def _router(x, g, w_r, b_r):
    N, D = x.shape
    tt = ROUTER_TILE
    E = N_EXPERTS
    return pl.pallas_call(
        functools.partial(_router_kernel, tt=tt),
        grid=(N // tt,),
        in_specs=[
            pl.BlockSpec((tt, D), lambda i: (i, 0)),
            pl.BlockSpec((1, D), lambda i: (0, 0)),
            pl.BlockSpec((D, E), lambda i: (0, 0)),
            pl.BlockSpec((1, E), lambda i: (0, 0)),
        ],
        out_specs=[
            pl.BlockSpec((tt, TOP_K), lambda i: (i, 0)),
            pl.BlockSpec((tt, TOP_K), lambda i: (i, 0)),
            pl.BlockSpec((tt, TOP_K), lambda i: (i, 0)),
            pl.BlockSpec((1, E), lambda i: (0, 0)),
        ],
        out_shape=[
            jax.ShapeDtypeStruct((N, TOP_K), I32),
            jax.ShapeDtypeStruct((N, TOP_K), F32),
            jax.ShapeDtypeStruct((N, TOP_K), I32),
            jax.ShapeDtypeStruct((1, E), F32),
        ],
        scratch_shapes=[pltpu.VMEM((tt, tt), BF16), pltpu.VMEM((1, E), F32)],
        compiler_params=_params(("arbitrary",)),
        name="moe_router",
    )(x, g.reshape(1, D), w_r, b_r.reshape(1, E))


def _row_copy_wait(src, dst, sem, rows, times):
    for _ in range(times):
        pltpu.make_async_copy(src, dst.at[pl.ds(0, rows)], sem).wait()


def _dispatch_kernel(pend_ref, nu_ref, dest_ref, x_ref, g_ref, xpad_ref, hbuf, zbuf, sems, zsem,
                     *, tt, tm, nt):
    i = pl.program_id(0)
    slot = i % 2

    @pl.when(i == 0)
    def _():
        zbuf[...] = jnp.zeros_like(zbuf)

        def seg_copy(e):
            start = pl.multiple_of(jnp.maximum(pend_ref[e] - tm, 0), tm)
            return pltpu.make_async_copy(zbuf, xpad_ref.at[pl.ds(start, tm)], zsem)

        def tail_copy(j):
            return pltpu.make_async_copy(zbuf, xpad_ref.at[pl.ds((nt - 1 - j) * tm, tm)], zsem)

        for e in range(N_EXPERTS):
            seg_copy(e).start()
        for j in range(N_EXPERTS):
            pl.when(nt - 1 - j >= nu_ref[0])(tail_copy(j).start)
        for e in range(N_EXPERTS):
            seg_copy(e).wait()
        for j in range(N_EXPERTS):
            pl.when(nt - 1 - j >= nu_ref[0])(tail_copy(j).wait)

    hb = hbuf.at[slot]
    hb[...] = _rms(x_ref[...], g_ref[...])

    def issue(t, carry):
        for k in range(TOP_K):
            d = dest_ref[t * TOP_K + k]
            pltpu.make_async_copy(hb.at[pl.ds(t, 1)], xpad_ref.at[pl.ds(d, 1)],
                                  sems.at[slot]).start(priority=k % 2)
        return carry

    lax.fori_loop(0, tt, issue, 0, unroll=8)

    @pl.when(i > 0)
    def _():
        _row_copy_wait(hbuf.at[1 - slot], xpad_ref, sems.at[1 - slot], tt, TOP_K)

    @pl.when(i == pl.num_programs(0) - 1)
    def _():
        _row_copy_wait(hb, xpad_ref, sems.at[slot], tt, TOP_K)


def _dispatch(x, g, dest, pend, n_used, P):
    N, D = x.shape
    tt = ROW_TILE
    tm = MOE_TILE
    grid_spec = pltpu.PrefetchScalarGridSpec(
        num_scalar_prefetch=2,
        grid=(N // tt,),
        in_specs=[
            pl.BlockSpec((tt * TOP_K,), lambda i, pe, nu: (i,), memory_space=pltpu.SMEM),
            pl.BlockSpec((tt, D), lambda i, pe, nu: (i, 0)),
            pl.BlockSpec((1, D), lambda i, pe, nu: (0, 0)),
        ],
        out_specs=pl.BlockSpec(memory_space=pl.ANY),
        scratch_shapes=[pltpu.VMEM((2, tt, D), F32), pltpu.VMEM((tm, D), F32),
                        pltpu.SemaphoreType.DMA((2,)), pltpu.SemaphoreType.DMA(())],
    )
    return pl.pallas_call(
        functools.partial(_dispatch_kernel, tt=tt, tm=tm, nt=P // tm),
        grid_spec=grid_spec,
        out_shape=jax.ShapeDtypeStruct((P, D), F32),
        compiler_params=_params(("arbitrary",)),
        name="moe_dispatch",
    )(pend, n_used, dest, x, g.reshape(1, D))


def _expert_kernel(te_ref, nu_ref, x_ref, wgu_ref, bgu_ref, wdn_ref, bdn_ref, o_ref,
                   wgu_bf, wdn_bf, *, F):
    i = pl.program_id(0)

    @pl.when(i < nu_ref[0])
    def _():
        prev = te_ref[jnp.maximum(i - 1, 0)]

        @pl.when((i == 0) | (te_ref[i] != prev))
        def _():
            wgu_bf[...] = wgu_ref[0].astype(BF16)
            wdn_bf[...] = wdn_ref[0].astype(BF16)

        x = x_ref[...].astype(BF16)
        gu = jnp.dot(x, wgu_bf[...], preferred_element_type=F32) + bgu_ref[0]
        gate = jnp.minimum(gu[:, :F], SWIGLU_LIMIT)
        up = jnp.clip(gu[:, F:], -SWIGLU_LIMIT, SWIGLU_LIMIT)
        glu = gate * jax.nn.sigmoid(SWIGLU_ALPHA * gate)
        a = ((up + 1.0) * glu).astype(BF16)
        o_ref[...] = jnp.dot(a, wdn_bf[...], preferred_element_type=F32) + bdn_ref[0]

    @pl.when(i >= nu_ref[0])
    def _():
        o_ref[...] = jnp.zeros_like(o_ref)


def _experts(tile_e, n_used, x_pad, w_gu, b_gu, w_dn, b_dn, layer):
    P, D = x_pad.shape
    _, E, _, F2 = w_gu.shape
    F = F2 // 2
    tm = MOE_TILE
    row = lambda i, te, nu: (jnp.minimum(i, nu[0] - 1), 0)
    exp4 = lambda i, te, nu: (layer, te[i], 0, 0)
    exp3 = lambda i, te, nu: (te[i], 0, 0)
    grid_spec = pltpu.PrefetchScalarGridSpec(
        num_scalar_prefetch=2,
        grid=(P // tm,),
        in_specs=[
            pl.BlockSpec((tm, D), row),
            pl.BlockSpec((None, 1, D, F2), exp4),
            pl.BlockSpec((1, 1, F2), exp3),
            pl.BlockSpec((None, 1, F, D), exp4),
            pl.BlockSpec((1, 1, D), exp3),
        ],
        out_specs=pl.BlockSpec((tm, D), lambda i, te, nu: (i, 0)),
        scratch_shapes=[pltpu.VMEM((D, F2), BF16), pltpu.VMEM((F, D), BF16)],
    )
    return pl.pallas_call(
        functools.partial(_expert_kernel, F=F),
        grid_spec=grid_spec,
        out_shape=jax.ShapeDtypeStruct((P, D), F32),
        compiler_params=_params(("arbitrary",)),
        name="moe_experts",
    )(tile_e, n_used, x_pad, w_gu, b_gu.reshape(E, 1, F2), w_dn, b_dn.reshape(E, 1, D))


def _combine_kernel(dest_ref, dnext_ref, x_ref, gate_ref, ypad_ref, o_ref, ybuf, sems, *, tt):
    i = pl.program_id(0)
    slot = i % 2

    def gather(idx_ref, s):
        def issue(t, carry):
            for k in range(TOP_K):
                d = idx_ref[t * TOP_K + k]
                pltpu.make_async_copy(ypad_ref.at[pl.ds(d, 1)], ybuf.at[s, k, pl.ds(t, 1)],
                                      sems.at[s]).start(priority=k % 2)
            return carry

        lax.fori_loop(0, tt, issue, 0, unroll=8)

    @pl.when(i == 0)
    def _():
        gather(dest_ref, slot)

    @pl.when(i + 1 < pl.num_programs(0))
    def _():
        gather(dnext_ref, 1 - slot)

    for k in range(TOP_K):
        pltpu.make_async_copy(ypad_ref.at[pl.ds(0, tt)], ybuf.at[slot, k], sems.at[slot]).wait()
    gate = gate_ref[...]
    acc = x_ref[...]
    for k in range(TOP_K):
        acc = acc + ybuf[slot, k] * gate[:, k:k + 1]
    o_ref[...] = acc


def _combine(x, gate, dest, y_pad):
    N, D = x.shape
    tt = ROW_TILE
    last = N // tt - 1
    return pl.pallas_call(
        functools.partial(_combine_kernel, tt=tt),
        grid=(N // tt,),
        in_specs=[
            pl.BlockSpec((tt * TOP_K,), lambda i: (i,), memory_space=pltpu.SMEM),
            pl.BlockSpec((tt * TOP_K,), lambda i: (jnp.minimum(i + 1, last),), memory_space=pltpu.SMEM),
            pl.BlockSpec((tt, D), lambda i: (i, 0)),
            pl.BlockSpec((tt, TOP_K), lambda i: (i, 0)),
            pl.BlockSpec(memory_space=pl.ANY),
        ],
        out_specs=pl.BlockSpec((tt, D), lambda i: (i, 0)),
        out_shape=jax.ShapeDtypeStruct((N, D), F32),
        scratch_shapes=[pltpu.VMEM((2, TOP_K, tt, D), F32), pltpu.SemaphoreType.DMA((2,))],
        compiler_params=_params(("arbitrary",)),
        name="moe_combine",
    )(dest, dest, x, gate, y_pad)


def _moe_layer(x, g, w_r, b_r, w_gu, b_gu, w_dn, b_dn, layer):
    N, D = x.shape
    E = N_EXPERTS
    tm = MOE_TILE
    e_idx, gate, rank, counts = _router(x, g, w_r, b_r)
    counts = counts[0].astype(I32)
    padded = ((counts + tm - 1) // tm) * tm
    pend = jnp.cumsum(padded)
    pstart = pend - padded
    dest = (pstart[e_idx] + rank).reshape(-1)
    P = ((N * TOP_K + E * (tm - 1) + tm - 1) // tm) * tm
    nt = P // tm
    tiles = jnp.arange(nt, dtype=I32)
    tile_e = jnp.minimum(jnp.sum(tiles[:, None] * tm >= pend[None, :], axis=1), E - 1).astype(I32)
    n_used = (pend[-1] // tm).astype(I32)
    tile_e = jnp.where(tiles < n_used, tile_e, tile_e[n_used - 1])
    x_pad = _dispatch(x, g, dest, pend.astype(I32), n_used.reshape(1), P)
    y_pad = _experts(tile_e, n_used.reshape(1), x_pad, w_gu, b_gu, w_dn, b_dn, layer)
    return _combine(x, gate, dest, y_pad)


def _bucket_maps():
    qi = np.arange(ATT_BLOCK)[None, :]
    kj = np.arange(2 * ATT_BLOCK)[:, None]
    delta = qi + ATT_BLOCK - kj
    buckets, valids = [], []
    max_exact = N_BUCKETS // 2
    for win, dil in DILATED_GROUPS:
        n = np.maximum(delta * dil, 0)
        nf = np.maximum(n, 1).astype(np.float32)
        large = max_exact + (np.log(nf / np.float32(max_exact)) / np.float32(math.log(REL_MAX_DIST / max_exact))
                             * np.float32(N_BUCKETS - max_exact)).astype(np.int32)
        large = np.minimum(large, N_BUCKETS - 1)
        buckets.append(np.where(n < max_exact, n, large).astype(np.int32))
        valids.append(((delta >= 0) & (delta <= win // dil)).astype(np.int32))
    return np.stack(buckets), np.stack(valids)


def _bias_kernel(tab_ref, bm_ref, valid_ref, o_ref):
    gh = pl.program_id(0)
    bm = bm_ref[0]
    acc = jnp.zeros(bm.shape, F32)
    for b in range(N_BUCKETS):
        acc = jnp.where(bm == b, tab_ref[b, gh], acc)
    band = jnp.where(valid_ref[0] > 0, acc, NEG_INF)
    key = lax.broadcasted_iota(I32, bm.shape, 0)
    o_ref[0, 0] = band
    o_ref[0, 1] = jnp.where(key >= ATT_BLOCK, band, NEG_INF)


def _bias_tables(rel_bias):
    bm, valid = _bucket_maps()
    GH = len(DILATED_GROUPS) * ATT_HEADS
    blk = (1, 2 * ATT_BLOCK, ATT_BLOCK)
    return pl.pallas_call(
        _bias_kernel,
        grid=(GH,),
        in_specs=[
            pl.BlockSpec(memory_space=pltpu.SMEM),
            pl.BlockSpec(blk, lambda i: (i // ATT_HEADS, 0, 0)),
            pl.BlockSpec(blk, lambda i: (i // ATT_HEADS, 0, 0)),
        ],
        out_specs=pl.BlockSpec((1, 2, 2 * ATT_BLOCK, ATT_BLOCK), lambda i: (i, 0, 0, 0)),
        out_shape=jax.ShapeDtypeStruct((GH, 2, 2 * ATT_BLOCK, ATT_BLOCK), F32),
        compiler_params=_params(("arbitrary",)),
        name="attn_bias",
    )(rel_bias, jnp.asarray(bm), jnp.asarray(valid))


def _cast_kernel(w_ref, o_ref):
    o_ref[...] = w_ref[...].astype(BF16)


def _to_bf16(w, layer, col_block):
    _, rows, cols = w.shape
    return pl.pallas_call(
        _cast_kernel,
        grid=(cols // col_block,),
        in_specs=[pl.BlockSpec((None, rows, col_block), lambda j: (layer, 0, j))],
        out_specs=pl.BlockSpec((rows, col_block), lambda j: (0, j)),
        out_shape=jax.ShapeDtypeStruct((rows, cols), BF16),
        compiler_params=_params(("parallel",)),
        name="cast_bf16",
    )(w)


def _qkv_kernel(x_ref, g_ref, wq_ref, wk_ref, wv_ref, qg_ref, kg_ref, qt_ref, k_ref, vt_ref,
                slab_ref, bd_ref, *, dil, R, nl):
    rc = pl.program_id(2)
    D = x_ref.shape[2]
    lanes = slab_ref.shape[2]
    first = (pl.program_id(0) == 0) & (pl.program_id(1) == 0) & (rc == 0)

    @pl.when(first)
    def _():
        r = lax.broadcasted_iota(I32, bd_ref.shape, 0)
        c = lax.broadcasted_iota(I32, bd_ref.shape, 1)
        hd_shift = HEAD_DIM.bit_length() - 1
        same_head = lax.shift_right_logical(r, hd_shift) == lax.shift_right_logical(c, hd_shift)
        bd_ref[...] = jnp.where(same_head, 1.0, 0.0).astype(BF16)

    if dil == 1:
        x = x_ref[0]
    else:
        @pl.when(rc == 0)
        def _():
            for c in range(D // lanes):
                slab_ref[c] = x_ref[0, :, c * lanes:(c + 1) * lanes]

        pieces = []
        for j in range(R):
            r = rc * R + j
            cols = [slab_ref[c, pl.ds(r, nl, stride=dil), :] for c in range(D // lanes)]
            pieces.append(jnp.concatenate(cols, axis=1))
        x = jnp.concatenate(pieces, axis=0)
    h = _rms(x, g_ref[...]).astype(BF16)
    nb = bd_ref.shape[0]

    def head_norm(y, gain):
        parts = []
        for c0 in range(0, y.shape[1], nb):
            yc = y[:, c0:c0 + nb]
            ssq = jnp.dot((yc * yc).astype(BF16), bd_ref[...], preferred_element_type=F32)
            parts.append(yc * lax.rsqrt(ssq * (1.0 / HEAD_DIM) + EPS))
        return jnp.concatenate(parts, axis=1) * gain

    q = head_norm(jnp.dot(h, wq_ref[...], preferred_element_type=F32), qg_ref[...])
    q = q * (HEAD_DIM ** -0.5)
    k = head_norm(jnp.dot(h, wk_ref[...], preferred_element_type=F32), kg_ref[...])
    v = jnp.dot(h, wv_ref[...], preferred_element_type=F32)
    blk = ATT_BLOCK
    for j in range(R):
        k_ref[0, j] = k[j * nl:(j + 1) * nl].astype(BF16)
        for c in range(nl // blk):
            rows = slice(j * nl + c * blk, j * nl + (c + 1) * blk)
            for hp in range(q.shape[1] // blk):
                cs = slice(hp * blk, (hp + 1) * blk)
                qt_ref[0, j, cs, c * blk:(c + 1) * blk] = q[rows, cs].T.astype(BF16)
                vt_ref[0, j, cs, c * blk:(c + 1) * blk] = v[rows, cs].T.astype(BF16)


def _qkv_group(x, g, w_bf, q_gain, k_gain, gi, dil):
    B, S, D = x.shape
    HD = ATT_HEADS * HEAD_DIM
    G = len(DILATED_GROUPS)
    L = S // dil
    R = min(dil, QKV_ROWS // ATT_BLOCK)
    nl = QKV_ROWS // R
    lanes = 128
    wspec = lambda s: pl.BlockSpec((D, HD), lambda b, l, r: (0, s * G + gi))
    tspec = pl.BlockSpec((1, R, HD, nl), lambda b, l, r: (b, r, 0, l))
    tshape = jax.ShapeDtypeStruct((B, dil, HD, L), BF16)
    return pl.pallas_call(
        functools.partial(_qkv_kernel, dil=dil, R=R, nl=nl),
        grid=(B, L // nl, dil // R),
        in_specs=[
            pl.BlockSpec((1, nl * dil, D), lambda b, l, r: (b, l, 0)),
            pl.BlockSpec((1, D), lambda b, l, r: (0, 0)),
            wspec(0), wspec(1), wspec(2),
            pl.BlockSpec((1, HD), lambda b, l, r: (0, 0)),
            pl.BlockSpec((1, HD), lambda b, l, r: (0, 0)),
        ],
        out_specs=[tspec, pl.BlockSpec((1, R, nl, HD), lambda b, l, r: (b, r, l, 0)), tspec],
        out_shape=[tshape, jax.ShapeDtypeStruct((B, dil, L, HD), BF16), tshape],
        scratch_shapes=[pltpu.VMEM((D // lanes, nl * dil if dil > 1 else 8, lanes), F32),
                        pltpu.VMEM((256, 256), BF16)],
        compiler_params=_params(("arbitrary", "arbitrary", "arbitrary")),
        name=f"attn_qkv_g{gi}",
    )(x, g.reshape(1, D), w_bf, w_bf, w_bf,
      jnp.tile(q_gain, ATT_HEADS).reshape(1, HD), jnp.tile(k_gain, ATT_HEADS).reshape(1, HD))


def _attn_kernel(qt_ref, kc_ref, kp_ref, vtc_ref, vtp_ref, bias_ref, o_ref, l_ref, s_ref, p_ref):
    n = pl.program_id(2)
    first = jnp.where(n == 0, 1, 0)
    blk = ATT_BLOCK
    pair = 2 * HEAD_DIM
    nk = s_ref.shape[1]
    k0 = 2 * blk - nk
    feat = lax.broadcasted_iota(I32, (pair, blk), 0)
    head_rows = [jnp.where(feat < HEAD_DIM, 1.0, 0.0).astype(BF16),
                 jnp.where(feat < HEAD_DIM, 0.0, 1.0).astype(BF16)]
    for pr in range(ATT_HEADS // 2):
        cs = slice(pr * pair, (pr + 1) * pair)
        qt2 = qt_ref[0, 0, cs, :]
        if k0:
            k2 = kc_ref[0, 0, :, cs]
        else:
            k2 = jnp.concatenate([kp_ref[0, 0, :, cs], kc_ref[0, 0, :, cs]], axis=0)
        for half in range(2):
            qth = qt2 * head_rows[half]
            s_ref[2 * pr + half] = jnp.dot(k2, qth, preferred_element_type=F32)
    lse_row = lax.broadcasted_iota(I32, (blk, blk), 0)
    lse_t = jnp.zeros((blk, blk), F32)
    invs = []
    for head in range(ATT_HEADS):
        s = s_ref[head] + bias_ref[head, first, k0:, :]
        m = jnp.max(s, axis=0, keepdims=True)
        p = jnp.exp(s - m)
        den = jnp.sum(p, axis=0, keepdims=True)
        p_ref[head] = p.astype(BF16)
        invs.append(1.0 / den)
        lse_t = jnp.where(lse_row == head, m + jnp.log(den), lse_t)
    for pr in range(ATT_HEADS // 2):
        cs = slice(pr * pair, (pr + 1) * pair)
        if k0:
            vt2 = vtc_ref[0, 0, cs, :]
        else:
            vt2 = jnp.concatenate([vtp_ref[0, 0, cs, :], vtc_ref[0, 0, cs, :]], axis=1)
        halves = []
        for half in range(2):
            head = 2 * pr + half
            vth = vt2[half * HEAD_DIM:(half + 1) * HEAD_DIM]
            ot = jnp.dot(vth, p_ref[head], preferred_element_type=F32)
            halves.append(ot * invs[head])
        o_ref[0, 0, :, cs] = jnp.concatenate(halves, axis=0).T
    l_ref[0, 0] = lse_t.T


def _attn_group(qt, k, vt, bias, gi):
    B, dil, L, HD = k.shape
    blk = ATT_BLOCK
    nb = L // blk
    cur = pl.BlockSpec((1, 1, blk, HD), lambda b, r, n: (b, r, n, 0))
    prev = pl.BlockSpec((1, 1, blk, HD), lambda b, r, n: (b, r, jnp.maximum(n - 1, 0), 0))
    tcur = pl.BlockSpec((1, 1, HD, blk), lambda b, r, n: (b, r, 0, n))
    tprev = pl.BlockSpec((1, 1, HD, blk), lambda b, r, n: (b, r, 0, jnp.maximum(n - 1, 0)))
    return pl.pallas_call(
        _attn_kernel,
        grid=(B, dil, nb),
        in_specs=[tcur, cur, prev, tcur, tprev,
                  pl.BlockSpec((ATT_HEADS, 2, 2 * blk, blk), lambda b, r, n: (gi, 0, 0, 0))],
        out_specs=[cur, pl.BlockSpec((1, 1, blk, blk), lambda b, r, n: (b, r, n, 0))],
        out_shape=[jax.ShapeDtypeStruct((B, dil, L, HD), F32),
                   jax.ShapeDtypeStruct((B, dil, L, blk), F32)],
        scratch_shapes=[pltpu.VMEM((ATT_HEADS, blk if nb == 1 else 2 * blk, blk), F32),
                        pltpu.VMEM((ATT_HEADS, blk if nb == 1 else 2 * blk, blk), BF16)],
        compiler_params=_params(("parallel", "parallel", "arbitrary")),
        name=f"attn_core_g{gi}",
    )(qt, k, k, vt, vt, bias)


def _attn_out_kernel(x_ref, o0, o1, o2, l0, l1, l2, wo_ref, out_ref, oslab, lslab, ex_ref, *, T):
    lanes = oslab.shape[3]
    HD = o0.shape[3]

    @pl.when((pl.program_id(0) == 0) & (pl.program_id(1) == 0))
    def _():
        h = lax.broadcasted_iota(I32, ex_ref.shape, 0)
        c = lax.broadcasted_iota(I32, ex_ref.shape, 1)
        owner = lax.shift_right_logical(c, HEAD_DIM.bit_length() - 1)
        ex_ref[...] = jnp.where(h == owner, 1.0, 0.0).astype(BF16)

    def token_order(o_ref, l_ref, dil, s):
        if dil == 1:
            return o_ref[0, 0], l_ref[0, 0]
        n = T // dil
        for r in range(dil):
            blk = o_ref[0, r]
            for c in range(HD // lanes):
                oslab[s, c, pl.ds(r, n, stride=dil), :] = blk[:, c * lanes:(c + 1) * lanes]
            lslab[s, pl.ds(r, n, stride=dil), :] = l_ref[0, r]
        o = jnp.concatenate([oslab[s, c] for c in range(HD // lanes)], axis=1)
        return o, lslab[s]

    groups = [token_order(o_ref, l_ref, dil, s)
              for s, ((_, dil), o_ref, l_ref) in enumerate(zip(DILATED_GROUPS, (o0, o1, o2), (l0, l1, l2)))]
    m = jnp.maximum(jnp.maximum(groups[0][1], groups[1][1]), groups[2][1])
    es = [jnp.exp(l - m) for _, l in groups]
    inv = 1.0 / (es[0] + es[1] + es[2])
    att = jnp.zeros((T, HD), F32)
    for (o, _), e in zip(groups, es):
        a = e * inv
        a_hi = a.astype(BF16)
        a_lo = (a - a_hi.astype(F32)).astype(BF16)
        wide = (jnp.dot(a_hi, ex_ref[...], preferred_element_type=F32)
                + jnp.dot(a_lo, ex_ref[...], preferred_element_type=F32))
        att = att + wide * o
    out_ref[0] = x_ref[0] + jnp.dot(att.astype(BF16), wo_ref[...], preferred_element_type=F32)


def _attn_out(x, os, ls, wo_bf):
    B, S, D = x.shape
    HD = wo_bf.shape[0]
    T = OUT_TILE
    lanes = 128
    ospec = lambda dil, w: pl.BlockSpec((1, dil, T // dil, w), lambda b, i: (b, 0, i, 0))
    dils = [dil for _, dil in DILATED_GROUPS]
    xspec = pl.BlockSpec((1, T, D), lambda b, i: (b, i, 0))
    return pl.pallas_call(
        functools.partial(_attn_out_kernel, T=T),
        grid=(B, S // T),
        in_specs=[xspec] + [ospec(d, HD) for d in dils] + [ospec(d, ATT_BLOCK) for d in dils]
                 + [pl.BlockSpec((HD, D), lambda b, i: (0, 0))],
        out_specs=xspec,
        out_shape=jax.ShapeDtypeStruct((B, S, D), F32),
        scratch_shapes=[pltpu.VMEM((len(dils), HD // lanes, T, lanes), F32),
                        pltpu.VMEM((len(dils), T, ATT_BLOCK), F32),
                        pltpu.VMEM((ATT_BLOCK, HD), BF16)],
        compiler_params=_params(("arbitrary", "arbitrary")),
        name="attn_out",
    )(x, *os, *ls, wo_bf)


def _attention_layer(x, g, w_qkv, q_gain, k_gain, w_o, rel_bias, layer):
    HD = ATT_HEADS * HEAD_DIM
    bias = _bias_tables(rel_bias)
    wqkv_bf = _to_bf16(w_qkv, layer, HD)
    wo_bf = _to_bf16(w_o, layer, w_o.shape[2])
    os, ls = [], []
    for gi, (_, dil) in enumerate(DILATED_GROUPS):
        qt, k, vt = _qkv_group(x, g, wqkv_bf, q_gain, k_gain, gi, dil)
        o, l = _attn_group(qt, k, vt, bias, gi)
        os.append(o)
        ls.append(l)
    return _attn_out(x, os, ls, wo_bf)


def kernel(x, norm_mix_g, norm_ffn_g, pool_w, pool_scale, attn_w_qkv, attn_q_gain, attn_k_gain,
           attn_w_o, rel_bias, moe_w_router, moe_b_router, moe_w_gate_up, moe_b_gate_up,
           moe_w_down, moe_b_down):
    B, S, D = x.shape
    depth = norm_mix_g.shape[0]
    for i in range(depth):
        j = i // 2
        if i % 2 == 0:
            x = _pool_layer(x, norm_mix_g[i], pool_w[j], pool_scale[j])
        else:
            x = _attention_layer(x, norm_mix_g[i], attn_w_qkv, attn_q_gain[j], attn_k_gain[j],
                                 attn_w_o, rel_bias, j)
        x = _moe_layer(x.reshape(B * S, D), norm_ffn_g[i], moe_w_router[i], moe_b_router[i],
                       moe_w_gate_up, moe_b_gate_up[i], moe_w_down, moe_b_down[i],
                       i).reshape(B, S, D)
    return x
```

```python
import functools
import math

import numpy as np
import jax
import jax.numpy as jnp
from jax import lax
from jax.experimental import pallas as pl
from jax.experimental.pallas import tpu as pltpu

F32 = jnp.float32
BF16 = jnp.bfloat16
I32 = jnp.int32

EPS = 1e-6
POOL_WINDOWS = (2, 4, 8, 16)
POOL_HALO = 16
DILATED_GROUPS = ((128, 1), (512, 4), (2048, 16))
ATT_HEADS = 16
HEAD_DIM = 64
ATT_BLOCK = 128
N_BUCKETS = 32
REL_MAX_DIST = 2048
NEG_INF = -1e30
N_EXPERTS = 32
TOP_K = 4
SWIGLU_LIMIT = 7.0
SWIGLU_ALPHA = 1.702

V7X_VMEM_LIMIT_BYTES = 56 * 1024 * 1024
MOE_TILE = 512
ROW_TILE = 256
ROUTER_TILE = 512
POOL_TILE = 512
QKV_ROWS = 512
LANES = 128
OUT_TILE = 256


def _rms(xf, g):
    ms = jnp.mean(xf * xf, axis=-1, keepdims=True)
    return xf * lax.rsqrt(ms + EPS) * g


def _params(sem, vmem=None):
    return pltpu.CompilerParams(dimension_semantics=sem,
                                vmem_limit_bytes=vmem or V7X_VMEM_LIMIT_BYTES)


def _pool_kernel(x_ref, halo_ref, g_ref, w_ref, sc_ref, o_ref, *, ts, dg):
    i = pl.program_id(1)
    x = x_ref[0]
    g = g_ref[...]
    h = _rms(x, g)
    hh = _rms(halo_ref[0], g)
    hh = jnp.where(i == 0, 0.0, hh)
    full = jnp.concatenate([hh, h], axis=0)
    pos = i * ts + lax.broadcasted_iota(I32, (ts, 1), 0)
    outs = []
    for gi, w in enumerate(POOL_WINDOWS):
        s = full[:, gi * dg:(gi + 1) * dg]
        sh = 1
        while sh < w:
            s = s + pltpu.roll(s, sh, 0)
            sh *= 2
        s = s[POOL_HALO:]
        cnt = jnp.minimum(pos + 1, w).astype(F32)
        p = s / cnt - h[:, gi * dg:(gi + 1) * dg]
        outs.append(jnp.dot(p.astype(BF16), w_ref[gi].astype(BF16),
                            preferred_element_type=F32))
    y = jnp.concatenate(outs, axis=1) * sc_ref[...]
    o_ref[0] = x + y


def _pool_layer(x, g, w_groups, scale):
    B, S, D = x.shape
    ts = POOL_TILE
    dg = D // len(POOL_WINDOWS)
    hb = ts // POOL_HALO
    return pl.pallas_call(
        functools.partial(_pool_kernel, ts=ts, dg=dg),
        grid=(B, S // ts),
        in_specs=[
            pl.BlockSpec((1, ts, D), lambda b, i: (b, i, 0)),
            pl.BlockSpec((1, POOL_HALO, D), lambda b, i: (b, jnp.maximum(i * hb - 1, 0), 0)),
            pl.BlockSpec((1, D), lambda b, i: (0, 0)),
            pl.BlockSpec((len(POOL_WINDOWS), dg, dg), lambda b, i: (0, 0, 0)),
            pl.BlockSpec((1, D), lambda b, i: (0, 0)),
        ],
        out_specs=pl.BlockSpec((1, ts, D), lambda b, i: (b, i, 0)),
        out_shape=jax.ShapeDtypeStruct((B, S, D), F32),
        compiler_params=_params(("parallel", "parallel")),
        name="pool_mixer",
    )(x, x, g.reshape(1, D), w_groups, scale.reshape(1, D))


def _split3_dot(a, b):
    a_hi = a.astype(BF16)
    a_lo = (a - a_hi.astype(F32)).astype(BF16)
    b_hi = b.astype(BF16)
    b_lo = (b - b_hi.astype(F32)).astype(BF16)
    d = functools.partial(jnp.dot, preferred_element_type=F32)
    return d(a_hi, b_hi) + (d(a_hi, b_lo) + d(a_lo, b_hi))


def _router_kernel(x_ref, g_ref, wr_ref, br_ref, e_ref, gate_ref, rank_ref, cnt_ref,
                   tri_ref, carry_ref, *, tt):
    i = pl.program_id(0)
    E = N_EXPERTS

    @pl.when(i == 0)
    def _():
        r = lax.broadcasted_iota(I32, (tt, tt), 0)
        c = lax.broadcasted_iota(I32, (tt, tt), 1)
        tri_ref[...] = jnp.where(c < r, 1.0, 0.0).astype(BF16)
        carry_ref[...] = jnp.zeros_like(carry_ref)

    h = _rms(x_ref[...], g_ref[...])
    logits = _split3_dot(h, wr_ref[...]) + br_ref[...]
    lane = lax.broadcasted_iota(I32, (tt, E), 1).astype(F32)
    lane_k = lax.broadcasted_iota(I32, (tt, TOP_K), 1)
    l = logits
    vals, sels = [], []
    e_out = jnp.zeros((tt, TOP_K), F32)
    for k in range(TOP_K):
        m = jnp.max(l, axis=-1, keepdims=True)
        idx = jnp.min(jnp.where(l == m, lane, float(E)), axis=-1, keepdims=True)
        sel = lane == idx
        vals.append(m)
        sels.append(sel)
        e_out = jnp.where(lane_k == k, idx, e_out)
        l = jnp.where(sel, -jnp.inf, l)
    ex = [jnp.exp(v - vals[0]) for v in vals]
    den = ex[0] + ex[1] + ex[2] + ex[3]
    multi = jnp.zeros((tt, E), F32)
    for sel in sels:
        multi = multi + jnp.where(sel, 1.0, 0.0)
    before = jnp.dot(tri_ref[...], multi.astype(BF16), preferred_element_type=F32)
    before = before + carry_ref[...]
    g_out = jnp.zeros((tt, TOP_K), F32)
    r_out = jnp.zeros((tt, TOP_K), F32)
    for k in range(TOP_K):
        g_out = jnp.where(lane_k == k, ex[k] / den, g_out)
        rk = jnp.sum(jnp.where(sels[k], before, 0.0), axis=-1, keepdims=True)
        r_out = jnp.where(lane_k == k, rk, r_out)
    e_ref[...] = e_out.astype(I32)
    gate_ref[...] = g_out
    rank_ref[...] = r_out.astype(I32)
    carry_ref[...] = carry_ref[...] + jnp.sum(multi, axis=0, keepdims=True)
    cnt_ref[...] = carry_ref[...]


def _router(x, g, w_r, b_r):
    N, D = x.shape
    tt = ROUTER_TILE
    E = N_EXPERTS
    return pl.pallas_call(
        functools.partial(_router_kernel, tt=tt),
        grid=(N // tt,),
        in_specs=[
            pl.BlockSpec((tt, D), lambda i: (i, 0)),
            pl.BlockSpec((1, D), lambda i: (0, 0)),
            pl.BlockSpec((D, E), lambda i: (0, 0)),
            pl.BlockSpec((1, E), lambda i: (0, 0)),
        ],
        out_specs=[
            pl.BlockSpec((tt, TOP_K), lambda i: (i, 0)),
            pl.BlockSpec((tt, TOP_K), lambda i: (i, 0)),
            pl.BlockSpec((tt, TOP_K), lambda i: (i, 0)),
            pl.BlockSpec((1, E), lambda i: (0, 0)),
        ],
        out_shape=[
            jax.ShapeDtypeStruct((N, TOP_K), I32),
            jax.ShapeDtypeStruct((N, TOP_K), F32),
            jax.ShapeDtypeStruct((N, TOP_K), I32),
            jax.ShapeDtypeStruct((1, E), F32),
        ],
        scratch_shapes=[pltpu.VMEM((tt, tt), BF16), pltpu.VMEM((1, E), F32)],
        compiler_params=_params(("arbitrary",)),
        name="moe_router",
    )(x, g.reshape(1, D), w_r, b_r.reshape(1, E))


def _to_tile_rows(ref, val):
    n, D = val.shape
    nsub = D // LANES
    for s in range(nsub):
        ref[pl.ds(s, n, stride=nsub), :] = val[:, s * LANES:(s + 1) * LANES]


def _from_tile_rows(ref, n, nsub):
    return jnp.concatenate([ref[pl.ds(s, n, stride=nsub), :] for s in range(nsub)], axis=1)


def _tile_row(ref, r, nsub, count=1):
    return ref.at[pl.ds(pl.multiple_of(r * nsub, nsub), count * nsub)]


def _row_copy_wait(src, dst, sem, times):
    for _ in range(times):
        pltpu.make_async_copy(src, dst.at[pl.ds(0, src.shape[0])], sem).wait()


def _dispatch_kernel(pend_ref, nu_ref, dest_ref, x_ref, g_ref, xpad_ref, hbuf, zbuf, sems, zsem,
                     *, tt, tm, nt, nsub):
    i = pl.program_id(0)
    slot = i % 2

    @pl.when(i == 0)
    def _():
        zbuf[...] = jnp.zeros_like(zbuf)

        def seg_copy(e):
            start = pl.multiple_of(jnp.maximum(pend_ref[e] - tm, 0), tm)
            return pltpu.make_async_copy(zbuf, _tile_row(xpad_ref, start, nsub, tm), zsem)

        def tail_copy(j):
            return pltpu.make_async_copy(zbuf, _tile_row(xpad_ref, (nt - 1 - j) * tm, nsub, tm), zsem)

        for e in range(N_EXPERTS):
            seg_copy(e).start()
        for j in range(N_EXPERTS):
            pl.when(nt - 1 - j >= nu_ref[0])(tail_copy(j).start)
        for e in range(N_EXPERTS):
            seg_copy(e).wait()
        for j in range(N_EXPERTS):
            pl.when(nt - 1 - j >= nu_ref[0])(tail_copy(j).wait)

    hb = hbuf.at[slot]
    _to_tile_rows(hb, _rms(x_ref[...], g_ref[...]))

    def issue(t, carry):
        for k in range(TOP_K):
            d = dest_ref[t * TOP_K + k]
            pltpu.make_async_copy(_tile_row(hb, t, nsub), _tile_row(xpad_ref, d, nsub),
                                  sems.at[slot]).start(priority=k % 2)
        return carry

    lax.fori_loop(0, tt, issue, 0, unroll=8)

    @pl.when(i > 0)
    def _():
        _row_copy_wait(hbuf.at[1 - slot], xpad_ref, sems.at[1 - slot], TOP_K)

    @pl.when(i == pl.num_programs(0) - 1)
    def _():
        _row_copy_wait(hb, xpad_ref, sems.at[slot], TOP_K)


def _dispatch(x, g, dest, pend, n_used, P):
    N, D = x.shape
    tt = ROW_TILE
    tm = MOE_TILE
    nsub = D // LANES
    grid_spec = pltpu.PrefetchScalarGridSpec(
        num_scalar_prefetch=2,
        grid=(N // tt,),
        in_specs=[
            pl.BlockSpec((tt * TOP_K,), lambda i, pe, nu: (i,), memory_space=pltpu.SMEM),
            pl.BlockSpec((tt, D), lambda i, pe, nu: (i, 0)),
            pl.BlockSpec((1, D), lambda i, pe, nu: (0, 0)),
        ],
        out_specs=pl.BlockSpec(memory_space=pl.ANY),
        scratch_shapes=[pltpu.VMEM((2, tt * nsub, LANES), F32), pltpu.VMEM((tm * nsub, LANES), F32),
                        pltpu.SemaphoreType.DMA((2,)), pltpu.SemaphoreType.DMA(())],
    )
    return pl.pallas_call(
        functools.partial(_dispatch_kernel, tt=tt, tm=tm, nt=P // tm, nsub=nsub),
        grid_spec=grid_spec,
        out_shape=jax.ShapeDtypeStruct((P * nsub, LANES), F32),
        compiler_params=_params(("arbitrary",)),
        name="moe_dispatch",
    )(pend, n_used, dest, x, g.reshape(1, D))


def _expert_kernel(te_ref, nu_ref, x_ref, wgu_ref, bgu_ref, wdn_ref, bdn_ref, o_ref,
                   wgu_bf, wdn_bf, *, F, tm, nsub):
    i = pl.program_id(0)

    @pl.when(i < nu_ref[0])
    def _():
        prev = te_ref[jnp.maximum(i - 1, 0)]

        @pl.when((i == 0) | (te_ref[i] != prev))
        def _():
            wgu_bf[...] = wgu_ref[0].astype(BF16)
            wdn_bf[...] = wdn_ref[0].astype(BF16)

        x = _from_tile_rows(x_ref, tm, nsub).astype(BF16)
        gu = jnp.dot(x, wgu_bf[...], preferred_element_type=F32) + bgu_ref[0]
        gate = jnp.minimum(gu[:, :F], SWIGLU_LIMIT)
        up = jnp.clip(gu[:, F:], -SWIGLU_LIMIT, SWIGLU_LIMIT)
        glu = gate * jax.nn.sigmoid(SWIGLU_ALPHA * gate)
        a = ((up + 1.0) * glu).astype(BF16)
        _to_tile_rows(o_ref, jnp.dot(a, wdn_bf[...], preferred_element_type=F32) + bdn_ref[0])

    @pl.when(i >= nu_ref[0])
    def _():
        o_ref[...] = jnp.zeros_like(o_ref)


def _experts(tile_e, n_used, x_pad, w_gu, b_gu, w_dn, b_dn, layer):
    _, E, D, F2 = w_gu.shape
    F = F2 // 2
    tm = MOE_TILE
    nsub = D // LANES
    P = x_pad.shape[0] // nsub
    row = lambda i, te, nu: (jnp.minimum(i, nu[0] - 1), 0)
    exp4 = lambda i, te, nu: (layer, te[i], 0, 0)
    exp3 = lambda i, te, nu: (te[i], 0, 0)
    grid_spec = pltpu.PrefetchScalarGridSpec(
        num_scalar_prefetch=2,
        grid=(P // tm,),
        in_specs=[
            pl.BlockSpec((tm * nsub, LANES), row),
            pl.BlockSpec((None, 1, D, F2), exp4),
            pl.BlockSpec((1, 1, F2), exp3),
            pl.BlockSpec((None, 1, F, D), exp4),
            pl.BlockSpec((1, 1, D), exp3),
        ],
        out_specs=pl.BlockSpec((tm * nsub, LANES), lambda i, te, nu: (i, 0)),
        scratch_shapes=[pltpu.VMEM((D, F2), BF16), pltpu.VMEM((F, D), BF16)],
    )
    return pl.pallas_call(
        functools.partial(_expert_kernel, F=F, tm=tm, nsub=nsub),
        grid_spec=grid_spec,
        out_shape=jax.ShapeDtypeStruct((P * nsub, LANES), F32),
        compiler_params=_params(("arbitrary",)),
        name="moe_experts",
    )(tile_e, n_used, x_pad, w_gu, b_gu.reshape(E, 1, F2), w_dn, b_dn.reshape(E, 1, D))


def _combine_kernel(dest_ref, dnext_ref, x_ref, gate_ref, ypad_ref, o_ref, ybuf, sems, *, tt, nsub):
    i = pl.program_id(0)
    slot = i % 2

    def gather(idx_ref, s):
        def issue(t, carry):
            for k in range(TOP_K):
                d = idx_ref[t * TOP_K + k]
                pltpu.make_async_copy(_tile_row(ypad_ref, d, nsub), _tile_row(ybuf.at[s, k], t, nsub),
                                      sems.at[s]).start(priority=k % 2)
            return carry

        lax.fori_loop(0, tt, issue, 0, unroll=8)

    @pl.when(i == 0)
    def _():
        gather(dest_ref, slot)

    @pl.when(i + 1 < pl.num_programs(0))
    def _():
        gather(dnext_ref, 1 - slot)

    for k in range(TOP_K):
        _row_copy_wait(ybuf.at[slot, k], ypad_ref, sems.at[slot], 1)
    gate = gate_ref[...]
    x = x_ref[...]
    cols = []
    for c in range(nsub):
        acc = x[:, c * LANES:(c + 1) * LANES]
        for k in range(TOP_K):
            acc = acc + ybuf[slot, k, pl.ds(c, tt, stride=nsub), :] * gate[:, k:k + 1]
        cols.append(acc)
    o_ref[...] = jnp.concatenate(cols, axis=1)


def _combine(x, gate, dest, y_pad):
    N, D = x.shape
    tt = ROW_TILE
    nsub = D // LANES
    last = N // tt - 1
    return pl.pallas_call(
        functools.partial(_combine_kernel, tt=tt, nsub=nsub),
        grid=(N // tt,),
        in_specs=[
            pl.BlockSpec((tt * TOP_K,), lambda i: (i,), memory_space=pltpu.SMEM),
            pl.BlockSpec((tt * TOP_K,), lambda i: (jnp.minimum(i + 1, last),), memory_space=pltpu.SMEM),
            pl.BlockSpec((tt, D), lambda i: (i, 0)),
            pl.BlockSpec((tt, TOP_K), lambda i: (i, 0)),
            pl.BlockSpec(memory_space=pl.ANY),
        ],
        out_specs=pl.BlockSpec((tt, D), lambda i: (i, 0)),
        out_shape=jax.ShapeDtypeStruct((N, D), F32),
        scratch_shapes=[pltpu.VMEM((2, TOP_K, tt * nsub, LANES), F32), pltpu.SemaphoreType.DMA((2,))],
        compiler_params=_params(("arbitrary",)),
        name="moe_combine",
    )(dest, dest, x, gate, y_pad)


def _moe_layer(x, g, w_r, b_r, w_gu, b_gu, w_dn, b_dn, layer):
    N, D = x.shape
    E = N_EXPERTS
    tm = MOE_TILE
    e_idx, gate, rank, counts = _router(x, g, w_r, b_r)
    counts = counts[0].astype(I32)
    padded = ((counts + tm - 1) // tm) * tm
    pend = jnp.cumsum(padded)
    pstart = pend - padded
    dest = (pstart[e_idx] + rank).reshape(-1)
    P = ((N * TOP_K + E * (tm - 1) + tm - 1) // tm) * tm
    nt = P // tm
    tiles = jnp.arange(nt, dtype=I32)
    tile_e = jnp.minimum(jnp.sum(tiles[:, None] * tm >= pend[None, :], axis=1), E - 1).astype(I32)
    n_used = (pend[-1] // tm).astype(I32)
    tile_e = jnp.where(tiles < n_used, tile_e, tile_e[n_used - 1])
    x_pad = _dispatch(x, g, dest, pend.astype(I32), n_used.reshape(1), P)
    y_pad = _experts(tile_e, n_used.reshape(1), x_pad, w_gu, b_gu, w_dn, b_dn, layer)
    return _combine(x, gate, dest, y_pad)


def _bucket_maps():
    qi = np.arange(ATT_BLOCK)[None, :]
    kj = np.arange(2 * ATT_BLOCK)[:, None]
    delta = qi + ATT_BLOCK - kj
    buckets, valids = [], []
    max_exact = N_BUCKETS // 2
    for win, dil in DILATED_GROUPS:
        n = np.maximum(delta * dil, 0)
        nf = np.maximum(n, 1).astype(np.float32)
        large = max_exact + (np.log(nf / np.float32(max_exact)) / np.float32(math.log(REL_MAX_DIST / max_exact))
                             * np.float32(N_BUCKETS - max_exact)).astype(np.int32)
        large = np.minimum(large, N_BUCKETS - 1)
        buckets.append(np.where(n < max_exact, n, large).astype(np.int32))
        valids.append(((delta >= 0) & (delta <= win // dil)).astype(np.int32))
    return np.stack(buckets), np.stack(valids)


def _bias_kernel(tab_ref, bm_ref, valid_ref, o_ref):
    gh = pl.program_id(0)
    bm = bm_ref[0]
    acc = jnp.zeros(bm.shape, F32)
    for b in range(N_BUCKETS):
        acc = jnp.where(bm == b, tab_ref[b, gh], acc)
    band = jnp.where(valid_ref[0] > 0, acc, NEG_INF)
    key = lax.broadcasted_iota(I32, bm.shape, 0)
    o_ref[0, 0] = band
    o_ref[0, 1] = jnp.where(key >= ATT_BLOCK, band, NEG_INF)


def _bias_tables(rel_bias):
    bm, valid = _bucket_maps()
    GH = len(DILATED_GROUPS) * ATT_HEADS
    blk = (1, 2 * ATT_BLOCK, ATT_BLOCK)
    return pl.pallas_call(
        _bias_kernel,
        grid=(GH,),
        in_specs=[
            pl.BlockSpec(memory_space=pltpu.SMEM),
            pl.BlockSpec(blk, lambda i: (i // ATT_HEADS, 0, 0)),
            pl.BlockSpec(blk, lambda i: (i // ATT_HEADS, 0, 0)),
        ],
        out_specs=pl.BlockSpec((1, 2, 2 * ATT_BLOCK, ATT_BLOCK), lambda i: (i, 0, 0, 0)),
        out_shape=jax.ShapeDtypeStruct((GH, 2, 2 * ATT_BLOCK, ATT_BLOCK), F32),
        compiler_params=_params(("arbitrary",)),
        name="attn_bias",
    )(rel_bias, jnp.asarray(bm), jnp.asarray(valid))


def _cast_kernel(w_ref, o_ref):
    o_ref[...] = w_ref[...].astype(BF16)


def _to_bf16(w, layer, col_block):
    _, rows, cols = w.shape
    return pl.pallas_call(
        _cast_kernel,
        grid=(cols // col_block,),
        in_specs=[pl.BlockSpec((None, rows, col_block), lambda j: (layer, 0, j))],
        out_specs=pl.BlockSpec((rows, col_block), lambda j: (0, j)),
        out_shape=jax.ShapeDtypeStruct((rows, cols), BF16),
        compiler_params=_params(("parallel",)),
        name="cast_bf16",
    )(w)


def _qkv_kernel(x_ref, g_ref, wq_ref, wk_ref, wv_ref, qg_ref, kg_ref, qt_ref, k_ref, vt_ref,
                slab_ref, bd_ref, *, dil, R, nl):
    rc = pl.program_id(2)
    D = x_ref.shape[2]
    lanes = slab_ref.shape[2]
    first = (pl.program_id(0) == 0) & (pl.program_id(1) == 0) & (rc == 0)

    @pl.when(first)
    def _():
        r = lax.broadcasted_iota(I32, bd_ref.shape, 0)
        c = lax.broadcasted_iota(I32, bd_ref.shape, 1)
        hd_shift = HEAD_DIM.bit_length() - 1
        same_head = lax.shift_right_logical(r, hd_shift) == lax.shift_right_logical(c, hd_shift)
        bd_ref[...] = jnp.where(same_head, 1.0, 0.0).astype(BF16)

    if dil == 1:
        x = x_ref[0]
    else:
        @pl.when(rc == 0)
        def _():
            for c in range(D // lanes):
                slab_ref[c] = x_ref[0, :, c * lanes:(c + 1) * lanes]

        pieces = []
        for j in range(R):
            r = rc * R + j
            cols = [slab_ref[c, pl.ds(r, nl, stride=dil), :] for c in range(D // lanes)]
            pieces.append(jnp.concatenate(cols, axis=1))
        x = jnp.concatenate(pieces, axis=0)
    h = _rms(x, g_ref[...]).astype(BF16)
    nb = bd_ref.shape[0]

    def head_norm(y, gain):
        parts = []
        for c0 in range(0, y.shape[1], nb):
            yc = y[:, c0:c0 + nb]
            ssq = jnp.dot((yc * yc).astype(BF16), bd_ref[...], preferred_element_type=F32)
            parts.append(yc * lax.rsqrt(ssq * (1.0 / HEAD_DIM) + EPS))
        return jnp.concatenate(parts, axis=1) * gain

    q = head_norm(jnp.dot(h, wq_ref[...], preferred_element_type=F32), qg_ref[...])
    q = q * (HEAD_DIM ** -0.5)
    k = head_norm(jnp.dot(h, wk_ref[...], preferred_element_type=F32), kg_ref[...])
    v = jnp.dot(h, wv_ref[...], preferred_element_type=F32)
    blk = ATT_BLOCK
    for j in range(R):
        k_ref[0, j] = k[j * nl:(j + 1) * nl].astype(BF16)
        for c in range(nl // blk):
            rows = slice(j * nl + c * blk, j * nl + (c + 1) * blk)
            for hp in range(q.shape[1] // blk):
                cs = slice(hp * blk, (hp + 1) * blk)
                qt_ref[0, j, cs, c * blk:(c + 1) * blk] = q[rows, cs].T.astype(BF16)
                vt_ref[0, j, cs, c * blk:(c + 1) * blk] = v[rows, cs].T.astype(BF16)


def _qkv_group(x, g, w_bf, q_gain, k_gain, gi, dil):
    B, S, D = x.shape
    HD = ATT_HEADS * HEAD_DIM
    G = len(DILATED_GROUPS)
    L = S // dil
    R = min(dil, QKV_ROWS // ATT_BLOCK)
    nl = QKV_ROWS // R
    lanes = 128
    wspec = lambda s: pl.BlockSpec((D, HD), lambda b, l, r: (0, s * G + gi))
    tspec = pl.BlockSpec((1, R, HD, nl), lambda b, l, r: (b, r, 0, l))
    tshape = jax.ShapeDtypeStruct((B, dil, HD, L), BF16)
    return pl.pallas_call(
        functools.partial(_qkv_kernel, dil=dil, R=R, nl=nl),
        grid=(B, L // nl, dil // R),
        in_specs=[
            pl.BlockSpec((1, nl * dil, D), lambda b, l, r: (b, l, 0)),
            pl.BlockSpec((1, D), lambda b, l, r: (0, 0)),
            wspec(0), wspec(1), wspec(2),
            pl.BlockSpec((1, HD), lambda b, l, r: (0, 0)),
            pl.BlockSpec((1, HD), lambda b, l, r: (0, 0)),
        ],
        out_specs=[tspec, pl.BlockSpec((1, R, nl, HD), lambda b, l, r: (b, r, l, 0)), tspec],
        out_shape=[tshape, jax.ShapeDtypeStruct((B, dil, L, HD), BF16), tshape],
        scratch_shapes=[pltpu.VMEM((D // lanes, nl * dil if dil > 1 else 8, lanes), F32),
                        pltpu.VMEM((256, 256), BF16)],
        compiler_params=_params(("arbitrary", "arbitrary", "arbitrary")),
        name=f"attn_qkv_g{gi}",
    )(x, g.reshape(1, D), w_bf, w_bf, w_bf,
      jnp.tile(q_gain, ATT_HEADS).reshape(1, HD), jnp.tile(k_gain, ATT_HEADS).reshape(1, HD))


def _attn_kernel(qt_ref, kc_ref, kp_ref, vtc_ref, vtp_ref, bias_ref, o_ref, l_ref, s_ref, p_ref):
    n = pl.program_id(2)
    first = jnp.where(n == 0, 1, 0)
    blk = ATT_BLOCK
    pair = 2 * HEAD_DIM
    nk = s_ref.shape[1]
    k0 = 2 * blk - nk
    feat = lax.broadcasted_iota(I32, (pair, blk), 0)
    head_rows = [jnp.where(feat < HEAD_DIM, 1.0, 0.0).astype(BF16),
                 jnp.where(feat < HEAD_DIM, 0.0, 1.0).astype(BF16)]
    for pr in range(ATT_HEADS // 2):
        cs = slice(pr * pair, (pr + 1) * pair)
        qt2 = qt_ref[0, 0, cs, :]
        if k0:
            k2 = kc_ref[0, 0, :, cs]
        else:
            k2 = jnp.concatenate([kp_ref[0, 0, :, cs], kc_ref[0, 0, :, cs]], axis=0)
        for half in range(2):
            qth = qt2 * head_rows[half]
            s_ref[2 * pr + half] = jnp.dot(k2, qth, preferred_element_type=F32)
    lse_row = lax.broadcasted_iota(I32, (blk, blk), 0)
    lse_t = jnp.zeros((blk, blk), F32)
    invs = []
    for head in range(ATT_HEADS):
        s = s_ref[head] + bias_ref[head, first, k0:, :]
        m = jnp.max(s, axis=0, keepdims=True)
        p = jnp.exp(s - m)
        den = jnp.sum(p, axis=0, keepdims=True)
        p_ref[head] = p.astype(BF16)
        invs.append(1.0 / den)
        lse_t = jnp.where(lse_row == head, m + jnp.log(den), lse_t)
    for pr in range(ATT_HEADS // 2):
        cs = slice(pr * pair, (pr + 1) * pair)
        if k0:
            vt2 = vtc_ref[0, 0, cs, :]
        else:
            vt2 = jnp.concatenate([vtp_ref[0, 0, cs, :], vtc_ref[0, 0, cs, :]], axis=1)
        halves = []
        for half in range(2):
            head = 2 * pr + half
            vth = vt2[half * HEAD_DIM:(half + 1) * HEAD_DIM]
            ot = jnp.dot(vth, p_ref[head], preferred_element_type=F32)
            halves.append(ot * invs[head])
        o_ref[0, 0, :, cs] = jnp.concatenate(halves, axis=0).T
    l_ref[0, 0] = lse_t.T


def _attn_group(qt, k, vt, bias, gi):
    B, dil, L, HD = k.shape
    blk = ATT_BLOCK
    nb = L // blk
    cur = pl.BlockSpec((1, 1, blk, HD), lambda b, r, n: (b, r, n, 0))
    prev = pl.BlockSpec((1, 1, blk, HD), lambda b, r, n: (b, r, jnp.maximum(n - 1, 0), 0))
    tcur = pl.BlockSpec((1, 1, HD, blk), lambda b, r, n: (b, r, 0, n))
    tprev = pl.BlockSpec((1, 1, HD, blk), lambda b, r, n: (b, r, 0, jnp.maximum(n - 1, 0)))
    return pl.pallas_call(
        _attn_kernel,
        grid=(B, dil, nb),
        in_specs=[tcur, cur, prev, tcur, tprev,
                  pl.BlockSpec((ATT_HEADS, 2, 2 * blk, blk), lambda b, r, n: (gi, 0, 0, 0))],
        out_specs=[cur, pl.BlockSpec((1, 1, blk, blk), lambda b, r, n: (b, r, n, 0))],
        out_shape=[jax.ShapeDtypeStruct((B, dil, L, HD), F32),
                   jax.ShapeDtypeStruct((B, dil, L, blk), F32)],
        scratch_shapes=[pltpu.VMEM((ATT_HEADS, blk if nb == 1 else 2 * blk, blk), F32),
                        pltpu.VMEM((ATT_HEADS, blk if nb == 1 else 2 * blk, blk), BF16)],
        compiler_params=_params(("parallel", "parallel", "arbitrary")),
        name=f"attn_core_g{gi}",
    )(qt, k, k, vt, vt, bias)


def _attn_out_kernel(x_ref, o0, o1, o2, l0, l1, l2, wo_ref, out_ref, oslab, lslab, ex_ref, *, T):
    lanes = oslab.shape[3]
    HD = o0.shape[3]

    @pl.when((pl.program_id(0) == 0) & (pl.program_id(1) == 0))
    def _():
        h = lax.broadcasted_iota(I32, ex_ref.shape, 0)
        c = lax.broadcasted_iota(I32, ex_ref.shape, 1)
        owner = lax.shift_right_logical(c, HEAD_DIM.bit_length() - 1)
        ex_ref[...] = jnp.where(h == owner, 1.0, 0.0).astype(BF16)

    def token_order(o_ref, l_ref, dil, s):
        if dil == 1:
            return o_ref[0, 0], l_ref[0, 0]
        n = T // dil
        for r in range(dil):
            blk = o_ref[0, r]
            for c in range(HD // lanes):
                oslab[s, c, pl.ds(r, n, stride=dil), :] = blk[:, c * lanes:(c + 1) * lanes]
            lslab[s, pl.ds(r, n, stride=dil), :] = l_ref[0, r]
        o = jnp.concatenate([oslab[s, c] for c in range(HD // lanes)], axis=1)
        return o, lslab[s]

    groups = [token_order(o_ref, l_ref, dil, s)
              for s, ((_, dil), o_ref, l_ref) in enumerate(zip(DILATED_GROUPS, (o0, o1, o2), (l0, l1, l2)))]
    m = jnp.maximum(jnp.maximum(groups[0][1], groups[1][1]), groups[2][1])
    es = [jnp.exp(l - m) for _, l in groups]
    inv = 1.0 / (es[0] + es[1] + es[2])
    att = jnp.zeros((T, HD), F32)
    for (o, _), e in zip(groups, es):
        a = e * inv
        a_hi = a.astype(BF16)
        a_lo = (a - a_hi.astype(F32)).astype(BF16)
        wide = (jnp.dot(a_hi, ex_ref[...], preferred_element_type=F32)
                + jnp.dot(a_lo, ex_ref[...], preferred_element_type=F32))
        att = att + wide * o
    out_ref[0] = x_ref[0] + jnp.dot(att.astype(BF16), wo_ref[...], preferred_element_type=F32)


def _attn_out(x, os, ls, wo_bf):
    B, S, D = x.shape
    HD = wo_bf.shape[0]
    T = OUT_TILE
    lanes = 128
    ospec = lambda dil, w: pl.BlockSpec((1, dil, T // dil, w), lambda b, i: (b, 0, i, 0))
    dils = [dil for _, dil in DILATED_GROUPS]
    xspec = pl.BlockSpec((1, T, D), lambda b, i: (b, i, 0))
    return pl.pallas_call(
        functools.partial(_attn_out_kernel, T=T),
        grid=(B, S // T),
        in_specs=[xspec] + [ospec(d, HD) for d in dils] + [ospec(d, ATT_BLOCK) for d in dils]
                 + [pl.BlockSpec((HD, D), lambda b, i: (0, 0))],
        out_specs=xspec,
        out_shape=jax.ShapeDtypeStruct((B, S, D), F32),
        scratch_shapes=[pltpu.VMEM((len(dils), HD // lanes, T, lanes), F32),
                        pltpu.VMEM((len(dils), T, ATT_BLOCK), F32),
                        pltpu.VMEM((ATT_BLOCK, HD), BF16)],
        compiler_params=_params(("arbitrary", "arbitrary")),
        name="attn_out",
    )(x, *os, *ls, wo_bf)


def _attention_layer(x, g, w_qkv, q_gain, k_gain, w_o, rel_bias, layer):
    HD = ATT_HEADS * HEAD_DIM
    bias = _bias_tables(rel_bias)
    wqkv_bf = _to_bf16(w_qkv, layer, HD)
    wo_bf = _to_bf16(w_o, layer, w_o.shape[2])
    os, ls = [], []
    for gi, (_, dil) in enumerate(DILATED_GROUPS):
        qt, k, vt = _qkv_group(x, g, wqkv_bf, q_gain, k_gain, gi, dil)
        o, l = _attn_group(qt, k, vt, bias, gi)
        os.append(o)
        ls.append(l)
    return _attn_out(x, os, ls, wo_bf)


def kernel(x, norm_mix_g, norm_ffn_g, pool_w, pool_scale, attn_w_qkv, attn_q_gain, attn_k_gain,
           attn_w_o, rel_bias, moe_w_router, moe_b_router, moe_w_gate_up, moe_b_gate_up,
           moe_w_down, moe_b_down):
    B, S, D = x.shape
    depth = norm_mix_g.shape[0]
    for i in range(depth):
        j = i // 2
        if i % 2 == 0:
            x = _pool_layer(x, norm_mix_g[i], pool_w[j], pool_scale[j])
        else:
            x = _attention_layer(x, norm_mix_g[i], attn_w_qkv, attn_q_gain[j], attn_k_gain[j],
                                 attn_w_o, rel_bias, j)
        x = _moe_layer(x.reshape(B * S, D), norm_ffn_g[i], moe_w_router[i], moe_b_router[i],
                       moe_w_gate_up, moe_b_gate_up[i], moe_w_down, moe_b_down[i],
                       i).reshape(B, S, D)
    return x
```

```python
import functools
import math

import numpy as np
import jax
import jax.numpy as jnp
from jax import lax
from jax.experimental import pallas as pl
from jax.experimental.pallas import tpu as pltpu

F32 = jnp.float32
BF16 = jnp.bfloat16
I32 = jnp.int32

EPS = 1e-6
POOL_WINDOWS = (2, 4, 8, 16)
POOL_HALO = 16
DILATED_GROUPS = ((128, 1), (512, 4), (2048, 16))
ATT_HEADS = 16
HEAD_DIM = 64
ATT_BLOCK = 128
N_BUCKETS = 32
REL_MAX_DIST = 2048
NEG_INF = -1e30
N_EXPERTS = 32
TOP_K = 4
SWIGLU_LIMIT = 7.0
SWIGLU_ALPHA = 1.702

V7X_VMEM_LIMIT_BYTES = 56 * 1024 * 1024
MOE_TILE = 512
ROW_TILE = 256
ROUTER_TILE = 512
POOL_TILE = 512
QKV_ROWS = 512
LANES = 128
OUT_TILE = 512


def _rms(xf, g):
    ms = jnp.mean(xf * xf, axis=-1, keepdims=True)
    return xf * lax.rsqrt(ms + EPS) * g


def _params(sem, vmem=None):
    return pltpu.CompilerParams(dimension_semantics=sem,
                                vmem_limit_bytes=vmem or V7X_VMEM_LIMIT_BYTES)


def _pool_kernel(x_ref, halo_ref, g_ref, w_ref, sc_ref, o_ref, *, ts, dg):
    i = pl.program_id(1)
    x = x_ref[0]
    g = g_ref[...]
    h = _rms(x, g)
    hh = _rms(halo_ref[0], g)
    hh = jnp.where(i == 0, 0.0, hh)
    full = jnp.concatenate([hh, h], axis=0)
    pos = i * ts + lax.broadcasted_iota(I32, (ts, 1), 0)
    outs = []
    for gi, w in enumerate(POOL_WINDOWS):
        s = full[:, gi * dg:(gi + 1) * dg]
        sh = 1
        while sh < w:
            s = s + pltpu.roll(s, sh, 0)
            sh *= 2
        s = s[POOL_HALO:]
        cnt = jnp.minimum(pos + 1, w).astype(F32)
        p = s / cnt - h[:, gi * dg:(gi + 1) * dg]
        outs.append(jnp.dot(p.astype(BF16), w_ref[gi].astype(BF16),
                            preferred_element_type=F32))
    y = jnp.concatenate(outs, axis=1) * sc_ref[...]
    o_ref[0] = x + y


def _pool_layer(x, g, w_groups, scale):
    B, S, D = x.shape
    ts = POOL_TILE
    dg = D // len(POOL_WINDOWS)
    hb = ts // POOL_HALO
    return pl.pallas_call(
        functools.partial(_pool_kernel, ts=ts, dg=dg),
        grid=(B, S // ts),
        in_specs=[
            pl.BlockSpec((1, ts, D), lambda b, i: (b, i, 0)),
            pl.BlockSpec((1, POOL_HALO, D), lambda b, i: (b, jnp.maximum(i * hb - 1, 0), 0)),
            pl.BlockSpec((1, D), lambda b, i: (0, 0)),
            pl.BlockSpec((len(POOL_WINDOWS), dg, dg), lambda b, i: (0, 0, 0)),
            pl.BlockSpec((1, D), lambda b, i: (0, 0)),
        ],
        out_specs=pl.BlockSpec((1, ts, D), lambda b, i: (b, i, 0)),
        out_shape=jax.ShapeDtypeStruct((B, S, D), F32),
        compiler_params=_params(("parallel", "parallel")),
        name="pool_mixer",
    )(x, x, g.reshape(1, D), w_groups, scale.reshape(1, D))


def _router_kernel(x_ref, g_ref, wr_ref, br_ref, e_ref, gate_ref, rank_ref, cnt_ref,
                   tri_ref, carry_ref, wcat_ref, *, tt):
    i = pl.program_id(0)
    E = N_EXPERTS

    @pl.when(i == 0)
    def _():
        r = lax.broadcasted_iota(I32, tri_ref.shape, 0)
        c = lax.broadcasted_iota(I32, tri_ref.shape, 1)
        tri_ref[...] = jnp.where(r < c, 1.0, 0.0).astype(BF16)
        carry_ref[...] = jnp.zeros_like(carry_ref)
        w = wr_ref[...]
        w_hi = w.astype(BF16)
        wcat_ref[:, :LANES] = w_hi
        wcat_ref[:, LANES:] = (w - w_hi.astype(F32)).astype(BF16)

    h = _rms(x_ref[...], g_ref[...])
    h_hi = h.astype(BF16)
    h_lo = (h - h_hi.astype(F32)).astype(BF16)
    both = jnp.dot(h_hi, wcat_ref[...], preferred_element_type=F32)
    cross = jnp.dot(h_lo, wcat_ref[:, :LANES], preferred_element_type=F32)
    logits = both[:, :LANES] + (both[:, LANES:] + cross) + br_ref[...]
    nblk = tt // LANES
    l = jnp.concatenate([logits[c * LANES:(c + 1) * LANES].T for c in range(nblk)], axis=1)[:E]
    row = lax.broadcasted_iota(I32, (E, tt), 0).astype(F32)
    vals, sels, idxs = [], [], []
    for k in range(TOP_K):
        m = jnp.max(l, axis=0, keepdims=True)
        idx = jnp.min(jnp.where(l == m, row, float(E)), axis=0, keepdims=True)
        sel = row == idx
        vals.append(m)
        sels.append(sel)
        idxs.append(idx)
        l = jnp.where(sel, -jnp.inf, l)
    ex = [jnp.exp(v - vals[0]) for v in vals]
    den = ex[0] + ex[1] + ex[2] + ex[3]
    multi = jnp.zeros((E, tt), F32)
    for sel in sels:
        multi = multi + jnp.where(sel, 1.0, 0.0)
    base = carry_ref[:, :1]
    parts = []
    for c in range(nblk):
        mc = multi[:, c * LANES:(c + 1) * LANES]
        parts.append(jnp.dot(mc.astype(BF16), tri_ref[...], preferred_element_type=F32) + base)
        base = base + jnp.sum(mc, axis=1, keepdims=True)
    before = jnp.concatenate(parts, axis=1)
    kk = lax.broadcasted_iota(I32, (TOP_K, tt), 0)
    e_out = jnp.zeros((TOP_K, tt), F32)
    g_out = jnp.zeros((TOP_K, tt), F32)
    r_out = jnp.zeros((TOP_K, tt), F32)
    for k in range(TOP_K):
        e_out = jnp.where(kk == k, idxs[k], e_out)
        g_out = jnp.where(kk == k, ex[k] / den, g_out)
        rk = jnp.sum(jnp.where(sels[k], before, 0.0), axis=0, keepdims=True)
        r_out = jnp.where(kk == k, rk, r_out)
    e_ref[...] = e_out.astype(I32)
    gate_ref[...] = g_out
    rank_ref[...] = r_out.astype(I32)
    carry_ref[...] = jnp.broadcast_to(base, carry_ref.shape)
    cnt_ref[...] = carry_ref[...]


def _router(x, g, w_r, b_r):
    N, D = x.shape
    tt = ROUTER_TILE
    E = N_EXPERTS
    w_pad = jnp.pad(w_r, ((0, 0), (0, LANES - E)))
    b_pad = jnp.pad(b_r, (0, LANES - E)).reshape(1, LANES)
    kspec = pl.BlockSpec((TOP_K, tt), lambda i: (0, i))
    return pl.pallas_call(
        functools.partial(_router_kernel, tt=tt),
        grid=(N // tt,),
        in_specs=[
            pl.BlockSpec((tt, D), lambda i: (i, 0)),
            pl.BlockSpec((1, D), lambda i: (0, 0)),
            pl.BlockSpec((D, LANES), lambda i: (0, 0)),
            pl.BlockSpec((1, LANES), lambda i: (0, 0)),
        ],
        out_specs=[kspec, kspec, kspec, pl.BlockSpec((E, LANES), lambda i: (0, 0))],
        out_shape=[
            jax.ShapeDtypeStruct((TOP_K, N), I32),
            jax.ShapeDtypeStruct((TOP_K, N), F32),
            jax.ShapeDtypeStruct((TOP_K, N), I32),
            jax.ShapeDtypeStruct((E, LANES), F32),
        ],
        scratch_shapes=[pltpu.VMEM((LANES, LANES), BF16), pltpu.VMEM((E, LANES), F32),
                        pltpu.VMEM((D, 2 * LANES), BF16)],
        compiler_params=_params(("arbitrary",)),
        name="moe_router",
    )(x, g.reshape(1, D), w_pad, b_pad)


def _to_tile_rows(ref, val):
    n, D = val.shape
    nsub = D // LANES
    for s in range(nsub):
        ref[pl.ds(s, n, stride=nsub), :] = val[:, s * LANES:(s + 1) * LANES]


def _from_tile_rows(ref, n, nsub):
    return jnp.concatenate([ref[pl.ds(s, n, stride=nsub), :] for s in range(nsub)], axis=1)


def _tile_row(ref, r, nsub, count=1):
    return ref.at[pl.ds(pl.multiple_of(r * nsub, nsub), count * nsub)]


def _row_copy_wait(src, dst, sem, times):
    for _ in range(times):
        pltpu.make_async_copy(src, dst.at[pl.ds(0, src.shape[0])], sem).wait()


def _dispatch_kernel(pend_ref, nu_ref, dest_ref, x_ref, g_ref, xpad_ref, hbuf, zbuf, sems, zsem,
                     *, tt, tm, nt, nsub):
    i = pl.program_id(0)
    slot = i % 2

    @pl.when(i == 0)
    def _():
        zbuf[...] = jnp.zeros_like(zbuf)

        def seg_copy(e):
            start = pl.multiple_of(jnp.maximum(pend_ref[e] - tm, 0), tm)
            return pltpu.make_async_copy(zbuf, _tile_row(xpad_ref, start, nsub, tm), zsem)

        def tail_copy(j):
            return pltpu.make_async_copy(zbuf, _tile_row(xpad_ref, (nt - 1 - j) * tm, nsub, tm), zsem)

        for e in range(N_EXPERTS):
            seg_copy(e).start()
        for j in range(N_EXPERTS):
            pl.when(nt - 1 - j >= nu_ref[0])(tail_copy(j).start)
        for e in range(N_EXPERTS):
            seg_copy(e).wait()
        for j in range(N_EXPERTS):
            pl.when(nt - 1 - j >= nu_ref[0])(tail_copy(j).wait)

    hb = hbuf.at[slot]
    _to_tile_rows(hb, _rms(x_ref[...], g_ref[...]))

    def issue(t, carry):
        for k in range(TOP_K):
            d = dest_ref[t * TOP_K + k]
            pltpu.make_async_copy(_tile_row(hb, t, nsub), _tile_row(xpad_ref, d, nsub),
                                  sems.at[slot]).start(priority=k % 2)
        return carry

    lax.fori_loop(0, tt, issue, 0, unroll=8)

    @pl.when(i > 0)
    def _():
        _row_copy_wait(hbuf.at[1 - slot], xpad_ref, sems.at[1 - slot], TOP_K)

    @pl.when(i == pl.num_programs(0) - 1)
    def _():
        _row_copy_wait(hb, xpad_ref, sems.at[slot], TOP_K)


def _dispatch(x, g, dest, pend, n_used, P):
    N, D = x.shape
    tt = ROW_TILE
    tm = MOE_TILE
    nsub = D // LANES
    grid_spec = pltpu.PrefetchScalarGridSpec(
        num_scalar_prefetch=2,
        grid=(N // tt,),
        in_specs=[
            pl.BlockSpec((tt * TOP_K,), lambda i, pe, nu: (i,), memory_space=pltpu.SMEM),
            pl.BlockSpec((tt, D), lambda i, pe, nu: (i, 0)),
            pl.BlockSpec((1, D), lambda i, pe, nu: (0, 0)),
        ],
        out_specs=pl.BlockSpec(memory_space=pl.ANY),
        scratch_shapes=[pltpu.VMEM((2, tt * nsub, LANES), F32), pltpu.VMEM((tm * nsub, LANES), F32),
                        pltpu.SemaphoreType.DMA((2,)), pltpu.SemaphoreType.DMA(())],
    )
    return pl.pallas_call(
        functools.partial(_dispatch_kernel, tt=tt, tm=tm, nt=P // tm, nsub=nsub),
        grid_spec=grid_spec,
        out_shape=jax.ShapeDtypeStruct((P * nsub, LANES), F32),
        compiler_params=_params(("arbitrary",)),
        name="moe_dispatch",
    )(pend, n_used, dest, x, g.reshape(1, D))


def _expert_kernel(te_ref, nu_ref, x_ref, wgu_ref, bgu_ref, wdn_ref, bdn_ref, o_ref,
                   wgu_bf, wdn_bf, *, F, tm, nsub):
    i = pl.program_id(0)

    @pl.when(i < nu_ref[0])
    def _():
        prev = te_ref[jnp.maximum(i - 1, 0)]

        @pl.when((i == 0) | (te_ref[i] != prev))
        def _():
            wgu_bf[...] = wgu_ref[0].astype(BF16)
            wdn_bf[...] = wdn_ref[0].astype(BF16)

        x = _from_tile_rows(x_ref, tm, nsub).astype(BF16)
        gu = jnp.dot(x, wgu_bf[...], preferred_element_type=F32) + bgu_ref[0]
        gate = jnp.minimum(gu[:, :F], SWIGLU_LIMIT)
        up = jnp.clip(gu[:, F:], -SWIGLU_LIMIT, SWIGLU_LIMIT)
        glu = gate * jax.nn.sigmoid(SWIGLU_ALPHA * gate)
        a = ((up + 1.0) * glu).astype(BF16)
        _to_tile_rows(o_ref, jnp.dot(a, wdn_bf[...], preferred_element_type=F32) + bdn_ref[0])

    @pl.when(i >= nu_ref[0])
    def _():
        o_ref[...] = jnp.zeros_like(o_ref)


def _experts(tile_e, n_used, x_pad, w_gu, b_gu, w_dn, b_dn, layer):
    _, E, D, F2 = w_gu.shape
    F = F2 // 2
    tm = MOE_TILE
    nsub = D // LANES
    P = x_pad.shape[0] // nsub
    row = lambda i, te, nu: (jnp.minimum(i, nu[0] - 1), 0)
    exp4 = lambda i, te, nu: (layer, te[i], 0, 0)
    exp3 = lambda i, te, nu: (te[i], 0, 0)
    grid_spec = pltpu.PrefetchScalarGridSpec(
        num_scalar_prefetch=2,
        grid=(P // tm,),
        in_specs=[
            pl.BlockSpec((tm * nsub, LANES), row),
            pl.BlockSpec((None, 1, D, F2), exp4),
            pl.BlockSpec((1, 1, F2), exp3),
            pl.BlockSpec((None, 1, F, D), exp4),
            pl.BlockSpec((1, 1, D), exp3),
        ],
        out_specs=pl.BlockSpec((tm * nsub, LANES), lambda i, te, nu: (i, 0)),
        scratch_shapes=[pltpu.VMEM((D, F2), BF16), pltpu.VMEM((F, D), BF16)],
    )
    return pl.pallas_call(
        functools.partial(_expert_kernel, F=F, tm=tm, nsub=nsub),
        grid_spec=grid_spec,
        out_shape=jax.ShapeDtypeStruct((P * nsub, LANES), F32),
        compiler_params=_params(("arbitrary",)),
        name="moe_experts",
    )(tile_e, n_used, x_pad, w_gu, b_gu.reshape(E, 1, F2), w_dn, b_dn.reshape(E, 1, D))


def _combine_kernel(dest_ref, dnext_ref, x_ref, gate_ref, ypad_ref, o_ref, ybuf, sems, *, tt, nsub):
    i = pl.program_id(0)
    slot = i % 2

    def gather(idx_ref, s):
        def issue(t, carry):
            for k in range(TOP_K):
                d = idx_ref[t * TOP_K + k]
                pltpu.make_async_copy(_tile_row(ypad_ref, d, nsub), _tile_row(ybuf.at[s, k], t, nsub),
                                      sems.at[s]).start(priority=k % 2)
            return carry

        lax.fori_loop(0, tt, issue, 0, unroll=8)

    @pl.when(i == 0)
    def _():
        gather(dest_ref, slot)

    @pl.when(i + 1 < pl.num_programs(0))
    def _():
        gather(dnext_ref, 1 - slot)

    for k in range(TOP_K):
        _row_copy_wait(ybuf.at[slot, k], ypad_ref, sems.at[slot], 1)
    gate = gate_ref[...]
    x = x_ref[...]
    cols = []
    for c in range(nsub):
        acc = x[:, c * LANES:(c + 1) * LANES]
        for k in range(TOP_K):
            acc = acc + ybuf[slot, k, pl.ds(c, tt, stride=nsub), :] * gate[:, k:k + 1]
        cols.append(acc)
    o_ref[...] = jnp.concatenate(cols, axis=1)


def _combine(x, gate, dest, y_pad):
    N, D = x.shape
    tt = ROW_TILE
    nsub = D // LANES
    last = N // tt - 1
    return pl.pallas_call(
        functools.partial(_combine_kernel, tt=tt, nsub=nsub),
        grid=(N // tt,),
        in_specs=[
            pl.BlockSpec((tt * TOP_K,), lambda i: (i,), memory_space=pltpu.SMEM),
            pl.BlockSpec((tt * TOP_K,), lambda i: (jnp.minimum(i + 1, last),), memory_space=pltpu.SMEM),
            pl.BlockSpec((tt, D), lambda i: (i, 0)),
            pl.BlockSpec((tt, TOP_K), lambda i: (i, 0)),
            pl.BlockSpec(memory_space=pl.ANY),
        ],
        out_specs=pl.BlockSpec((tt, D), lambda i: (i, 0)),
        out_shape=jax.ShapeDtypeStruct((N, D), F32),
        scratch_shapes=[pltpu.VMEM((2, TOP_K, tt * nsub, LANES), F32), pltpu.SemaphoreType.DMA((2,))],
        compiler_params=_params(("arbitrary",)),
        name="moe_combine",
    )(dest, dest, x, gate, y_pad)


def _moe_layer(x, g, w_r, b_r, w_gu, b_gu, w_dn, b_dn, layer):
    N, D = x.shape
    E = N_EXPERTS
    tm = MOE_TILE
    e_idx, gate_t, rank, counts = _router(x, g, w_r, b_r)
    counts = counts[:, 0].astype(I32)
    padded = ((counts + tm - 1) // tm) * tm
    pend = jnp.cumsum(padded)
    pstart = pend - padded
    dest = (pstart[e_idx] + rank).T.reshape(-1)
    gate = gate_t.T
    P = ((N * TOP_K + E * (tm - 1) + tm - 1) // tm) * tm
    nt = P // tm
    tiles = jnp.arange(nt, dtype=I32)
    tile_e = jnp.minimum(jnp.sum(tiles[:, None] * tm >= pend[None, :], axis=1), E - 1).astype(I32)
    n_used = (pend[-1] // tm).astype(I32)
    tile_e = jnp.where(tiles < n_used, tile_e, tile_e[n_used - 1])
    x_pad = _dispatch(x, g, dest, pend.astype(I32), n_used.reshape(1), P)
    y_pad = _experts(tile_e, n_used.reshape(1), x_pad, w_gu, b_gu, w_dn, b_dn, layer)
    return _combine(x, gate, dest, y_pad)


def _bucket_maps():
    qi = np.arange(ATT_BLOCK)[None, :]
    kj = np.arange(2 * ATT_BLOCK)[:, None]
    delta = qi + ATT_BLOCK - kj
    buckets, valids = [], []
    max_exact = N_BUCKETS // 2
    for win, dil in DILATED_GROUPS:
        n = np.maximum(delta * dil, 0)
        nf = np.maximum(n, 1).astype(np.float32)
        large = max_exact + (np.log(nf / np.float32(max_exact)) / np.float32(math.log(REL_MAX_DIST / max_exact))
                             * np.float32(N_BUCKETS - max_exact)).astype(np.int32)
        large = np.minimum(large, N_BUCKETS - 1)
        buckets.append(np.where(n < max_exact, n, large).astype(np.int32))
        valids.append(((delta >= 0) & (delta <= win // dil)).astype(np.int32))
    return np.stack(buckets), np.stack(valids)


def _bias_kernel(tab_ref, bm_ref, valid_ref, o_ref):
    gh = pl.program_id(0)
    bm = bm_ref[0]
    acc = jnp.zeros(bm.shape, F32)
    for b in range(N_BUCKETS):
        acc = jnp.where(bm == b, tab_ref[b, gh], acc)
    band = jnp.where(valid_ref[0] > 0, acc, NEG_INF)
    key = lax.broadcasted_iota(I32, bm.shape, 0)
    o_ref[0, 0] = band
    o_ref[0, 1] = jnp.where(key >= ATT_BLOCK, band, NEG_INF)


def _bias_tables(rel_bias):
    bm, valid = _bucket_maps()
    GH = len(DILATED_GROUPS) * ATT_HEADS
    blk = (1, 2 * ATT_BLOCK, ATT_BLOCK)
    return pl.pallas_call(
        _bias_kernel,
        grid=(GH,),
        in_specs=[
            pl.BlockSpec(memory_space=pltpu.SMEM),
            pl.BlockSpec(blk, lambda i: (i // ATT_HEADS, 0, 0)),
            pl.BlockSpec(blk, lambda i: (i // ATT_HEADS, 0, 0)),
        ],
        out_specs=pl.BlockSpec((1, 2, 2 * ATT_BLOCK, ATT_BLOCK), lambda i: (i, 0, 0, 0)),
        out_shape=jax.ShapeDtypeStruct((GH, 2, 2 * ATT_BLOCK, ATT_BLOCK), F32),
        compiler_params=_params(("arbitrary",)),
        name="attn_bias",
    )(rel_bias, jnp.asarray(bm), jnp.asarray(valid))


def _cast_kernel(w_ref, o_ref):
    o_ref[...] = w_ref[...].astype(BF16)


def _to_bf16(w, layer, col_block):
    _, rows, cols = w.shape
    return pl.pallas_call(
        _cast_kernel,
        grid=(cols // col_block,),
        in_specs=[pl.BlockSpec((None, rows, col_block), lambda j: (layer, 0, j))],
        out_specs=pl.BlockSpec((rows, col_block), lambda j: (0, j)),
        out_shape=jax.ShapeDtypeStruct((rows, cols), BF16),
        compiler_params=_params(("parallel",)),
        name="cast_bf16",
    )(w)


def _qkv_kernel(x_ref, g_ref, wq_ref, wk_ref, wv_ref, qg_ref, kg_ref, qt_ref, k_ref, vt_ref,
                slab_ref, bd_ref, *, dil, R, nl):
    rc = pl.program_id(2)
    D = x_ref.shape[2]
    lanes = slab_ref.shape[2]
    first = (pl.program_id(0) == 0) & (pl.program_id(1) == 0) & (rc == 0)

    @pl.when(first)
    def _():
        r = lax.broadcasted_iota(I32, bd_ref.shape, 0)
        c = lax.broadcasted_iota(I32, bd_ref.shape, 1)
        hd_shift = HEAD_DIM.bit_length() - 1
        same_head = lax.shift_right_logical(r, hd_shift) == lax.shift_right_logical(c, hd_shift)
        bd_ref[...] = jnp.where(same_head, 1.0, 0.0).astype(BF16)

    if dil == 1:
        x = x_ref[0]
    else:
        @pl.when(rc == 0)
        def _():
            for c in range(D // lanes):
                slab_ref[c] = x_ref[0, :, c * lanes:(c + 1) * lanes]

        pieces = []
        for j in range(R):
            r = rc * R + j
            cols = [slab_ref[c, pl.ds(r, nl, stride=dil), :] for c in range(D // lanes)]
            pieces.append(jnp.concatenate(cols, axis=1))
        x = jnp.concatenate(pieces, axis=0)
    h = _rms(x, g_ref[...]).astype(BF16)
    nb = bd_ref.shape[0]

    def head_norm(y, gain):
        parts = []
        for c0 in range(0, y.shape[1], nb):
            yc = y[:, c0:c0 + nb]
            ssq = jnp.dot((yc * yc).astype(BF16), bd_ref[...], preferred_element_type=F32)
            parts.append(yc * lax.rsqrt(ssq * (1.0 / HEAD_DIM) + EPS))
        return jnp.concatenate(parts, axis=1) * gain

    q = head_norm(jnp.dot(h, wq_ref[...], preferred_element_type=F32), qg_ref[...])
    q = q * (HEAD_DIM ** -0.5)
    k = head_norm(jnp.dot(h, wk_ref[...], preferred_element_type=F32), kg_ref[...])
    v = jnp.dot(h, wv_ref[...], preferred_element_type=F32)
    blk = ATT_BLOCK
    for j in range(R):
        k_ref[0, j] = k[j * nl:(j + 1) * nl].astype(BF16)
        for c in range(nl // blk):
            rows = slice(j * nl + c * blk, j * nl + (c + 1) * blk)
            for hp in range(q.shape[1] // blk):
                cs = slice(hp * blk, (hp + 1) * blk)
                qt_ref[0, j, cs, c * blk:(c + 1) * blk] = q[rows, cs].T.astype(BF16)
                vt_ref[0, j, cs, c * blk:(c + 1) * blk] = v[rows, cs].T.astype(BF16)


def _qkv_group(x, g, w_bf, q_gain, k_gain, gi, dil):
    B, S, D = x.shape
    HD = ATT_HEADS * HEAD_DIM
    G = len(DILATED_GROUPS)
    L = S // dil
    R = min(dil, QKV_ROWS // ATT_BLOCK)
    nl = QKV_ROWS // R
    lanes = 128
    wspec = lambda s: pl.BlockSpec((D, HD), lambda b, l, r: (0, s * G + gi))
    tspec = pl.BlockSpec((1, R, HD, nl), lambda b, l, r: (b, r, 0, l))
    tshape = jax.ShapeDtypeStruct((B, dil, HD, L), BF16)
    return pl.pallas_call(
        functools.partial(_qkv_kernel, dil=dil, R=R, nl=nl),
        grid=(B, L // nl, dil // R),
        in_specs=[
            pl.BlockSpec((1, nl * dil, D), lambda b, l, r: (b, l, 0)),
            pl.BlockSpec((1, D), lambda b, l, r: (0, 0)),
            wspec(0), wspec(1), wspec(2),
            pl.BlockSpec((1, HD), lambda b, l, r: (0, 0)),
            pl.BlockSpec((1, HD), lambda b, l, r: (0, 0)),
        ],
        out_specs=[tspec, pl.BlockSpec((1, R, nl, HD), lambda b, l, r: (b, r, l, 0)), tspec],
        out_shape=[tshape, jax.ShapeDtypeStruct((B, dil, L, HD), BF16), tshape],
        scratch_shapes=[pltpu.VMEM((D // lanes, nl * dil if dil > 1 else 8, lanes), F32),
                        pltpu.VMEM((256, 256), BF16)],
        compiler_params=_params(("arbitrary", "arbitrary", "arbitrary")),
        name=f"attn_qkv_g{gi}",
    )(x, g.reshape(1, D), w_bf, w_bf, w_bf,
      jnp.tile(q_gain, ATT_HEADS).reshape(1, HD), jnp.tile(k_gain, ATT_HEADS).reshape(1, HD))


def _attn_kernel(qt_ref, kc_ref, kp_ref, vtc_ref, vtp_ref, bias_ref, o_ref, l_ref, s_ref, p_ref):
    n = pl.program_id(2)
    first = jnp.where(n == 0, 1, 0)
    blk = ATT_BLOCK
    pair = 2 * HEAD_DIM
    nk = s_ref.shape[1]
    k0 = 2 * blk - nk
    feat = lax.broadcasted_iota(I32, (pair, blk), 0)
    head_rows = [jnp.where(feat < HEAD_DIM, 1.0, 0.0).astype(BF16),
                 jnp.where(feat < HEAD_DIM, 0.0, 1.0).astype(BF16)]
    lse_row = lax.broadcasted_iota(I32, (blk, blk), 0)
    invs = {}
    lse_rows = []

    def scores(pr):
        cs = slice(pr * pair, (pr + 1) * pair)
        qt2 = qt_ref[0, 0, cs, :]
        if k0:
            k2 = kc_ref[0, 0, :, cs]
        else:
            k2 = jnp.concatenate([kp_ref[0, 0, :, cs], kc_ref[0, 0, :, cs]], axis=0)
        for half in range(2):
            qth = qt2 * head_rows[half]
            s_ref[2 * pr + half] = jnp.dot(k2, qth, preferred_element_type=F32)

    def softmax(pr):
        for head in (2 * pr, 2 * pr + 1):
            s = s_ref[head] + bias_ref[head, first, k0:, :]
            m = jnp.max(s, axis=0, keepdims=True)
            p = jnp.exp(s - m)
            den = jnp.sum(p, axis=0, keepdims=True)
            p_ref[head] = p.astype(BF16)
            invs[head] = 1.0 / den
            lse_rows.append(m + jnp.log(den))

    def outputs(pr):
        cs = slice(pr * pair, (pr + 1) * pair)
        if k0:
            vt2 = vtc_ref[0, 0, cs, :]
        else:
            vt2 = jnp.concatenate([vtp_ref[0, 0, cs, :], vtc_ref[0, 0, cs, :]], axis=1)
        halves = []
        for half in range(2):
            head = 2 * pr + half
            vth = vt2[half * HEAD_DIM:(half + 1) * HEAD_DIM]
            ot = jnp.dot(vth, p_ref[head], preferred_element_type=F32)
            halves.append(ot * invs[head])
        o_ref[0, 0, :, cs] = jnp.concatenate(halves, axis=0).T

    npairs = ATT_HEADS // 2
    for pr in range(npairs):
        scores(pr)
    for pr in range(npairs):
        softmax(pr)
    for pr in range(npairs):
        outputs(pr)
    lse_t = jnp.zeros((blk, blk), F32)
    for head, row in enumerate(lse_rows):
        lse_t = jnp.where(lse_row == head, row, lse_t)
    l_ref[0, 0] = lse_t.T


def _attn_group(qt, k, vt, bias, gi):
    B, dil, L, HD = k.shape
    blk = ATT_BLOCK
    nb = L // blk
    cur = pl.BlockSpec((1, 1, blk, HD), lambda b, r, n: (b, r, n, 0))
    prev = pl.BlockSpec((1, 1, blk, HD), lambda b, r, n: (b, r, jnp.maximum(n - 1, 0), 0))
    tcur = pl.BlockSpec((1, 1, HD, blk), lambda b, r, n: (b, r, 0, n))
    tprev = pl.BlockSpec((1, 1, HD, blk), lambda b, r, n: (b, r, 0, jnp.maximum(n - 1, 0)))
    return pl.pallas_call(
        _attn_kernel,
        grid=(B, dil, nb),
        in_specs=[tcur, cur, prev, tcur, tprev,
                  pl.BlockSpec((ATT_HEADS, 2, 2 * blk, blk), lambda b, r, n: (gi, 0, 0, 0))],
        out_specs=[cur, pl.BlockSpec((1, 1, blk, blk), lambda b, r, n: (b, r, n, 0))],
        out_shape=[jax.ShapeDtypeStruct((B, dil, L, HD), F32),
                   jax.ShapeDtypeStruct((B, dil, L, blk), F32)],
        scratch_shapes=[pltpu.VMEM((ATT_HEADS, blk if nb == 1 else 2 * blk, blk), F32),
                        pltpu.VMEM((ATT_HEADS, blk if nb == 1 else 2 * blk, blk), BF16)],
        compiler_params=_params(("parallel", "parallel", "arbitrary")),
        name=f"attn_core_g{gi}",
    )(qt, k, k, vt, vt, bias)


def _attn_out_kernel(x_ref, o0, o1, o2, l0, l1, l2, wo_ref, out_ref, oslab, lslab, ex_ref, *, T):
    lanes = oslab.shape[3]
    HD = o0.shape[3]

    @pl.when((pl.program_id(0) == 0) & (pl.program_id(1) == 0))
    def _():
        h = lax.broadcasted_iota(I32, ex_ref.shape, 0) & (ATT_BLOCK - 1)
        c = lax.broadcasted_iota(I32, ex_ref.shape, 1)
        owner = lax.shift_right_logical(c, HEAD_DIM.bit_length() - 1)
        ex_ref[...] = jnp.where(h == owner, 1.0, 0.0).astype(BF16)

    def token_order(o_ref, l_ref, dil, s):
        if dil == 1:
            return o_ref[0, 0], l_ref[0, 0]
        n = T // dil
        for r in range(dil):
            blk = o_ref[0, r]
            for c in range(HD // lanes):
                oslab[s, c, pl.ds(r, n, stride=dil), :] = blk[:, c * lanes:(c + 1) * lanes]
            lslab[s, pl.ds(r, n, stride=dil), :] = l_ref[0, r]
        o = jnp.concatenate([oslab[s, c] for c in range(HD // lanes)], axis=1)
        return o, lslab[s]

    groups = [token_order(o_ref, l_ref, dil, s)
              for s, ((_, dil), o_ref, l_ref) in enumerate(zip(DILATED_GROUPS, (o0, o1, o2), (l0, l1, l2)))]
    m = jnp.maximum(jnp.maximum(groups[0][1], groups[1][1]), groups[2][1])
    es = [jnp.exp(l - m) for _, l in groups]
    inv = 1.0 / (es[0] + es[1] + es[2])
    att = jnp.zeros((T, HD), F32)
    for (o, _), e in zip(groups, es):
        a = e * inv
        a_hi = a.astype(BF16)
        a_lo = (a - a_hi.astype(F32)).astype(BF16)
        wide = jnp.dot(jnp.concatenate([a_hi, a_lo], axis=1), ex_ref[...], preferred_element_type=F32)
        att = att + wide * o
    out_ref[0] = x_ref[0] + jnp.dot(att.astype(BF16), wo_ref[...], preferred_element_type=F32)


def _attn_out(x, os, ls, wo_bf):
    B, S, D = x.shape
    HD = wo_bf.shape[0]
    T = OUT_TILE
    lanes = 128
    ospec = lambda dil, w: pl.BlockSpec((1, dil, T // dil, w), lambda b, i: (b, 0, i, 0))
    dils = [dil for _, dil in DILATED_GROUPS]
    xspec = pl.BlockSpec((1, T, D), lambda b, i: (b, i, 0))
    return pl.pallas_call(
        functools.partial(_attn_out_kernel, T=T),
        grid=(B, S // T),
        in_specs=[xspec] + [ospec(d, HD) for d in dils] + [ospec(d, ATT_BLOCK) for d in dils]
                 + [pl.BlockSpec((HD, D), lambda b, i: (0, 0))],
        out_specs=xspec,
        out_shape=jax.ShapeDtypeStruct((B, S, D), F32),
        scratch_shapes=[pltpu.VMEM((len(dils), HD // lanes, T, lanes), F32),
                        pltpu.VMEM((len(dils), T, ATT_BLOCK), F32),
                        pltpu.VMEM((2 * ATT_BLOCK, HD), BF16)],
        compiler_params=_params(("arbitrary", "arbitrary")),
        name="attn_out",
    )(x, *os, *ls, wo_bf)


def _attention_layer(x, g, w_qkv, q_gain, k_gain, w_o, rel_bias, layer):
    HD = ATT_HEADS * HEAD_DIM
    bias = _bias_tables(rel_bias)
    wqkv_bf = _to_bf16(w_qkv, layer, HD)
    wo_bf = _to_bf16(w_o, layer, w_o.shape[2])
    os, ls = [], []
    for gi, (_, dil) in enumerate(DILATED_GROUPS):
        qt, k, vt = _qkv_group(x, g, wqkv_bf, q_gain, k_gain, gi, dil)
        o, l = _attn_group(qt, k, vt, bias, gi)
        os.append(o)
        ls.append(l)
    return _attn_out(x, os, ls, wo_bf)


def kernel(x, norm_mix_g, norm_ffn_g, pool_w, pool_scale, attn_w_qkv, attn_q_gain, attn_k_gain,
           attn_w_o, rel_bias, moe_w_router, moe_b_router, moe_w_gate_up, moe_b_gate_up,
           moe_w_down, moe_b_down):
    B, S, D = x.shape
    depth = norm_mix_g.shape[0]
    for i in range(depth):
        j = i // 2
        if i % 2 == 0:
            x = _pool_layer(x, norm_mix_g[i], pool_w[j], pool_scale[j])
        else:
            x = _attention_layer(x, norm_mix_g[i], attn_w_qkv, attn_q_gain[j], attn_k_gain[j],
                                 attn_w_o, rel_bias, j)
        x = _moe_layer(x.reshape(B * S, D), norm_ffn_g[i], moe_w_router[i], moe_b_router[i],
                       moe_w_gate_up, moe_b_gate_up[i], moe_w_down, moe_b_down[i],
                       i).reshape(B, S, D)
    return x
```

```python
import functools
import math

import numpy as np
import jax
import jax.numpy as jnp
from jax import lax
from jax.experimental import pallas as pl
from jax.experimental.pallas import tpu as pltpu

F32 = jnp.float32
BF16 = jnp.bfloat16
I32 = jnp.int32

EPS = 1e-6
POOL_WINDOWS = (2, 4, 8, 16)
POOL_HALO = 16
DILATED_GROUPS = ((128, 1), (512, 4), (2048, 16))
ATT_HEADS = 16
HEAD_DIM = 64
ATT_BLOCK = 128
N_BUCKETS = 32
REL_MAX_DIST = 2048
NEG_INF = -1e30
N_EXPERTS = 32
TOP_K = 4
SWIGLU_LIMIT = 7.0
SWIGLU_ALPHA = 1.702

V7X_VMEM_LIMIT_BYTES = 56 * 1024 * 1024
MOE_TILE = 512
ROW_TILE = 256
ROUTER_TILE = 512
POOL_TILE = 512
QKV_ROWS = 512
LANES = 128
OUT_TILE = 512


def _rms(xf, g):
    ms = jnp.mean(xf * xf, axis=-1, keepdims=True)
    return xf * lax.rsqrt(ms + EPS) * g


def _params(sem, vmem=None):
    return pltpu.CompilerParams(dimension_semantics=sem,
                                vmem_limit_bytes=vmem or V7X_VMEM_LIMIT_BYTES)


def _pool_kernel(x_ref, halo_ref, g_ref, w_ref, sc_ref, o_ref, *, ts, dg):
    i = pl.program_id(1)
    x = x_ref[0]
    g = g_ref[...]
    h = _rms(x, g)
    hh = _rms(halo_ref[0], g)
    hh = jnp.where(i == 0, 0.0, hh)
    full = jnp.concatenate([hh, h], axis=0)
    pos = i * ts + lax.broadcasted_iota(I32, (ts, 1), 0)
    outs = []
    for gi, w in enumerate(POOL_WINDOWS):
        s = full[:, gi * dg:(gi + 1) * dg]
        sh = 1
        while sh < w:
            s = s + pltpu.roll(s, sh, 0)
            sh *= 2
        s = s[POOL_HALO:]
        cnt = jnp.minimum(pos + 1, w).astype(F32)
        p = s / cnt - h[:, gi * dg:(gi + 1) * dg]
        outs.append(jnp.dot(p.astype(BF16), w_ref[gi].astype(BF16),
                            preferred_element_type=F32))
    y = jnp.concatenate(outs, axis=1) * sc_ref[...]
    o_ref[0] = x + y


def _pool_layer(x, g, w_groups, scale):
    B, S, D = x.shape
    ts = POOL_TILE
    dg = D // len(POOL_WINDOWS)
    hb = ts // POOL_HALO
    return pl.pallas_call(
        functools.partial(_pool_kernel, ts=ts, dg=dg),
        grid=(B, S // ts),
        in_specs=[
            pl.BlockSpec((1, ts, D), lambda b, i: (b, i, 0)),
            pl.BlockSpec((1, POOL_HALO, D), lambda b, i: (b, jnp.maximum(i * hb - 1, 0), 0)),
            pl.BlockSpec((1, D), lambda b, i: (0, 0)),
            pl.BlockSpec((len(POOL_WINDOWS), dg, dg), lambda b, i: (0, 0, 0)),
            pl.BlockSpec((1, D), lambda b, i: (0, 0)),
        ],
        out_specs=pl.BlockSpec((1, ts, D), lambda b, i: (b, i, 0)),
        out_shape=jax.ShapeDtypeStruct((B, S, D), F32),
        compiler_params=_params(("parallel", "parallel")),
        name="pool_mixer",
    )(x, x, g.reshape(1, D), w_groups, scale.reshape(1, D))


def _router_kernel(x_ref, g_ref, wr_ref, br_ref, e_ref, gate_ref, rank_ref, cnt_ref, ht_ref,
                   tri_ref, carry_ref, wcat_ref, *, tt):
    i = pl.program_id(0)
    E = N_EXPERTS

    @pl.when(i == 0)
    def _():
        r = lax.broadcasted_iota(I32, tri_ref.shape, 0)
        c = lax.broadcasted_iota(I32, tri_ref.shape, 1)
        tri_ref[...] = jnp.where(r < c, 1.0, 0.0).astype(BF16)
        carry_ref[...] = jnp.zeros_like(carry_ref)
        w = wr_ref[...]
        w_hi = w.astype(BF16)
        wcat_ref[:, :LANES] = w_hi
        wcat_ref[:, LANES:] = (w - w_hi.astype(F32)).astype(BF16)

    h = _rms(x_ref[...], g_ref[...])
    h_hi = h.astype(BF16)
    h_lo = (h - h_hi.astype(F32)).astype(BF16)
    both = jnp.dot(h_hi, wcat_ref[...], preferred_element_type=F32)
    cross = jnp.dot(h_lo, wcat_ref[:, :LANES], preferred_element_type=F32)
    logits = both[:, :LANES] + (both[:, LANES:] + cross) + br_ref[...]
    nblk = tt // LANES
    l = jnp.concatenate([logits[c * LANES:(c + 1) * LANES].T for c in range(nblk)], axis=1)[:E]
    row = lax.broadcasted_iota(I32, (E, tt), 0).astype(F32)
    vals, sels, idxs = [], [], []
    for k in range(TOP_K):
        m = jnp.max(l, axis=0, keepdims=True)
        idx = jnp.min(jnp.where(l == m, row, float(E)), axis=0, keepdims=True)
        sel = row == idx
        vals.append(m)
        sels.append(sel)
        idxs.append(idx)
        l = jnp.where(sel, -jnp.inf, l)
    ex = [jnp.exp(v - vals[0]) for v in vals]
    den = ex[0] + ex[1] + ex[2] + ex[3]
    multi = jnp.zeros((E, tt), F32)
    for sel in sels:
        multi = multi + jnp.where(sel, 1.0, 0.0)
    base = carry_ref[:, :1]
    parts = []
    for c in range(nblk):
        mc = multi[:, c * LANES:(c + 1) * LANES]
        parts.append(jnp.dot(mc.astype(BF16), tri_ref[...], preferred_element_type=F32) + base)
        base = base + jnp.sum(mc, axis=1, keepdims=True)
    before = jnp.concatenate(parts, axis=1)
    kk = lax.broadcasted_iota(I32, (TOP_K, tt), 0)
    e_out = jnp.zeros((TOP_K, tt), F32)
    g_out = jnp.zeros((TOP_K, tt), F32)
    r_out = jnp.zeros((TOP_K, tt), F32)
    for k in range(TOP_K):
        e_out = jnp.where(kk == k, idxs[k], e_out)
        g_out = jnp.where(kk == k, ex[k] / den, g_out)
        rk = jnp.sum(jnp.where(sels[k], before, 0.0), axis=0, keepdims=True)
        r_out = jnp.where(kk == k, rk, r_out)
    e_ref[...] = e_out.astype(I32)
    gate_ref[...] = g_out
    rank_ref[...] = r_out.astype(I32)
    carry_ref[...] = jnp.broadcast_to(base, carry_ref.shape)
    cnt_ref[...] = carry_ref[...]
    _to_tile_rows(ht_ref, h)


def _router(x, g, w_r, b_r):
    N, D = x.shape
    tt = ROUTER_TILE
    E = N_EXPERTS
    w_pad = jnp.pad(w_r, ((0, 0), (0, LANES - E)))
    b_pad = jnp.pad(b_r, (0, LANES - E)).reshape(1, LANES)
    kspec = pl.BlockSpec((TOP_K, tt), lambda i: (0, i))
    return pl.pallas_call(
        functools.partial(_router_kernel, tt=tt),
        grid=(N // tt,),
        in_specs=[
            pl.BlockSpec((tt, D), lambda i: (i, 0)),
            pl.BlockSpec((1, D), lambda i: (0, 0)),
            pl.BlockSpec((D, LANES), lambda i: (0, 0)),
            pl.BlockSpec((1, LANES), lambda i: (0, 0)),
        ],
        out_specs=[kspec, kspec, kspec, pl.BlockSpec((E, LANES), lambda i: (0, 0)),
                   pl.BlockSpec((tt * (D // LANES), LANES), lambda i: (i, 0))],
        out_shape=[
            jax.ShapeDtypeStruct((TOP_K, N), I32),
            jax.ShapeDtypeStruct((TOP_K, N), F32),
            jax.ShapeDtypeStruct((TOP_K, N), I32),
            jax.ShapeDtypeStruct((E, LANES), F32),
            jax.ShapeDtypeStruct((N * (D // LANES), LANES), F32),
        ],
        scratch_shapes=[pltpu.VMEM((LANES, LANES), BF16), pltpu.VMEM((E, LANES), F32),
                        pltpu.VMEM((D, 2 * LANES), BF16)],
        compiler_params=_params(("arbitrary",)),
        name="moe_router",
    )(x, g.reshape(1, D), w_pad, b_pad)


def _to_tile_rows(ref, val):
    n, D = val.shape
    nsub = D // LANES
    for s in range(nsub):
        ref[pl.ds(s, n, stride=nsub), :] = val[:, s * LANES:(s + 1) * LANES]


def _from_tile_rows(ref, n, nsub):
    return jnp.concatenate([ref[pl.ds(s, n, stride=nsub), :] for s in range(nsub)], axis=1)


def _tile_row(ref, r, nsub, count=1):
    return ref.at[pl.ds(pl.multiple_of(r * nsub, nsub), count * nsub)]


def _row_copy_wait(src, dst, sem, times):
    for _ in range(times):
        pltpu.make_async_copy(src, dst.at[pl.ds(0, src.shape[0])], sem).wait()


def _dispatch_kernel(pend_ref, nu_ref, dest_ref, h_ref, xpad_ref, zbuf, sems, zsem,
                     *, tt, tm, nt, nsub):
    i = pl.program_id(0)
    slot = i % 2

    @pl.when(i == 0)
    def _():
        zbuf[...] = jnp.zeros_like(zbuf)

        def seg_copy(e):
            start = pl.multiple_of(jnp.maximum(pend_ref[e] - tm, 0), tm)
            return pltpu.make_async_copy(zbuf, _tile_row(xpad_ref, start, nsub, tm), zsem)

        def tail_copy(j):
            return pltpu.make_async_copy(zbuf, _tile_row(xpad_ref, (nt - 1 - j) * tm, nsub, tm), zsem)

        for e in range(N_EXPERTS):
            seg_copy(e).start()
        for j in range(N_EXPERTS):
            pl.when(nt - 1 - j >= nu_ref[0])(tail_copy(j).start)
        for e in range(N_EXPERTS):
            seg_copy(e).wait()
        for j in range(N_EXPERTS):
            pl.when(nt - 1 - j >= nu_ref[0])(tail_copy(j).wait)

    group = 4

    def issue(gi, carry):
        t0 = gi * group
        ds = [dest_ref[t0 * TOP_K + j] for j in range(group * TOP_K)]
        for j, d in enumerate(ds):
            src = _tile_row(h_ref, i * tt + t0 + j // TOP_K, nsub)
            pltpu.make_async_copy(src, _tile_row(xpad_ref, d, nsub), sems.at[slot]).start(priority=j % 2)
        return carry

    lax.fori_loop(0, tt // group, issue, 0, unroll=2)

    def wait_step(s):
        _row_copy_wait(h_ref.at[pl.ds(0, tt * nsub)], xpad_ref, sems.at[s], TOP_K)

    pl.when(i > 0)(lambda: wait_step(1 - slot))
    pl.when(i == pl.num_programs(0) - 1)(lambda: wait_step(slot))


def _dispatch(h_tiles, dest, pend, n_used, P, nsub):
    N = h_tiles.shape[0] // nsub
    tt = ROW_TILE
    tm = MOE_TILE
    grid_spec = pltpu.PrefetchScalarGridSpec(
        num_scalar_prefetch=2,
        grid=(N // tt,),
        in_specs=[
            pl.BlockSpec((tt * TOP_K,), lambda i, pe, nu: (i,), memory_space=pltpu.SMEM),
            pl.BlockSpec(memory_space=pl.ANY),
        ],
        out_specs=pl.BlockSpec(memory_space=pl.ANY),
        scratch_shapes=[pltpu.VMEM((tm * nsub, LANES), F32),
                        pltpu.SemaphoreType.DMA((2,)), pltpu.SemaphoreType.DMA(())],
    )
    return pl.pallas_call(
        functools.partial(_dispatch_kernel, tt=tt, tm=tm, nt=P // tm, nsub=nsub),
        grid_spec=grid_spec,
        out_shape=jax.ShapeDtypeStruct((P * nsub, LANES), F32),
        compiler_params=_params(("arbitrary",)),
        name="moe_dispatch",
    )(pend, n_used, dest, h_tiles)


def _expert_kernel(te_ref, nu_ref, x_ref, wgu_ref, bgu_ref, wdn_ref, bdn_ref, o_ref,
                   wgu_bf, wdn_bf, *, F, tm, nsub):
    i = pl.program_id(0)

    @pl.when(i < nu_ref[0])
    def _():
        prev = te_ref[jnp.maximum(i - 1, 0)]

        @pl.when((i == 0) | (te_ref[i] != prev))
        def _():
            wgu_bf[...] = wgu_ref[0].astype(BF16)
            wdn_bf[...] = wdn_ref[0].astype(BF16)

        x = _from_tile_rows(x_ref, tm, nsub).astype(BF16)
        gu = jnp.dot(x, wgu_bf[...], preferred_element_type=F32) + bgu_ref[0]
        gate = jnp.minimum(gu[:, :F], SWIGLU_LIMIT)
        up = jnp.clip(gu[:, F:], -SWIGLU_LIMIT, SWIGLU_LIMIT)
        glu = gate * jax.nn.sigmoid(SWIGLU_ALPHA * gate)
        a = ((up + 1.0) * glu).astype(BF16)
        _to_tile_rows(o_ref, jnp.dot(a, wdn_bf[...], preferred_element_type=F32) + bdn_ref[0])

    @pl.when(i >= nu_ref[0])
    def _():
        o_ref[...] = jnp.zeros_like(o_ref)


def _experts(tile_e, n_used, x_pad, w_gu, b_gu, w_dn, b_dn, layer):
    _, E, D, F2 = w_gu.shape
    F = F2 // 2
    tm = MOE_TILE
    nsub = D // LANES
    P = x_pad.shape[0] // nsub
    row = lambda i, te, nu: (jnp.minimum(i, nu[0] - 1), 0)
    exp4 = lambda i, te, nu: (layer, te[i], 0, 0)
    exp3 = lambda i, te, nu: (te[i], 0, 0)
    grid_spec = pltpu.PrefetchScalarGridSpec(
        num_scalar_prefetch=2,
        grid=(P // tm,),
        in_specs=[
            pl.BlockSpec((tm * nsub, LANES), row),
            pl.BlockSpec((None, 1, D, F2), exp4),
            pl.BlockSpec((1, 1, F2), exp3),
            pl.BlockSpec((None, 1, F, D), exp4),
            pl.BlockSpec((1, 1, D), exp3),
        ],
        out_specs=pl.BlockSpec((tm * nsub, LANES), lambda i, te, nu: (i, 0)),
        scratch_shapes=[pltpu.VMEM((D, F2), BF16), pltpu.VMEM((F, D), BF16)],
    )
    return pl.pallas_call(
        functools.partial(_expert_kernel, F=F, tm=tm, nsub=nsub),
        grid_spec=grid_spec,
        out_shape=jax.ShapeDtypeStruct((P * nsub, LANES), F32),
        compiler_params=_params(("arbitrary",)),
        name="moe_experts",
    )(tile_e, n_used, x_pad, w_gu, b_gu.reshape(E, 1, F2), w_dn, b_dn.reshape(E, 1, D))


def _combine_kernel(dest_ref, dnext_ref, x_ref, gate_ref, ypad_ref, o_ref, ybuf, sems, *, tt, nsub):
    i = pl.program_id(0)
    slot = i % 2

    def gather(idx_ref, s):
        def issue(t, carry):
            for k in range(TOP_K):
                d = idx_ref[t * TOP_K + k]
                pltpu.make_async_copy(_tile_row(ypad_ref, d, nsub), _tile_row(ybuf.at[s, k], t, nsub),
                                      sems.at[s]).start(priority=k % 2)
            return carry

        lax.fori_loop(0, tt, issue, 0, unroll=8)

    @pl.when(i == 0)
    def _():
        gather(dest_ref, slot)

    @pl.when(i + 1 < pl.num_programs(0))
    def _():
        gather(dnext_ref, 1 - slot)

    for k in range(TOP_K):
        _row_copy_wait(ybuf.at[slot, k], ypad_ref, sems.at[slot], 1)
    gate = gate_ref[...]
    x = x_ref[...]
    cols = []
    for c in range(nsub):
        acc = x[:, c * LANES:(c + 1) * LANES]
        for k in range(TOP_K):
            acc = acc + ybuf[slot, k, pl.ds(c, tt, stride=nsub), :] * gate[:, k:k + 1]
        cols.append(acc)
    o_ref[...] = jnp.concatenate(cols, axis=1)


def _combine(x, gate, dest, y_pad):
    N, D = x.shape
    tt = ROW_TILE
    nsub = D // LANES
    last = N // tt - 1
    return pl.pallas_call(
        functools.partial(_combine_kernel, tt=tt, nsub=nsub),
        grid=(N // tt,),
        in_specs=[
            pl.BlockSpec((tt * TOP_K,), lambda i: (i,), memory_space=pltpu.SMEM),
            pl.BlockSpec((tt * TOP_K,), lambda i: (jnp.minimum(i + 1, last),), memory_space=pltpu.SMEM),
            pl.BlockSpec((tt, D), lambda i: (i, 0)),
            pl.BlockSpec((tt, TOP_K), lambda i: (i, 0)),
            pl.BlockSpec(memory_space=pl.ANY),
        ],
        out_specs=pl.BlockSpec((tt, D), lambda i: (i, 0)),
        out_shape=jax.ShapeDtypeStruct((N, D), F32),
        scratch_shapes=[pltpu.VMEM((2, TOP_K, tt * nsub, LANES), F32), pltpu.SemaphoreType.DMA((2,))],
        compiler_params=_params(("arbitrary",)),
        name="moe_combine",
    )(dest, dest, x, gate, y_pad)


def _moe_layer(x, g, w_r, b_r, w_gu, b_gu, w_dn, b_dn, layer):
    N, D = x.shape
    E = N_EXPERTS
    tm = MOE_TILE
    e_idx, gate_t, rank, counts, h_tiles = _router(x, g, w_r, b_r)
    counts = counts[:, 0].astype(I32)
    padded = ((counts + tm - 1) // tm) * tm
    pend = jnp.cumsum(padded)
    pstart = pend - padded
    seg = jnp.zeros_like(rank)
    for e in range(E):
        seg = jnp.where(e_idx == e, pstart[e], seg)
    dest = (seg + rank).T.reshape(-1)
    gate = gate_t.T
    P = ((N * TOP_K + E * (tm - 1) + tm - 1) // tm) * tm
    nt = P // tm
    tiles = jnp.arange(nt, dtype=I32)
    tile_e = jnp.minimum(jnp.sum(tiles[:, None] * tm >= pend[None, :], axis=1), E - 1).astype(I32)
    n_used = (pend[-1] // tm).astype(I32)
    tile_e = jnp.where(tiles < n_used, tile_e, tile_e[n_used - 1])
    x_pad = _dispatch(h_tiles, dest, pend.astype(I32), n_used.reshape(1), P, D // LANES)
    y_pad = _experts(tile_e, n_used.reshape(1), x_pad, w_gu, b_gu, w_dn, b_dn, layer)
    return _combine(x, gate, dest, y_pad)


def _bucket_maps():
    qi = np.arange(ATT_BLOCK)[None, :]
    kj = np.arange(2 * ATT_BLOCK)[:, None]
    delta = qi + ATT_BLOCK - kj
    buckets, valids = [], []
    max_exact = N_BUCKETS // 2
    for win, dil in DILATED_GROUPS:
        n = np.maximum(delta * dil, 0)
        nf = np.maximum(n, 1).astype(np.float32)
        large = max_exact + (np.log(nf / np.float32(max_exact)) / np.float32(math.log(REL_MAX_DIST / max_exact))
                             * np.float32(N_BUCKETS - max_exact)).astype(np.int32)
        large = np.minimum(large, N_BUCKETS - 1)
        buckets.append(np.where(n < max_exact, n, large).astype(np.int32))
        valids.append(((delta >= 0) & (delta <= win // dil)).astype(np.int32))
    return np.stack(buckets), np.stack(valids)


def _bias_kernel(tab_ref, bm_ref, valid_ref, o_ref):
    gh = pl.program_id(0)
    bm = bm_ref[0]
    acc = jnp.zeros(bm.shape, F32)
    for b in range(N_BUCKETS):
        acc = jnp.where(bm == b, tab_ref[b, gh], acc)
    band = jnp.where(valid_ref[0] > 0, acc, NEG_INF)
    key = lax.broadcasted_iota(I32, bm.shape, 0)
    o_ref[0, 0] = band
    o_ref[0, 1] = jnp.where(key >= ATT_BLOCK, band, NEG_INF)


def _bias_tables(rel_bias):
    bm, valid = _bucket_maps()
    GH = len(DILATED_GROUPS) * ATT_HEADS
    blk = (1, 2 * ATT_BLOCK, ATT_BLOCK)
    return pl.pallas_call(
        _bias_kernel,
        grid=(GH,),
        in_specs=[
            pl.BlockSpec(memory_space=pltpu.SMEM),
            pl.BlockSpec(blk, lambda i: (i // ATT_HEADS, 0, 0)),
            pl.BlockSpec(blk, lambda i: (i // ATT_HEADS, 0, 0)),
        ],
        out_specs=pl.BlockSpec((1, 2, 2 * ATT_BLOCK, ATT_BLOCK), lambda i: (i, 0, 0, 0)),
        out_shape=jax.ShapeDtypeStruct((GH, 2, 2 * ATT_BLOCK, ATT_BLOCK), F32),
        compiler_params=_params(("arbitrary",)),
        name="attn_bias",
    )(rel_bias, jnp.asarray(bm), jnp.asarray(valid))


def _cast_kernel(w_ref, o_ref):
    o_ref[...] = w_ref[...].astype(BF16)


def _to_bf16(w, layer, col_block):
    _, rows, cols = w.shape
    return pl.pallas_call(
        _cast_kernel,
        grid=(cols // col_block,),
        in_specs=[pl.BlockSpec((None, rows, col_block), lambda j: (layer, 0, j))],
        out_specs=pl.BlockSpec((rows, col_block), lambda j: (0, j)),
        out_shape=jax.ShapeDtypeStruct((rows, cols), BF16),
        compiler_params=_params(("parallel",)),
        name="cast_bf16",
    )(w)


def _qkv_kernel(x_ref, g_ref, wq_ref, wk_ref, wv_ref, qg_ref, kg_ref, qt_ref, k_ref, vt_ref,
                slab_ref, bd_ref, *, dil, R, nl):
    rc = pl.program_id(2)
    D = x_ref.shape[2]
    lanes = slab_ref.shape[2]
    first = (pl.program_id(0) == 0) & (pl.program_id(1) == 0) & (rc == 0)

    @pl.when(first)
    def _():
        r = lax.broadcasted_iota(I32, bd_ref.shape, 0)
        c = lax.broadcasted_iota(I32, bd_ref.shape, 1)
        hd_shift = HEAD_DIM.bit_length() - 1
        same_head = lax.shift_right_logical(r, hd_shift) == lax.shift_right_logical(c, hd_shift)
        bd_ref[...] = jnp.where(same_head, 1.0, 0.0).astype(BF16)

    if dil == 1:
        x = x_ref[0]
    else:
        @pl.when(rc == 0)
        def _():
            for c in range(D // lanes):
                slab_ref[c] = x_ref[0, :, c * lanes:(c + 1) * lanes]

        pieces = []
        for j in range(R):
            r = rc * R + j
            cols = [slab_ref[c, pl.ds(r, nl, stride=dil), :] for c in range(D // lanes)]
            pieces.append(jnp.concatenate(cols, axis=1))
        x = jnp.concatenate(pieces, axis=0)
    h = _rms(x, g_ref[...]).astype(BF16)
    nb = bd_ref.shape[0]

    def head_norm(y, gain):
        parts = []
        for c0 in range(0, y.shape[1], nb):
            yc = y[:, c0:c0 + nb]
            ssq = jnp.dot((yc * yc).astype(BF16), bd_ref[...], preferred_element_type=F32)
            parts.append(yc * lax.rsqrt(ssq * (1.0 / HEAD_DIM) + EPS))
        return jnp.concatenate(parts, axis=1) * gain

    q = head_norm(jnp.dot(h, wq_ref[...], preferred_element_type=F32), qg_ref[...])
    q = q * (HEAD_DIM ** -0.5)
    k = head_norm(jnp.dot(h, wk_ref[...], preferred_element_type=F32), kg_ref[...])
    v = jnp.dot(h, wv_ref[...], preferred_element_type=F32)
    blk = ATT_BLOCK
    for j in range(R):
        k_ref[0, j] = k[j * nl:(j + 1) * nl].astype(BF16)
        for c in range(nl // blk):
            rows = slice(j * nl + c * blk, j * nl + (c + 1) * blk)
            for hp in range(q.shape[1] // blk):
                cs = slice(hp * blk, (hp + 1) * blk)
                qt_ref[0, j, cs, c * blk:(c + 1) * blk] = q[rows, cs].T.astype(BF16)
                vt_ref[0, j, cs, c * blk:(c + 1) * blk] = v[rows, cs].T.astype(BF16)


def _qkv_group(x, g, w_bf, q_gain, k_gain, gi, dil):
    B, S, D = x.shape
    HD = ATT_HEADS * HEAD_DIM
    G = len(DILATED_GROUPS)
    L = S // dil
    R = min(dil, QKV_ROWS // ATT_BLOCK)
    nl = QKV_ROWS // R
    lanes = 128
    wspec = lambda s: pl.BlockSpec((D, HD), lambda b, l, r: (0, s * G + gi))
    tspec = pl.BlockSpec((1, R, HD, nl), lambda b, l, r: (b, r, 0, l))
    tshape = jax.ShapeDtypeStruct((B, dil, HD, L), BF16)
    return pl.pallas_call(
        functools.partial(_qkv_kernel, dil=dil, R=R, nl=nl),
        grid=(B, L // nl, dil // R),
        in_specs=[
            pl.BlockSpec((1, nl * dil, D), lambda b, l, r: (b, l, 0)),
            pl.BlockSpec((1, D), lambda b, l, r: (0, 0)),
            wspec(0), wspec(1), wspec(2),
            pl.BlockSpec((1, HD), lambda b, l, r: (0, 0)),
            pl.BlockSpec((1, HD), lambda b, l, r: (0, 0)),
        ],
        out_specs=[tspec, pl.BlockSpec((1, R, nl, HD), lambda b, l, r: (b, r, l, 0)), tspec],
        out_shape=[tshape, jax.ShapeDtypeStruct((B, dil, L, HD), BF16), tshape],
        scratch_shapes=[pltpu.VMEM((D // lanes, nl * dil if dil > 1 else 8, lanes), F32),
                        pltpu.VMEM((256, 256), BF16)],
        compiler_params=_params(("arbitrary", "arbitrary", "arbitrary")),
        name=f"attn_qkv_g{gi}",
    )(x, g.reshape(1, D), w_bf, w_bf, w_bf,
      jnp.tile(q_gain, ATT_HEADS).reshape(1, HD), jnp.tile(k_gain, ATT_HEADS).reshape(1, HD))


def _attn_kernel(qt_ref, kc_ref, kp_ref, vtc_ref, vtp_ref, bias_ref, o_ref, l_ref, s_ref, p_ref):
    n = pl.program_id(2)
    first = jnp.where(n == 0, 1, 0)
    blk = ATT_BLOCK
    pair = 2 * HEAD_DIM
    nk = s_ref.shape[1]
    k0 = 2 * blk - nk
    feat = lax.broadcasted_iota(I32, (pair, blk), 0)
    head_rows = [jnp.where(feat < HEAD_DIM, 1.0, 0.0).astype(BF16),
                 jnp.where(feat < HEAD_DIM, 0.0, 1.0).astype(BF16)]
    lse_row = lax.broadcasted_iota(I32, (blk, blk), 0)
    invs = {}
    lse_rows = []

    def scores(pr):
        cs = slice(pr * pair, (pr + 1) * pair)
        qt2 = qt_ref[0, 0, cs, :]
        if k0:
            k2 = kc_ref[0, 0, :, cs]
        else:
            k2 = jnp.concatenate([kp_ref[0, 0, :, cs], kc_ref[0, 0, :, cs]], axis=0)
        for half in range(2):
            qth = qt2 * head_rows[half]
            s_ref[2 * pr + half] = jnp.dot(k2, qth, preferred_element_type=F32)

    def softmax(pr):
        for head in (2 * pr, 2 * pr + 1):
            s = s_ref[head] + bias_ref[head, first, k0:, :]
            m = jnp.max(s, axis=0, keepdims=True)
            p = jnp.exp(s - m)
            den = jnp.sum(p, axis=0, keepdims=True)
            p_ref[head] = p.astype(BF16)
            invs[head] = 1.0 / den
            lse_rows.append(m + jnp.log(den))

    def outputs(pr):
        cs = slice(pr * pair, (pr + 1) * pair)
        if k0:
            vt2 = vtc_ref[0, 0, cs, :]
        else:
            vt2 = jnp.concatenate([vtp_ref[0, 0, cs, :], vtc_ref[0, 0, cs, :]], axis=1)
        halves = []
        for half in range(2):
            head = 2 * pr + half
            vth = vt2[half * HEAD_DIM:(half + 1) * HEAD_DIM]
            ot = jnp.dot(vth, p_ref[head], preferred_element_type=F32)
            halves.append(ot * invs[head])
        o_ref[0, 0, :, cs] = jnp.concatenate(halves, axis=0).T

    npairs = ATT_HEADS // 2
    for pr in range(npairs):
        scores(pr)
    for pr in range(npairs):
        softmax(pr)
    for pr in range(npairs):
        outputs(pr)
    lse_t = jnp.zeros((blk, blk), F32)
    for head, row in enumerate(lse_rows):
        lse_t = jnp.where(lse_row == head, row, lse_t)
    l_ref[0, 0] = lse_t.T


def _attn_group(qt, k, vt, bias, gi):
    B, dil, L, HD = k.shape
    blk = ATT_BLOCK
    nb = L // blk
    cur = pl.BlockSpec((1, 1, blk, HD), lambda b, r, n: (b, r, n, 0))
    prev = pl.BlockSpec((1, 1, blk, HD), lambda b, r, n: (b, r, jnp.maximum(n - 1, 0), 0))
    tcur = pl.BlockSpec((1, 1, HD, blk), lambda b, r, n: (b, r, 0, n))
    tprev = pl.BlockSpec((1, 1, HD, blk), lambda b, r, n: (b, r, 0, jnp.maximum(n - 1, 0)))
    return pl.pallas_call(
        _attn_kernel,
        grid=(B, dil, nb),
        in_specs=[tcur, cur, prev, tcur, tprev,
                  pl.BlockSpec((ATT_HEADS, 2, 2 * blk, blk), lambda b, r, n: (gi, 0, 0, 0))],
        out_specs=[cur, pl.BlockSpec((1, 1, blk, blk), lambda b, r, n: (b, r, n, 0))],
        out_shape=[jax.ShapeDtypeStruct((B, dil, L, HD), F32),
                   jax.ShapeDtypeStruct((B, dil, L, blk), F32)],
        scratch_shapes=[pltpu.VMEM((ATT_HEADS, blk if nb == 1 else 2 * blk, blk), F32),
                        pltpu.VMEM((ATT_HEADS, blk if nb == 1 else 2 * blk, blk), BF16)],
        compiler_params=_params(("parallel", "parallel", "arbitrary")),
        name=f"attn_core_g{gi}",
    )(qt, k, k, vt, vt, bias)


def _attn_out_kernel(x_ref, o0, o1, o2, l0, l1, l2, wo_ref, out_ref, oslab, lslab, ex_ref, *, T):
    lanes = oslab.shape[3]
    HD = o0.shape[3]

    @pl.when((pl.program_id(0) == 0) & (pl.program_id(1) == 0))
    def _():
        h = lax.broadcasted_iota(I32, ex_ref.shape, 0) & (ATT_BLOCK - 1)
        c = lax.broadcasted_iota(I32, ex_ref.shape, 1)
        owner = lax.shift_right_logical(c, HEAD_DIM.bit_length() - 1)
        ex_ref[...] = jnp.where(h == owner, 1.0, 0.0).astype(BF16)

    def token_order(o_ref, l_ref, dil, s):
        if dil == 1:
            return o_ref[0, 0], l_ref[0, 0]
        n = T // dil
        for r in range(dil):
            blk = o_ref[0, r]
            for c in range(HD // lanes):
                oslab[s, c, pl.ds(r, n, stride=dil), :] = blk[:, c * lanes:(c + 1) * lanes]
            lslab[s, pl.ds(r, n, stride=dil), :] = l_ref[0, r]
        o = jnp.concatenate([oslab[s, c] for c in range(HD // lanes)], axis=1)
        return o, lslab[s]

    groups = [token_order(o_ref, l_ref, dil, s)
              for s, ((_, dil), o_ref, l_ref) in enumerate(zip(DILATED_GROUPS, (o0, o1, o2), (l0, l1, l2)))]
    m = jnp.maximum(jnp.maximum(groups[0][1], groups[1][1]), groups[2][1])
    es = [jnp.exp(l - m) for _, l in groups]
    inv = 1.0 / (es[0] + es[1] + es[2])
    att = jnp.zeros((T, HD), F32)
    for (o, _), e in zip(groups, es):
        a = e * inv
        a_hi = a.astype(BF16)
        a_lo = (a - a_hi.astype(F32)).astype(BF16)
        wide = jnp.dot(jnp.concatenate([a_hi, a_lo], axis=1), ex_ref[...], preferred_element_type=F32)
        att = att + wide * o
    out_ref[0] = x_ref[0] + jnp.dot(att.astype(BF16), wo_ref[...], preferred_element_type=F32)


def _attn_out(x, os, ls, wo_bf):
    B, S, D = x.shape
    HD = wo_bf.shape[0]
    T = OUT_TILE
    lanes = 128
    ospec = lambda dil, w: pl.BlockSpec((1, dil, T // dil, w), lambda b, i: (b, 0, i, 0))
    dils = [dil for _, dil in DILATED_GROUPS]
    xspec = pl.BlockSpec((1, T, D), lambda b, i: (b, i, 0))
    return pl.pallas_call(
        functools.partial(_attn_out_kernel, T=T),
        grid=(B, S // T),
        in_specs=[xspec] + [ospec(d, HD) for d in dils] + [ospec(d, ATT_BLOCK) for d in dils]
                 + [pl.BlockSpec((HD, D), lambda b, i: (0, 0))],
        out_specs=xspec,
        out_shape=jax.ShapeDtypeStruct((B, S, D), F32),
        scratch_shapes=[pltpu.VMEM((len(dils), HD // lanes, T, lanes), F32),
                        pltpu.VMEM((len(dils), T, ATT_BLOCK), F32),
                        pltpu.VMEM((2 * ATT_BLOCK, HD), BF16)],
        compiler_params=_params(("arbitrary", "arbitrary")),
        name="attn_out",
    )(x, *os, *ls, wo_bf)


def _attention_layer(x, g, w_qkv, q_gain, k_gain, w_o, rel_bias, layer):
    HD = ATT_HEADS * HEAD_DIM
    bias = _bias_tables(rel_bias)
    wqkv_bf = _to_bf16(w_qkv, layer, HD)
    wo_bf = _to_bf16(w_o, layer, w_o.shape[2])
    os, ls = [], []
    for gi, (_, dil) in enumerate(DILATED_GROUPS):
        qt, k, vt = _qkv_group(x, g, wqkv_bf, q_gain, k_gain, gi, dil)
        o, l = _attn_group(qt, k, vt, bias, gi)
        os.append(o)
        ls.append(l)
    return _attn_out(x, os, ls, wo_bf)


def kernel(x, norm_mix_g, norm_ffn_g, pool_w, pool_scale, attn_w_qkv, attn_q_gain, attn_k_gain,
           attn_w_o, rel_bias, moe_w_router, moe_b_router, moe_w_gate_up, moe_b_gate_up,
           moe_w_down, moe_b_down):
    B, S, D = x.shape
    depth = norm_mix_g.shape[0]
    for i in range(depth):
        j = i // 2
        if i % 2 == 0:
            x = _pool_layer(x, norm_mix_g[i], pool_w[j], pool_scale[j])
        else:
            x = _attention_layer(x, norm_mix_g[i], attn_w_qkv, attn_q_gain[j], attn_k_gain[j],
                                 attn_w_o, rel_bias, j)
        x = _moe_layer(x.reshape(B * S, D), norm_ffn_g[i], moe_w_router[i], moe_b_router[i],
                       moe_w_gate_up, moe_b_gate_up[i], moe_w_down, moe_b_down[i],
                       i).reshape(B, S, D)
    return x
```

```python
import functools
import math

import numpy as np
import jax
import jax.numpy as jnp
from jax import lax
from jax.experimental import pallas as pl
from jax.experimental.pallas import tpu as pltpu

F32 = jnp.float32
BF16 = jnp.bfloat16
I32 = jnp.int32

EPS = 1e-6
POOL_WINDOWS = (2, 4, 8, 16)
POOL_HALO = 16
DILATED_GROUPS = ((128, 1), (512, 4), (2048, 16))
ATT_HEADS = 16
HEAD_DIM = 64
ATT_BLOCK = 128
N_BUCKETS = 32
REL_MAX_DIST = 2048
NEG_INF = -1e30
N_EXPERTS = 32
TOP_K = 4
SWIGLU_LIMIT = 7.0
SWIGLU_ALPHA = 1.702

V7X_VMEM_LIMIT_BYTES = 56 * 1024 * 1024
MOE_TILE = 512
ROW_TILE = 512
ROUTER_TILE = 512
POOL_TILE = 512
QKV_ROWS = 512
LANES = 128
OUT_TILE = 512


def _rms(xf, g):
    ms = jnp.mean(xf * xf, axis=-1, keepdims=True)
    return xf * lax.rsqrt(ms + EPS) * g


def _params(sem, vmem=None):
    return pltpu.CompilerParams(dimension_semantics=sem,
                                vmem_limit_bytes=vmem or V7X_VMEM_LIMIT_BYTES)


def _pool_kernel(x_ref, halo_ref, g_ref, w_ref, sc_ref, o_ref, *, ts, dg):
    i = pl.program_id(1)
    x = x_ref[0]
    g = g_ref[...]
    h = _rms(x, g)
    hh = _rms(halo_ref[0], g)
    hh = jnp.where(i == 0, 0.0, hh)
    full = jnp.concatenate([hh, h], axis=0)
    pos = i * ts + lax.broadcasted_iota(I32, (ts, 1), 0)
    outs = []
    for gi, w in enumerate(POOL_WINDOWS):
        s = full[:, gi * dg:(gi + 1) * dg]
        sh = 1
        while sh < w:
            s = s + pltpu.roll(s, sh, 0)
            sh *= 2
        s = s[POOL_HALO:]
        cnt = jnp.minimum(pos + 1, w).astype(F32)
        p = s / cnt - h[:, gi * dg:(gi + 1) * dg]
        outs.append(jnp.dot(p.astype(BF16), w_ref[gi].astype(BF16),
                            preferred_element_type=F32))
    y = jnp.concatenate(outs, axis=1) * sc_ref[...]
    o_ref[0] = x + y


def _pool_layer(x, g, w_groups, scale):
    B, S, D = x.shape
    ts = POOL_TILE
    dg = D // len(POOL_WINDOWS)
    hb = ts // POOL_HALO
    return pl.pallas_call(
        functools.partial(_pool_kernel, ts=ts, dg=dg),
        grid=(B, S // ts),
        in_specs=[
            pl.BlockSpec((1, ts, D), lambda b, i: (b, i, 0)),
            pl.BlockSpec((1, POOL_HALO, D), lambda b, i: (b, jnp.maximum(i * hb - 1, 0), 0)),
            pl.BlockSpec((1, D), lambda b, i: (0, 0)),
            pl.BlockSpec((len(POOL_WINDOWS), dg, dg), lambda b, i: (0, 0, 0)),
            pl.BlockSpec((1, D), lambda b, i: (0, 0)),
        ],
        out_specs=pl.BlockSpec((1, ts, D), lambda b, i: (b, i, 0)),
        out_shape=jax.ShapeDtypeStruct((B, S, D), F32),
        compiler_params=_params(("parallel", "parallel")),
        name="pool_mixer",
    )(x, x, g.reshape(1, D), w_groups, scale.reshape(1, D))


def _router_kernel(x_ref, g_ref, wr_ref, br_ref, e_ref, gate_ref, rank_ref, cnt_ref,
                   tri_ref, carry_ref, wcat_ref, *, tt):
    i = pl.program_id(0)
    E = N_EXPERTS

    @pl.when(i == 0)
    def _():
        r = lax.broadcasted_iota(I32, tri_ref.shape, 0)
        c = lax.broadcasted_iota(I32, tri_ref.shape, 1)
        tri_ref[...] = jnp.where(r < c, 1.0, 0.0).astype(BF16)
        carry_ref[...] = jnp.zeros_like(carry_ref)
        w = wr_ref[...]
        w_hi = w.astype(BF16)
        wcat_ref[:, :LANES] = w_hi
        wcat_ref[:, LANES:] = (w - w_hi.astype(F32)).astype(BF16)

    h = _rms(x_ref[...], g_ref[...])
    h_hi = h.astype(BF16)
    h_lo = (h - h_hi.astype(F32)).astype(BF16)
    both = jnp.dot(h_hi, wcat_ref[...], preferred_element_type=F32)
    cross = jnp.dot(h_lo, wcat_ref[:, :LANES], preferred_element_type=F32)
    logits = both[:, :LANES] + (both[:, LANES:] + cross) + br_ref[...]
    nblk = tt // LANES
    l = jnp.concatenate([logits[c * LANES:(c + 1) * LANES].T for c in range(nblk)], axis=1)[:E]
    row = lax.broadcasted_iota(I32, (E, tt), 0).astype(F32)
    vals, sels, idxs = [], [], []
    for k in range(TOP_K):
        m = jnp.max(l, axis=0, keepdims=True)
        idx = jnp.min(jnp.where(l == m, row, float(E)), axis=0, keepdims=True)
        sel = row == idx
        vals.append(m)
        sels.append(sel)
        idxs.append(idx)
        l = jnp.where(sel, -jnp.inf, l)
    ex = [jnp.exp(v - vals[0]) for v in vals]
    den = ex[0] + ex[1] + ex[2] + ex[3]
    multi = jnp.zeros((E, tt), F32)
    for sel in sels:
        multi = multi + jnp.where(sel, 1.0, 0.0)
    base = carry_ref[:, :1]
    parts = []
    for c in range(nblk):
        mc = multi[:, c * LANES:(c + 1) * LANES]
        parts.append(jnp.dot(mc.astype(BF16), tri_ref[...], preferred_element_type=F32) + base)
        base = base + jnp.sum(mc, axis=1, keepdims=True)
    before = jnp.concatenate(parts, axis=1)
    kk = lax.broadcasted_iota(I32, (TOP_K, tt), 0)
    e_out = jnp.zeros((TOP_K, tt), F32)
    g_out = jnp.zeros((TOP_K, tt), F32)
    r_out = jnp.zeros((TOP_K, tt), F32)
    for k in range(TOP_K):
        e_out = jnp.where(kk == k, idxs[k], e_out)
        g_out = jnp.where(kk == k, ex[k] / den, g_out)
        rk = jnp.sum(jnp.where(sels[k], before, 0.0), axis=0, keepdims=True)
        r_out = jnp.where(kk == k, rk, r_out)
    e_ref[...] = e_out.astype(I32)
    gate_ref[...] = g_out
    rank_ref[...] = r_out.astype(I32)
    carry_ref[...] = jnp.broadcast_to(base, carry_ref.shape)
    cnt_ref[...] = carry_ref[...]


def _router(x, g, w_r, b_r):
    N, D = x.shape
    tt = ROUTER_TILE
    E = N_EXPERTS
    w_pad = jnp.pad(w_r, ((0, 0), (0, LANES - E)))
    b_pad = jnp.pad(b_r, (0, LANES - E)).reshape(1, LANES)
    kspec = pl.BlockSpec((TOP_K, tt), lambda i: (0, i))
    return pl.pallas_call(
        functools.partial(_router_kernel, tt=tt),
        grid=(N // tt,),
        in_specs=[
            pl.BlockSpec((tt, D), lambda i: (i, 0)),
            pl.BlockSpec((1, D), lambda i: (0, 0)),
            pl.BlockSpec((D, LANES), lambda i: (0, 0)),
            pl.BlockSpec((1, LANES), lambda i: (0, 0)),
        ],
        out_specs=[kspec, kspec, kspec, pl.BlockSpec((E, LANES), lambda i: (0, 0))],
        out_shape=[
            jax.ShapeDtypeStruct((TOP_K, N), I32),
            jax.ShapeDtypeStruct((TOP_K, N), F32),
            jax.ShapeDtypeStruct((TOP_K, N), I32),
            jax.ShapeDtypeStruct((E, LANES), F32),
        ],
        scratch_shapes=[pltpu.VMEM((LANES, LANES), BF16), pltpu.VMEM((E, LANES), F32),
                        pltpu.VMEM((D, 2 * LANES), BF16)],
        compiler_params=_params(("arbitrary",)),
        name="moe_router",
    )(x, g.reshape(1, D), w_pad, b_pad)


def _to_tile_rows(ref, val):
    n, D = val.shape
    nsub = D // LANES
    for s in range(nsub):
        ref[pl.ds(s, n, stride=nsub), :] = val[:, s * LANES:(s + 1) * LANES]


def _from_tile_rows(ref, n, nsub):
    return jnp.concatenate([ref[pl.ds(s, n, stride=nsub), :] for s in range(nsub)], axis=1)


def _tile_row(ref, r, nsub, count=1):
    return ref.at[pl.ds(pl.multiple_of(r * nsub, nsub), count * nsub)]


def _row_copy_wait(src, dst, sem, times):
    for _ in range(times):
        pltpu.make_async_copy(src, dst.at[pl.ds(0, src.shape[0])], sem).wait()


def _dispatch_kernel(pend_ref, nu_ref, dest_ref, x_ref, g_ref, xpad_ref, hbuf, zbuf, sems, zsem,
                     *, tt, tm, nt, nsub):
    i = pl.program_id(0)
    slot = i % 2

    @pl.when(i == 0)
    def _():
        zbuf[...] = jnp.zeros_like(zbuf)

        def seg_copy(e):
            start = pl.multiple_of(jnp.maximum(pend_ref[e] - tm, 0), tm)
            return pltpu.make_async_copy(zbuf, _tile_row(xpad_ref, start, nsub, tm), zsem)

        def tail_copy(j):
            return pltpu.make_async_copy(zbuf, _tile_row(xpad_ref, (nt - 1 - j) * tm, nsub, tm), zsem)

        for e in range(N_EXPERTS):
            seg_copy(e).start()
        for j in range(N_EXPERTS):
            pl.when(nt - 1 - j >= nu_ref[0])(tail_copy(j).start)
        for e in range(N_EXPERTS):
            seg_copy(e).wait()
        for j in range(N_EXPERTS):
            pl.when(nt - 1 - j >= nu_ref[0])(tail_copy(j).wait)

    hb = hbuf.at[slot]
    _to_tile_rows(hb, _rms(x_ref[...], g_ref[...]))

    def issue(t, carry):
        for k in range(TOP_K):
            d = dest_ref[t * TOP_K + k]
            pltpu.make_async_copy(_tile_row(hb, t, nsub), _tile_row(xpad_ref, d, nsub),
                                  sems.at[slot]).start(priority=k % 2)
        return carry

    lax.fori_loop(0, tt, issue, 0, unroll=8)

    @pl.when(i > 0)
    def _():
        _row_copy_wait(hbuf.at[1 - slot], xpad_ref, sems.at[1 - slot], TOP_K)

    @pl.when(i == pl.num_programs(0) - 1)
    def _():
        _row_copy_wait(hb, xpad_ref, sems.at[slot], TOP_K)


def _dispatch(x, g, dest, pend, n_used, P):
    N, D = x.shape
    tt = ROW_TILE
    tm = MOE_TILE
    nsub = D // LANES
    grid_spec = pltpu.PrefetchScalarGridSpec(
        num_scalar_prefetch=2,
        grid=(N // tt,),
        in_specs=[
            pl.BlockSpec((tt * TOP_K,), lambda i, pe, nu: (i,), memory_space=pltpu.SMEM),
            pl.BlockSpec((tt, D), lambda i, pe, nu: (i, 0)),
            pl.BlockSpec((1, D), lambda i, pe, nu: (0, 0)),
        ],
        out_specs=pl.BlockSpec(memory_space=pl.ANY),
        scratch_shapes=[pltpu.VMEM((2, tt * nsub, LANES), F32), pltpu.VMEM((tm * nsub, LANES), F32),
                        pltpu.SemaphoreType.DMA((2,)), pltpu.SemaphoreType.DMA(())],
    )
    return pl.pallas_call(
        functools.partial(_dispatch_kernel, tt=tt, tm=tm, nt=P // tm, nsub=nsub),
        grid_spec=grid_spec,
        out_shape=jax.ShapeDtypeStruct((P * nsub, LANES), F32),
        compiler_params=_params(("arbitrary",)),
        name="moe_dispatch",
    )(pend, n_used, dest, x, g.reshape(1, D))


def _expert_kernel(te_ref, nu_ref, x_ref, wgu_ref, bgu_ref, wdn_ref, bdn_ref, o_ref,
                   wgu_bf, wdn_bf, *, F, tm, nsub):
    i = pl.program_id(0)

    @pl.when(i < nu_ref[0])
    def _():
        prev = te_ref[jnp.maximum(i - 1, 0)]

        @pl.when((i == 0) | (te_ref[i] != prev))
        def _():
            wgu_bf[...] = wgu_ref[0].astype(BF16)
            wdn_bf[...] = wdn_ref[0].astype(BF16)

        x = _from_tile_rows(x_ref, tm, nsub).astype(BF16)
        gu = jnp.dot(x, wgu_bf[...], preferred_element_type=F32) + bgu_ref[0]
        gate = jnp.minimum(gu[:, :F], SWIGLU_LIMIT)
        up = jnp.clip(gu[:, F:], -SWIGLU_LIMIT, SWIGLU_LIMIT)
        glu = gate * jax.nn.sigmoid(SWIGLU_ALPHA * gate)
        a = ((up + 1.0) * glu).astype(BF16)
        _to_tile_rows(o_ref, jnp.dot(a, wdn_bf[...], preferred_element_type=F32) + bdn_ref[0])

    @pl.when(i >= nu_ref[0])
    def _():
        o_ref[...] = jnp.zeros_like(o_ref)


def _experts(tile_e, n_used, x_pad, w_gu, b_gu, w_dn, b_dn, layer):
    _, E, D, F2 = w_gu.shape
    F = F2 // 2
    tm = MOE_TILE
    nsub = D // LANES
    P = x_pad.shape[0] // nsub
    row = lambda i, te, nu: (jnp.minimum(i, nu[0] - 1), 0)
    exp4 = lambda i, te, nu: (layer, te[i], 0, 0)
    exp3 = lambda i, te, nu: (te[i], 0, 0)
    grid_spec = pltpu.PrefetchScalarGridSpec(
        num_scalar_prefetch=2,
        grid=(P // tm,),
        in_specs=[
            pl.BlockSpec((tm * nsub, LANES), row),
            pl.BlockSpec((None, 1, D, F2), exp4),
            pl.BlockSpec((1, 1, F2), exp3),
            pl.BlockSpec((None, 1, F, D), exp4),
            pl.BlockSpec((1, 1, D), exp3),
        ],
        out_specs=pl.BlockSpec((tm * nsub, LANES), lambda i, te, nu: (i, 0)),
        scratch_shapes=[pltpu.VMEM((D, F2), BF16), pltpu.VMEM((F, D), BF16)],
    )
    return pl.pallas_call(
        functools.partial(_expert_kernel, F=F, tm=tm, nsub=nsub),
        grid_spec=grid_spec,
        out_shape=jax.ShapeDtypeStruct((P * nsub, LANES), F32),
        compiler_params=_params(("arbitrary",)),
        name="moe_experts",
    )(tile_e, n_used, x_pad, w_gu, b_gu.reshape(E, 1, F2), w_dn, b_dn.reshape(E, 1, D))


def _combine_kernel(dest_ref, dnext_ref, x_ref, gate_ref, ypad_ref, o_ref, ybuf, sems, *, tt, nsub):
    i = pl.program_id(0)
    slot = i % 2

    def gather(idx_ref, s):
        def issue(t, carry):
            for k in range(TOP_K):
                d = idx_ref[t * TOP_K + k]
                pltpu.make_async_copy(_tile_row(ypad_ref, d, nsub), _tile_row(ybuf.at[s, k], t, nsub),
                                      sems.at[s]).start(priority=k % 2)
            return carry

        lax.fori_loop(0, tt, issue, 0, unroll=8)

    @pl.when(i == 0)
    def _():
        gather(dest_ref, slot)

    @pl.when(i + 1 < pl.num_programs(0))
    def _():
        gather(dnext_ref, 1 - slot)

    for k in range(TOP_K):
        _row_copy_wait(ybuf.at[slot, k], ypad_ref, sems.at[slot], 1)
    gate = gate_ref[...]
    x = x_ref[...]
    cols = []
    for c in range(nsub):
        acc = x[:, c * LANES:(c + 1) * LANES]
        for k in range(TOP_K):
            acc = acc + ybuf[slot, k, pl.ds(c, tt, stride=nsub), :] * gate[:, k:k + 1]
        cols.append(acc)
    o_ref[...] = jnp.concatenate(cols, axis=1)


def _combine(x, gate, dest, y_pad):
    N, D = x.shape
    tt = ROW_TILE
    nsub = D // LANES
    last = N // tt - 1
    return pl.pallas_call(
        functools.partial(_combine_kernel, tt=tt, nsub=nsub),
        grid=(N // tt,),
        in_specs=[
            pl.BlockSpec((tt * TOP_K,), lambda i: (i,), memory_space=pltpu.SMEM),
            pl.BlockSpec((tt * TOP_K,), lambda i: (jnp.minimum(i + 1, last),), memory_space=pltpu.SMEM),
            pl.BlockSpec((tt, D), lambda i: (i, 0)),
            pl.BlockSpec((tt, TOP_K), lambda i: (i, 0)),
            pl.BlockSpec(memory_space=pl.ANY),
        ],
        out_specs=pl.BlockSpec((tt, D), lambda i: (i, 0)),
        out_shape=jax.ShapeDtypeStruct((N, D), F32),
        scratch_shapes=[pltpu.VMEM((2, TOP_K, tt * nsub, LANES), F32), pltpu.SemaphoreType.DMA((2,))],
        compiler_params=_params(("arbitrary",)),
        name="moe_combine",
    )(dest, dest, x, gate, y_pad)


def _moe_layer(x, g, w_r, b_r, w_gu, b_gu, w_dn, b_dn, layer):
    N, D = x.shape
    E = N_EXPERTS
    tm = MOE_TILE
    e_idx, gate_t, rank, counts = _router(x, g, w_r, b_r)
    counts = counts[:, 0].astype(I32)
    padded = ((counts + tm - 1) // tm) * tm
    pend = jnp.cumsum(padded)
    pstart = pend - padded
    seg = jnp.zeros_like(rank)
    for e in range(E):
        seg = jnp.where(e_idx == e, pstart[e], seg)
    dest = (seg + rank).T.reshape(-1)
    gate = gate_t.T
    P = ((N * TOP_K + E * (tm - 1) + tm - 1) // tm) * tm
    nt = P // tm
    tiles = jnp.arange(nt, dtype=I32)
    tile_e = jnp.minimum(jnp.sum(tiles[:, None] * tm >= pend[None, :], axis=1), E - 1).astype(I32)
    n_used = (pend[-1] // tm).astype(I32)
    tile_e = jnp.where(tiles < n_used, tile_e, tile_e[n_used - 1])
    x_pad = _dispatch(x, g, dest, pend.astype(I32), n_used.reshape(1), P)
    y_pad = _experts(tile_e, n_used.reshape(1), x_pad, w_gu, b_gu, w_dn, b_dn, layer)
    return _combine(x, gate, dest, y_pad)


def _bucket_maps():
    qi = np.arange(ATT_BLOCK)[None, :]
    kj = np.arange(2 * ATT_BLOCK)[:, None]
    delta = qi + ATT_BLOCK - kj
    buckets, valids = [], []
    max_exact = N_BUCKETS // 2
    for win, dil in DILATED_GROUPS:
        n = np.maximum(delta * dil, 0)
        nf = np.maximum(n, 1).astype(np.float32)
        large = max_exact + (np.log(nf / np.float32(max_exact)) / np.float32(math.log(REL_MAX_DIST / max_exact))
                             * np.float32(N_BUCKETS - max_exact)).astype(np.int32)
        large = np.minimum(large, N_BUCKETS - 1)
        buckets.append(np.where(n < max_exact, n, large).astype(np.int32))
        valids.append(((delta >= 0) & (delta <= win // dil)).astype(np.int32))
    return np.stack(buckets), np.stack(valids)


def _bias_kernel(tab_ref, bm_ref, valid_ref, o_ref):
    gh = pl.program_id(0)
    bm = bm_ref[0]
    acc = jnp.zeros(bm.shape, F32)
    for b in range(N_BUCKETS):
        acc = jnp.where(bm == b, tab_ref[b, gh], acc)
    band = jnp.where(valid_ref[0] > 0, acc, NEG_INF)
    key = lax.broadcasted_iota(I32, bm.shape, 0)
    o_ref[0, 0] = band
    o_ref[0, 1] = jnp.where(key >= ATT_BLOCK, band, NEG_INF)


def _bias_tables(rel_bias):
    bm, valid = _bucket_maps()
    GH = len(DILATED_GROUPS) * ATT_HEADS
    blk = (1, 2 * ATT_BLOCK, ATT_BLOCK)
    return pl.pallas_call(
        _bias_kernel,
        grid=(GH,),
        in_specs=[
            pl.BlockSpec(memory_space=pltpu.SMEM),
            pl.BlockSpec(blk, lambda i: (i // ATT_HEADS, 0, 0)),
            pl.BlockSpec(blk, lambda i: (i // ATT_HEADS, 0, 0)),
        ],
        out_specs=pl.BlockSpec((1, 2, 2 * ATT_BLOCK, ATT_BLOCK), lambda i: (i, 0, 0, 0)),
        out_shape=jax.ShapeDtypeStruct((GH, 2, 2 * ATT_BLOCK, ATT_BLOCK), F32),
        compiler_params=_params(("arbitrary",)),
        name="attn_bias",
    )(rel_bias, jnp.asarray(bm), jnp.asarray(valid))


def _cast_kernel(w_ref, o_ref):
    o_ref[...] = w_ref[...].astype(BF16)


def _to_bf16(w, layer, col_block):
    _, rows, cols = w.shape
    return pl.pallas_call(
        _cast_kernel,
        grid=(cols // col_block,),
        in_specs=[pl.BlockSpec((None, rows, col_block), lambda j: (layer, 0, j))],
        out_specs=pl.BlockSpec((rows, col_block), lambda j: (0, j)),
        out_shape=jax.ShapeDtypeStruct((rows, cols), BF16),
        compiler_params=_params(("parallel",)),
        name="cast_bf16",
    )(w)


def _qkv_kernel(x_ref, g_ref, wq_ref, wk_ref, wv_ref, qg_ref, kg_ref, qt_ref, k_ref, vt_ref,
                slab_ref, bd_ref, *, dil, R, nl):
    rc = pl.program_id(2)
    D = x_ref.shape[2]
    lanes = slab_ref.shape[2]
    first = (pl.program_id(0) == 0) & (pl.program_id(1) == 0) & (rc == 0)

    @pl.when(first)
    def _():
        r = lax.broadcasted_iota(I32, bd_ref.shape, 0)
        c = lax.broadcasted_iota(I32, bd_ref.shape, 1)
        hd_shift = HEAD_DIM.bit_length() - 1
        same_head = lax.shift_right_logical(r, hd_shift) == lax.shift_right_logical(c, hd_shift)
        bd_ref[...] = jnp.where(same_head, 1.0, 0.0).astype(BF16)

    if dil == 1:
        x = x_ref[0]
    else:
        @pl.when(rc == 0)
        def _():
            for c in range(D // lanes):
                slab_ref[c] = x_ref[0, :, c * lanes:(c + 1) * lanes]

        pieces = []
        for j in range(R):
            r = rc * R + j
            cols = [slab_ref[c, pl.ds(r, nl, stride=dil), :] for c in range(D // lanes)]
            pieces.append(jnp.concatenate(cols, axis=1))
        x = jnp.concatenate(pieces, axis=0)
    h = _rms(x, g_ref[...]).astype(BF16)
    nb = bd_ref.shape[0]

    def head_norm(y, gain):
        parts = []
        for c0 in range(0, y.shape[1], nb):
            yc = y[:, c0:c0 + nb]
            ssq = jnp.dot((yc * yc).astype(BF16), bd_ref[...], preferred_element_type=F32)
            parts.append(yc * lax.rsqrt(ssq * (1.0 / HEAD_DIM) + EPS))
        return jnp.concatenate(parts, axis=1) * gain

    q = head_norm(jnp.dot(h, wq_ref[...], preferred_element_type=F32), qg_ref[...])
    q = q * (HEAD_DIM ** -0.5)
    k = head_norm(jnp.dot(h, wk_ref[...], preferred_element_type=F32), kg_ref[...])
    v = jnp.dot(h, wv_ref[...], preferred_element_type=F32)
    blk = ATT_BLOCK
    for j in range(R):
        k_ref[0, j] = k[j * nl:(j + 1) * nl].astype(BF16)
        for c in range(nl // blk):
            rows = slice(j * nl + c * blk, j * nl + (c + 1) * blk)
            for hp in range(q.shape[1] // blk):
                cs = slice(hp * blk, (hp + 1) * blk)
                qt_ref[0, j, cs, c * blk:(c + 1) * blk] = q[rows, cs].T.astype(BF16)
                vt_ref[0, j, cs, c * blk:(c + 1) * blk] = v[rows, cs].T.astype(BF16)


def _qkv_group(x, g, w_bf, q_gain, k_gain, gi, dil):
    B, S, D = x.shape
    HD = ATT_HEADS * HEAD_DIM
    G = len(DILATED_GROUPS)
    L = S // dil
    R = min(dil, QKV_ROWS // ATT_BLOCK)
    nl = QKV_ROWS // R
    lanes = 128
    wspec = lambda s: pl.BlockSpec((D, HD), lambda b, l, r: (0, s * G + gi))
    tspec = pl.BlockSpec((1, R, HD, nl), lambda b, l, r: (b, r, 0, l))
    tshape = jax.ShapeDtypeStruct((B, dil, HD, L), BF16)
    return pl.pallas_call(
        functools.partial(_qkv_kernel, dil=dil, R=R, nl=nl),
        grid=(B, L // nl, dil // R),
        in_specs=[
            pl.BlockSpec((1, nl * dil, D), lambda b, l, r: (b, l, 0)),
            pl.BlockSpec((1, D), lambda b, l, r: (0, 0)),
            wspec(0), wspec(1), wspec(2),
            pl.BlockSpec((1, HD), lambda b, l, r: (0, 0)),
            pl.BlockSpec((1, HD), lambda b, l, r: (0, 0)),
        ],
        out_specs=[tspec, pl.BlockSpec((1, R, nl, HD), lambda b, l, r: (b, r, l, 0)), tspec],
        out_shape=[tshape, jax.ShapeDtypeStruct((B, dil, L, HD), BF16), tshape],
        scratch_shapes=[pltpu.VMEM((D // lanes, nl * dil if dil > 1 else 8, lanes), F32),
                        pltpu.VMEM((256, 256), BF16)],
        compiler_params=_params(("arbitrary", "arbitrary", "arbitrary")),
        name=f"attn_qkv_g{gi}",
    )(x, g.reshape(1, D), w_bf, w_bf, w_bf,
      jnp.tile(q_gain, ATT_HEADS).reshape(1, HD), jnp.tile(k_gain, ATT_HEADS).reshape(1, HD))


def _attn_kernel(qt_ref, kc_ref, kp_ref, vtc_ref, vtp_ref, bias_ref, o_ref, l_ref, s_ref, p_ref):
    n = pl.program_id(2)
    first = jnp.where(n == 0, 1, 0)
    blk = ATT_BLOCK
    pair = 2 * HEAD_DIM
    nk = s_ref.shape[1]
    k0 = 2 * blk - nk
    feat = lax.broadcasted_iota(I32, (pair, blk), 0)
    head_rows = [jnp.where(feat < HEAD_DIM, 1.0, 0.0).astype(BF16),
                 jnp.where(feat < HEAD_DIM, 0.0, 1.0).astype(BF16)]
    lse_row = lax.broadcasted_iota(I32, (blk, blk), 0)
    invs = {}
    lse_rows = []

    def scores(pr):
        cs = slice(pr * pair, (pr + 1) * pair)
        qt2 = qt_ref[0, 0, cs, :]
        if k0:
            k2 = kc_ref[0, 0, :, cs]
        else:
            k2 = jnp.concatenate([kp_ref[0, 0, :, cs], kc_ref[0, 0, :, cs]], axis=0)
        for half in range(2):
            qth = qt2 * head_rows[half]
            s_ref[2 * pr + half] = jnp.dot(k2, qth, preferred_element_type=F32)

    def softmax(pr):
        for head in (2 * pr, 2 * pr + 1):
            s = s_ref[head] + bias_ref[head, first, k0:, :]
            m = jnp.max(s, axis=0, keepdims=True)
            p = jnp.exp(s - m)
            den = jnp.sum(p, axis=0, keepdims=True)
            p_ref[head] = p.astype(BF16)
            invs[head] = 1.0 / den
            lse_rows.append(m + jnp.log(den))

    def outputs(pr):
        cs = slice(pr * pair, (pr + 1) * pair)
        if k0:
            vt2 = vtc_ref[0, 0, cs, :]
        else:
            vt2 = jnp.concatenate([vtp_ref[0, 0, cs, :], vtc_ref[0, 0, cs, :]], axis=1)
        halves = []
        for half in range(2):
            head = 2 * pr + half
            vth = vt2[half * HEAD_DIM:(half + 1) * HEAD_DIM]
            ot = jnp.dot(vth, p_ref[head], preferred_element_type=F32)
            halves.append(ot * invs[head])
        o_ref[0, 0, :, cs] = jnp.concatenate(halves, axis=0).T

    npairs = ATT_HEADS // 2
    for pr in range(npairs):
        scores(pr)
    for pr in range(npairs):
        softmax(pr)
    for pr in range(npairs):
        outputs(pr)
    lse_t = jnp.zeros((blk, blk), F32)
    for head, row in enumerate(lse_rows):
        lse_t = jnp.where(lse_row == head, row, lse_t)
    l_ref[0, 0] = lse_t.T


def _attn_group(qt, k, vt, bias, gi):
    B, dil, L, HD = k.shape
    blk = ATT_BLOCK
    nb = L // blk
    cur = pl.BlockSpec((1, 1, blk, HD), lambda b, r, n: (b, r, n, 0))
    prev = pl.BlockSpec((1, 1, blk, HD), lambda b, r, n: (b, r, jnp.maximum(n - 1, 0), 0))
    tcur = pl.BlockSpec((1, 1, HD, blk), lambda b, r, n: (b, r, 0, n))
    tprev = pl.BlockSpec((1, 1, HD, blk), lambda b, r, n: (b, r, 0, jnp.maximum(n - 1, 0)))
    return pl.pallas_call(
        _attn_kernel,
        grid=(B, dil, nb),
        in_specs=[tcur, cur, prev, tcur, tprev,
                  pl.BlockSpec((ATT_HEADS, 2, 2 * blk, blk), lambda b, r, n: (gi, 0, 0, 0))],
        out_specs=[cur, pl.BlockSpec((1, 1, blk, blk), lambda b, r, n: (b, r, n, 0))],
        out_shape=[jax.ShapeDtypeStruct((B, dil, L, HD), F32),
                   jax.ShapeDtypeStruct((B, dil, L, blk), F32)],
        scratch_shapes=[pltpu.VMEM((ATT_HEADS, blk if nb == 1 else 2 * blk, blk), F32),
                        pltpu.VMEM((ATT_HEADS, blk if nb == 1 else 2 * blk, blk), BF16)],
        compiler_params=_params(("parallel", "parallel", "arbitrary")),
        name=f"attn_core_g{gi}",
    )(qt, k, k, vt, vt, bias)


def _attn_out_kernel(x_ref, o0, o1, o2, l0, l1, l2, wo_ref, out_ref, oslab, lslab, ex_ref, *, T):
    lanes = oslab.shape[3]
    HD = o0.shape[3]

    @pl.when((pl.program_id(0) == 0) & (pl.program_id(1) == 0))
    def _():
        h = lax.broadcasted_iota(I32, ex_ref.shape, 0) & (ATT_BLOCK - 1)
        c = lax.broadcasted_iota(I32, ex_ref.shape, 1)
        owner = lax.shift_right_logical(c, HEAD_DIM.bit_length() - 1)
        ex_ref[...] = jnp.where(h == owner, 1.0, 0.0).astype(BF16)

    def token_order(o_ref, l_ref, dil, s):
        if dil == 1:
            return o_ref[0, 0], l_ref[0, 0]
        n = T // dil
        for r in range(dil):
            blk = o_ref[0, r]
            for c in range(HD // lanes):
                oslab[s, c, pl.ds(r, n, stride=dil), :] = blk[:, c * lanes:(c + 1) * lanes]
            lslab[s, pl.ds(r, n, stride=dil), :] = l_ref[0, r]
        o = jnp.concatenate([oslab[s, c] for c in range(HD // lanes)], axis=1)
        return o, lslab[s]

    groups = [token_order(o_ref, l_ref, dil, s)
              for s, ((_, dil), o_ref, l_ref) in enumerate(zip(DILATED_GROUPS, (o0, o1, o2), (l0, l1, l2)))]
    m = jnp.maximum(jnp.maximum(groups[0][1], groups[1][1]), groups[2][1])
    es = [jnp.exp(l - m) for _, l in groups]
    inv = 1.0 / (es[0] + es[1] + es[2])
    att = jnp.zeros((T, HD), F32)
    for (o, _), e in zip(groups, es):
        a = e * inv
        a_hi = a.astype(BF16)
        a_lo = (a - a_hi.astype(F32)).astype(BF16)
        wide = jnp.dot(jnp.concatenate([a_hi, a_lo], axis=1), ex_ref[...], preferred_element_type=F32)
        att = att + wide * o
    out_ref[0] = x_ref[0] + jnp.dot(att.astype(BF16), wo_ref[...], preferred_element_type=F32)


def _attn_out(x, os, ls, wo_bf):
    B, S, D = x.shape
    HD = wo_bf.shape[0]
    T = OUT_TILE
    lanes = 128
    ospec = lambda dil, w: pl.BlockSpec((1, dil, T // dil, w), lambda b, i: (b, 0, i, 0))
    dils = [dil for _, dil in DILATED_GROUPS]
    xspec = pl.BlockSpec((1, T, D), lambda b, i: (b, i, 0))
    return pl.pallas_call(
        functools.partial(_attn_out_kernel, T=T),
        grid=(B, S // T),
        in_specs=[xspec] + [ospec(d, HD) for d in dils] + [ospec(d, ATT_BLOCK) for d in dils]
                 + [pl.BlockSpec((HD, D), lambda b, i: (0, 0))],
        out_specs=xspec,
        out_shape=jax.ShapeDtypeStruct((B, S, D), F32),
        scratch_shapes=[pltpu.VMEM((len(dils), HD // lanes, T, lanes), F32),
                        pltpu.VMEM((len(dils), T, ATT_BLOCK), F32),
                        pltpu.VMEM((2 * ATT_BLOCK, HD), BF16)],
        compiler_params=_params(("arbitrary", "arbitrary")),
        name="attn_out",
    )(x, *os, *ls, wo_bf)


def _attention_layer(x, g, w_qkv, q_gain, k_gain, w_o, rel_bias, layer):
    HD = ATT_HEADS * HEAD_DIM
    bias = _bias_tables(rel_bias)
    wqkv_bf = _to_bf16(w_qkv, layer, HD)
    wo_bf = _to_bf16(w_o, layer, w_o.shape[2])
    os, ls = [], []
    for gi, (_, dil) in enumerate(DILATED_GROUPS):
        qt, k, vt = _qkv_group(x, g, wqkv_bf, q_gain, k_gain, gi, dil)
        o, l = _attn_group(qt, k, vt, bias, gi)
        os.append(o)
        ls.append(l)
    return _attn_out(x, os, ls, wo_bf)


def kernel(x, norm_mix_g, norm_ffn_g, pool_w, pool_scale, attn_w_qkv, attn_q_gain, attn_k_gain,
           attn_w_o, rel_bias, moe_w_router, moe_b_router, moe_w_gate_up, moe_b_gate_up,
           moe_w_down, moe_b_down):
    B, S, D = x.shape
    depth = norm_mix_g.shape[0]
    for i in range(depth):
        j = i // 2
        if i % 2 == 0:
            x = _pool_layer(x, norm_mix_g[i], pool_w[j], pool_scale[j])
        else:
            x = _attention_layer(x, norm_mix_g[i], attn_w_qkv, attn_q_gain[j], attn_k_gain[j],
                                 attn_w_o, rel_bias, j)
        x = _moe_layer(x.reshape(B * S, D), norm_ffn_g[i], moe_w_router[i], moe_b_router[i],
                       moe_w_gate_up, moe_b_gate_up[i], moe_w_down, moe_b_down[i],
                       i).reshape(B, S, D)
    return x
```

```python
import functools
import math

import numpy as np
import jax
import jax.numpy as jnp
from jax import lax
from jax.experimental import pallas as pl
from jax.experimental.pallas import tpu as pltpu

F32 = jnp.float32
BF16 = jnp.bfloat16
I32 = jnp.int32

EPS = 1e-6
POOL_WINDOWS = (2, 4, 8, 16)
POOL_HALO = 16
DILATED_GROUPS = ((128, 1), (512, 4), (2048, 16))
ATT_HEADS = 16
HEAD_DIM = 64
ATT_BLOCK = 128
N_BUCKETS = 32
REL_MAX_DIST = 2048
NEG_INF = -1e30
N_EXPERTS = 32
TOP_K = 4
SWIGLU_LIMIT = 7.0
SWIGLU_ALPHA = 1.702

V7X_VMEM_LIMIT_BYTES = 56 * 1024 * 1024
MOE_TILE = 512
DISPATCH_TILE = 512
COMBINE_TILE = 256
ROUTER_TILE = 512
POOL_TILE = 512
QKV_ROWS = 512
LANES = 128
OUT_TILE = 512


def _rms(xf, g):
    ms = jnp.mean(xf * xf, axis=-1, keepdims=True)
    return xf * lax.rsqrt(ms + EPS) * g


def _params(sem, vmem=None):
    return pltpu.CompilerParams(dimension_semantics=sem,
                                vmem_limit_bytes=vmem or V7X_VMEM_LIMIT_BYTES)


def _pool_kernel(x_ref, halo_ref, g_ref, w_ref, sc_ref, o_ref, *, ts, dg):
    i = pl.program_id(1)
    x = x_ref[0]
    g = g_ref[...]
    h = _rms(x, g)
    hh = _rms(halo_ref[0], g)
    hh = jnp.where(i == 0, 0.0, hh)
    full = jnp.concatenate([hh, h], axis=0)
    pos = i * ts + lax.broadcasted_iota(I32, (ts, 1), 0)
    outs = []
    for gi, w in enumerate(POOL_WINDOWS):
        s = full[:, gi * dg:(gi + 1) * dg]
        sh = 1
        while sh < w:
            s = s + pltpu.roll(s, sh, 0)
            sh *= 2
        s = s[POOL_HALO:]
        cnt = jnp.minimum(pos + 1, w).astype(F32)
        p = s / cnt - h[:, gi * dg:(gi + 1) * dg]
        outs.append(jnp.dot(p.astype(BF16), w_ref[gi].astype(BF16),
                            preferred_element_type=F32))
    y = jnp.concatenate(outs, axis=1) * sc_ref[...]
    o_ref[0] = x + y


def _pool_layer(x, g, w_groups, scale):
    B, S, D = x.shape
    ts = POOL_TILE
    dg = D // len(POOL_WINDOWS)
    hb = ts // POOL_HALO
    return pl.pallas_call(
        functools.partial(_pool_kernel, ts=ts, dg=dg),
        grid=(B, S // ts),
        in_specs=[
            pl.BlockSpec((1, ts, D), lambda b, i: (b, i, 0)),
            pl.BlockSpec((1, POOL_HALO, D), lambda b, i: (b, jnp.maximum(i * hb - 1, 0), 0)),
            pl.BlockSpec((1, D), lambda b, i: (0, 0)),
            pl.BlockSpec((len(POOL_WINDOWS), dg, dg), lambda b, i: (0, 0, 0)),
            pl.BlockSpec((1, D), lambda b, i: (0, 0)),
        ],
        out_specs=pl.BlockSpec((1, ts, D), lambda b, i: (b, i, 0)),
        out_shape=jax.ShapeDtypeStruct((B, S, D), F32),
        compiler_params=_params(("parallel", "parallel")),
        name="pool_mixer",
    )(x, x, g.reshape(1, D), w_groups, scale.reshape(1, D))


def _router_kernel(x_ref, g_ref, wr_ref, br_ref, e_ref, gate_ref, rank_ref, cnt_ref,
                   tri_ref, carry_ref, wcat_ref, *, tt):
    i = pl.program_id(0)
    E = N_EXPERTS

    @pl.when(i == 0)
    def _():
        r = lax.broadcasted_iota(I32, tri_ref.shape, 0)
        c = lax.broadcasted_iota(I32, tri_ref.shape, 1)
        tri_ref[...] = jnp.where(r < c, 1.0, 0.0).astype(BF16)
        carry_ref[...] = jnp.zeros_like(carry_ref)
        w = wr_ref[...]
        w_hi = w.astype(BF16)
        wcat_ref[:, :LANES] = w_hi
        wcat_ref[:, LANES:] = (w - w_hi.astype(F32)).astype(BF16)

    h = _rms(x_ref[...], g_ref[...])
    h_hi = h.astype(BF16)
    h_lo = (h - h_hi.astype(F32)).astype(BF16)
    both = jnp.dot(h_hi, wcat_ref[...], preferred_element_type=F32)
    cross = jnp.dot(h_lo, wcat_ref[:, :LANES], preferred_element_type=F32)
    logits = both[:, :LANES] + (both[:, LANES:] + cross) + br_ref[...]
    nblk = tt // LANES
    l = jnp.concatenate([logits[c * LANES:(c + 1) * LANES].T for c in range(nblk)], axis=1)[:E]
    row = lax.broadcasted_iota(I32, (E, tt), 0).astype(F32)
    vals, sels, idxs = [], [], []
    for k in range(TOP_K):
        m = jnp.max(l, axis=0, keepdims=True)
        idx = jnp.min(jnp.where(l == m, row, float(E)), axis=0, keepdims=True)
        sel = row == idx
        vals.append(m)
        sels.append(sel)
        idxs.append(idx)
        l = jnp.where(sel, -jnp.inf, l)
    ex = [jnp.exp(v - vals[0]) for v in vals]
    den = ex[0] + ex[1] + ex[2] + ex[3]
    multi = jnp.zeros((E, tt), F32)
    for sel in sels:
        multi = multi + jnp.where(sel, 1.0, 0.0)
    base = carry_ref[:, :1]
    parts = []
    for c in range(nblk):
        mc = multi[:, c * LANES:(c + 1) * LANES]
        parts.append(jnp.dot(mc.astype(BF16), tri_ref[...], preferred_element_type=F32) + base)
        base = base + jnp.sum(mc, axis=1, keepdims=True)
    before = jnp.concatenate(parts, axis=1)
    kk = lax.broadcasted_iota(I32, (TOP_K, tt), 0)
    e_out = jnp.zeros((TOP_K, tt), F32)
    g_out = jnp.zeros((TOP_K, tt), F32)
    r_out = jnp.zeros((TOP_K, tt), F32)
    for k in range(TOP_K):
        e_out = jnp.where(kk == k, idxs[k], e_out)
        g_out = jnp.where(kk == k, ex[k] / den, g_out)
        rk = jnp.sum(jnp.where(sels[k], before, 0.0), axis=0, keepdims=True)
        r_out = jnp.where(kk == k, rk, r_out)
    e_ref[...] = e_out.astype(I32)
    gate_ref[...] = g_out
    rank_ref[...] = r_out.astype(I32)
    carry_ref[...] = jnp.broadcast_to(base, carry_ref.shape)
    cnt_ref[...] = carry_ref[...]


def _router(x, g, w_r, b_r):
    N, D = x.shape
    tt = ROUTER_TILE
    E = N_EXPERTS
    w_pad = jnp.pad(w_r, ((0, 0), (0, LANES - E)))
    b_pad = jnp.pad(b_r, (0, LANES - E)).reshape(1, LANES)
    kspec = pl.BlockSpec((TOP_K, tt), lambda i: (0, i))
    return pl.pallas_call(
        functools.partial(_router_kernel, tt=tt),
        grid=(N // tt,),
        in_specs=[
            pl.BlockSpec((tt, D), lambda i: (i, 0)),
            pl.BlockSpec((1, D), lambda i: (0, 0)),
            pl.BlockSpec((D, LANES), lambda i: (0, 0)),
            pl.BlockSpec((1, LANES), lambda i: (0, 0)),
        ],
        out_specs=[kspec, kspec, kspec, pl.BlockSpec((E, LANES), lambda i: (0, 0))],
        out_shape=[
            jax.ShapeDtypeStruct((TOP_K, N), I32),
            jax.ShapeDtypeStruct((TOP_K, N), F32),
            jax.ShapeDtypeStruct((TOP_K, N), I32),
            jax.ShapeDtypeStruct((E, LANES), F32),
        ],
        scratch_shapes=[pltpu.VMEM((LANES, LANES), BF16), pltpu.VMEM((E, LANES), F32),
                        pltpu.VMEM((D, 2 * LANES), BF16)],
        compiler_params=_params(("arbitrary",)),
        name="moe_router",
    )(x, g.reshape(1, D), w_pad, b_pad)


def _to_tile_rows(ref, val):
    n, D = val.shape
    nsub = D // LANES
    for s in range(nsub):
        ref[pl.ds(s, n, stride=nsub), :] = val[:, s * LANES:(s + 1) * LANES]


def _from_tile_rows(ref, n, nsub):
    return jnp.concatenate([ref[pl.ds(s, n, stride=nsub), :] for s in range(nsub)], axis=1)


def _tile_row(ref, r, nsub, count=1):
    return ref.at[pl.ds(pl.multiple_of(r * nsub, nsub), count * nsub)]


def _row_copy_wait(src, dst, sem, times):
    for _ in range(times):
        pltpu.make_async_copy(src, dst.at[pl.ds(0, src.shape[0])], sem).wait()


def _dispatch_kernel(pend_ref, nu_ref, dest_ref, x_ref, g_ref, xpad_ref, hbuf, zbuf, sems, zsem,
                     *, tt, tm, nt, nsub):
    i = pl.program_id(0)
    slot = i % 2

    @pl.when(i == 0)
    def _():
        zbuf[...] = jnp.zeros_like(zbuf)

        def seg_copy(e):
            start = pl.multiple_of(jnp.maximum(pend_ref[e] - tm, 0), tm)
            return pltpu.make_async_copy(zbuf, _tile_row(xpad_ref, start, nsub, tm), zsem)

        def tail_copy(j):
            return pltpu.make_async_copy(zbuf, _tile_row(xpad_ref, (nt - 1 - j) * tm, nsub, tm), zsem)

        for e in range(N_EXPERTS):
            seg_copy(e).start()
        for j in range(N_EXPERTS):
            pl.when(nt - 1 - j >= nu_ref[0])(tail_copy(j).start)
        for e in range(N_EXPERTS):
            seg_copy(e).wait()
        for j in range(N_EXPERTS):
            pl.when(nt - 1 - j >= nu_ref[0])(tail_copy(j).wait)

    hb = hbuf.at[slot]
    _to_tile_rows(hb, _rms(x_ref[...], g_ref[...]))

    def issue(t, carry):
        for k in range(TOP_K):
            d = dest_ref[t * TOP_K + k]
            pltpu.make_async_copy(_tile_row(hb, t, nsub), _tile_row(xpad_ref, d, nsub),
                                  sems.at[slot]).start(priority=k % 2)
        return carry

    lax.fori_loop(0, tt, issue, 0, unroll=64)

    @pl.when(i > 0)
    def _():
        _row_copy_wait(hbuf.at[1 - slot], xpad_ref, sems.at[1 - slot], TOP_K)

    @pl.when(i == pl.num_programs(0) - 1)
    def _():
        _row_copy_wait(hb, xpad_ref, sems.at[slot], TOP_K)


def _dispatch(x, g, dest, pend, n_used, P):
    N, D = x.shape
    tt = DISPATCH_TILE
    tm = MOE_TILE
    nsub = D // LANES
    grid_spec = pltpu.PrefetchScalarGridSpec(
        num_scalar_prefetch=2,
        grid=(N // tt,),
        in_specs=[
            pl.BlockSpec((tt * TOP_K,), lambda i, pe, nu: (i,), memory_space=pltpu.SMEM),
            pl.BlockSpec((tt, D), lambda i, pe, nu: (i, 0)),
            pl.BlockSpec((1, D), lambda i, pe, nu: (0, 0)),
        ],
        out_specs=pl.BlockSpec(memory_space=pl.ANY),
        scratch_shapes=[pltpu.VMEM((2, tt * nsub, LANES), F32), pltpu.VMEM((tm * nsub, LANES), F32),
                        pltpu.SemaphoreType.DMA((2,)), pltpu.SemaphoreType.DMA(())],
    )
    return pl.pallas_call(
        functools.partial(_dispatch_kernel, tt=tt, tm=tm, nt=P // tm, nsub=nsub),
        grid_spec=grid_spec,
        out_shape=jax.ShapeDtypeStruct((P * nsub, LANES), F32),
        compiler_params=_params(("arbitrary",)),
        name="moe_dispatch",
    )(pend, n_used, dest, x, g.reshape(1, D))


def _expert_kernel(te_ref, nu_ref, x_ref, wgu_ref, bgu_ref, wdn_ref, bdn_ref, o_ref,
                   wgu_bf, wdn_bf, *, F, tm, nsub):
    i = pl.program_id(0)

    @pl.when(i < nu_ref[0])
    def _():
        prev = te_ref[jnp.maximum(i - 1, 0)]

        @pl.when((i == 0) | (te_ref[i] != prev))
        def _():
            wgu_bf[...] = wgu_ref[0].astype(BF16)
            wdn_bf[...] = wdn_ref[0].astype(BF16)

        x = _from_tile_rows(x_ref, tm, nsub).astype(BF16)
        gu = jnp.dot(x, wgu_bf[...], preferred_element_type=F32) + bgu_ref[0]
        gate = jnp.minimum(gu[:, :F], SWIGLU_LIMIT)
        up = jnp.clip(gu[:, F:], -SWIGLU_LIMIT, SWIGLU_LIMIT)
        glu = gate * jax.nn.sigmoid(SWIGLU_ALPHA * gate)
        a = ((up + 1.0) * glu).astype(BF16)
        _to_tile_rows(o_ref, jnp.dot(a, wdn_bf[...], preferred_element_type=F32) + bdn_ref[0])

    @pl.when(i >= nu_ref[0])
    def _():
        o_ref[...] = jnp.zeros_like(o_ref)


def _experts(tile_e, n_used, x_pad, w_gu, b_gu, w_dn, b_dn, layer):
    _, E, D, F2 = w_gu.shape
    F = F2 // 2
    tm = MOE_TILE
    nsub = D // LANES
    P = x_pad.shape[0] // nsub
    row = lambda i, te, nu: (jnp.minimum(i, nu[0] - 1), 0)
    exp4 = lambda i, te, nu: (layer, te[i], 0, 0)
    exp3 = lambda i, te, nu: (te[i], 0, 0)
    grid_spec = pltpu.PrefetchScalarGridSpec(
        num_scalar_prefetch=2,
        grid=(P // tm,),
        in_specs=[
            pl.BlockSpec((tm * nsub, LANES), row),
            pl.BlockSpec((None, 1, D, F2), exp4),
            pl.BlockSpec((1, 1, F2), exp3),
            pl.BlockSpec((None, 1, F, D), exp4),
            pl.BlockSpec((1, 1, D), exp3),
        ],
        out_specs=pl.BlockSpec((tm * nsub, LANES), lambda i, te, nu: (i, 0)),
        scratch_shapes=[pltpu.VMEM((D, F2), BF16), pltpu.VMEM((F, D), BF16)],
    )
    return pl.pallas_call(
        functools.partial(_expert_kernel, F=F, tm=tm, nsub=nsub),
        grid_spec=grid_spec,
        out_shape=jax.ShapeDtypeStruct((P * nsub, LANES), F32),
        compiler_params=_params(("arbitrary",)),
        name="moe_experts",
    )(tile_e, n_used, x_pad, w_gu, b_gu.reshape(E, 1, F2), w_dn, b_dn.reshape(E, 1, D))


def _combine_kernel(dest_ref, dnext_ref, x_ref, gate_ref, ypad_ref, o_ref, ybuf, sems, *, tt, nsub):
    i = pl.program_id(0)
    slot = i % 2

    def gather(idx_ref, s):
        def issue(t, carry):
            for k in range(TOP_K):
                d = idx_ref[t * TOP_K + k]
                pltpu.make_async_copy(_tile_row(ypad_ref, d, nsub), _tile_row(ybuf.at[s, k], t, nsub),
                                      sems.at[s]).start(priority=k % 2)
            return carry

        lax.fori_loop(0, tt, issue, 0, unroll=64)

    @pl.when(i == 0)
    def _():
        gather(dest_ref, slot)

    @pl.when(i + 1 < pl.num_programs(0))
    def _():
        gather(dnext_ref, 1 - slot)

    for k in range(TOP_K):
        _row_copy_wait(ybuf.at[slot, k], ypad_ref, sems.at[slot], 1)
    gate = gate_ref[...]
    x = x_ref[...]
    cols = []
    for c in range(nsub):
        acc = x[:, c * LANES:(c + 1) * LANES]
        for k in range(TOP_K):
            acc = acc + ybuf[slot, k, pl.ds(c, tt, stride=nsub), :] * gate[:, k:k + 1]
        cols.append(acc)
    o_ref[...] = jnp.concatenate(cols, axis=1)


def _combine(x, gate, dest, y_pad):
    N, D = x.shape
    tt = COMBINE_TILE
    nsub = D // LANES
    last = N // tt - 1
    return pl.pallas_call(
        functools.partial(_combine_kernel, tt=tt, nsub=nsub),
        grid=(N // tt,),
        in_specs=[
            pl.BlockSpec((tt * TOP_K,), lambda i: (i,), memory_space=pltpu.SMEM),
            pl.BlockSpec((tt * TOP_K,), lambda i: (jnp.minimum(i + 1, last),), memory_space=pltpu.SMEM),
            pl.BlockSpec((tt, D), lambda i: (i, 0)),
            pl.BlockSpec((tt, TOP_K), lambda i: (i, 0)),
            pl.BlockSpec(memory_space=pl.ANY),
        ],
        out_specs=pl.BlockSpec((tt, D), lambda i: (i, 0)),
        out_shape=jax.ShapeDtypeStruct((N, D), F32),
        scratch_shapes=[pltpu.VMEM((2, TOP_K, tt * nsub, LANES), F32), pltpu.SemaphoreType.DMA((2,))],
        compiler_params=_params(("arbitrary",)),
        name="moe_combine",
    )(dest, dest, x, gate, y_pad)


def _moe_layer(x, g, w_r, b_r, w_gu, b_gu, w_dn, b_dn, layer):
    N, D = x.shape
    E = N_EXPERTS
    tm = MOE_TILE
    e_idx, gate_t, rank, counts = _router(x, g, w_r, b_r)
    counts = counts[:, 0].astype(I32)
    padded = ((counts + tm - 1) // tm) * tm
    pend = jnp.cumsum(padded)
    pstart = pend - padded
    seg = jnp.zeros_like(rank)
    for e in range(E):
        seg = jnp.where(e_idx == e, pstart[e], seg)
    dest = (seg + rank).T.reshape(-1)
    gate = gate_t.T
    P = ((N * TOP_K + E * (tm - 1) + tm - 1) // tm) * tm
    nt = P // tm
    tiles = jnp.arange(nt, dtype=I32)
    tile_e = jnp.minimum(jnp.sum(tiles[:, None] * tm >= pend[None, :], axis=1), E - 1).astype(I32)
    n_used = (pend[-1] // tm).astype(I32)
    tile_e = jnp.where(tiles < n_used, tile_e, tile_e[n_used - 1])
    x_pad = _dispatch(x, g, dest, pend.astype(I32), n_used.reshape(1), P)
    y_pad = _experts(tile_e, n_used.reshape(1), x_pad, w_gu, b_gu, w_dn, b_dn, layer)
    return _combine(x, gate, dest, y_pad)


def _bucket_maps():
    qi = np.arange(ATT_BLOCK)[None, :]
    kj = np.arange(2 * ATT_BLOCK)[:, None]
    delta = qi + ATT_BLOCK - kj
    buckets, valids = [], []
    max_exact = N_BUCKETS // 2
    for win, dil in DILATED_GROUPS:
        n = np.maximum(delta * dil, 0)
        nf = np.maximum(n, 1).astype(np.float32)
        large = max_exact + (np.log(nf / np.float32(max_exact)) / np.float32(math.log(REL_MAX_DIST / max_exact))
                             * np.float32(N_BUCKETS - max_exact)).astype(np.int32)
        large = np.minimum(large, N_BUCKETS - 1)
        buckets.append(np.where(n < max_exact, n, large).astype(np.int32))
        valids.append(((delta >= 0) & (delta <= win // dil)).astype(np.int32))
    return np.stack(buckets), np.stack(valids)


def _bias_kernel(tab_ref, bm_ref, valid_ref, o_ref):
    gh = pl.program_id(0)
    bm = bm_ref[0]
    acc = jnp.zeros(bm.shape, F32)
    for b in range(N_BUCKETS):
        acc = jnp.where(bm == b, tab_ref[b, gh], acc)
    band = jnp.where(valid_ref[0] > 0, acc, NEG_INF)
    key = lax.broadcasted_iota(I32, bm.shape, 0)
    o_ref[0, 0] = band
    o_ref[0, 1] = jnp.where(key >= ATT_BLOCK, band, NEG_INF)


def _bias_tables(rel_bias):
    bm, valid = _bucket_maps()
    GH = len(DILATED_GROUPS) * ATT_HEADS
    blk = (1, 2 * ATT_BLOCK, ATT_BLOCK)
    return pl.pallas_call(
        _bias_kernel,
        grid=(GH,),
        in_specs=[
            pl.BlockSpec(memory_space=pltpu.SMEM),
            pl.BlockSpec(blk, lambda i: (i // ATT_HEADS, 0, 0)),
            pl.BlockSpec(blk, lambda i: (i // ATT_HEADS, 0, 0)),
        ],
        out_specs=pl.BlockSpec((1, 2, 2 * ATT_BLOCK, ATT_BLOCK), lambda i: (i, 0, 0, 0)),
        out_shape=jax.ShapeDtypeStruct((GH, 2, 2 * ATT_BLOCK, ATT_BLOCK), F32),
        compiler_params=_params(("arbitrary",)),
        name="attn_bias",
    )(rel_bias, jnp.asarray(bm), jnp.asarray(valid))


def _cast_kernel(w_ref, o_ref):
    o_ref[...] = w_ref[...].astype(BF16)


def _to_bf16(w, layer, col_block):
    _, rows, cols = w.shape
    return pl.pallas_call(
        _cast_kernel,
        grid=(cols // col_block,),
        in_specs=[pl.BlockSpec((None, rows, col_block), lambda j: (layer, 0, j))],
        out_specs=pl.BlockSpec((rows, col_block), lambda j: (0, j)),
        out_shape=jax.ShapeDtypeStruct((rows, cols), BF16),
        compiler_params=_params(("parallel",)),
        name="cast_bf16",
    )(w)


def _qkv_kernel(x_ref, g_ref, wq_ref, wk_ref, wv_ref, qg_ref, kg_ref, qt_ref, k_ref, vt_ref,
                slab_ref, bd_ref, *, dil, R, nl):
    rc = pl.program_id(2)
    D = x_ref.shape[2]
    lanes = slab_ref.shape[2]
    first = (pl.program_id(0) == 0) & (pl.program_id(1) == 0) & (rc == 0)

    @pl.when(first)
    def _():
        r = lax.broadcasted_iota(I32, bd_ref.shape, 0)
        c = lax.broadcasted_iota(I32, bd_ref.shape, 1)
        hd_shift = HEAD_DIM.bit_length() - 1
        same_head = lax.shift_right_logical(r, hd_shift) == lax.shift_right_logical(c, hd_shift)
        bd_ref[...] = jnp.where(same_head, 1.0, 0.0).astype(BF16)

    if dil == 1:
        x = x_ref[0]
    else:
        @pl.when(rc == 0)
        def _():
            for c in range(D // lanes):
                slab_ref[c] = x_ref[0, :, c * lanes:(c + 1) * lanes]

        pieces = []
        for j in range(R):
            r = rc * R + j
            cols = [slab_ref[c, pl.ds(r, nl, stride=dil), :] for c in range(D // lanes)]
            pieces.append(jnp.concatenate(cols, axis=1))
        x = jnp.concatenate(pieces, axis=0)
    h = _rms(x, g_ref[...]).astype(BF16)
    nb = bd_ref.shape[0]

    def head_norm(y, gain):
        parts = []
        for c0 in range(0, y.shape[1], nb):
            yc = y[:, c0:c0 + nb]
            ssq = jnp.dot((yc * yc).astype(BF16), bd_ref[...], preferred_element_type=F32)
            parts.append(yc * lax.rsqrt(ssq * (1.0 / HEAD_DIM) + EPS))
        return jnp.concatenate(parts, axis=1) * gain

    q = head_norm(jnp.dot(h, wq_ref[...], preferred_element_type=F32), qg_ref[...])
    q = q * (HEAD_DIM ** -0.5)
    k = head_norm(jnp.dot(h, wk_ref[...], preferred_element_type=F32), kg_ref[...])
    v = jnp.dot(h, wv_ref[...], preferred_element_type=F32)
    blk = ATT_BLOCK
    for j in range(R):
        k_ref[0, j] = k[j * nl:(j + 1) * nl].astype(BF16)
        for c in range(nl // blk):
            rows = slice(j * nl + c * blk, j * nl + (c + 1) * blk)
            for hp in range(q.shape[1] // blk):
                cs = slice(hp * blk, (hp + 1) * blk)
                qt_ref[0, j, cs, c * blk:(c + 1) * blk] = q[rows, cs].T.astype(BF16)
                vt_ref[0, j, cs, c * blk:(c + 1) * blk] = v[rows, cs].T.astype(BF16)


def _qkv_group(x, g, w_bf, q_gain, k_gain, gi, dil):
    B, S, D = x.shape
    HD = ATT_HEADS * HEAD_DIM
    G = len(DILATED_GROUPS)
    L = S // dil
    R = min(dil, QKV_ROWS // ATT_BLOCK)
    nl = QKV_ROWS // R
    lanes = 128
    wspec = lambda s: pl.BlockSpec((D, HD), lambda b, l, r: (0, s * G + gi))
    tspec = pl.BlockSpec((1, R, HD, nl), lambda b, l, r: (b, r, 0, l))
    tshape = jax.ShapeDtypeStruct((B, dil, HD, L), BF16)
    return pl.pallas_call(
        functools.partial(_qkv_kernel, dil=dil, R=R, nl=nl),
        grid=(B, L // nl, dil // R),
        in_specs=[
            pl.BlockSpec((1, nl * dil, D), lambda b, l, r: (b, l, 0)),
            pl.BlockSpec((1, D), lambda b, l, r: (0, 0)),
            wspec(0), wspec(1), wspec(2),
            pl.BlockSpec((1, HD), lambda b, l, r: (0, 0)),
            pl.BlockSpec((1, HD), lambda b, l, r: (0, 0)),
        ],
        out_specs=[tspec, pl.BlockSpec((1, R, nl, HD), lambda b, l, r: (b, r, l, 0)), tspec],
        out_shape=[tshape, jax.ShapeDtypeStruct((B, dil, L, HD), BF16), tshape],
        scratch_shapes=[pltpu.VMEM((D // lanes, nl * dil if dil > 1 else 8, lanes), F32),
                        pltpu.VMEM((256, 256), BF16)],
        compiler_params=_params(("arbitrary", "arbitrary", "arbitrary")),
        name=f"attn_qkv_g{gi}",
    )(x, g.reshape(1, D), w_bf, w_bf, w_bf,
      jnp.tile(q_gain, ATT_HEADS).reshape(1, HD), jnp.tile(k_gain, ATT_HEADS).reshape(1, HD))


def _attn_kernel(*refs, single):
    if single:
        qt_ref, kc_ref, vtc_ref, bias_ref, o_ref, l_ref, s_ref, p_ref = refs
    else:
        qt_ref, kc_ref, kp_ref, vtc_ref, vtp_ref, bias_ref, o_ref, l_ref, s_ref, p_ref = refs
    blk = ATT_BLOCK
    pair = 2 * HEAD_DIM
    nitems = 2
    first = jnp.where(pl.program_id(2) == 0, 1, 0)
    if single:
        k0 = blk
        variants = [1, 1]
        q_of = lambda it, cs: qt_ref[0, it, cs, :]
        k_of = lambda it, cs: kc_ref[0, it, :, cs]
        vt_of = lambda it, cs: vtc_ref[0, it, cs, :]
    else:
        k0 = 0
        variants = [first, 0]

        def q_of(it, cs):
            return qt_ref[0, 0, cs, it * blk:(it + 1) * blk]

        def k_of(it, cs):
            if it == 0:
                return jnp.concatenate([kp_ref[0, 0, :, cs], kc_ref[0, 0, :blk, cs]], axis=0)
            return kc_ref[0, 0, :, cs]

        def vt_of(it, cs):
            if it == 0:
                return jnp.concatenate([vtp_ref[0, 0, cs, :], vtc_ref[0, 0, cs, :blk]], axis=1)
            return vtc_ref[0, 0, cs, :]

    feat = lax.broadcasted_iota(I32, (pair, blk), 0)
    head_rows = [jnp.where(feat < HEAD_DIM, 1.0, 0.0).astype(BF16),
                 jnp.where(feat < HEAD_DIM, 0.0, 1.0).astype(BF16)]
    invs = {}
    lse_rows = {}

    def scores(it, pr):
        cs = slice(pr * pair, (pr + 1) * pair)
        qt2 = q_of(it, cs)
        k2 = k_of(it, cs)
        for half in range(2):
            qth = qt2 * head_rows[half]
            s_ref[it * ATT_HEADS + 2 * pr + half] = jnp.dot(k2, qth, preferred_element_type=F32)

    def softmax(it, pr):
        for head in (2 * pr, 2 * pr + 1):
            s = s_ref[it * ATT_HEADS + head] + bias_ref[head, variants[it], k0:, :]
            m = jnp.max(s, axis=0, keepdims=True)
            p = jnp.exp(s - m)
            den = jnp.sum(p, axis=0, keepdims=True)
            p_ref[it * ATT_HEADS + head] = p.astype(BF16)
            invs[it, head] = 1.0 / den
            lse_rows[it, head] = m + jnp.log(den)

    def outputs(it, pr):
        cs = slice(pr * pair, (pr + 1) * pair)
        vt2 = vt_of(it, cs)
        halves = []
        for half in range(2):
            head = 2 * pr + half
            vth = vt2[half * HEAD_DIM:(half + 1) * HEAD_DIM]
            ot = jnp.dot(vth, p_ref[it * ATT_HEADS + head], preferred_element_type=F32)
            halves.append(ot * invs[it, head])
        val = jnp.concatenate(halves, axis=0).T
        if single:
            o_ref[0, it, :, cs] = val
        else:
            o_ref[0, 0, it * blk:(it + 1) * blk, cs] = val

    work = [(it, pr) for it in range(nitems) for pr in range(ATT_HEADS // 2)]
    for it, pr in work:
        scores(it, pr)
    for it, pr in work:
        softmax(it, pr)
    for it, pr in work:
        outputs(it, pr)
    lse_row = lax.broadcasted_iota(I32, (blk, blk), 0)
    for it in range(nitems):
        lse_t = jnp.zeros((blk, blk), F32)
        for head in range(ATT_HEADS):
            lse_t = jnp.where(lse_row == head, lse_rows[it, head], lse_t)
        if single:
            l_ref[0, it] = lse_t.T
        else:
            l_ref[0, 0, it * blk:(it + 1) * blk, :] = lse_t.T


def _attn_group(qt, k, vt, bias, gi):
    B, dil, L, HD = k.shape
    blk = ATT_BLOCK
    nb = L // blk
    single = nb == 1
    bias_spec = pl.BlockSpec((ATT_HEADS, 2, 2 * blk, blk), lambda b, r, n: (gi, 0, 0, 0))
    if single:
        rows = pl.BlockSpec((1, 2, blk, HD), lambda b, r, n: (b, r, 0, 0))
        cols = pl.BlockSpec((1, 2, HD, blk), lambda b, r, n: (b, r, 0, 0))
        grid = (B, dil // 2, 1)
        in_specs = [cols, rows, cols, bias_spec]
        args = (qt, k, vt, bias)
        out_specs = [rows, pl.BlockSpec((1, 2, blk, blk), lambda b, r, n: (b, r, 0, 0))]
        nk = blk
    else:
        rows = pl.BlockSpec((1, 1, 2 * blk, HD), lambda b, r, n: (b, r, n, 0))
        cols = pl.BlockSpec((1, 1, HD, 2 * blk), lambda b, r, n: (b, r, 0, n))
        prow = pl.BlockSpec((1, 1, blk, HD), lambda b, r, n: (b, r, jnp.maximum(2 * n - 1, 0), 0))
        pcol = pl.BlockSpec((1, 1, HD, blk), lambda b, r, n: (b, r, 0, jnp.maximum(2 * n - 1, 0)))
        grid = (B, dil, nb // 2)
        in_specs = [cols, rows, prow, cols, pcol, bias_spec]
        args = (qt, k, k, vt, vt, bias)
        out_specs = [rows, pl.BlockSpec((1, 1, 2 * blk, blk), lambda b, r, n: (b, r, n, 0))]
        nk = 2 * blk
    return pl.pallas_call(
        functools.partial(_attn_kernel, single=single),
        grid=grid,
        in_specs=in_specs,
        out_specs=out_specs,
        out_shape=[jax.ShapeDtypeStruct((B, dil, L, HD), F32),
                   jax.ShapeDtypeStruct((B, dil, L, blk), F32)],
        scratch_shapes=[pltpu.VMEM((2 * ATT_HEADS, nk, blk), F32),
                        pltpu.VMEM((2 * ATT_HEADS, nk, blk), BF16)],
        compiler_params=_params(("parallel", "parallel", "arbitrary")),
        name=f"attn_core_g{gi}",
    )(*args)


def _attn_out_kernel(x_ref, o0, o1, o2, l0, l1, l2, wo_ref, out_ref, oslab, lslab, ex_ref, *, T):
    lanes = oslab.shape[3]
    HD = o0.shape[3]

    @pl.when((pl.program_id(0) == 0) & (pl.program_id(1) == 0))
    def _():
        h = lax.broadcasted_iota(I32, ex_ref.shape, 0) & (ATT_BLOCK - 1)
        c = lax.broadcasted_iota(I32, ex_ref.shape, 1)
        owner = lax.shift_right_logical(c, HEAD_DIM.bit_length() - 1)
        ex_ref[...] = jnp.where(h == owner, 1.0, 0.0).astype(BF16)

    def token_order(o_ref, l_ref, dil, s):
        if dil == 1:
            return o_ref[0, 0], l_ref[0, 0]
        n = T // dil
        for r in range(dil):
            blk = o_ref[0, r]
            for c in range(HD // lanes):
                oslab[s, c, pl.ds(r, n, stride=dil), :] = blk[:, c * lanes:(c + 1) * lanes]
            lslab[s, pl.ds(r, n, stride=dil), :] = l_ref[0, r]
        o = jnp.concatenate([oslab[s, c] for c in range(HD // lanes)], axis=1)
        return o, lslab[s]

    groups = [token_order(o_ref, l_ref, dil, s)
              for s, ((_, dil), o_ref, l_ref) in enumerate(zip(DILATED_GROUPS, (o0, o1, o2), (l0, l1, l2)))]
    m = jnp.maximum(jnp.maximum(groups[0][1], groups[1][1]), groups[2][1])
    es = [jnp.exp(l - m) for _, l in groups]
    inv = 1.0 / (es[0] + es[1] + es[2])
    att = jnp.zeros((T, HD), F32)
    for (o, _), e in zip(groups, es):
        a = e * inv
        a_hi = a.astype(BF16)
        a_lo = (a - a_hi.astype(F32)).astype(BF16)
        wide = jnp.dot(jnp.concatenate([a_hi, a_lo], axis=1), ex_ref[...], preferred_element_type=F32)
        att = att + wide * o
    out_ref[0] = x_ref[0] + jnp.dot(att.astype(BF16), wo_ref[...], preferred_element_type=F32)


def _attn_out(x, os, ls, wo_bf):
    B, S, D = x.shape
    HD = wo_bf.shape[0]
    T = OUT_TILE
    lanes = 128
    ospec = lambda dil, w: pl.BlockSpec((1, dil, T // dil, w), lambda b, i: (b, 0, i, 0))
    dils = [dil for _, dil in DILATED_GROUPS]
    xspec = pl.BlockSpec((1, T, D), lambda b, i: (b, i, 0))
    return pl.pallas_call(
        functools.partial(_attn_out_kernel, T=T),
        grid=(B, S // T),
        in_specs=[xspec] + [ospec(d, HD) for d in dils] + [ospec(d, ATT_BLOCK) for d in dils]
                 + [pl.BlockSpec((HD, D), lambda b, i: (0, 0))],
        out_specs=xspec,
        out_shape=jax.ShapeDtypeStruct((B, S, D), F32),
        scratch_shapes=[pltpu.VMEM((len(dils), HD // lanes, T, lanes), F32),
                        pltpu.VMEM((len(dils), T, ATT_BLOCK), F32),
                        pltpu.VMEM((2 * ATT_BLOCK, HD), BF16)],
        compiler_params=_params(("arbitrary", "arbitrary")),
        name="attn_out",
    )(x, *os, *ls, wo_bf)


def _attention_layer(x, g, w_qkv, q_gain, k_gain, w_o, rel_bias, layer):
    HD = ATT_HEADS * HEAD_DIM
    bias = _bias_tables(rel_bias)
    wqkv_bf = _to_bf16(w_qkv, layer, HD)
    wo_bf = _to_bf16(w_o, layer, w_o.shape[2])
    os, ls = [], []
    for gi, (_, dil) in enumerate(DILATED_GROUPS):
        qt, k, vt = _qkv_group(x, g, wqkv_bf, q_gain, k_gain, gi, dil)
        o, l = _attn_group(qt, k, vt, bias, gi)
        os.append(o)
        ls.append(l)
    return _attn_out(x, os, ls, wo_bf)


def kernel(x, norm_mix_g, norm_ffn_g, pool_w, pool_scale, attn_w_qkv, attn_q_gain, attn_k_gain,
           attn_w_o, rel_bias, moe_w_router, moe_b_router, moe_w_gate_up, moe_b_gate_up,
           moe_w_down, moe_b_down):
    B, S, D = x.shape
    depth = norm_mix_g.shape[0]
    for i in range(depth):
        j = i // 2
        if i % 2 == 0:
            x = _pool_layer(x, norm_mix_g[i], pool_w[j], pool_scale[j])
        else:
            x = _attention_layer(x, norm_mix_g[i], attn_w_qkv, attn_q_gain[j], attn_k_gain[j],
                                 attn_w_o, rel_bias, j)
        x = _moe_layer(x.reshape(B * S, D), norm_ffn_g[i], moe_w_router[i], moe_b_router[i],
                       moe_w_gate_up, moe_b_gate_up[i], moe_w_down, moe_b_down[i],
                       i).reshape(B, S, D)
    return x
```

```python
import functools
import math

import numpy as np
import jax
import jax.numpy as jnp
from jax import lax
from jax.experimental import pallas as pl
from jax.experimental.pallas import tpu as pltpu

F32 = jnp.float32
BF16 = jnp.bfloat16
I32 = jnp.int32

EPS = 1e-6
POOL_WINDOWS = (2, 4, 8, 16)
POOL_HALO = 16
DILATED_GROUPS = ((128, 1), (512, 4), (2048, 16))
ATT_HEADS = 16
HEAD_DIM = 64
ATT_BLOCK = 128
N_BUCKETS = 32
REL_MAX_DIST = 2048
NEG_INF = -1e30
N_EXPERTS = 32
TOP_K = 4
SWIGLU_LIMIT = 7.0
SWIGLU_ALPHA = 1.702

V7X_VMEM_LIMIT_BYTES = 56 * 1024 * 1024
MOE_TILE = 512
DISPATCH_TILE = 512
COMBINE_TILE = 256
ROUTER_TILE = 512
POOL_TILE = 512
QKV_ROWS = 512
LANES = 128
OUT_TILE = 512


def _rms(xf, g):
    ms = jnp.mean(xf * xf, axis=-1, keepdims=True)
    return xf * lax.rsqrt(ms + EPS) * g


def _params(sem, vmem=None):
    return pltpu.CompilerParams(dimension_semantics=sem,
                                vmem_limit_bytes=vmem or V7X_VMEM_LIMIT_BYTES)


def _pool_kernel(x_ref, halo_ref, g_ref, w_ref, sc_ref, o_ref, *, ts, dg):
    i = pl.program_id(1)
    x = x_ref[0]
    g = g_ref[...]
    h = _rms(x, g)
    hh = _rms(halo_ref[0], g)
    hh = jnp.where(i == 0, 0.0, hh)
    full = jnp.concatenate([hh, h], axis=0)
    pos = i * ts + lax.broadcasted_iota(I32, (ts, 1), 0)
    outs = []
    for gi, w in enumerate(POOL_WINDOWS):
        s = full[:, gi * dg:(gi + 1) * dg]
        sh = 1
        while sh < w:
            s = s + pltpu.roll(s, sh, 0)
            sh *= 2
        s = s[POOL_HALO:]
        cnt = jnp.minimum(pos + 1, w).astype(F32)
        p = s / cnt - h[:, gi * dg:(gi + 1) * dg]
        outs.append(jnp.dot(p.astype(BF16), w_ref[gi].astype(BF16),
                            preferred_element_type=F32))
    y = jnp.concatenate(outs, axis=1) * sc_ref[...]
    o_ref[0] = x + y


def _pool_layer(x, g, w_groups, scale):
    B, S, D = x.shape
    ts = POOL_TILE
    dg = D // len(POOL_WINDOWS)
    hb = ts // POOL_HALO
    return pl.pallas_call(
        functools.partial(_pool_kernel, ts=ts, dg=dg),
        grid=(B, S // ts),
        in_specs=[
            pl.BlockSpec((1, ts, D), lambda b, i: (b, i, 0)),
            pl.BlockSpec((1, POOL_HALO, D), lambda b, i: (b, jnp.maximum(i * hb - 1, 0), 0)),
            pl.BlockSpec((1, D), lambda b, i: (0, 0)),
            pl.BlockSpec((len(POOL_WINDOWS), dg, dg), lambda b, i: (0, 0, 0)),
            pl.BlockSpec((1, D), lambda b, i: (0, 0)),
        ],
        out_specs=pl.BlockSpec((1, ts, D), lambda b, i: (b, i, 0)),
        out_shape=jax.ShapeDtypeStruct((B, S, D), F32),
        compiler_params=_params(("parallel", "parallel")),
        name="pool_mixer",
    )(x, x, g.reshape(1, D), w_groups, scale.reshape(1, D))


def _router_kernel(x_ref, g_ref, wr_ref, br_ref, e_ref, gate_ref, rank_ref, cnt_ref,
                   tri_ref, carry_ref, wcat_ref, *, tt):
    i = pl.program_id(0)
    E = N_EXPERTS

    @pl.when(i == 0)
    def _():
        r = lax.broadcasted_iota(I32, tri_ref.shape, 0)
        c = lax.broadcasted_iota(I32, tri_ref.shape, 1)
        tri_ref[...] = jnp.where(r < c, 1.0, 0.0).astype(BF16)
        carry_ref[...] = jnp.zeros_like(carry_ref)
        w = wr_ref[...]
        w_hi = w.astype(BF16)
        wcat_ref[:, :LANES] = w_hi
        wcat_ref[:, LANES:] = (w - w_hi.astype(F32)).astype(BF16)

    h = _rms(x_ref[...], g_ref[...])
    h_hi = h.astype(BF16)
    h_lo = (h - h_hi.astype(F32)).astype(BF16)
    both = jnp.dot(h_hi, wcat_ref[...], preferred_element_type=F32)
    cross = jnp.dot(h_lo, wcat_ref[:, :LANES], preferred_element_type=F32)
    logits = both[:, :LANES] + (both[:, LANES:] + cross) + br_ref[...]
    nblk = tt // LANES
    l = jnp.concatenate([logits[c * LANES:(c + 1) * LANES].T for c in range(nblk)], axis=1)[:E]
    row = lax.broadcasted_iota(I32, (E, tt), 0).astype(F32)
    vals, sels, idxs = [], [], []
    for k in range(TOP_K):
        m = jnp.max(l, axis=0, keepdims=True)
        idx = jnp.min(jnp.where(l == m, row, float(E)), axis=0, keepdims=True)
        sel = row == idx
        vals.append(m)
        sels.append(sel)
        idxs.append(idx)
        l = jnp.where(sel, -jnp.inf, l)
    ex = [jnp.exp(v - vals[0]) for v in vals]
    den = ex[0] + ex[1] + ex[2] + ex[3]
    multi = jnp.zeros((E, tt), F32)
    for sel in sels:
        multi = multi + jnp.where(sel, 1.0, 0.0)
    base = carry_ref[:, :1]
    parts = []
    for c in range(nblk):
        mc = multi[:, c * LANES:(c + 1) * LANES]
        parts.append(jnp.dot(mc.astype(BF16), tri_ref[...], preferred_element_type=F32) + base)
        base = base + jnp.sum(mc, axis=1, keepdims=True)
    before = jnp.concatenate(parts, axis=1)
    kk = lax.broadcasted_iota(I32, (TOP_K, tt), 0)
    e_out = jnp.zeros((TOP_K, tt), F32)
    g_out = jnp.zeros((TOP_K, tt), F32)
    r_out = jnp.zeros((TOP_K, tt), F32)
    for k in range(TOP_K):
        e_out = jnp.where(kk == k, idxs[k], e_out)
        g_out = jnp.where(kk == k, ex[k] / den, g_out)
        rk = jnp.sum(jnp.where(sels[k], before, 0.0), axis=0, keepdims=True)
        r_out = jnp.where(kk == k, rk, r_out)
    e_ref[...] = e_out.astype(I32)
    gate_ref[...] = g_out
    rank_ref[...] = r_out.astype(I32)
    carry_ref[...] = jnp.broadcast_to(base, carry_ref.shape)
    cnt_ref[...] = carry_ref[...]


def _router(x, g, w_r, b_r):
    N, D = x.shape
    tt = ROUTER_TILE
    E = N_EXPERTS
    w_pad = jnp.pad(w_r, ((0, 0), (0, LANES - E)))
    b_pad = jnp.pad(b_r, (0, LANES - E)).reshape(1, LANES)
    kspec = pl.BlockSpec((TOP_K, tt), lambda i: (0, i))
    return pl.pallas_call(
        functools.partial(_router_kernel, tt=tt),
        grid=(N // tt,),
        in_specs=[
            pl.BlockSpec((tt, D), lambda i: (i, 0)),
            pl.BlockSpec((1, D), lambda i: (0, 0)),
            pl.BlockSpec((D, LANES), lambda i: (0, 0)),
            pl.BlockSpec((1, LANES), lambda i: (0, 0)),
        ],
        out_specs=[kspec, kspec, kspec, pl.BlockSpec((E, LANES), lambda i: (0, 0))],
        out_shape=[
            jax.ShapeDtypeStruct((TOP_K, N), I32),
            jax.ShapeDtypeStruct((TOP_K, N), F32),
            jax.ShapeDtypeStruct((TOP_K, N), I32),
            jax.ShapeDtypeStruct((E, LANES), F32),
        ],
        scratch_shapes=[pltpu.VMEM((LANES, LANES), BF16), pltpu.VMEM((E, LANES), F32),
                        pltpu.VMEM((D, 2 * LANES), BF16)],
        compiler_params=_params(("arbitrary",)),
        name="moe_router",
    )(x, g.reshape(1, D), w_pad, b_pad)


def _to_tile_rows(ref, val):
    n, D = val.shape
    nsub = D // LANES
    for s in range(nsub):
        ref[pl.ds(s, n, stride=nsub), :] = val[:, s * LANES:(s + 1) * LANES]


def _from_tile_rows(ref, n, nsub):
    return jnp.concatenate([ref[pl.ds(s, n, stride=nsub), :] for s in range(nsub)], axis=1)


def _tile_row(ref, r, nsub, count=1):
    return ref.at[pl.ds(pl.multiple_of(r * nsub, nsub), count * nsub)]


def _row_copy_wait(src, dst, sem, times):
    for _ in range(times):
        pltpu.make_async_copy(src, dst.at[pl.ds(0, src.shape[0])], sem).wait()


def _dispatch_kernel(pend_ref, nu_ref, dest_ref, x_ref, g_ref, xpad_ref, hbuf, zbuf, sems, zsem,
                     *, tt, tm, nt, nsub):
    i = pl.program_id(0)
    slot = i % 2

    @pl.when(i == 0)
    def _():
        zbuf[...] = jnp.zeros_like(zbuf)

        def seg_copy(e):
            start = pl.multiple_of(jnp.maximum(pend_ref[e] - tm, 0), tm)
            return pltpu.make_async_copy(zbuf, _tile_row(xpad_ref, start, nsub, tm), zsem)

        def tail_copy(j):
            return pltpu.make_async_copy(zbuf, _tile_row(xpad_ref, (nt - 1 - j) * tm, nsub, tm), zsem)

        for e in range(N_EXPERTS):
            seg_copy(e).start()
        for j in range(N_EXPERTS):
            pl.when(nt - 1 - j >= nu_ref[0])(tail_copy(j).start)
        for e in range(N_EXPERTS):
            seg_copy(e).wait()
        for j in range(N_EXPERTS):
            pl.when(nt - 1 - j >= nu_ref[0])(tail_copy(j).wait)

    hb = hbuf.at[slot]
    _to_tile_rows(hb, _rms(x_ref[...], g_ref[...]))

    def issue(t, carry):
        for k in range(TOP_K):
            d = dest_ref[t * TOP_K + k]
            pltpu.make_async_copy(_tile_row(hb, t, nsub), _tile_row(xpad_ref, d, nsub),
                                  sems.at[slot]).start(priority=k % 2)
        return carry

    lax.fori_loop(0, tt, issue, 0, unroll=64)

    @pl.when(i > 0)
    def _():
        _row_copy_wait(hbuf.at[1 - slot], xpad_ref, sems.at[1 - slot], TOP_K)

    @pl.when(i == pl.num_programs(0) - 1)
    def _():
        _row_copy_wait(hb, xpad_ref, sems.at[slot], TOP_K)


def _dispatch(x, g, dest, pend, n_used, P):
    N, D = x.shape
    tt = DISPATCH_TILE
    tm = MOE_TILE
    nsub = D // LANES
    grid_spec = pltpu.PrefetchScalarGridSpec(
        num_scalar_prefetch=2,
        grid=(N // tt,),
        in_specs=[
            pl.BlockSpec((tt * TOP_K,), lambda i, pe, nu: (i,), memory_space=pltpu.SMEM),
            pl.BlockSpec((tt, D), lambda i, pe, nu: (i, 0)),
            pl.BlockSpec((1, D), lambda i, pe, nu: (0, 0)),
        ],
        out_specs=pl.BlockSpec(memory_space=pl.ANY),
        scratch_shapes=[pltpu.VMEM((2, tt * nsub, LANES), F32), pltpu.VMEM((tm * nsub, LANES), F32),
                        pltpu.SemaphoreType.DMA((2,)), pltpu.SemaphoreType.DMA(())],
    )
    return pl.pallas_call(
        functools.partial(_dispatch_kernel, tt=tt, tm=tm, nt=P // tm, nsub=nsub),
        grid_spec=grid_spec,
        out_shape=jax.ShapeDtypeStruct((P * nsub, LANES), F32),
        compiler_params=_params(("arbitrary",)),
        name="moe_dispatch",
    )(pend, n_used, dest, x, g.reshape(1, D))


def _expert_kernel(te_ref, nx_ref, nu_ref, x_ref, wgu_hbm, bgu_ref, wdn_hbm, bdn_ref, o_ref,
                   wgu_f32, wdn_f32, wgu_bf, wdn_bf, sems, *, layer, F, tm, nsub):
    i = pl.program_id(0)

    def weight_copies(e):
        return (pltpu.make_async_copy(wgu_hbm.at[layer, e], wgu_f32, sems.at[0]),
                pltpu.make_async_copy(wdn_hbm.at[layer, e], wdn_f32, sems.at[1]))

    @pl.when(i < nu_ref[0])
    def _():
        e = te_ref[i]

        @pl.when(i == 0)
        def _():
            for c in weight_copies(e):
                c.start()

        @pl.when((i == 0) | (e != te_ref[jnp.maximum(i - 1, 0)]))
        def _():
            for c in weight_copies(e):
                c.wait()
            wgu_bf[...] = wgu_f32[...].astype(BF16)
            wdn_bf[...] = wdn_f32[...].astype(BF16)
            nxt = nx_ref[i]

            @pl.when(nxt != e)
            def _():
                for c in weight_copies(nxt):
                    c.start()

        x = _from_tile_rows(x_ref, tm, nsub).astype(BF16)
        gu = jnp.dot(x, wgu_bf[...], preferred_element_type=F32) + bgu_ref[0]
        gate = jnp.minimum(gu[:, :F], SWIGLU_LIMIT)
        up = jnp.clip(gu[:, F:], -SWIGLU_LIMIT, SWIGLU_LIMIT)
        glu = gate * jax.nn.sigmoid(SWIGLU_ALPHA * gate)
        a = ((up + 1.0) * glu).astype(BF16)
        _to_tile_rows(o_ref, jnp.dot(a, wdn_bf[...], preferred_element_type=F32) + bdn_ref[0])

    @pl.when(i >= nu_ref[0])
    def _():
        o_ref[...] = jnp.zeros_like(o_ref)


def _experts(tile_e, next_e, n_used, x_pad, w_gu, b_gu, w_dn, b_dn, layer):
    _, E, D, F2 = w_gu.shape
    F = F2 // 2
    tm = MOE_TILE
    nsub = D // LANES
    P = x_pad.shape[0] // nsub
    row = lambda i, te, nx, nu: (jnp.minimum(i, nu[0] - 1), 0)
    exp3 = lambda i, te, nx, nu: (te[i], 0, 0)
    grid_spec = pltpu.PrefetchScalarGridSpec(
        num_scalar_prefetch=3,
        grid=(P // tm,),
        in_specs=[
            pl.BlockSpec((tm * nsub, LANES), row),
            pl.BlockSpec(memory_space=pl.ANY),
            pl.BlockSpec((1, 1, F2), exp3),
            pl.BlockSpec(memory_space=pl.ANY),
            pl.BlockSpec((1, 1, D), exp3),
        ],
        out_specs=pl.BlockSpec((tm * nsub, LANES), lambda i, te, nx, nu: (i, 0)),
        scratch_shapes=[pltpu.VMEM((D, F2), F32), pltpu.VMEM((F, D), F32),
                        pltpu.VMEM((D, F2), BF16), pltpu.VMEM((F, D), BF16),
                        pltpu.SemaphoreType.DMA((2,))],
    )
    return pl.pallas_call(
        functools.partial(_expert_kernel, layer=layer, F=F, tm=tm, nsub=nsub),
        grid_spec=grid_spec,
        out_shape=jax.ShapeDtypeStruct((P * nsub, LANES), F32),
        compiler_params=_params(("arbitrary",)),
        name="moe_experts",
    )(tile_e, next_e, n_used, x_pad, w_gu, b_gu.reshape(E, 1, F2), w_dn, b_dn.reshape(E, 1, D))


def _combine_kernel(dest_ref, dnext_ref, x_ref, gate_ref, ypad_ref, o_ref, ybuf, sems, *, tt, nsub):
    i = pl.program_id(0)
    slot = i % 2

    def gather(idx_ref, s):
        def issue(t, carry):
            for k in range(TOP_K):
                d = idx_ref[t * TOP_K + k]
                pltpu.make_async_copy(_tile_row(ypad_ref, d, nsub), _tile_row(ybuf.at[s, k], t, nsub),
                                      sems.at[s]).start(priority=k % 2)
            return carry

        lax.fori_loop(0, tt, issue, 0, unroll=64)

    @pl.when(i == 0)
    def _():
        gather(dest_ref, slot)

    @pl.when(i + 1 < pl.num_programs(0))
    def _():
        gather(dnext_ref, 1 - slot)

    for k in range(TOP_K):
        _row_copy_wait(ybuf.at[slot, k], ypad_ref, sems.at[slot], 1)
    gate = gate_ref[...]
    x = x_ref[...]
    cols = []
    for c in range(nsub):
        acc = x[:, c * LANES:(c + 1) * LANES]
        for k in range(TOP_K):
            acc = acc + ybuf[slot, k, pl.ds(c, tt, stride=nsub), :] * gate[:, k:k + 1]
        cols.append(acc)
    o_ref[...] = jnp.concatenate(cols, axis=1)


def _combine(x, gate, dest, y_pad):
    N, D = x.shape
    tt = COMBINE_TILE
    nsub = D // LANES
    last = N // tt - 1
    return pl.pallas_call(
        functools.partial(_combine_kernel, tt=tt, nsub=nsub),
        grid=(N // tt,),
        in_specs=[
            pl.BlockSpec((tt * TOP_K,), lambda i: (i,), memory_space=pltpu.SMEM),
            pl.BlockSpec((tt * TOP_K,), lambda i: (jnp.minimum(i + 1, last),), memory_space=pltpu.SMEM),
            pl.BlockSpec((tt, D), lambda i: (i, 0)),
            pl.BlockSpec((tt, TOP_K), lambda i: (i, 0)),
            pl.BlockSpec(memory_space=pl.ANY),
        ],
        out_specs=pl.BlockSpec((tt, D), lambda i: (i, 0)),
        out_shape=jax.ShapeDtypeStruct((N, D), F32),
        scratch_shapes=[pltpu.VMEM((2, TOP_K, tt * nsub, LANES), F32), pltpu.SemaphoreType.DMA((2,))],
        compiler_params=_params(("arbitrary",)),
        name="moe_combine",
    )(dest, dest, x, gate, y_pad)


def _moe_layer(x, g, w_r, b_r, w_gu, b_gu, w_dn, b_dn, layer):
    N, D = x.shape
    E = N_EXPERTS
    tm = MOE_TILE
    e_idx, gate_t, rank, counts = _router(x, g, w_r, b_r)
    counts = counts[:, 0].astype(I32)
    padded = ((counts + tm - 1) // tm) * tm
    pend = jnp.cumsum(padded)
    pstart = pend - padded
    seg = jnp.zeros_like(rank)
    for e in range(E):
        seg = jnp.where(e_idx == e, pstart[e], seg)
    dest = (seg + rank).T.reshape(-1)
    gate = gate_t.T
    P = ((N * TOP_K + E * (tm - 1) + tm - 1) // tm) * tm
    nt = P // tm
    tiles = jnp.arange(nt, dtype=I32)
    tile_e = jnp.minimum(jnp.sum(tiles[:, None] * tm >= pend[None, :], axis=1), E - 1).astype(I32)
    n_used = (pend[-1] // tm).astype(I32)
    tile_e = jnp.where(tiles < n_used, tile_e, tile_e[n_used - 1])
    experts = jnp.arange(E, dtype=I32)
    later = lax.cummin(jnp.where(counts > 0, experts, E), reverse=True)
    after = jnp.concatenate([later[1:], jnp.full((1,), E, I32)])
    next_e = jnp.where(after < E, after, experts)[tile_e]
    x_pad = _dispatch(x, g, dest, pend.astype(I32), n_used.reshape(1), P)
    y_pad = _experts(tile_e, next_e.astype(I32), n_used.reshape(1), x_pad, w_gu, b_gu, w_dn, b_dn, layer)
    return _combine(x, gate, dest, y_pad)


def _bucket_maps():
    qi = np.arange(ATT_BLOCK)[None, :]
    kj = np.arange(2 * ATT_BLOCK)[:, None]
    delta = qi + ATT_BLOCK - kj
    buckets, valids = [], []
    max_exact = N_BUCKETS // 2
    for win, dil in DILATED_GROUPS:
        n = np.maximum(delta * dil, 0)
        nf = np.maximum(n, 1).astype(np.float32)
        large = max_exact + (np.log(nf / np.float32(max_exact)) / np.float32(math.log(REL_MAX_DIST / max_exact))
                             * np.float32(N_BUCKETS - max_exact)).astype(np.int32)
        large = np.minimum(large, N_BUCKETS - 1)
        buckets.append(np.where(n < max_exact, n, large).astype(np.int32))
        valids.append(((delta >= 0) & (delta <= win // dil)).astype(np.int32))
    return np.stack(buckets), np.stack(valids)


def _bias_kernel(tab_ref, bm_ref, valid_ref, o_ref):
    gh = pl.program_id(0)
    bm = bm_ref[0]
    acc = jnp.zeros(bm.shape, F32)
    for b in range(N_BUCKETS):
        acc = jnp.where(bm == b, tab_ref[b, gh], acc)
    band = jnp.where(valid_ref[0] > 0, acc, NEG_INF)
    key = lax.broadcasted_iota(I32, bm.shape, 0)
    o_ref[0, 0] = band
    o_ref[0, 1] = jnp.where(key >= ATT_BLOCK, band, NEG_INF)


def _bias_tables(rel_bias):
    bm, valid = _bucket_maps()
    GH = len(DILATED_GROUPS) * ATT_HEADS
    blk = (1, 2 * ATT_BLOCK, ATT_BLOCK)
    return pl.pallas_call(
        _bias_kernel,
        grid=(GH,),
        in_specs=[
            pl.BlockSpec(memory_space=pltpu.SMEM),
            pl.BlockSpec(blk, lambda i: (i // ATT_HEADS, 0, 0)),
            pl.BlockSpec(blk, lambda i: (i // ATT_HEADS, 0, 0)),
        ],
        out_specs=pl.BlockSpec((1, 2, 2 * ATT_BLOCK, ATT_BLOCK), lambda i: (i, 0, 0, 0)),
        out_shape=jax.ShapeDtypeStruct((GH, 2, 2 * ATT_BLOCK, ATT_BLOCK), F32),
        compiler_params=_params(("arbitrary",)),
        name="attn_bias",
    )(rel_bias, jnp.asarray(bm), jnp.asarray(valid))


def _cast_kernel(w_ref, o_ref):
    o_ref[...] = w_ref[...].astype(BF16)


def _to_bf16(w, layer, col_block):
    _, rows, cols = w.shape
    return pl.pallas_call(
        _cast_kernel,
        grid=(cols // col_block,),
        in_specs=[pl.BlockSpec((None, rows, col_block), lambda j: (layer, 0, j))],
        out_specs=pl.BlockSpec((rows, col_block), lambda j: (0, j)),
        out_shape=jax.ShapeDtypeStruct((rows, cols), BF16),
        compiler_params=_params(("parallel",)),
        name="cast_bf16",
    )(w)


def _qkv_kernel(x_ref, g_ref, wq_ref, wk_ref, wv_ref, qg_ref, kg_ref, qt_ref, k_ref, vt_ref,
                slab_ref, bd_ref, *, dil, R, nl):
    rc = pl.program_id(2)
    D = x_ref.shape[2]
    lanes = slab_ref.shape[2]
    first = (pl.program_id(0) == 0) & (pl.program_id(1) == 0) & (rc == 0)

    @pl.when(first)
    def _():
        r = lax.broadcasted_iota(I32, bd_ref.shape, 0)
        c = lax.broadcasted_iota(I32, bd_ref.shape, 1)
        hd_shift = HEAD_DIM.bit_length() - 1
        same_head = lax.shift_right_logical(r, hd_shift) == lax.shift_right_logical(c, hd_shift)
        bd_ref[...] = jnp.where(same_head, 1.0, 0.0).astype(BF16)

    if dil == 1:
        x = x_ref[0]
    else:
        @pl.when(rc == 0)
        def _():
            for c in range(D // lanes):
                slab_ref[c] = x_ref[0, :, c * lanes:(c + 1) * lanes]

        pieces = []
        for j in range(R):
            r = rc * R + j
            cols = [slab_ref[c, pl.ds(r, nl, stride=dil), :] for c in range(D // lanes)]
            pieces.append(jnp.concatenate(cols, axis=1))
        x = jnp.concatenate(pieces, axis=0)
    h = _rms(x, g_ref[...]).astype(BF16)
    nb = bd_ref.shape[0]

    def head_norm(y, gain):
        parts = []
        for c0 in range(0, y.shape[1], nb):
            yc = y[:, c0:c0 + nb]
            ssq = jnp.dot((yc * yc).astype(BF16), bd_ref[...], preferred_element_type=F32)
            parts.append(yc * lax.rsqrt(ssq * (1.0 / HEAD_DIM) + EPS))
        return jnp.concatenate(parts, axis=1) * gain

    q = head_norm(jnp.dot(h, wq_ref[...], preferred_element_type=F32), qg_ref[...])
    q = q * (HEAD_DIM ** -0.5)
    k = head_norm(jnp.dot(h, wk_ref[...], preferred_element_type=F32), kg_ref[...])
    v = jnp.dot(h, wv_ref[...], preferred_element_type=F32)
    blk = ATT_BLOCK
    for j in range(R):
        k_ref[0, j] = k[j * nl:(j + 1) * nl].astype(BF16)
        for c in range(nl // blk):
            rows = slice(j * nl + c * blk, j * nl + (c + 1) * blk)
            for hp in range(q.shape[1] // blk):
                cs = slice(hp * blk, (hp + 1) * blk)
                qt_ref[0, j, cs, c * blk:(c + 1) * blk] = q[rows, cs].T.astype(BF16)
                vt_ref[0, j, cs, c * blk:(c + 1) * blk] = v[rows, cs].T.astype(BF16)


def _qkv_group(x, g, w_bf, q_gain, k_gain, gi, dil):
    B, S, D = x.shape
    HD = ATT_HEADS * HEAD_DIM
    G = len(DILATED_GROUPS)
    L = S // dil
    R = min(dil, QKV_ROWS // ATT_BLOCK)
    nl = QKV_ROWS // R
    lanes = 128
    wspec = lambda s: pl.BlockSpec((D, HD), lambda b, l, r: (0, s * G + gi))
    tspec = pl.BlockSpec((1, R, HD, nl), lambda b, l, r: (b, r, 0, l))
    tshape = jax.ShapeDtypeStruct((B, dil, HD, L), BF16)
    return pl.pallas_call(
        functools.partial(_qkv_kernel, dil=dil, R=R, nl=nl),
        grid=(B, L // nl, dil // R),
        in_specs=[
            pl.BlockSpec((1, nl * dil, D), lambda b, l, r: (b, l, 0)),
            pl.BlockSpec((1, D), lambda b, l, r: (0, 0)),
            wspec(0), wspec(1), wspec(2),
            pl.BlockSpec((1, HD), lambda b, l, r: (0, 0)),
            pl.BlockSpec((1, HD), lambda b, l, r: (0, 0)),
        ],
        out_specs=[tspec, pl.BlockSpec((1, R, nl, HD), lambda b, l, r: (b, r, l, 0)), tspec],
        out_shape=[tshape, jax.ShapeDtypeStruct((B, dil, L, HD), BF16), tshape],
        scratch_shapes=[pltpu.VMEM((D // lanes, nl * dil if dil > 1 else 8, lanes), F32),
                        pltpu.VMEM((256, 256), BF16)],
        compiler_params=_params(("arbitrary", "arbitrary", "arbitrary")),
        name=f"attn_qkv_g{gi}",
    )(x, g.reshape(1, D), w_bf, w_bf, w_bf,
      jnp.tile(q_gain, ATT_HEADS).reshape(1, HD), jnp.tile(k_gain, ATT_HEADS).reshape(1, HD))


def _attn_kernel(*refs, single):
    if single:
        qt_ref, kc_ref, vtc_ref, bias_ref, o_ref, l_ref, s_ref, p_ref = refs
    else:
        qt_ref, kc_ref, kp_ref, vtc_ref, vtp_ref, bias_ref, o_ref, l_ref, s_ref, p_ref = refs
    blk = ATT_BLOCK
    pair = 2 * HEAD_DIM
    nitems = 2
    first = jnp.where(pl.program_id(2) == 0, 1, 0)
    if single:
        k0 = blk
        variants = [1, 1]
        q_of = lambda it, cs: qt_ref[0, it, cs, :]
        k_of = lambda it, cs: kc_ref[0, it, :, cs]
        vt_of = lambda it, cs: vtc_ref[0, it, cs, :]
    else:
        k0 = 0
        variants = [first, 0]

        def q_of(it, cs):
            return qt_ref[0, 0, cs, it * blk:(it + 1) * blk]

        def k_of(it, cs):
            if it == 0:
                return jnp.concatenate([kp_ref[0, 0, :, cs], kc_ref[0, 0, :blk, cs]], axis=0)
            return kc_ref[0, 0, :, cs]

        def vt_of(it, cs):
            if it == 0:
                return jnp.concatenate([vtp_ref[0, 0, cs, :], vtc_ref[0, 0, cs, :blk]], axis=1)
            return vtc_ref[0, 0, cs, :]

    feat = lax.broadcasted_iota(I32, (pair, blk), 0)
    head_rows = [jnp.where(feat < HEAD_DIM, 1.0, 0.0).astype(BF16),
                 jnp.where(feat < HEAD_DIM, 0.0, 1.0).astype(BF16)]
    invs = {}
    lse_rows = {}

    def scores(it, pr):
        cs = slice(pr * pair, (pr + 1) * pair)
        qt2 = q_of(it, cs)
        k2 = k_of(it, cs)
        for half in range(2):
            qth = qt2 * head_rows[half]
            s_ref[it * ATT_HEADS + 2 * pr + half] = jnp.dot(k2, qth, preferred_element_type=F32)

    def softmax(it, pr):
        for head in (2 * pr, 2 * pr + 1):
            s = s_ref[it * ATT_HEADS + head] + bias_ref[head, variants[it], k0:, :]
            m = jnp.max(s, axis=0, keepdims=True)
            p = jnp.exp(s - m)
            den = jnp.sum(p, axis=0, keepdims=True)
            p_ref[it * ATT_HEADS + head] = p.astype(BF16)
            invs[it, head] = 1.0 / den
            lse_rows[it, head] = m + jnp.log(den)

    def outputs(it, pr):
        cs = slice(pr * pair, (pr + 1) * pair)
        vt2 = vt_of(it, cs)
        halves = []
        for half in range(2):
            head = 2 * pr + half
            vth = vt2[half * HEAD_DIM:(half + 1) * HEAD_DIM]
            ot = jnp.dot(vth, p_ref[it * ATT_HEADS + head], preferred_element_type=F32)
            halves.append(ot * invs[it, head])
        val = jnp.concatenate(halves, axis=0).T
        if single:
            o_ref[0, it, :, cs] = val
        else:
            o_ref[0, 0, it * blk:(it + 1) * blk, cs] = val

    work = [(it, pr) for it in range(nitems) for pr in range(ATT_HEADS // 2)]
    for it, pr in work:
        scores(it, pr)
    for it, pr in work:
        softmax(it, pr)
    for it, pr in work:
        outputs(it, pr)
    lse_row = lax.broadcasted_iota(I32, (blk, blk), 0)
    for it in range(nitems):
        lse_t = jnp.zeros((blk, blk), F32)
        for head in range(ATT_HEADS):
            lse_t = jnp.where(lse_row == head, lse_rows[it, head], lse_t)
        if single:
            l_ref[0, it] = lse_t.T
        else:
            l_ref[0, 0, it * blk:(it + 1) * blk, :] = lse_t.T


def _attn_group(qt, k, vt, bias, gi):
    B, dil, L, HD = k.shape
    blk = ATT_BLOCK
    nb = L // blk
    single = nb == 1
    bias_spec = pl.BlockSpec((ATT_HEADS, 2, 2 * blk, blk), lambda b, r, n: (gi, 0, 0, 0))
    if single:
        rows = pl.BlockSpec((1, 2, blk, HD), lambda b, r, n: (b, r, 0, 0))
        cols = pl.BlockSpec((1, 2, HD, blk), lambda b, r, n: (b, r, 0, 0))
        grid = (B, dil // 2, 1)
        in_specs = [cols, rows, cols, bias_spec]
        args = (qt, k, vt, bias)
        out_specs = [rows, pl.BlockSpec((1, 2, blk, blk), lambda b, r, n: (b, r, 0, 0))]
        nk = blk
    else:
        rows = pl.BlockSpec((1, 1, 2 * blk, HD), lambda b, r, n: (b, r, n, 0))
        cols = pl.BlockSpec((1, 1, HD, 2 * blk), lambda b, r, n: (b, r, 0, n))
        prow = pl.BlockSpec((1, 1, blk, HD), lambda b, r, n: (b, r, jnp.maximum(2 * n - 1, 0), 0))
        pcol = pl.BlockSpec((1, 1, HD, blk), lambda b, r, n: (b, r, 0, jnp.maximum(2 * n - 1, 0)))
        grid = (B, dil, nb // 2)
        in_specs = [cols, rows, prow, cols, pcol, bias_spec]
        args = (qt, k, k, vt, vt, bias)
        out_specs = [rows, pl.BlockSpec((1, 1, 2 * blk, blk), lambda b, r, n: (b, r, n, 0))]
        nk = 2 * blk
    return pl.pallas_call(
        functools.partial(_attn_kernel, single=single),
        grid=grid,
        in_specs=in_specs,
        out_specs=out_specs,
        out_shape=[jax.ShapeDtypeStruct((B, dil, L, HD), F32),
                   jax.ShapeDtypeStruct((B, dil, L, blk), F32)],
        scratch_shapes=[pltpu.VMEM((2 * ATT_HEADS, nk, blk), F32),
                        pltpu.VMEM((2 * ATT_HEADS, nk, blk), BF16)],
        compiler_params=_params(("parallel", "parallel", "arbitrary")),
        name=f"attn_core_g{gi}",
    )(*args)


def _attn_out_kernel(x_ref, o0, o1, o2, l0, l1, l2, wo_ref, out_ref, oslab, lslab, ex_ref, *, T):
    lanes = oslab.shape[3]
    HD = o0.shape[3]

    @pl.when((pl.program_id(0) == 0) & (pl.program_id(1) == 0))
    def _():
        h = lax.broadcasted_iota(I32, ex_ref.shape, 0) & (ATT_BLOCK - 1)
        c = lax.broadcasted_iota(I32, ex_ref.shape, 1)
        owner = lax.shift_right_logical(c, HEAD_DIM.bit_length() - 1)
        ex_ref[...] = jnp.where(h == owner, 1.0, 0.0).astype(BF16)

    def token_order(o_ref, l_ref, dil, s):
        if dil == 1:
            return o_ref[0, 0], l_ref[0, 0]
        n = T // dil
        for r in range(dil):
            blk = o_ref[0, r]
            for c in range(HD // lanes):
                oslab[s, c, pl.ds(r, n, stride=dil), :] = blk[:, c * lanes:(c + 1) * lanes]
            lslab[s, pl.ds(r, n, stride=dil), :] = l_ref[0, r]
        o = jnp.concatenate([oslab[s, c] for c in range(HD // lanes)], axis=1)
        return o, lslab[s]

    groups = [token_order(o_ref, l_ref, dil, s)
              for s, ((_, dil), o_ref, l_ref) in enumerate(zip(DILATED_GROUPS, (o0, o1, o2), (l0, l1, l2)))]
    m = jnp.maximum(jnp.maximum(groups[0][1], groups[1][1]), groups[2][1])
    es = [jnp.exp(l - m) for _, l in groups]
    inv = 1.0 / (es[0] + es[1] + es[2])
    att = jnp.zeros((T, HD), F32)
    for (o, _), e in zip(groups, es):
        a = e * inv
        a_hi = a.astype(BF16)
        a_lo = (a - a_hi.astype(F32)).astype(BF16)
        wide = jnp.dot(jnp.concatenate([a_hi, a_lo], axis=1), ex_ref[...], preferred_element_type=F32)
        att = att + wide * o
    out_ref[0] = x_ref[0] + jnp.dot(att.astype(BF16), wo_ref[...], preferred_element_type=F32)


def _attn_out(x, os, ls, wo_bf):
    B, S, D = x.shape
    HD = wo_bf.shape[0]
    T = OUT_TILE
    lanes = 128
    ospec = lambda dil, w: pl.BlockSpec((1, dil, T // dil, w), lambda b, i: (b, 0, i, 0))
    dils = [dil for _, dil in DILATED_GROUPS]
    xspec = pl.BlockSpec((1, T, D), lambda b, i: (b, i, 0))
    return pl.pallas_call(
        functools.partial(_attn_out_kernel, T=T),
        grid=(B, S // T),
        in_specs=[xspec] + [ospec(d, HD) for d in dils] + [ospec(d, ATT_BLOCK) for d in dils]
                 + [pl.BlockSpec((HD, D), lambda b, i: (0, 0))],
        out_specs=xspec,
        out_shape=jax.ShapeDtypeStruct((B, S, D), F32),
        scratch_shapes=[pltpu.VMEM((len(dils), HD // lanes, T, lanes), F32),
                        pltpu.VMEM((len(dils), T, ATT_BLOCK), F32),
                        pltpu.VMEM((2 * ATT_BLOCK, HD), BF16)],
        compiler_params=_params(("arbitrary", "arbitrary")),
        name="attn_out",
    )(x, *os, *ls, wo_bf)


def _attention_layer(x, g, w_qkv, q_gain, k_gain, w_o, rel_bias, layer):
    HD = ATT_HEADS * HEAD_DIM
    bias = _bias_tables(rel_bias)
    wqkv_bf = _to_bf16(w_qkv, layer, HD)
    wo_bf = _to_bf16(w_o, layer, w_o.shape[2])
    os, ls = [], []
    for gi, (_, dil) in enumerate(DILATED_GROUPS):
        qt, k, vt = _qkv_group(x, g, wqkv_bf, q_gain, k_gain, gi, dil)
        o, l = _attn_group(qt, k, vt, bias, gi)
        os.append(o)
        ls.append(l)
    return _attn_out(x, os, ls, wo_bf)


def kernel(x, norm_mix_g, norm_ffn_g, pool_w, pool_scale, attn_w_qkv, attn_q_gain, attn_k_gain,
           attn_w_o, rel_bias, moe_w_router, moe_b_router, moe_w_gate_up, moe_b_gate_up,
           moe_w_down, moe_b_down):
    B, S, D = x.shape
    depth = norm_mix_g.shape[0]
    for i in range(depth):
        j = i // 2
        if i % 2 == 0:
            x = _pool_layer(x, norm_mix_g[i], pool_w[j], pool_scale[j])
        else:
            x = _attention_layer(x, norm_mix_g[i], attn_w_qkv, attn_q_gain[j], attn_k_gain[j],
                                 attn_w_o, rel_bias, j)
        x = _moe_layer(x.reshape(B * S, D), norm_ffn_g[i], moe_w_router[i], moe_b_router[i],
                       moe_w_gate_up, moe_b_gate_up[i], moe_w_down, moe_b_down[i],
                       i).reshape(B, S, D)
    return x
```

```python
import functools
import math

import numpy as np
import jax
import jax.numpy as jnp
from jax import lax
from jax.experimental import pallas as pl
from jax.experimental.pallas import tpu as pltpu

F32 = jnp.float32
BF16 = jnp.bfloat16
I32 = jnp.int32

EPS = 1e-6
POOL_WINDOWS = (2, 4, 8, 16)
POOL_HALO = 16
DILATED_GROUPS = ((128, 1), (512, 4), (2048, 16))
ATT_HEADS = 16
HEAD_DIM = 64
ATT_BLOCK = 128
ATT_ITEMS = 8
N_BUCKETS = 32
REL_MAX_DIST = 2048
NEG_INF = -1e30
N_EXPERTS = 32
TOP_K = 4
SWIGLU_LIMIT = 7.0
SWIGLU_ALPHA = 1.702

V7X_VMEM_LIMIT_BYTES = 56 * 1024 * 1024
MOE_TILE = 512
DISPATCH_TILE = 512
COMBINE_TILE = 256
ROUTER_TILE = 512
POOL_TILE = 512
QKV_ROWS = 512
LANES = 128
OUT_TILE = 512


def _rms(xf, g):
    ms = jnp.mean(xf * xf, axis=-1, keepdims=True)
    return xf * lax.rsqrt(ms + EPS) * g


def _params(sem, vmem=None):
    return pltpu.CompilerParams(dimension_semantics=sem,
                                vmem_limit_bytes=vmem or V7X_VMEM_LIMIT_BYTES)


def _pool_kernel(x_ref, halo_ref, g_ref, w_ref, sc_ref, o_ref, *, ts, dg):
    i = pl.program_id(1)
    x = x_ref[0]
    g = g_ref[...]
    h = _rms(x, g)
    hh = _rms(halo_ref[0], g)
    hh = jnp.where(i == 0, 0.0, hh)
    full = jnp.concatenate([hh, h], axis=0)
    pos = i * ts + lax.broadcasted_iota(I32, (ts, 1), 0)
    outs = []
    for gi, w in enumerate(POOL_WINDOWS):
        s = full[:, gi * dg:(gi + 1) * dg]
        sh = 1
        while sh < w:
            s = s + pltpu.roll(s, sh, 0)
            sh *= 2
        s = s[POOL_HALO:]
        cnt = jnp.minimum(pos + 1, w).astype(F32)
        p = s / cnt - h[:, gi * dg:(gi + 1) * dg]
        outs.append(jnp.dot(p.astype(BF16), w_ref[gi].astype(BF16),
                            preferred_element_type=F32))
    y = jnp.concatenate(outs, axis=1) * sc_ref[...]
    o_ref[0] = x + y


def _pool_layer(x, g, w_groups, scale):
    B, S, D = x.shape
    ts = POOL_TILE
    dg = D // len(POOL_WINDOWS)
    hb = ts // POOL_HALO
    return pl.pallas_call(
        functools.partial(_pool_kernel, ts=ts, dg=dg),
        grid=(B, S // ts),
        in_specs=[
            pl.BlockSpec((1, ts, D), lambda b, i: (b, i, 0)),
            pl.BlockSpec((1, POOL_HALO, D), lambda b, i: (b, jnp.maximum(i * hb - 1, 0), 0)),
            pl.BlockSpec((1, D), lambda b, i: (0, 0)),
            pl.BlockSpec((len(POOL_WINDOWS), dg, dg), lambda b, i: (0, 0, 0)),
            pl.BlockSpec((1, D), lambda b, i: (0, 0)),
        ],
        out_specs=pl.BlockSpec((1, ts, D), lambda b, i: (b, i, 0)),
        out_shape=jax.ShapeDtypeStruct((B, S, D), F32),
        compiler_params=_params(("parallel", "parallel")),
        name="pool_mixer",
    )(x, x, g.reshape(1, D), w_groups, scale.reshape(1, D))


def _router_kernel(x_ref, g_ref, wr_ref, br_ref, e_ref, gate_ref, rank_ref, cnt_ref,
                   tri_ref, carry_ref, wcat_ref, *, tt):
    i = pl.program_id(0)
    E = N_EXPERTS

    @pl.when(i == 0)
    def _():
        r = lax.broadcasted_iota(I32, tri_ref.shape, 0)
        c = lax.broadcasted_iota(I32, tri_ref.shape, 1)
        tri_ref[...] = jnp.where(r < c, 1.0, 0.0).astype(BF16)
        carry_ref[...] = jnp.zeros_like(carry_ref)
        w = wr_ref[...]
        w_hi = w.astype(BF16)
        wcat_ref[:, :LANES] = w_hi
        wcat_ref[:, LANES:] = (w - w_hi.astype(F32)).astype(BF16)

    h = _rms(x_ref[...], g_ref[...])
    h_hi = h.astype(BF16)
    h_lo = (h - h_hi.astype(F32)).astype(BF16)
    both = jnp.dot(h_hi, wcat_ref[...], preferred_element_type=F32)
    cross = jnp.dot(h_lo, wcat_ref[:, :LANES], preferred_element_type=F32)
    logits = both[:, :LANES] + (both[:, LANES:] + cross) + br_ref[...]
    nblk = tt // LANES
    l = jnp.concatenate([logits[c * LANES:(c + 1) * LANES].T for c in range(nblk)], axis=1)[:E]
    row = lax.broadcasted_iota(I32, (E, tt), 0).astype(F32)
    vals, sels, idxs = [], [], []
    for k in range(TOP_K):
        m = jnp.max(l, axis=0, keepdims=True)
        idx = jnp.min(jnp.where(l == m, row, float(E)), axis=0, keepdims=True)
        sel = row == idx
        vals.append(m)
        sels.append(sel)
        idxs.append(idx)
        l = jnp.where(sel, -jnp.inf, l)
    ex = [jnp.exp(v - vals[0]) for v in vals]
    den = ex[0] + ex[1] + ex[2] + ex[3]
    multi = jnp.zeros((E, tt), F32)
    for sel in sels:
        multi = multi + jnp.where(sel, 1.0, 0.0)
    base = carry_ref[:, :1]
    parts = []
    for c in range(nblk):
        mc = multi[:, c * LANES:(c + 1) * LANES]
        parts.append(jnp.dot(mc.astype(BF16), tri_ref[...], preferred_element_type=F32) + base)
        base = base + jnp.sum(mc, axis=1, keepdims=True)
    before = jnp.concatenate(parts, axis=1)
    kk = lax.broadcasted_iota(I32, (TOP_K, tt), 0)
    e_out = jnp.zeros((TOP_K, tt), F32)
    g_out = jnp.zeros((TOP_K, tt), F32)
    r_out = jnp.zeros((TOP_K, tt), F32)
    for k in range(TOP_K):
        e_out = jnp.where(kk == k, idxs[k], e_out)
        g_out = jnp.where(kk == k, ex[k] / den, g_out)
        rk = jnp.sum(jnp.where(sels[k], before, 0.0), axis=0, keepdims=True)
        r_out = jnp.where(kk == k, rk, r_out)
    e_ref[...] = e_out.astype(I32)
    gate_ref[...] = g_out
    rank_ref[...] = r_out.astype(I32)
    carry_ref[...] = jnp.broadcast_to(base, carry_ref.shape)
    cnt_ref[...] = carry_ref[...]


def _router(x, g, w_r, b_r):
    N, D = x.shape
    tt = ROUTER_TILE
    E = N_EXPERTS
    w_pad = jnp.pad(w_r, ((0, 0), (0, LANES - E)))
    b_pad = jnp.pad(b_r, (0, LANES - E)).reshape(1, LANES)
    kspec = pl.BlockSpec((TOP_K, tt), lambda i: (0, i))
    return pl.pallas_call(
        functools.partial(_router_kernel, tt=tt),
        grid=(N // tt,),
        in_specs=[
            pl.BlockSpec((tt, D), lambda i: (i, 0)),
            pl.BlockSpec((1, D), lambda i: (0, 0)),
            pl.BlockSpec((D, LANES), lambda i: (0, 0)),
            pl.BlockSpec((1, LANES), lambda i: (0, 0)),
        ],
        out_specs=[kspec, kspec, kspec, pl.BlockSpec((E, LANES), lambda i: (0, 0))],
        out_shape=[
            jax.ShapeDtypeStruct((TOP_K, N), I32),
            jax.ShapeDtypeStruct((TOP_K, N), F32),
            jax.ShapeDtypeStruct((TOP_K, N), I32),
            jax.ShapeDtypeStruct((E, LANES), F32),
        ],
        scratch_shapes=[pltpu.VMEM((LANES, LANES), BF16), pltpu.VMEM((E, LANES), F32),
                        pltpu.VMEM((D, 2 * LANES), BF16)],
        compiler_params=_params(("arbitrary",)),
        name="moe_router",
    )(x, g.reshape(1, D), w_pad, b_pad)


def _to_tile_rows(ref, val):
    n, D = val.shape
    nsub = D // LANES
    for s in range(nsub):
        ref[pl.ds(s, n, stride=nsub), :] = val[:, s * LANES:(s + 1) * LANES]


def _from_tile_rows(ref, n, nsub):
    return jnp.concatenate([ref[pl.ds(s, n, stride=nsub), :] for s in range(nsub)], axis=1)


def _tile_row(ref, r, nsub, count=1):
    return ref.at[pl.ds(pl.multiple_of(r * nsub, nsub), count * nsub)]


def _row_copy_wait(src, dst, sem, times):
    for _ in range(times):
        pltpu.make_async_copy(src, dst.at[pl.ds(0, src.shape[0])], sem).wait()


def _dispatch_kernel(pend_ref, nu_ref, dest_ref, x_ref, g_ref, xpad_ref, hbuf, zbuf, sems, zsem,
                     *, tt, tm, nt, nsub):
    i = pl.program_id(0)
    slot = i % 2

    @pl.when(i == 0)
    def _():
        zbuf[...] = jnp.zeros_like(zbuf)

        def seg_copy(e):
            start = pl.multiple_of(jnp.maximum(pend_ref[e] - tm, 0), tm)
            return pltpu.make_async_copy(zbuf, _tile_row(xpad_ref, start, nsub, tm), zsem)

        def tail_copy(j):
            return pltpu.make_async_copy(zbuf, _tile_row(xpad_ref, (nt - 1 - j) * tm, nsub, tm), zsem)

        for e in range(N_EXPERTS):
            seg_copy(e).start()
        for j in range(N_EXPERTS):
            pl.when(nt - 1 - j >= nu_ref[0])(tail_copy(j).start)
        for e in range(N_EXPERTS):
            seg_copy(e).wait()
        for j in range(N_EXPERTS):
            pl.when(nt - 1 - j >= nu_ref[0])(tail_copy(j).wait)

    hb = hbuf.at[slot]
    _to_tile_rows(hb, _rms(x_ref[...], g_ref[...]))

    def issue(t, carry):
        for k in range(TOP_K):
            d = dest_ref[t * TOP_K + k]
            pltpu.make_async_copy(_tile_row(hb, t, nsub), _tile_row(xpad_ref, d, nsub),
                                  sems.at[slot]).start(priority=k % 2)
        return carry

    lax.fori_loop(0, tt, issue, 0, unroll=64)

    @pl.when(i > 0)
    def _():
        _row_copy_wait(hbuf.at[1 - slot], xpad_ref, sems.at[1 - slot], TOP_K)

    @pl.when(i == pl.num_programs(0) - 1)
    def _():
        _row_copy_wait(hb, xpad_ref, sems.at[slot], TOP_K)


def _dispatch(x, g, dest, pend, n_used, P):
    N, D = x.shape
    tt = DISPATCH_TILE
    tm = MOE_TILE
    nsub = D // LANES
    grid_spec = pltpu.PrefetchScalarGridSpec(
        num_scalar_prefetch=2,
        grid=(N // tt,),
        in_specs=[
            pl.BlockSpec((tt * TOP_K,), lambda i, pe, nu: (i,), memory_space=pltpu.SMEM),
            pl.BlockSpec((tt, D), lambda i, pe, nu: (i, 0)),
            pl.BlockSpec((1, D), lambda i, pe, nu: (0, 0)),
        ],
        out_specs=pl.BlockSpec(memory_space=pl.ANY),
        scratch_shapes=[pltpu.VMEM((2, tt * nsub, LANES), F32), pltpu.VMEM((tm * nsub, LANES), F32),
                        pltpu.SemaphoreType.DMA((2,)), pltpu.SemaphoreType.DMA(())],
    )
    return pl.pallas_call(
        functools.partial(_dispatch_kernel, tt=tt, tm=tm, nt=P // tm, nsub=nsub),
        grid_spec=grid_spec,
        out_shape=jax.ShapeDtypeStruct((P * nsub, LANES), F32),
        compiler_params=_params(("arbitrary",)),
        name="moe_dispatch",
    )(pend, n_used, dest, x, g.reshape(1, D))


def _expert_kernel(te_ref, nx_ref, nu_ref, x_ref, wgu_hbm, bgu_ref, wdn_hbm, bdn_ref, o_ref,
                   wgu_f32, wdn_f32, wgu_bf, wdn_bf, sems, *, layer, F, tm, nsub):
    i = pl.program_id(0)

    def weight_copies(e):
        return (pltpu.make_async_copy(wgu_hbm.at[layer, e], wgu_f32, sems.at[0]),
                pltpu.make_async_copy(wdn_hbm.at[layer, e], wdn_f32, sems.at[1]))

    @pl.when(i < nu_ref[0])
    def _():
        e = te_ref[i]

        @pl.when(i == 0)
        def _():
            for c in weight_copies(e):
                c.start()

        @pl.when((i == 0) | (e != te_ref[jnp.maximum(i - 1, 0)]))
        def _():
            for c in weight_copies(e):
                c.wait()
            wgu_bf[...] = wgu_f32[...].astype(BF16)
            wdn_bf[...] = wdn_f32[...].astype(BF16)
            nxt = nx_ref[i]

            @pl.when(nxt != e)
            def _():
                for c in weight_copies(nxt):
                    c.start()

        x = _from_tile_rows(x_ref, tm, nsub).astype(BF16)
        gu = jnp.dot(x, wgu_bf[...], preferred_element_type=F32) + bgu_ref[0]
        gate = jnp.minimum(gu[:, :F], SWIGLU_LIMIT)
        up = jnp.clip(gu[:, F:], -SWIGLU_LIMIT, SWIGLU_LIMIT)
        glu = gate * jax.nn.sigmoid(SWIGLU_ALPHA * gate)
        a = ((up + 1.0) * glu).astype(BF16)
        _to_tile_rows(o_ref, jnp.dot(a, wdn_bf[...], preferred_element_type=F32) + bdn_ref[0])

    @pl.when(i >= nu_ref[0])
    def _():
        o_ref[...] = jnp.zeros_like(o_ref)


def _experts(tile_e, next_e, n_used, x_pad, w_gu, b_gu, w_dn, b_dn, layer):
    _, E, D, F2 = w_gu.shape
    F = F2 // 2
    tm = MOE_TILE
    nsub = D // LANES
    P = x_pad.shape[0] // nsub
    row = lambda i, te, nx, nu: (jnp.minimum(i, nu[0] - 1), 0)
    exp3 = lambda i, te, nx, nu: (te[i], 0, 0)
    grid_spec = pltpu.PrefetchScalarGridSpec(
        num_scalar_prefetch=3,
        grid=(P // tm,),
        in_specs=[
            pl.BlockSpec((tm * nsub, LANES), row),
            pl.BlockSpec(memory_space=pl.ANY),
            pl.BlockSpec((1, 1, F2), exp3),
            pl.BlockSpec(memory_space=pl.ANY),
            pl.BlockSpec((1, 1, D), exp3),
        ],
        out_specs=pl.BlockSpec((tm * nsub, LANES), lambda i, te, nx, nu: (i, 0)),
        scratch_shapes=[pltpu.VMEM((D, F2), F32), pltpu.VMEM((F, D), F32),
                        pltpu.VMEM((D, F2), BF16), pltpu.VMEM((F, D), BF16),
                        pltpu.SemaphoreType.DMA((2,))],
    )
    return pl.pallas_call(
        functools.partial(_expert_kernel, layer=layer, F=F, tm=tm, nsub=nsub),
        grid_spec=grid_spec,
        out_shape=jax.ShapeDtypeStruct((P * nsub, LANES), F32),
        compiler_params=_params(("arbitrary",)),
        name="moe_experts",
    )(tile_e, next_e, n_used, x_pad, w_gu, b_gu.reshape(E, 1, F2), w_dn, b_dn.reshape(E, 1, D))


def _combine_kernel(dest_ref, dnext_ref, x_ref, gate_ref, ypad_ref, o_ref, ybuf, sems, *, tt, nsub):
    i = pl.program_id(0)
    slot = i % 2

    def gather(idx_ref, s):
        def issue(t, carry):
            for k in range(TOP_K):
                d = idx_ref[t * TOP_K + k]
                pltpu.make_async_copy(_tile_row(ypad_ref, d, nsub), _tile_row(ybuf.at[s, k], t, nsub),
                                      sems.at[s]).start(priority=k % 2)
            return carry

        lax.fori_loop(0, tt, issue, 0, unroll=64)

    @pl.when(i == 0)
    def _():
        gather(dest_ref, slot)

    @pl.when(i + 1 < pl.num_programs(0))
    def _():
        gather(dnext_ref, 1 - slot)

    for k in range(TOP_K):
        _row_copy_wait(ybuf.at[slot, k], ypad_ref, sems.at[slot], 1)
    gate = gate_ref[...]
    x = x_ref[...]
    cols = []
    for c in range(nsub):
        acc = x[:, c * LANES:(c + 1) * LANES]
        for k in range(TOP_K):
            acc = acc + ybuf[slot, k, pl.ds(c, tt, stride=nsub), :] * gate[:, k:k + 1]
        cols.append(acc)
    o_ref[...] = jnp.concatenate(cols, axis=1)


def _combine(x, gate, dest, y_pad):
    N, D = x.shape
    tt = COMBINE_TILE
    nsub = D // LANES
    last = N // tt - 1
    return pl.pallas_call(
        functools.partial(_combine_kernel, tt=tt, nsub=nsub),
        grid=(N // tt,),
        in_specs=[
            pl.BlockSpec((tt * TOP_K,), lambda i: (i,), memory_space=pltpu.SMEM),
            pl.BlockSpec((tt * TOP_K,), lambda i: (jnp.minimum(i + 1, last),), memory_space=pltpu.SMEM),
            pl.BlockSpec((tt, D), lambda i: (i, 0)),
            pl.BlockSpec((tt, TOP_K), lambda i: (i, 0)),
            pl.BlockSpec(memory_space=pl.ANY),
        ],
        out_specs=pl.BlockSpec((tt, D), lambda i: (i, 0)),
        out_shape=jax.ShapeDtypeStruct((N, D), F32),
        scratch_shapes=[pltpu.VMEM((2, TOP_K, tt * nsub, LANES), F32), pltpu.SemaphoreType.DMA((2,))],
        compiler_params=_params(("arbitrary",)),
        name="moe_combine",
    )(dest, dest, x, gate, y_pad)


def _moe_layer(x, g, w_r, b_r, w_gu, b_gu, w_dn, b_dn, layer):
    N, D = x.shape
    E = N_EXPERTS
    tm = MOE_TILE
    e_idx, gate_t, rank, counts = _router(x, g, w_r, b_r)
    counts = counts[:, 0].astype(I32)
    padded = ((counts + tm - 1) // tm) * tm
    pend = jnp.cumsum(padded)
    pstart = pend - padded
    seg = jnp.zeros_like(rank)
    for e in range(E):
        seg = jnp.where(e_idx == e, pstart[e], seg)
    dest = (seg + rank).T.reshape(-1)
    gate = gate_t.T
    P = ((N * TOP_K + E * (tm - 1) + tm - 1) // tm) * tm
    nt = P // tm
    tiles = jnp.arange(nt, dtype=I32)
    tile_e = jnp.minimum(jnp.sum(tiles[:, None] * tm >= pend[None, :], axis=1), E - 1).astype(I32)
    n_used = (pend[-1] // tm).astype(I32)
    tile_e = jnp.where(tiles < n_used, tile_e, tile_e[n_used - 1])
    experts = jnp.arange(E, dtype=I32)
    later = lax.cummin(jnp.where(counts > 0, experts, E), reverse=True)
    after = jnp.concatenate([later[1:], jnp.full((1,), E, I32)])
    next_e = jnp.where(after < E, after, experts)[tile_e]
    x_pad = _dispatch(x, g, dest, pend.astype(I32), n_used.reshape(1), P)
    y_pad = _experts(tile_e, next_e.astype(I32), n_used.reshape(1), x_pad, w_gu, b_gu, w_dn, b_dn, layer)
    return _combine(x, gate, dest, y_pad)


def _bucket_maps():
    qi = np.arange(ATT_BLOCK)[None, :]
    kj = np.arange(2 * ATT_BLOCK)[:, None]
    delta = qi + ATT_BLOCK - kj
    buckets, valids = [], []
    max_exact = N_BUCKETS // 2
    for win, dil in DILATED_GROUPS:
        n = np.maximum(delta * dil, 0)
        nf = np.maximum(n, 1).astype(np.float32)
        large = max_exact + (np.log(nf / np.float32(max_exact)) / np.float32(math.log(REL_MAX_DIST / max_exact))
                             * np.float32(N_BUCKETS - max_exact)).astype(np.int32)
        large = np.minimum(large, N_BUCKETS - 1)
        buckets.append(np.where(n < max_exact, n, large).astype(np.int32))
        valids.append(((delta >= 0) & (delta <= win // dil)).astype(np.int32))
    return np.stack(buckets), np.stack(valids)


def _bias_kernel(tab_ref, bm_ref, valid_ref, o_ref):
    gh = pl.program_id(0)
    bm = bm_ref[0]
    acc = jnp.zeros(bm.shape, F32)
    for b in range(N_BUCKETS):
        acc = jnp.where(bm == b, tab_ref[b, gh], acc)
    band = jnp.where(valid_ref[0] > 0, acc, NEG_INF)
    key = lax.broadcasted_iota(I32, bm.shape, 0)
    o_ref[0, 0] = band
    o_ref[0, 1] = jnp.where(key >= ATT_BLOCK, band, NEG_INF)


def _bias_tables(rel_bias):
    bm, valid = _bucket_maps()
    GH = len(DILATED_GROUPS) * ATT_HEADS
    blk = (1, 2 * ATT_BLOCK, ATT_BLOCK)
    return pl.pallas_call(
        _bias_kernel,
        grid=(GH,),
        in_specs=[
            pl.BlockSpec(memory_space=pltpu.SMEM),
            pl.BlockSpec(blk, lambda i: (i // ATT_HEADS, 0, 0)),
            pl.BlockSpec(blk, lambda i: (i // ATT_HEADS, 0, 0)),
        ],
        out_specs=pl.BlockSpec((1, 2, 2 * ATT_BLOCK, ATT_BLOCK), lambda i: (i, 0, 0, 0)),
        out_shape=jax.ShapeDtypeStruct((GH, 2, 2 * ATT_BLOCK, ATT_BLOCK), F32),
        compiler_params=_params(("arbitrary",)),
        name="attn_bias",
    )(rel_bias, jnp.asarray(bm), jnp.asarray(valid))


def _cast_kernel(w_ref, o_ref):
    o_ref[...] = w_ref[...].astype(BF16)


def _to_bf16(w, layer, col_block):
    _, rows, cols = w.shape
    return pl.pallas_call(
        _cast_kernel,
        grid=(cols // col_block,),
        in_specs=[pl.BlockSpec((None, rows, col_block), lambda j: (layer, 0, j))],
        out_specs=pl.BlockSpec((rows, col_block), lambda j: (0, j)),
        out_shape=jax.ShapeDtypeStruct((rows, cols), BF16),
        compiler_params=_params(("parallel",)),
        name="cast_bf16",
    )(w)


def _qkv_kernel(x_ref, g_ref, wq_ref, wk_ref, wv_ref, qg_ref, kg_ref, qt_ref, k_ref, vt_ref,
                slab_ref, bd_ref, *, dil, R, nl):
    rc = pl.program_id(2)
    D = x_ref.shape[2]
    lanes = slab_ref.shape[2]
    first = (pl.program_id(0) == 0) & (pl.program_id(1) == 0) & (rc == 0)

    @pl.when(first)
    def _():
        r = lax.broadcasted_iota(I32, bd_ref.shape, 0)
        c = lax.broadcasted_iota(I32, bd_ref.shape, 1)
        hd_shift = HEAD_DIM.bit_length() - 1
        same_head = lax.shift_right_logical(r, hd_shift) == lax.shift_right_logical(c, hd_shift)
        bd_ref[...] = jnp.where(same_head, 1.0, 0.0).astype(BF16)

    if dil == 1:
        x = x_ref[0]
    else:
        @pl.when(rc == 0)
        def _():
            for c in range(D // lanes):
                slab_ref[c] = x_ref[0, :, c * lanes:(c + 1) * lanes]

        pieces = []
        for j in range(R):
            r = rc * R + j
            cols = [slab_ref[c, pl.ds(r, nl, stride=dil), :] for c in range(D // lanes)]
            pieces.append(jnp.concatenate(cols, axis=1))
        x = jnp.concatenate(pieces, axis=0)
    h = _rms(x, g_ref[...]).astype(BF16)
    nb = bd_ref.shape[0]

    def head_norm(y, gain):
        parts = []
        for c0 in range(0, y.shape[1], nb):
            yc = y[:, c0:c0 + nb]
            ssq = jnp.dot((yc * yc).astype(BF16), bd_ref[...], preferred_element_type=F32)
            parts.append(yc * lax.rsqrt(ssq * (1.0 / HEAD_DIM) + EPS))
        return jnp.concatenate(parts, axis=1) * gain

    q = head_norm(jnp.dot(h, wq_ref[...], preferred_element_type=F32), qg_ref[...])
    q = q * (HEAD_DIM ** -0.5)
    k = head_norm(jnp.dot(h, wk_ref[...], preferred_element_type=F32), kg_ref[...])
    v = jnp.dot(h, wv_ref[...], preferred_element_type=F32)
    blk = ATT_BLOCK
    for j in range(R):
        k_ref[0, j] = k[j * nl:(j + 1) * nl].astype(BF16)
        for c in range(nl // blk):
            rows = slice(j * nl + c * blk, j * nl + (c + 1) * blk)
            for hp in range(q.shape[1] // blk):
                cs = slice(hp * blk, (hp + 1) * blk)
                qt_ref[0, j, cs, c * blk:(c + 1) * blk] = q[rows, cs].T.astype(BF16)
                vt_ref[0, j, cs, c * blk:(c + 1) * blk] = v[rows, cs].T.astype(BF16)


def _qkv_group(x, g, w_bf, q_gain, k_gain, gi, dil):
    B, S, D = x.shape
    HD = ATT_HEADS * HEAD_DIM
    G = len(DILATED_GROUPS)
    L = S // dil
    R = min(dil, QKV_ROWS // ATT_BLOCK)
    nl = QKV_ROWS // R
    lanes = 128
    wspec = lambda s: pl.BlockSpec((D, HD), lambda b, l, r: (0, s * G + gi))
    tspec = pl.BlockSpec((1, R, HD, nl), lambda b, l, r: (b, r, 0, l))
    tshape = jax.ShapeDtypeStruct((B, dil, HD, L), BF16)
    return pl.pallas_call(
        functools.partial(_qkv_kernel, dil=dil, R=R, nl=nl),
        grid=(B, L // nl, dil // R),
        in_specs=[
            pl.BlockSpec((1, nl * dil, D), lambda b, l, r: (b, l, 0)),
            pl.BlockSpec((1, D), lambda b, l, r: (0, 0)),
            wspec(0), wspec(1), wspec(2),
            pl.BlockSpec((1, HD), lambda b, l, r: (0, 0)),
            pl.BlockSpec((1, HD), lambda b, l, r: (0, 0)),
        ],
        out_specs=[tspec, pl.BlockSpec((1, R, nl, HD), lambda b, l, r: (b, r, l, 0)), tspec],
        out_shape=[tshape, jax.ShapeDtypeStruct((B, dil, L, HD), BF16), tshape],
        scratch_shapes=[pltpu.VMEM((D // lanes, nl * dil if dil > 1 else 8, lanes), F32),
                        pltpu.VMEM((256, 256), BF16)],
        compiler_params=_params(("arbitrary", "arbitrary", "arbitrary")),
        name=f"attn_qkv_g{gi}",
    )(x, g.reshape(1, D), w_bf, w_bf, w_bf,
      jnp.tile(q_gain, ATT_HEADS).reshape(1, HD), jnp.tile(k_gain, ATT_HEADS).reshape(1, HD))


def _attn_kernel(*refs, single, nitems):
    if single:
        qt_ref, kc_ref, vtc_ref, bias_ref, o_ref, l_ref, s_ref, p_ref = refs
    else:
        qt_ref, kc_ref, kp_ref, vtc_ref, vtp_ref, bias_ref, o_ref, l_ref, s_ref, p_ref = refs
    blk = ATT_BLOCK
    pair = 2 * HEAD_DIM
    first = jnp.where(pl.program_id(2) == 0, 1, 0)
    if single:
        k0 = blk
        variants = [1] * nitems
        q_of = lambda it, cs: qt_ref[0, it, cs, :]
        k_of = lambda it, cs: kc_ref[0, it, :, cs]
        vt_of = lambda it, cs: vtc_ref[0, it, cs, :]
    else:
        k0 = 0
        variants = [first] + [0] * (nitems - 1)

        def q_of(it, cs):
            return qt_ref[0, 0, cs, it * blk:(it + 1) * blk]

        def k_of(it, cs):
            if it == 0:
                return jnp.concatenate([kp_ref[0, 0, :, cs], kc_ref[0, 0, :blk, cs]], axis=0)
            return kc_ref[0, 0, (it - 1) * blk:(it + 1) * blk, cs]

        def vt_of(it, cs):
            if it == 0:
                return jnp.concatenate([vtp_ref[0, 0, cs, :], vtc_ref[0, 0, cs, :blk]], axis=1)
            return vtc_ref[0, 0, cs, (it - 1) * blk:(it + 1) * blk]

    feat = lax.broadcasted_iota(I32, (pair, blk), 0)
    head_rows = [jnp.where(feat < HEAD_DIM, 1.0, 0.0).astype(BF16),
                 jnp.where(feat < HEAD_DIM, 0.0, 1.0).astype(BF16)]
    invs = {}
    lse_rows = {}

    def scores(it, pr):
        cs = slice(pr * pair, (pr + 1) * pair)
        qt2 = q_of(it, cs)
        k2 = k_of(it, cs)
        for half in range(2):
            qth = qt2 * head_rows[half]
            s_ref[it * ATT_HEADS + 2 * pr + half] = jnp.dot(k2, qth, preferred_element_type=F32)

    def softmax(it, pr):
        for head in (2 * pr, 2 * pr + 1):
            s = s_ref[it * ATT_HEADS + head] + bias_ref[head, variants[it], k0:, :]
            m = jnp.max(s, axis=0, keepdims=True)
            p = jnp.exp(s - m)
            den = jnp.sum(p, axis=0, keepdims=True)
            p_ref[it * ATT_HEADS + head] = p.astype(BF16)
            invs[it, head] = 1.0 / den
            lse_rows[it, head] = m + jnp.log(den)

    def outputs(it, pr):
        cs = slice(pr * pair, (pr + 1) * pair)
        vt2 = vt_of(it, cs)
        halves = []
        for half in range(2):
            head = 2 * pr + half
            vth = vt2[half * HEAD_DIM:(half + 1) * HEAD_DIM]
            ot = jnp.dot(vth, p_ref[it * ATT_HEADS + head], preferred_element_type=F32)
            halves.append(ot * invs[it, head])
        val = jnp.concatenate(halves, axis=0).T
        if single:
            o_ref[0, it, :, cs] = val
        else:
            o_ref[0, 0, it * blk:(it + 1) * blk, cs] = val

    work = [(it, pr) for it in range(nitems) for pr in range(ATT_HEADS // 2)]
    for it, pr in work:
        scores(it, pr)
    for it, pr in work:
        softmax(it, pr)
    for it, pr in work:
        outputs(it, pr)
    lse_row = lax.broadcasted_iota(I32, (blk, blk), 0)
    for it in range(nitems):
        lse_t = jnp.zeros((blk, blk), F32)
        for head in range(ATT_HEADS):
            lse_t = jnp.where(lse_row == head, lse_rows[it, head], lse_t)
        if single:
            l_ref[0, it] = lse_t.T
        else:
            l_ref[0, 0, it * blk:(it + 1) * blk, :] = lse_t.T


def _attn_group(qt, k, vt, bias, gi):
    B, dil, L, HD = k.shape
    blk = ATT_BLOCK
    nb = L // blk
    single = nb == 1
    ni = min(ATT_ITEMS, dil if single else nb)
    bias_spec = pl.BlockSpec((ATT_HEADS, 2, 2 * blk, blk), lambda b, r, n: (gi, 0, 0, 0))
    if single:
        rows = pl.BlockSpec((1, ni, blk, HD), lambda b, r, n: (b, r, 0, 0))
        cols = pl.BlockSpec((1, ni, HD, blk), lambda b, r, n: (b, r, 0, 0))
        grid = (B, dil // ni, 1)
        in_specs = [cols, rows, cols, bias_spec]
        args = (qt, k, vt, bias)
        out_specs = [rows, pl.BlockSpec((1, ni, blk, blk), lambda b, r, n: (b, r, 0, 0))]
        nk = blk
    else:
        rows = pl.BlockSpec((1, 1, ni * blk, HD), lambda b, r, n: (b, r, n, 0))
        cols = pl.BlockSpec((1, 1, HD, ni * blk), lambda b, r, n: (b, r, 0, n))
        prow = pl.BlockSpec((1, 1, blk, HD), lambda b, r, n: (b, r, jnp.maximum(ni * n - 1, 0), 0))
        pcol = pl.BlockSpec((1, 1, HD, blk), lambda b, r, n: (b, r, 0, jnp.maximum(ni * n - 1, 0)))
        grid = (B, dil, nb // ni)
        in_specs = [cols, rows, prow, cols, pcol, bias_spec]
        args = (qt, k, k, vt, vt, bias)
        out_specs = [rows, pl.BlockSpec((1, 1, ni * blk, blk), lambda b, r, n: (b, r, n, 0))]
        nk = 2 * blk
    return pl.pallas_call(
        functools.partial(_attn_kernel, single=single, nitems=ni),
        grid=grid,
        in_specs=in_specs,
        out_specs=out_specs,
        out_shape=[jax.ShapeDtypeStruct((B, dil, L, HD), F32),
                   jax.ShapeDtypeStruct((B, dil, L, blk), F32)],
        scratch_shapes=[pltpu.VMEM((ni * ATT_HEADS, nk, blk), F32),
                        pltpu.VMEM((ni * ATT_HEADS, nk, blk), BF16)],
        compiler_params=_params(("parallel", "parallel", "arbitrary")),
        name=f"attn_core_g{gi}",
    )(*args)


def _attn_out_kernel(x_ref, o0, o1, o2, l0, l1, l2, wo_ref, out_ref, oslab, lslab, ex_ref, *, T):
    lanes = oslab.shape[3]
    HD = o0.shape[3]

    @pl.when((pl.program_id(0) == 0) & (pl.program_id(1) == 0))
    def _():
        h = lax.broadcasted_iota(I32, ex_ref.shape, 0) & (ATT_BLOCK - 1)
        c = lax.broadcasted_iota(I32, ex_ref.shape, 1)
        owner = lax.shift_right_logical(c, HEAD_DIM.bit_length() - 1)
        ex_ref[...] = jnp.where(h == owner, 1.0, 0.0).astype(BF16)

    def token_order(o_ref, l_ref, dil, s):
        if dil == 1:
            return o_ref[0, 0], l_ref[0, 0]
        n = T // dil
        for r in range(dil):
            blk = o_ref[0, r]
            for c in range(HD // lanes):
                oslab[s, c, pl.ds(r, n, stride=dil), :] = blk[:, c * lanes:(c + 1) * lanes]
            lslab[s, pl.ds(r, n, stride=dil), :] = l_ref[0, r]
        o = jnp.concatenate([oslab[s, c] for c in range(HD // lanes)], axis=1)
        return o, lslab[s]

    groups = [token_order(o_ref, l_ref, dil, s)
              for s, ((_, dil), o_ref, l_ref) in enumerate(zip(DILATED_GROUPS, (o0, o1, o2), (l0, l1, l2)))]
    m = jnp.maximum(jnp.maximum(groups[0][1], groups[1][1]), groups[2][1])
    es = [jnp.exp(l - m) for _, l in groups]
    inv = 1.0 / (es[0] + es[1] + es[2])
    att = jnp.zeros((T, HD), F32)
    for (o, _), e in zip(groups, es):
        a = e * inv
        a_hi = a.astype(BF16)
        a_lo = (a - a_hi.astype(F32)).astype(BF16)
        wide = jnp.dot(jnp.concatenate([a_hi, a_lo], axis=1), ex_ref[...], preferred_element_type=F32)
        att = att + wide * o
    out_ref[0] = x_ref[0] + jnp.dot(att.astype(BF16), wo_ref[...], preferred_element_type=F32)


def _attn_out(x, os, ls, wo_bf):
    B, S, D = x.shape
    HD = wo_bf.shape[0]
    T = OUT_TILE
    lanes = 128
    ospec = lambda dil, w: pl.BlockSpec((1, dil, T // dil, w), lambda b, i: (b, 0, i, 0))
    dils = [dil for _, dil in DILATED_GROUPS]
    xspec = pl.BlockSpec((1, T, D), lambda b, i: (b, i, 0))
    return pl.pallas_call(
        functools.partial(_attn_out_kernel, T=T),
        grid=(B, S // T),
        in_specs=[xspec] + [ospec(d, HD) for d in dils] + [ospec(d, ATT_BLOCK) for d in dils]
                 + [pl.BlockSpec((HD, D), lambda b, i: (0, 0))],
        out_specs=xspec,
        out_shape=jax.ShapeDtypeStruct((B, S, D), F32),
        scratch_shapes=[pltpu.VMEM((len(dils), HD // lanes, T, lanes), F32),
                        pltpu.VMEM((len(dils), T, ATT_BLOCK), F32),
                        pltpu.VMEM((2 * ATT_BLOCK, HD), BF16)],
        compiler_params=_params(("arbitrary", "arbitrary")),
        name="attn_out",
    )(x, *os, *ls, wo_bf)


def _attention_layer(x, g, w_qkv, q_gain, k_gain, w_o, rel_bias, layer):
    HD = ATT_HEADS * HEAD_DIM
    bias = _bias_tables(rel_bias)
    wqkv_bf = _to_bf16(w_qkv, layer, HD)
    wo_bf = _to_bf16(w_o, layer, w_o.shape[2])
    os, ls = [], []
    for gi, (_, dil) in enumerate(DILATED_GROUPS):
        qt, k, vt = _qkv_group(x, g, wqkv_bf, q_gain, k_gain, gi, dil)
        o, l = _attn_group(qt, k, vt, bias, gi)
        os.append(o)
        ls.append(l)
    return _attn_out(x, os, ls, wo_bf)


def kernel(x, norm_mix_g, norm_ffn_g, pool_w, pool_scale, attn_w_qkv, attn_q_gain, attn_k_gain,
           attn_w_o, rel_bias, moe_w_router, moe_b_router, moe_w_gate_up, moe_b_gate_up,
           moe_w_down, moe_b_down):
    B, S, D = x.shape
    depth = norm_mix_g.shape[0]
    for i in range(depth):
        j = i // 2
        if i % 2 == 0:
            x = _pool_layer(x, norm_mix_g[i], pool_w[j], pool_scale[j])
        else:
            x = _attention_layer(x, norm_mix_g[i], attn_w_qkv, attn_q_gain[j], attn_k_gain[j],
                                 attn_w_o, rel_bias, j)
        x = _moe_layer(x.reshape(B * S, D), norm_ffn_g[i], moe_w_router[i], moe_b_router[i],
                       moe_w_gate_up, moe_b_gate_up[i], moe_w_down, moe_b_down[i],
                       i).reshape(B, S, D)
    return x
```

```python
import functools
import math

import numpy as np
import jax
import jax.numpy as jnp
from jax import lax
from jax.experimental import pallas as pl
from jax.experimental.pallas import tpu as pltpu

F32 = jnp.float32
BF16 = jnp.bfloat16
I32 = jnp.int32

EPS = 1e-6
POOL_WINDOWS = (2, 4, 8, 16)
POOL_HALO = 16
DILATED_GROUPS = ((128, 1), (512, 4), (2048, 16))
ATT_HEADS = 16
HEAD_DIM = 64
ATT_BLOCK = 128
ATT_ITEMS = 8
N_BUCKETS = 32
REL_MAX_DIST = 2048
NEG_INF = -1e30
N_EXPERTS = 32
TOP_K = 4
SWIGLU_LIMIT = 7.0
SWIGLU_ALPHA = 1.702

V7X_VMEM_LIMIT_BYTES = 56 * 1024 * 1024
MOE_TILE = 512
DISPATCH_TILE = 512
COMBINE_TILE = 256
ROUTER_TILE = 1024
POOL_TILE = 1024
QKV_ROWS = 1024
LANES = 128
OUT_TILE = 512


def _rms(xf, g):
    ms = jnp.mean(xf * xf, axis=-1, keepdims=True)
    return xf * lax.rsqrt(ms + EPS) * g


def _params(sem, vmem=None):
    return pltpu.CompilerParams(dimension_semantics=sem,
                                vmem_limit_bytes=vmem or V7X_VMEM_LIMIT_BYTES)


def _pool_kernel(x_ref, halo_ref, g_ref, w_ref, sc_ref, o_ref, *, ts, dg):
    i = pl.program_id(1)
    x = x_ref[0]
    g = g_ref[...]
    h = _rms(x, g)
    hh = _rms(halo_ref[0], g)
    hh = jnp.where(i == 0, 0.0, hh)
    full = jnp.concatenate([hh, h], axis=0)
    pos = i * ts + lax.broadcasted_iota(I32, (ts, 1), 0)
    outs = []
    for gi, w in enumerate(POOL_WINDOWS):
        s = full[:, gi * dg:(gi + 1) * dg]
        sh = 1
        while sh < w:
            s = s + pltpu.roll(s, sh, 0)
            sh *= 2
        s = s[POOL_HALO:]
        cnt = jnp.minimum(pos + 1, w).astype(F32)
        p = s / cnt - h[:, gi * dg:(gi + 1) * dg]
        outs.append(jnp.dot(p.astype(BF16), w_ref[gi].astype(BF16),
                            preferred_element_type=F32))
    y = jnp.concatenate(outs, axis=1) * sc_ref[...]
    o_ref[0] = x + y


def _pool_layer(x, g, w_groups, scale):
    B, S, D = x.shape
    ts = POOL_TILE
    dg = D // len(POOL_WINDOWS)
    hb = ts // POOL_HALO
    return pl.pallas_call(
        functools.partial(_pool_kernel, ts=ts, dg=dg),
        grid=(B, S // ts),
        in_specs=[
            pl.BlockSpec((1, ts, D), lambda b, i: (b, i, 0)),
            pl.BlockSpec((1, POOL_HALO, D), lambda b, i: (b, jnp.maximum(i * hb - 1, 0), 0)),
            pl.BlockSpec((1, D), lambda b, i: (0, 0)),
            pl.BlockSpec((len(POOL_WINDOWS), dg, dg), lambda b, i: (0, 0, 0)),
            pl.BlockSpec((1, D), lambda b, i: (0, 0)),
        ],
        out_specs=pl.BlockSpec((1, ts, D), lambda b, i: (b, i, 0)),
        out_shape=jax.ShapeDtypeStruct((B, S, D), F32),
        compiler_params=_params(("parallel", "parallel")),
        name="pool_mixer",
    )(x, x, g.reshape(1, D), w_groups, scale.reshape(1, D))


def _router_kernel(x_ref, g_ref, wr_ref, br_ref, e_ref, gate_ref, rank_ref, cnt_ref,
                   tri_ref, carry_ref, wcat_ref, *, tt):
    i = pl.program_id(0)
    E = N_EXPERTS

    @pl.when(i == 0)
    def _():
        r = lax.broadcasted_iota(I32, tri_ref.shape, 0)
        c = lax.broadcasted_iota(I32, tri_ref.shape, 1)
        tri_ref[...] = jnp.where(r < c, 1.0, 0.0).astype(BF16)
        carry_ref[...] = jnp.zeros_like(carry_ref)
        w = wr_ref[...]
        w_hi = w.astype(BF16)
        wcat_ref[:, :LANES] = w_hi
        wcat_ref[:, LANES:] = (w - w_hi.astype(F32)).astype(BF16)

    h = _rms(x_ref[...], g_ref[...])
    h_hi = h.astype(BF16)
    h_lo = (h - h_hi.astype(F32)).astype(BF16)
    both = jnp.dot(h_hi, wcat_ref[...], preferred_element_type=F32)
    cross = jnp.dot(h_lo, wcat_ref[:, :LANES], preferred_element_type=F32)
    logits = both[:, :LANES] + (both[:, LANES:] + cross) + br_ref[...]
    nblk = tt // LANES
    l = jnp.concatenate([logits[c * LANES:(c + 1) * LANES].T for c in range(nblk)], axis=1)[:E]
    row = lax.broadcasted_iota(I32, (E, tt), 0).astype(F32)
    vals, sels, idxs = [], [], []
    for k in range(TOP_K):
        m = jnp.max(l, axis=0, keepdims=True)
        idx = jnp.min(jnp.where(l == m, row, float(E)), axis=0, keepdims=True)
        sel = row == idx
        vals.append(m)
        sels.append(sel)
        idxs.append(idx)
        l = jnp.where(sel, -jnp.inf, l)
    ex = [jnp.exp(v - vals[0]) for v in vals]
    den = ex[0] + ex[1] + ex[2] + ex[3]
    multi = jnp.zeros((E, tt), F32)
    for sel in sels:
        multi = multi + jnp.where(sel, 1.0, 0.0)
    base = carry_ref[:, :1]
    parts = []
    for c in range(nblk):
        mc = multi[:, c * LANES:(c + 1) * LANES]
        parts.append(jnp.dot(mc.astype(BF16), tri_ref[...], preferred_element_type=F32) + base)
        base = base + jnp.sum(mc, axis=1, keepdims=True)
    before = jnp.concatenate(parts, axis=1)
    kk = lax.broadcasted_iota(I32, (TOP_K, tt), 0)
    e_out = jnp.zeros((TOP_K, tt), F32)
    g_out = jnp.zeros((TOP_K, tt), F32)
    r_out = jnp.zeros((TOP_K, tt), F32)
    for k in range(TOP_K):
        e_out = jnp.where(kk == k, idxs[k], e_out)
        g_out = jnp.where(kk == k, ex[k] / den, g_out)
        rk = jnp.sum(jnp.where(sels[k], before, 0.0), axis=0, keepdims=True)
        r_out = jnp.where(kk == k, rk, r_out)
    e_ref[...] = e_out.astype(I32)
    gate_ref[...] = g_out
    rank_ref[...] = r_out.astype(I32)
    carry_ref[...] = jnp.broadcast_to(base, carry_ref.shape)
    cnt_ref[...] = carry_ref[...]


def _router(x, g, w_r, b_r):
    N, D = x.shape
    tt = ROUTER_TILE
    E = N_EXPERTS
    w_pad = jnp.pad(w_r, ((0, 0), (0, LANES - E)))
    b_pad = jnp.pad(b_r, (0, LANES - E)).reshape(1, LANES)
    kspec = pl.BlockSpec((TOP_K, tt), lambda i: (0, i))
    return pl.pallas_call(
        functools.partial(_router_kernel, tt=tt),
        grid=(N // tt,),
        in_specs=[
            pl.BlockSpec((tt, D), lambda i: (i, 0)),
            pl.BlockSpec((1, D), lambda i: (0, 0)),
            pl.BlockSpec((D, LANES), lambda i: (0, 0)),
            pl.BlockSpec((1, LANES), lambda i: (0, 0)),
        ],
        out_specs=[kspec, kspec, kspec, pl.BlockSpec((E, LANES), lambda i: (0, 0))],
        out_shape=[
            jax.ShapeDtypeStruct((TOP_K, N), I32),
            jax.ShapeDtypeStruct((TOP_K, N), F32),
            jax.ShapeDtypeStruct((TOP_K, N), I32),
            jax.ShapeDtypeStruct((E, LANES), F32),
        ],
        scratch_shapes=[pltpu.VMEM((LANES, LANES), BF16), pltpu.VMEM((E, LANES), F32),
                        pltpu.VMEM((D, 2 * LANES), BF16)],
        compiler_params=_params(("arbitrary",)),
        name="moe_router",
    )(x, g.reshape(1, D), w_pad, b_pad)


def _to_tile_rows(ref, val):
    n, D = val.shape
    nsub = D // LANES
    for s in range(nsub):
        ref[pl.ds(s, n, stride=nsub), :] = val[:, s * LANES:(s + 1) * LANES]


def _from_tile_rows(ref, n, nsub):
    return jnp.concatenate([ref[pl.ds(s, n, stride=nsub), :] for s in range(nsub)], axis=1)


def _tile_row(ref, r, nsub, count=1):
    return ref.at[pl.ds(pl.multiple_of(r * nsub, nsub), count * nsub)]


def _row_copy_wait(src, dst, sem, times):
    for _ in range(times):
        pltpu.make_async_copy(src, dst.at[pl.ds(0, src.shape[0])], sem).wait()


def _dispatch_kernel(pend_ref, nu_ref, dest_ref, x_ref, g_ref, xpad_ref, hbuf, zbuf, sems, zsem,
                     *, tt, tm, nt, nsub):
    i = pl.program_id(0)
    slot = i % 2

    @pl.when(i == 0)
    def _():
        zbuf[...] = jnp.zeros_like(zbuf)

        def seg_copy(e):
            start = pl.multiple_of(jnp.maximum(pend_ref[e] - tm, 0), tm)
            return pltpu.make_async_copy(zbuf, _tile_row(xpad_ref, start, nsub, tm), zsem)

        def tail_copy(j):
            return pltpu.make_async_copy(zbuf, _tile_row(xpad_ref, (nt - 1 - j) * tm, nsub, tm), zsem)

        for e in range(N_EXPERTS):
            seg_copy(e).start()
        for j in range(N_EXPERTS):
            pl.when(nt - 1 - j >= nu_ref[0])(tail_copy(j).start)
        for e in range(N_EXPERTS):
            seg_copy(e).wait()
        for j in range(N_EXPERTS):
            pl.when(nt - 1 - j >= nu_ref[0])(tail_copy(j).wait)

    hb = hbuf.at[slot]
    _to_tile_rows(hb, _rms(x_ref[...], g_ref[...]))

    def issue(t, carry):
        for k in range(TOP_K):
            d = dest_ref[t * TOP_K + k]
            pltpu.make_async_copy(_tile_row(hb, t, nsub), _tile_row(xpad_ref, d, nsub),
                                  sems.at[slot]).start(priority=k % 2)
        return carry

    lax.fori_loop(0, tt, issue, 0, unroll=64)

    @pl.when(i > 0)
    def _():
        _row_copy_wait(hbuf.at[1 - slot], xpad_ref, sems.at[1 - slot], TOP_K)

    @pl.when(i == pl.num_programs(0) - 1)
    def _():
        _row_copy_wait(hb, xpad_ref, sems.at[slot], TOP_K)


def _dispatch(x, g, dest, pend, n_used, P):
    N, D = x.shape
    tt = DISPATCH_TILE
    tm = MOE_TILE
    nsub = D // LANES
    grid_spec = pltpu.PrefetchScalarGridSpec(
        num_scalar_prefetch=2,
        grid=(N // tt,),
        in_specs=[
            pl.BlockSpec((tt * TOP_K,), lambda i, pe, nu: (i,), memory_space=pltpu.SMEM),
            pl.BlockSpec((tt, D), lambda i, pe, nu: (i, 0)),
            pl.BlockSpec((1, D), lambda i, pe, nu: (0, 0)),
        ],
        out_specs=pl.BlockSpec(memory_space=pl.ANY),
        scratch_shapes=[pltpu.VMEM((2, tt * nsub, LANES), F32), pltpu.VMEM((tm * nsub, LANES), F32),
                        pltpu.SemaphoreType.DMA((2,)), pltpu.SemaphoreType.DMA(())],
    )
    return pl.pallas_call(
        functools.partial(_dispatch_kernel, tt=tt, tm=tm, nt=P // tm, nsub=nsub),
        grid_spec=grid_spec,
        out_shape=jax.ShapeDtypeStruct((P * nsub, LANES), F32),
        compiler_params=_params(("arbitrary",)),
        name="moe_dispatch",
    )(pend, n_used, dest, x, g.reshape(1, D))


def _expert_kernel(te_ref, nx_ref, nu_ref, x_ref, wgu_hbm, bgu_ref, wdn_hbm, bdn_ref, o_ref,
                   wgu_f32, wdn_f32, wgu_bf, wdn_bf, sems, *, layer, F, tm, nsub):
    i = pl.program_id(0)

    def weight_copies(e):
        return (pltpu.make_async_copy(wgu_hbm.at[layer, e], wgu_f32, sems.at[0]),
                pltpu.make_async_copy(wdn_hbm.at[layer, e], wdn_f32, sems.at[1]))

    @pl.when(i < nu_ref[0])
    def _():
        e = te_ref[i]

        @pl.when(i == 0)
        def _():
            for c in weight_copies(e):
                c.start()

        @pl.when((i == 0) | (e != te_ref[jnp.maximum(i - 1, 0)]))
        def _():
            for c in weight_copies(e):
                c.wait()
            wgu_bf[...] = wgu_f32[...].astype(BF16)
            wdn_bf[...] = wdn_f32[...].astype(BF16)
            nxt = nx_ref[i]

            @pl.when(nxt != e)
            def _():
                for c in weight_copies(nxt):
                    c.start()

        x = _from_tile_rows(x_ref, tm, nsub).astype(BF16)
        gu = jnp.dot(x, wgu_bf[...], preferred_element_type=F32) + bgu_ref[0]
        gate = jnp.minimum(gu[:, :F], SWIGLU_LIMIT)
        up = jnp.clip(gu[:, F:], -SWIGLU_LIMIT, SWIGLU_LIMIT)
        glu = gate * jax.nn.sigmoid(SWIGLU_ALPHA * gate)
        a = ((up + 1.0) * glu).astype(BF16)
        _to_tile_rows(o_ref, jnp.dot(a, wdn_bf[...], preferred_element_type=F32) + bdn_ref[0])

    @pl.when(i >= nu_ref[0])
    def _():
        o_ref[...] = jnp.zeros_like(o_ref)


def _experts(tile_e, next_e, n_used, x_pad, w_gu, b_gu, w_dn, b_dn, layer):
    _, E, D, F2 = w_gu.shape
    F = F2 // 2
    tm = MOE_TILE
    nsub = D // LANES
    P = x_pad.shape[0] // nsub
    row = lambda i, te, nx, nu: (jnp.minimum(i, nu[0] - 1), 0)
    exp3 = lambda i, te, nx, nu: (te[i], 0, 0)
    grid_spec = pltpu.PrefetchScalarGridSpec(
        num_scalar_prefetch=3,
        grid=(P // tm,),
        in_specs=[
            pl.BlockSpec((tm * nsub, LANES), row),
            pl.BlockSpec(memory_space=pl.ANY),
            pl.BlockSpec((1, 1, F2), exp3),
            pl.BlockSpec(memory_space=pl.ANY),
            pl.BlockSpec((1, 1, D), exp3),
        ],
        out_specs=pl.BlockSpec((tm * nsub, LANES), lambda i, te, nx, nu: (i, 0)),
        scratch_shapes=[pltpu.VMEM((D, F2), F32), pltpu.VMEM((F, D), F32),
                        pltpu.VMEM((D, F2), BF16), pltpu.VMEM((F, D), BF16),
                        pltpu.SemaphoreType.DMA((2,))],
    )
    return pl.pallas_call(
        functools.partial(_expert_kernel, layer=layer, F=F, tm=tm, nsub=nsub),
        grid_spec=grid_spec,
        out_shape=jax.ShapeDtypeStruct((P * nsub, LANES), F32),
        compiler_params=_params(("arbitrary",)),
        name="moe_experts",
    )(tile_e, next_e, n_used, x_pad, w_gu, b_gu.reshape(E, 1, F2), w_dn, b_dn.reshape(E, 1, D))


def _combine_kernel(dest_ref, dnext_ref, x_ref, gate_ref, ypad_ref, o_ref, ybuf, sems, *, tt, nsub):
    i = pl.program_id(0)
    slot = i % 2

    def gather(idx_ref, s):
        def issue(t, carry):
            for k in range(TOP_K):
                d = idx_ref[t * TOP_K + k]
                pltpu.make_async_copy(_tile_row(ypad_ref, d, nsub), _tile_row(ybuf.at[s, k], t, nsub),
                                      sems.at[s]).start(priority=k % 2)
            return carry

        lax.fori_loop(0, tt, issue, 0, unroll=64)

    @pl.when(i == 0)
    def _():
        gather(dest_ref, slot)

    @pl.when(i + 1 < pl.num_programs(0))
    def _():
        gather(dnext_ref, 1 - slot)

    for k in range(TOP_K):
        _row_copy_wait(ybuf.at[slot, k], ypad_ref, sems.at[slot], 1)
    gate = gate_ref[...]
    x = x_ref[...]
    cols = []
    for c in range(nsub):
        acc = x[:, c * LANES:(c + 1) * LANES]
        for k in range(TOP_K):
            acc = acc + ybuf[slot, k, pl.ds(c, tt, stride=nsub), :] * gate[:, k:k + 1]
        cols.append(acc)
    o_ref[...] = jnp.concatenate(cols, axis=1)


def _combine(x, gate, dest, y_pad):
    N, D = x.shape
    tt = COMBINE_TILE
    nsub = D // LANES
    last = N // tt - 1
    return pl.pallas_call(
        functools.partial(_combine_kernel, tt=tt, nsub=nsub),
        grid=(N // tt,),
        in_specs=[
            pl.BlockSpec((tt * TOP_K,), lambda i: (i,), memory_space=pltpu.SMEM),
            pl.BlockSpec((tt * TOP_K,), lambda i: (jnp.minimum(i + 1, last),), memory_space=pltpu.SMEM),
            pl.BlockSpec((tt, D), lambda i: (i, 0)),
            pl.BlockSpec((tt, TOP_K), lambda i: (i, 0)),
            pl.BlockSpec(memory_space=pl.ANY),
        ],
        out_specs=pl.BlockSpec((tt, D), lambda i: (i, 0)),
        out_shape=jax.ShapeDtypeStruct((N, D), F32),
        scratch_shapes=[pltpu.VMEM((2, TOP_K, tt * nsub, LANES), F32), pltpu.SemaphoreType.DMA((2,))],
        compiler_params=_params(("arbitrary",)),
        name="moe_combine",
    )(dest, dest, x, gate, y_pad)


def _moe_layer(x, g, w_r, b_r, w_gu, b_gu, w_dn, b_dn, layer):
    N, D = x.shape
    E = N_EXPERTS
    tm = MOE_TILE
    e_idx, gate_t, rank, counts = _router(x, g, w_r, b_r)
    counts = counts[:, 0].astype(I32)
    padded = ((counts + tm - 1) // tm) * tm
    pend = jnp.cumsum(padded)
    pstart = pend - padded
    seg = jnp.zeros_like(rank)
    for e in range(E):
        seg = jnp.where(e_idx == e, pstart[e], seg)
    dest = (seg + rank).T.reshape(-1)
    gate = gate_t.T
    P = ((N * TOP_K + E * (tm - 1) + tm - 1) // tm) * tm
    nt = P // tm
    tiles = jnp.arange(nt, dtype=I32)
    tile_e = jnp.minimum(jnp.sum(tiles[:, None] * tm >= pend[None, :], axis=1), E - 1).astype(I32)
    n_used = (pend[-1] // tm).astype(I32)
    tile_e = jnp.where(tiles < n_used, tile_e, tile_e[n_used - 1])
    experts = jnp.arange(E, dtype=I32)
    later = lax.cummin(jnp.where(counts > 0, experts, E), reverse=True)
    after = jnp.concatenate([later[1:], jnp.full((1,), E, I32)])
    next_e = jnp.where(after < E, after, experts)[tile_e]
    x_pad = _dispatch(x, g, dest, pend.astype(I32), n_used.reshape(1), P)
    y_pad = _experts(tile_e, next_e.astype(I32), n_used.reshape(1), x_pad, w_gu, b_gu, w_dn, b_dn, layer)
    return _combine(x, gate, dest, y_pad)


def _bucket_maps():
    qi = np.arange(ATT_BLOCK)[None, :]
    kj = np.arange(2 * ATT_BLOCK)[:, None]
    delta = qi + ATT_BLOCK - kj
    buckets, valids = [], []
    max_exact = N_BUCKETS // 2
    for win, dil in DILATED_GROUPS:
        n = np.maximum(delta * dil, 0)
        nf = np.maximum(n, 1).astype(np.float32)
        large = max_exact + (np.log(nf / np.float32(max_exact)) / np.float32(math.log(REL_MAX_DIST / max_exact))
                             * np.float32(N_BUCKETS - max_exact)).astype(np.int32)
        large = np.minimum(large, N_BUCKETS - 1)
        buckets.append(np.where(n < max_exact, n, large).astype(np.int32))
        valids.append(((delta >= 0) & (delta <= win // dil)).astype(np.int32))
    return np.stack(buckets), np.stack(valids)


def _bias_kernel(tab_ref, bm_ref, valid_ref, o_ref):
    gh = pl.program_id(0)
    bm = bm_ref[0]
    acc = jnp.zeros(bm.shape, F32)
    for b in range(N_BUCKETS):
        acc = jnp.where(bm == b, tab_ref[b, gh], acc)
    band = jnp.where(valid_ref[0] > 0, acc, NEG_INF)
    key = lax.broadcasted_iota(I32, bm.shape, 0)
    o_ref[0, 0] = band
    o_ref[0, 1] = jnp.where(key >= ATT_BLOCK, band, NEG_INF)


def _bias_tables(rel_bias):
    bm, valid = _bucket_maps()
    GH = len(DILATED_GROUPS) * ATT_HEADS
    blk = (1, 2 * ATT_BLOCK, ATT_BLOCK)
    return pl.pallas_call(
        _bias_kernel,
        grid=(GH,),
        in_specs=[
            pl.BlockSpec(memory_space=pltpu.SMEM),
            pl.BlockSpec(blk, lambda i: (i // ATT_HEADS, 0, 0)),
            pl.BlockSpec(blk, lambda i: (i // ATT_HEADS, 0, 0)),
        ],
        out_specs=pl.BlockSpec((1, 2, 2 * ATT_BLOCK, ATT_BLOCK), lambda i: (i, 0, 0, 0)),
        out_shape=jax.ShapeDtypeStruct((GH, 2, 2 * ATT_BLOCK, ATT_BLOCK), F32),
        compiler_params=_params(("arbitrary",)),
        name="attn_bias",
    )(rel_bias, jnp.asarray(bm), jnp.asarray(valid))


def _cast_kernel(w_ref, o_ref):
    o_ref[...] = w_ref[...].astype(BF16)


def _to_bf16(w, layer, col_block):
    _, rows, cols = w.shape
    return pl.pallas_call(
        _cast_kernel,
        grid=(cols // col_block,),
        in_specs=[pl.BlockSpec((None, rows, col_block), lambda j: (layer, 0, j))],
        out_specs=pl.BlockSpec((rows, col_block), lambda j: (0, j)),
        out_shape=jax.ShapeDtypeStruct((rows, cols), BF16),
        compiler_params=_params(("parallel",)),
        name="cast_bf16",
    )(w)


def _qkv_kernel(x_ref, g_ref, wq_ref, wk_ref, wv_ref, qg_ref, kg_ref, qt_ref, k_ref, vt_ref,
                slab_ref, bd_ref, *, dil, R, nl):
    rc = pl.program_id(2)
    D = x_ref.shape[2]
    lanes = slab_ref.shape[2]
    first = (pl.program_id(0) == 0) & (pl.program_id(1) == 0) & (rc == 0)

    @pl.when(first)
    def _():
        r = lax.broadcasted_iota(I32, bd_ref.shape, 0)
        c = lax.broadcasted_iota(I32, bd_ref.shape, 1)
        hd_shift = HEAD_DIM.bit_length() - 1
        same_head = lax.shift_right_logical(r, hd_shift) == lax.shift_right_logical(c, hd_shift)
        bd_ref[...] = jnp.where(same_head, 1.0, 0.0).astype(BF16)

    if dil == 1:
        x = x_ref[0]
    else:
        @pl.when(rc == 0)
        def _():
            for c in range(D // lanes):
                slab_ref[c] = x_ref[0, :, c * lanes:(c + 1) * lanes]

        pieces = []
        for j in range(R):
            r = rc * R + j
            cols = [slab_ref[c, pl.ds(r, nl, stride=dil), :] for c in range(D // lanes)]
            pieces.append(jnp.concatenate(cols, axis=1))
        x = jnp.concatenate(pieces, axis=0)
    h = _rms(x, g_ref[...]).astype(BF16)
    nb = bd_ref.shape[0]

    def head_norm(y, gain):
        parts = []
        for c0 in range(0, y.shape[1], nb):
            yc = y[:, c0:c0 + nb]
            ssq = jnp.dot((yc * yc).astype(BF16), bd_ref[...], preferred_element_type=F32)
            parts.append(yc * lax.rsqrt(ssq * (1.0 / HEAD_DIM) + EPS))
        return jnp.concatenate(parts, axis=1) * gain

    q = head_norm(jnp.dot(h, wq_ref[...], preferred_element_type=F32), qg_ref[...])
    q = q * (HEAD_DIM ** -0.5)
    k = head_norm(jnp.dot(h, wk_ref[...], preferred_element_type=F32), kg_ref[...])
    v = jnp.dot(h, wv_ref[...], preferred_element_type=F32)
    blk = ATT_BLOCK
    for j in range(R):
        k_ref[0, j] = k[j * nl:(j + 1) * nl].astype(BF16)
        for c in range(nl // blk):
            rows = slice(j * nl + c * blk, j * nl + (c + 1) * blk)
            for hp in range(q.shape[1] // blk):
                cs = slice(hp * blk, (hp + 1) * blk)
                qt_ref[0, j, cs, c * blk:(c + 1) * blk] = q[rows, cs].T.astype(BF16)
                vt_ref[0, j, cs, c * blk:(c + 1) * blk] = v[rows, cs].T.astype(BF16)


def _qkv_group(x, g, w_bf, q_gain, k_gain, gi, dil):
    B, S, D = x.shape
    HD = ATT_HEADS * HEAD_DIM
    G = len(DILATED_GROUPS)
    L = S // dil
    R = min(dil, QKV_ROWS // ATT_BLOCK)
    nl = QKV_ROWS // R
    lanes = 128
    wspec = lambda s: pl.BlockSpec((D, HD), lambda b, l, r: (0, s * G + gi))
    tspec = pl.BlockSpec((1, R, HD, nl), lambda b, l, r: (b, r, 0, l))
    tshape = jax.ShapeDtypeStruct((B, dil, HD, L), BF16)
    return pl.pallas_call(
        functools.partial(_qkv_kernel, dil=dil, R=R, nl=nl),
        grid=(B, L // nl, dil // R),
        in_specs=[
            pl.BlockSpec((1, nl * dil, D), lambda b, l, r: (b, l, 0)),
            pl.BlockSpec((1, D), lambda b, l, r: (0, 0)),
            wspec(0), wspec(1), wspec(2),
            pl.BlockSpec((1, HD), lambda b, l, r: (0, 0)),
            pl.BlockSpec((1, HD), lambda b, l, r: (0, 0)),
        ],
        out_specs=[tspec, pl.BlockSpec((1, R, nl, HD), lambda b, l, r: (b, r, l, 0)), tspec],
        out_shape=[tshape, jax.ShapeDtypeStruct((B, dil, L, HD), BF16), tshape],
        scratch_shapes=[pltpu.VMEM((D // lanes, nl * dil if dil > 1 else 8, lanes), F32),
                        pltpu.VMEM((256, 256), BF16)],
        compiler_params=_params(("arbitrary", "arbitrary", "arbitrary")),
        name=f"attn_qkv_g{gi}",
    )(x, g.reshape(1, D), w_bf, w_bf, w_bf,
      jnp.tile(q_gain, ATT_HEADS).reshape(1, HD), jnp.tile(k_gain, ATT_HEADS).reshape(1, HD))


def _attn_kernel(*refs, single, nitems):
    if single:
        qt_ref, kc_ref, vtc_ref, bias_ref, o_ref, l_ref, s_ref, p_ref = refs
    else:
        qt_ref, kc_ref, kp_ref, vtc_ref, vtp_ref, bias_ref, o_ref, l_ref, s_ref, p_ref = refs
    blk = ATT_BLOCK
    pair = 2 * HEAD_DIM
    first = jnp.where(pl.program_id(2) == 0, 1, 0)
    if single:
        k0 = blk
        variants = [1] * nitems
        q_of = lambda it, cs: qt_ref[0, it, cs, :]
        k_of = lambda it, cs: kc_ref[0, it, :, cs]
        vt_of = lambda it, cs: vtc_ref[0, it, cs, :]
    else:
        k0 = 0
        variants = [first] + [0] * (nitems - 1)

        def q_of(it, cs):
            return qt_ref[0, 0, cs, it * blk:(it + 1) * blk]

        def k_of(it, cs):
            if it == 0:
                return jnp.concatenate([kp_ref[0, 0, :, cs], kc_ref[0, 0, :blk, cs]], axis=0)
            return kc_ref[0, 0, (it - 1) * blk:(it + 1) * blk, cs]

        def vt_of(it, cs):
            if it == 0:
                return jnp.concatenate([vtp_ref[0, 0, cs, :], vtc_ref[0, 0, cs, :blk]], axis=1)
            return vtc_ref[0, 0, cs, (it - 1) * blk:(it + 1) * blk]

    feat = lax.broadcasted_iota(I32, (pair, blk), 0)
    head_rows = [jnp.where(feat < HEAD_DIM, 1.0, 0.0).astype(BF16),
                 jnp.where(feat < HEAD_DIM, 0.0, 1.0).astype(BF16)]
    invs = {}
    lse_rows = {}

    def scores(it, pr):
        cs = slice(pr * pair, (pr + 1) * pair)
        qt2 = q_of(it, cs)
        k2 = k_of(it, cs)
        for half in range(2):
            qth = qt2 * head_rows[half]
            s_ref[it * ATT_HEADS + 2 * pr + half] = jnp.dot(k2, qth, preferred_element_type=F32)

    def softmax(it, pr):
        for head in (2 * pr, 2 * pr + 1):
            s = s_ref[it * ATT_HEADS + head] + bias_ref[head, variants[it], k0:, :]
            m = jnp.max(s, axis=0, keepdims=True)
            p = jnp.exp(s - m)
            den = jnp.sum(p, axis=0, keepdims=True)
            p_ref[it * ATT_HEADS + head] = p.astype(BF16)
            invs[it, head] = 1.0 / den
            lse_rows[it, head] = m + jnp.log(den)

    def outputs(it, pr):
        cs = slice(pr * pair, (pr + 1) * pair)
        vt2 = vt_of(it, cs)
        halves = []
        for half in range(2):
            head = 2 * pr + half
            vth = vt2[half * HEAD_DIM:(half + 1) * HEAD_DIM]
            ot = jnp.dot(vth, p_ref[it * ATT_HEADS + head], preferred_element_type=F32)
            halves.append(ot * invs[it, head])
        val = jnp.concatenate(halves, axis=0).T
        if single:
            o_ref[0, it, :, cs] = val
        else:
            o_ref[0, 0, it * blk:(it + 1) * blk, cs] = val

    work = [(it, pr) for it in range(nitems) for pr in range(ATT_HEADS // 2)]
    for it, pr in work:
        scores(it, pr)
    for it, pr in work:
        softmax(it, pr)
    for it, pr in work:
        outputs(it, pr)
    lse_row = lax.broadcasted_iota(I32, (blk, blk), 0)
    for it in range(nitems):
        lse_t = jnp.zeros((blk, blk), F32)
        for head in range(ATT_HEADS):
            lse_t = jnp.where(lse_row == head, lse_rows[it, head], lse_t)
        if single:
            l_ref[0, it] = lse_t.T
        else:
            l_ref[0, 0, it * blk:(it + 1) * blk, :] = lse_t.T


def _attn_group(qt, k, vt, bias, gi):
    B, dil, L, HD = k.shape
    blk = ATT_BLOCK
    nb = L // blk
    single = nb == 1
    ni = min(ATT_ITEMS, dil if single else nb)
    bias_spec = pl.BlockSpec((ATT_HEADS, 2, 2 * blk, blk), lambda b, r, n: (gi, 0, 0, 0))
    if single:
        rows = pl.BlockSpec((1, ni, blk, HD), lambda b, r, n: (b, r, 0, 0))
        cols = pl.BlockSpec((1, ni, HD, blk), lambda b, r, n: (b, r, 0, 0))
        grid = (B, dil // ni, 1)
        in_specs = [cols, rows, cols, bias_spec]
        args = (qt, k, vt, bias)
        out_specs = [rows, pl.BlockSpec((1, ni, blk, blk), lambda b, r, n: (b, r, 0, 0))]
        nk = blk
    else:
        rows = pl.BlockSpec((1, 1, ni * blk, HD), lambda b, r, n: (b, r, n, 0))
        cols = pl.BlockSpec((1, 1, HD, ni * blk), lambda b, r, n: (b, r, 0, n))
        prow = pl.BlockSpec((1, 1, blk, HD), lambda b, r, n: (b, r, jnp.maximum(ni * n - 1, 0), 0))
        pcol = pl.BlockSpec((1, 1, HD, blk), lambda b, r, n: (b, r, 0, jnp.maximum(ni * n - 1, 0)))
        grid = (B, dil, nb // ni)
        in_specs = [cols, rows, prow, cols, pcol, bias_spec]
        args = (qt, k, k, vt, vt, bias)
        out_specs = [rows, pl.BlockSpec((1, 1, ni * blk, blk), lambda b, r, n: (b, r, n, 0))]
        nk = 2 * blk
    return pl.pallas_call(
        functools.partial(_attn_kernel, single=single, nitems=ni),
        grid=grid,
        in_specs=in_specs,
        out_specs=out_specs,
        out_shape=[jax.ShapeDtypeStruct((B, dil, L, HD), F32),
                   jax.ShapeDtypeStruct((B, dil, L, blk), F32)],
        scratch_shapes=[pltpu.VMEM((ni * ATT_HEADS, nk, blk), F32),
                        pltpu.VMEM((ni * ATT_HEADS, nk, blk), BF16)],
        compiler_params=_params(("parallel", "parallel", "arbitrary")),
        name=f"attn_core_g{gi}",
    )(*args)


def _attn_out_kernel(x_ref, o0, o1, o2, l0, l1, l2, wo_ref, out_ref, oslab, lslab, ex_ref, *, T):
    lanes = oslab.shape[3]
    HD = o0.shape[3]

    @pl.when((pl.program_id(0) == 0) & (pl.program_id(1) == 0))
    def _():
        h = lax.broadcasted_iota(I32, ex_ref.shape, 0) & (ATT_BLOCK - 1)
        c = lax.broadcasted_iota(I32, ex_ref.shape, 1)
        owner = lax.shift_right_logical(c, HEAD_DIM.bit_length() - 1)
        ex_ref[...] = jnp.where(h == owner, 1.0, 0.0).astype(BF16)

    def token_order(o_ref, l_ref, dil, s):
        if dil == 1:
            return o_ref[0, 0], l_ref[0, 0]
        n = T // dil
        for r in range(dil):
            blk = o_ref[0, r]
            for c in range(HD // lanes):
                oslab[s, c, pl.ds(r, n, stride=dil), :] = blk[:, c * lanes:(c + 1) * lanes]
            lslab[s, pl.ds(r, n, stride=dil), :] = l_ref[0, r]
        o = jnp.concatenate([oslab[s, c] for c in range(HD // lanes)], axis=1)
        return o, lslab[s]

    groups = [token_order(o_ref, l_ref, dil, s)
              for s, ((_, dil), o_ref, l_ref) in enumerate(zip(DILATED_GROUPS, (o0, o1, o2), (l0, l1, l2)))]
    m = jnp.maximum(jnp.maximum(groups[0][1], groups[1][1]), groups[2][1])
    es = [jnp.exp(l - m) for _, l in groups]
    inv = 1.0 / (es[0] + es[1] + es[2])
    att = jnp.zeros((T, HD), F32)
    for (o, _), e in zip(groups, es):
        a = e * inv
        a_hi = a.astype(BF16)
        a_lo = (a - a_hi.astype(F32)).astype(BF16)
        wide = jnp.dot(jnp.concatenate([a_hi, a_lo], axis=1), ex_ref[...], preferred_element_type=F32)
        att = att + wide * o
    out_ref[0] = x_ref[0] + jnp.dot(att.astype(BF16), wo_ref[...], preferred_element_type=F32)


def _attn_out(x, os, ls, wo_bf):
    B, S, D = x.shape
    HD = wo_bf.shape[0]
    T = OUT_TILE
    lanes = 128
    ospec = lambda dil, w: pl.BlockSpec((1, dil, T // dil, w), lambda b, i: (b, 0, i, 0))
    dils = [dil for _, dil in DILATED_GROUPS]
    xspec = pl.BlockSpec((1, T, D), lambda b, i: (b, i, 0))
    return pl.pallas_call(
        functools.partial(_attn_out_kernel, T=T),
        grid=(B, S // T),
        in_specs=[xspec] + [ospec(d, HD) for d in dils] + [ospec(d, ATT_BLOCK) for d in dils]
                 + [pl.BlockSpec((HD, D), lambda b, i: (0, 0))],
        out_specs=xspec,
        out_shape=jax.ShapeDtypeStruct((B, S, D), F32),
        scratch_shapes=[pltpu.VMEM((len(dils), HD // lanes, T, lanes), F32),
                        pltpu.VMEM((len(dils), T, ATT_BLOCK), F32),
                        pltpu.VMEM((2 * ATT_BLOCK, HD), BF16)],
        compiler_params=_params(("arbitrary", "arbitrary")),
        name="attn_out",
    )(x, *os, *ls, wo_bf)


def _attention_layer(x, g, w_qkv, q_gain, k_gain, w_o, rel_bias, layer):
    HD = ATT_HEADS * HEAD_DIM
    bias = _bias_tables(rel_bias)
    wqkv_bf = _to_bf16(w_qkv, layer, HD)
    wo_bf = _to_bf16(w_o, layer, w_o.shape[2])
    os, ls = [], []
    for gi, (_, dil) in enumerate(DILATED_GROUPS):
        qt, k, vt = _qkv_group(x, g, wqkv_bf, q_gain, k_gain, gi, dil)
        o, l = _attn_group(qt, k, vt, bias, gi)
        os.append(o)
        ls.append(l)
    return _attn_out(x, os, ls, wo_bf)


def kernel(x, norm_mix_g, norm_ffn_g, pool_w, pool_scale, attn_w_qkv, attn_q_gain, attn_k_gain,
           attn_w_o, rel_bias, moe_w_router, moe_b_router, moe_w_gate_up, moe_b_gate_up,
           moe_w_down, moe_b_down):
    B, S, D = x.shape
    depth = norm_mix_g.shape[0]
    for i in range(depth):
        j = i // 2
        if i % 2 == 0:
            x = _pool_layer(x, norm_mix_g[i], pool_w[j], pool_scale[j])
        else:
            x = _attention_layer(x, norm_mix_g[i], attn_w_qkv, attn_q_gain[j], attn_k_gain[j],
                                 attn_w_o, rel_bias, j)
        x = _moe_layer(x.reshape(B * S, D), norm_ffn_g[i], moe_w_router[i], moe_b_router[i],
                       moe_w_gate_up, moe_b_gate_up[i], moe_w_down, moe_b_down[i],
                       i).reshape(B, S, D)
    return x
```

```python
import functools
import math

import numpy as np
import jax
import jax.numpy as jnp
from jax import lax
from jax.experimental import pallas as pl
from jax.experimental.pallas import tpu as pltpu

F32 = jnp.float32
BF16 = jnp.bfloat16
I32 = jnp.int32

EPS = 1e-6
POOL_WINDOWS = (2, 4, 8, 16)
POOL_HALO = 16
DILATED_GROUPS = ((128, 1), (512, 4), (2048, 16))
ATT_HEADS = 16
HEAD_DIM = 64
ATT_BLOCK = 128
ATT_ITEMS = 8
N_BUCKETS = 32
REL_MAX_DIST = 2048
NEG_INF = -1e30
N_EXPERTS = 32
TOP_K = 4
SWIGLU_LIMIT = 7.0
SWIGLU_ALPHA = 1.702

V7X_VMEM_LIMIT_BYTES = 56 * 1024 * 1024
MOE_TILE = 512
DISPATCH_TILE = 512
COMBINE_TILE = 256
ROUTER_TILE = 1024
POOL_TILE = 1024
QKV_ROWS = 1024
LANES = 128
OUT_TILE = 512


def _rms(xf, g):
    ms = jnp.mean(xf * xf, axis=-1, keepdims=True)
    return xf * lax.rsqrt(ms + EPS) * g


def _params(sem, vmem=None):
    return pltpu.CompilerParams(dimension_semantics=sem,
                                vmem_limit_bytes=vmem or V7X_VMEM_LIMIT_BYTES)


def _pool_kernel(x_ref, halo_ref, g_ref, w_ref, sc_ref, o_ref, *, ts, dg):
    i = pl.program_id(1)
    x = x_ref[0]
    g = g_ref[...]
    h = _rms(x, g)
    hh = _rms(halo_ref[0], g)
    hh = jnp.where(i == 0, 0.0, hh)
    full = jnp.concatenate([hh, h], axis=0)
    pos = i * ts + lax.broadcasted_iota(I32, (ts, 1), 0)
    outs = []
    for gi, w in enumerate(POOL_WINDOWS):
        s = full[:, gi * dg:(gi + 1) * dg]
        sh = 1
        while sh < w:
            s = s + pltpu.roll(s, sh, 0)
            sh *= 2
        s = s[POOL_HALO:]
        cnt = jnp.minimum(pos + 1, w).astype(F32)
        p = s / cnt - h[:, gi * dg:(gi + 1) * dg]
        outs.append(jnp.dot(p.astype(BF16), w_ref[gi].astype(BF16),
                            preferred_element_type=F32))
    y = jnp.concatenate(outs, axis=1) * sc_ref[...]
    o_ref[0] = x + y


def _pool_layer(x, g, w_groups, scale):
    B, S, D = x.shape
    ts = POOL_TILE
    dg = D // len(POOL_WINDOWS)
    hb = ts // POOL_HALO
    return pl.pallas_call(
        functools.partial(_pool_kernel, ts=ts, dg=dg),
        grid=(B, S // ts),
        in_specs=[
            pl.BlockSpec((1, ts, D), lambda b, i: (b, i, 0)),
            pl.BlockSpec((1, POOL_HALO, D), lambda b, i: (b, jnp.maximum(i * hb - 1, 0), 0)),
            pl.BlockSpec((1, D), lambda b, i: (0, 0)),
            pl.BlockSpec((len(POOL_WINDOWS), dg, dg), lambda b, i: (0, 0, 0)),
            pl.BlockSpec((1, D), lambda b, i: (0, 0)),
        ],
        out_specs=pl.BlockSpec((1, ts, D), lambda b, i: (b, i, 0)),
        out_shape=jax.ShapeDtypeStruct((B, S, D), F32),
        compiler_params=_params(("parallel", "parallel")),
        name="pool_mixer",
    )(x, x, g.reshape(1, D), w_groups, scale.reshape(1, D))


def _router_kernel(x_ref, g_ref, wr_ref, br_ref, e_ref, gate_ref, rank_ref, cnt_ref,
                   tri_ref, carry_ref, wcat_ref, *, tt):
    i = pl.program_id(0)
    E = N_EXPERTS

    @pl.when(i == 0)
    def _():
        r = lax.broadcasted_iota(I32, tri_ref.shape, 0)
        c = lax.broadcasted_iota(I32, tri_ref.shape, 1)
        tri_ref[...] = jnp.where(r < c, 1.0, 0.0).astype(BF16)
        carry_ref[...] = jnp.zeros_like(carry_ref)
        w = wr_ref[...]
        w_hi = w.astype(BF16)
        wcat_ref[:, :LANES] = w_hi
        wcat_ref[:, LANES:] = (w - w_hi.astype(F32)).astype(BF16)

    h = _rms(x_ref[...], g_ref[...])
    h_hi = h.astype(BF16)
    h_lo = (h - h_hi.astype(F32)).astype(BF16)
    both = jnp.dot(h_hi, wcat_ref[...], preferred_element_type=F32)
    cross = jnp.dot(h_lo, wcat_ref[:, :LANES], preferred_element_type=F32)
    logits = both[:, :LANES] + (both[:, LANES:] + cross) + br_ref[...]
    nblk = tt // LANES
    l = jnp.concatenate([logits[c * LANES:(c + 1) * LANES].T for c in range(nblk)], axis=1)[:E]
    row = lax.broadcasted_iota(I32, (E, tt), 0).astype(F32)
    vals, sels, idxs = [], [], []
    for k in range(TOP_K):
        m = jnp.max(l, axis=0, keepdims=True)
        idx = jnp.min(jnp.where(l == m, row, float(E)), axis=0, keepdims=True)
        sel = row == idx
        vals.append(m)
        sels.append(sel)
        idxs.append(idx)
        l = jnp.where(sel, -jnp.inf, l)
    ex = [jnp.exp(v - vals[0]) for v in vals]
    den = ex[0] + ex[1] + ex[2] + ex[3]
    multi = jnp.zeros((E, tt), F32)
    for sel in sels:
        multi = multi + jnp.where(sel, 1.0, 0.0)
    base = carry_ref[:, :1]
    parts = []
    for c in range(nblk):
        mc = multi[:, c * LANES:(c + 1) * LANES]
        parts.append(jnp.dot(mc.astype(BF16), tri_ref[...], preferred_element_type=F32) + base)
        base = base + jnp.sum(mc, axis=1, keepdims=True)
    before = jnp.concatenate(parts, axis=1)
    kk = lax.broadcasted_iota(I32, (TOP_K, tt), 0)
    e_out = jnp.zeros((TOP_K, tt), F32)
    g_out = jnp.zeros((TOP_K, tt), F32)
    r_out = jnp.zeros((TOP_K, tt), F32)
    for k in range(TOP_K):
        e_out = jnp.where(kk == k, idxs[k], e_out)
        g_out = jnp.where(kk == k, ex[k] / den, g_out)
        rk = jnp.sum(jnp.where(sels[k], before, 0.0), axis=0, keepdims=True)
        r_out = jnp.where(kk == k, rk, r_out)
    e_ref[...] = e_out.astype(I32)
    gate_ref[...] = g_out
    rank_ref[...] = r_out.astype(I32)
    carry_ref[...] = jnp.broadcast_to(base, carry_ref.shape)
    cnt_ref[...] = carry_ref[...]


def _router(x, g, w_r, b_r):
    N, D = x.shape
    tt = ROUTER_TILE
    E = N_EXPERTS
    w_pad = jnp.pad(w_r, ((0, 0), (0, LANES - E)))
    b_pad = jnp.pad(b_r, (0, LANES - E)).reshape(1, LANES)
    kspec = pl.BlockSpec((TOP_K, tt), lambda i: (0, i))
    return pl.pallas_call(
        functools.partial(_router_kernel, tt=tt),
        grid=(N // tt,),
        in_specs=[
            pl.BlockSpec((tt, D), lambda i: (i, 0)),
            pl.BlockSpec((1, D), lambda i: (0, 0)),
            pl.BlockSpec((D, LANES), lambda i: (0, 0)),
            pl.BlockSpec((1, LANES), lambda i: (0, 0)),
        ],
        out_specs=[kspec, kspec, kspec, pl.BlockSpec((E, LANES), lambda i: (0, 0))],
        out_shape=[
            jax.ShapeDtypeStruct((TOP_K, N), I32),
            jax.ShapeDtypeStruct((TOP_K, N), F32),
            jax.ShapeDtypeStruct((TOP_K, N), I32),
            jax.ShapeDtypeStruct((E, LANES), F32),
        ],
        scratch_shapes=[pltpu.VMEM((LANES, LANES), BF16), pltpu.VMEM((E, LANES), F32),
                        pltpu.VMEM((D, 2 * LANES), BF16)],
        compiler_params=_params(("arbitrary",)),
        name="moe_router",
    )(x, g.reshape(1, D), w_pad, b_pad)


def _to_tile_rows(ref, val):
    n, D = val.shape
    nsub = D // LANES
    for s in range(nsub):
        ref[pl.ds(s, n, stride=nsub), :] = val[:, s * LANES:(s + 1) * LANES]


def _from_tile_rows(ref, n, nsub):
    return jnp.concatenate([ref[pl.ds(s, n, stride=nsub), :] for s in range(nsub)], axis=1)


def _tile_row(ref, r, nsub, count=1):
    return ref.at[pl.ds(pl.multiple_of(r * nsub, nsub), count * nsub)]


def _row_copy_wait(src, dst, sem, times):
    for _ in range(times):
        pltpu.make_async_copy(src, dst.at[pl.ds(0, src.shape[0])], sem).wait()


def _dispatch_kernel(pend_ref, nu_ref, dest_ref, x_ref, g_ref, xpad_ref, hbuf, zbuf, sems, zsem,
                     *, tt, tm, nt, nsub):
    i = pl.program_id(0)
    slot = i % 2

    @pl.when(i == 0)
    def _():
        zbuf[...] = jnp.zeros_like(zbuf)

        def seg_copy(e):
            start = pl.multiple_of(jnp.maximum(pend_ref[e] - tm, 0), tm)
            return pltpu.make_async_copy(zbuf, _tile_row(xpad_ref, start, nsub, tm), zsem)

        def tail_copy(j):
            return pltpu.make_async_copy(zbuf, _tile_row(xpad_ref, (nt - 1 - j) * tm, nsub, tm), zsem)

        for e in range(N_EXPERTS):
            seg_copy(e).start()
        for j in range(N_EXPERTS):
            pl.when(nt - 1 - j >= nu_ref[0])(tail_copy(j).start)
        for e in range(N_EXPERTS):
            seg_copy(e).wait()
        for j in range(N_EXPERTS):
            pl.when(nt - 1 - j >= nu_ref[0])(tail_copy(j).wait)

    hb = hbuf.at[slot]
    _to_tile_rows(hb, _rms(x_ref[...], g_ref[...]))

    def issue(t, carry):
        for k in range(TOP_K):
            d = dest_ref[t * TOP_K + k]
            pltpu.make_async_copy(_tile_row(hb, t, nsub), _tile_row(xpad_ref, d, nsub),
                                  sems.at[slot]).start(priority=k % 2)
        return carry

    lax.fori_loop(0, tt, issue, 0, unroll=64)

    @pl.when(i > 0)
    def _():
        _row_copy_wait(hbuf.at[1 - slot], xpad_ref, sems.at[1 - slot], TOP_K)

    @pl.when(i == pl.num_programs(0) - 1)
    def _():
        _row_copy_wait(hb, xpad_ref, sems.at[slot], TOP_K)


def _dispatch(x, g, dest, pend, n_used, P):
    N, D = x.shape
    tt = DISPATCH_TILE
    tm = MOE_TILE
    nsub = D // LANES
    grid_spec = pltpu.PrefetchScalarGridSpec(
        num_scalar_prefetch=2,
        grid=(N // tt,),
        in_specs=[
            pl.BlockSpec((tt * TOP_K,), lambda i, pe, nu: (i,), memory_space=pltpu.SMEM),
            pl.BlockSpec((tt, D), lambda i, pe, nu: (i, 0)),
            pl.BlockSpec((1, D), lambda i, pe, nu: (0, 0)),
        ],
        out_specs=pl.BlockSpec(memory_space=pl.ANY),
        scratch_shapes=[pltpu.VMEM((2, tt * nsub, LANES), F32), pltpu.VMEM((tm * nsub, LANES), F32),
                        pltpu.SemaphoreType.DMA((2,)), pltpu.SemaphoreType.DMA(())],
    )
    return pl.pallas_call(
        functools.partial(_dispatch_kernel, tt=tt, tm=tm, nt=P // tm, nsub=nsub),
        grid_spec=grid_spec,
        out_shape=jax.ShapeDtypeStruct((P * nsub, LANES), F32),
        compiler_params=_params(("arbitrary",)),
        name="moe_dispatch",
    )(pend, n_used, dest, x, g.reshape(1, D))


def _expert_kernel(te_ref, nx_ref, nu_ref, x_ref, wgu_hbm, bgu_ref, wdn_hbm, bdn_ref, o_ref,
                   wgu_f32, wdn_f32, wgu_bf, wdn_bf, sems, *, layer, F, tm, nsub):
    i = pl.program_id(0)

    def weight_copies(e):
        return (pltpu.make_async_copy(wgu_hbm.at[layer, e], wgu_f32, sems.at[0]),
                pltpu.make_async_copy(wdn_hbm.at[layer, e], wdn_f32, sems.at[1]))

    @pl.when(i < nu_ref[0])
    def _():
        e = te_ref[i]

        @pl.when(i == 0)
        def _():
            for c in weight_copies(e):
                c.start()

        @pl.when((i == 0) | (e != te_ref[jnp.maximum(i - 1, 0)]))
        def _():
            for c in weight_copies(e):
                c.wait()
            wgu_bf[...] = wgu_f32[...].astype(BF16)
            wdn_bf[...] = wdn_f32[...].astype(BF16)
            nxt = nx_ref[i]

            @pl.when(nxt != e)
            def _():
                for c in weight_copies(nxt):
                    c.start()

        x = _from_tile_rows(x_ref, tm, nsub).astype(BF16)
        gu = jnp.dot(x, wgu_bf[...], preferred_element_type=F32) + bgu_ref[0]
        gate = jnp.minimum(gu[:, :F], SWIGLU_LIMIT)
        up = jnp.clip(gu[:, F:], -SWIGLU_LIMIT, SWIGLU_LIMIT)
        glu = gate * jax.nn.sigmoid(SWIGLU_ALPHA * gate)
        a = ((up + 1.0) * glu).astype(BF16)
        _to_tile_rows(o_ref, jnp.dot(a, wdn_bf[...], preferred_element_type=F32) + bdn_ref[0])

    @pl.when(i >= nu_ref[0])
    def _():
        o_ref[...] = jnp.zeros_like(o_ref)


def _experts(tile_e, next_e, n_used, x_pad, w_gu, b_gu, w_dn, b_dn, layer):
    _, E, D, F2 = w_gu.shape
    F = F2 // 2
    tm = MOE_TILE
    nsub = D // LANES
    P = x_pad.shape[0] // nsub
    row = lambda i, te, nx, nu: (jnp.minimum(i, nu[0] - 1), 0)
    exp3 = lambda i, te, nx, nu: (te[i], 0, 0)
    grid_spec = pltpu.PrefetchScalarGridSpec(
        num_scalar_prefetch=3,
        grid=(P // tm,),
        in_specs=[
            pl.BlockSpec((tm * nsub, LANES), row),
            pl.BlockSpec(memory_space=pl.ANY),
            pl.BlockSpec((1, 1, F2), exp3),
            pl.BlockSpec(memory_space=pl.ANY),
            pl.BlockSpec((1, 1, D), exp3),
        ],
        out_specs=pl.BlockSpec((tm * nsub, LANES), lambda i, te, nx, nu: (i, 0)),
        scratch_shapes=[pltpu.VMEM((D, F2), F32), pltpu.VMEM((F, D), F32),
                        pltpu.VMEM((D, F2), BF16), pltpu.VMEM((F, D), BF16),
                        pltpu.SemaphoreType.DMA((2,))],
    )
    return pl.pallas_call(
        functools.partial(_expert_kernel, layer=layer, F=F, tm=tm, nsub=nsub),
        grid_spec=grid_spec,
        out_shape=jax.ShapeDtypeStruct((P * nsub, LANES), F32),
        compiler_params=_params(("arbitrary",)),
        name="moe_experts",
    )(tile_e, next_e, n_used, x_pad, w_gu, b_gu.reshape(E, 1, F2), w_dn, b_dn.reshape(E, 1, D))


def _combine_kernel(dest_ref, dnext_ref, x_ref, gate_ref, ypad_ref, o_ref, ybuf, sems, *, tt, nsub):
    i = pl.program_id(0)
    slot = i % 2

    def gather(idx_ref, s):
        def issue(t, carry):
            for k in range(TOP_K):
                d = idx_ref[t * TOP_K + k]
                pltpu.make_async_copy(_tile_row(ypad_ref, d, nsub), _tile_row(ybuf.at[s, k], t, nsub),
                                      sems.at[s]).start(priority=k % 2)
            return carry

        lax.fori_loop(0, tt, issue, 0, unroll=64)

    @pl.when(i == 0)
    def _():
        gather(dest_ref, slot)

    @pl.when(i + 1 < pl.num_programs(0))
    def _():
        gather(dnext_ref, 1 - slot)

    for k in range(TOP_K):
        _row_copy_wait(ybuf.at[slot, k], ypad_ref, sems.at[slot], 1)
    gate = gate_ref[...]
    x = x_ref[...]
    cols = []
    for c in range(nsub):
        acc = x[:, c * LANES:(c + 1) * LANES]
        for k in range(TOP_K):
            acc = acc + ybuf[slot, k, pl.ds(c, tt, stride=nsub), :] * gate[:, k:k + 1]
        cols.append(acc)
    o_ref[...] = jnp.concatenate(cols, axis=1)


def _combine(x, gate, dest, y_pad):
    N, D = x.shape
    tt = COMBINE_TILE
    nsub = D // LANES
    last = N // tt - 1
    return pl.pallas_call(
        functools.partial(_combine_kernel, tt=tt, nsub=nsub),
        grid=(N // tt,),
        in_specs=[
            pl.BlockSpec((tt * TOP_K,), lambda i: (i,), memory_space=pltpu.SMEM),
            pl.BlockSpec((tt * TOP_K,), lambda i: (jnp.minimum(i + 1, last),), memory_space=pltpu.SMEM),
            pl.BlockSpec((tt, D), lambda i: (i, 0)),
            pl.BlockSpec((tt, TOP_K), lambda i: (i, 0)),
            pl.BlockSpec(memory_space=pl.ANY),
        ],
        out_specs=pl.BlockSpec((tt, D), lambda i: (i, 0)),
        out_shape=jax.ShapeDtypeStruct((N, D), F32),
        scratch_shapes=[pltpu.VMEM((2, TOP_K, tt * nsub, LANES), F32), pltpu.SemaphoreType.DMA((2,))],
        compiler_params=_params(("arbitrary",)),
        name="moe_combine",
    )(dest, dest, x, gate, y_pad)


def _dest_kernel(pstart_ref, e_ref, rank_ref, o_ref):
    e = e_ref[...]
    seg = jnp.zeros(e.shape, I32)
    for ex in range(N_EXPERTS):
        seg = jnp.where(e == ex, pstart_ref[ex], seg)
    o_ref[...] = seg + rank_ref[...]


def _dest_rows(e_idx, rank, pstart):
    return pl.pallas_call(
        _dest_kernel,
        in_specs=[pl.BlockSpec(memory_space=pltpu.SMEM), pl.BlockSpec(e_idx.shape, lambda: (0, 0)),
                  pl.BlockSpec(rank.shape, lambda: (0, 0))],
        out_specs=pl.BlockSpec(rank.shape, lambda: (0, 0)),
        out_shape=jax.ShapeDtypeStruct(rank.shape, I32),
        name="moe_dest",
    )(pstart, e_idx, rank)


def _moe_layer(x, g, w_r, b_r, w_gu, b_gu, w_dn, b_dn, layer):
    N, D = x.shape
    E = N_EXPERTS
    tm = MOE_TILE
    e_idx, gate_t, rank, counts = _router(x, g, w_r, b_r)
    counts = counts[:, 0].astype(I32)
    padded = ((counts + tm - 1) // tm) * tm
    pend = jnp.cumsum(padded)
    pstart = pend - padded
    dest = _dest_rows(e_idx, rank, pstart.astype(I32)).T.reshape(-1)
    gate = gate_t.T
    P = ((N * TOP_K + E * (tm - 1) + tm - 1) // tm) * tm
    nt = P // tm
    tiles = jnp.arange(nt, dtype=I32)
    tile_e = jnp.minimum(jnp.sum(tiles[:, None] * tm >= pend[None, :], axis=1), E - 1).astype(I32)
    n_used = (pend[-1] // tm).astype(I32)
    tile_e = jnp.where(tiles < n_used, tile_e, tile_e[n_used - 1])
    experts = jnp.arange(E, dtype=I32)
    later = lax.cummin(jnp.where(counts > 0, experts, E), reverse=True)
    after = jnp.concatenate([later[1:], jnp.full((1,), E, I32)])
    next_e = jnp.where(after < E, after, experts)[tile_e]
    x_pad = _dispatch(x, g, dest, pend.astype(I32), n_used.reshape(1), P)
    y_pad = _experts(tile_e, next_e.astype(I32), n_used.reshape(1), x_pad, w_gu, b_gu, w_dn, b_dn, layer)
    return _combine(x, gate, dest, y_pad)


def _bucket_maps():
    qi = np.arange(ATT_BLOCK)[None, :]
    kj = np.arange(2 * ATT_BLOCK)[:, None]
    delta = qi + ATT_BLOCK - kj
    buckets, valids = [], []
    max_exact = N_BUCKETS // 2
    for win, dil in DILATED_GROUPS:
        n = np.maximum(delta * dil, 0)
        nf = np.maximum(n, 1).astype(np.float32)
        large = max_exact + (np.log(nf / np.float32(max_exact)) / np.float32(math.log(REL_MAX_DIST / max_exact))
                             * np.float32(N_BUCKETS - max_exact)).astype(np.int32)
        large = np.minimum(large, N_BUCKETS - 1)
        buckets.append(np.where(n < max_exact, n, large).astype(np.int32))
        valids.append(((delta >= 0) & (delta <= win // dil)).astype(np.int32))
    return np.stack(buckets), np.stack(valids)


def _bias_kernel(tab_ref, bm_ref, valid_ref, o_ref):
    g = pl.program_id(0)
    bm = bm_ref[0]
    key = lax.broadcasted_iota(I32, bm.shape, 0)

    def one_head(h, carry):
        acc = jnp.zeros(bm.shape, F32)
        for b in range(N_BUCKETS):
            acc = jnp.where(bm == b, tab_ref[b, g * ATT_HEADS + h], acc)
        band = jnp.where(valid_ref[0] > 0, acc, NEG_INF)
        o_ref[h, 0] = band
        o_ref[h, 1] = jnp.where(key >= ATT_BLOCK, band, NEG_INF)
        return carry

    lax.fori_loop(0, ATT_HEADS, one_head, 0)


def _bias_tables(rel_bias):
    bm, valid = _bucket_maps()
    GH = len(DILATED_GROUPS) * ATT_HEADS
    blk = (1, 2 * ATT_BLOCK, ATT_BLOCK)
    return pl.pallas_call(
        _bias_kernel,
        grid=(len(DILATED_GROUPS),),
        in_specs=[
            pl.BlockSpec(memory_space=pltpu.SMEM),
            pl.BlockSpec(blk, lambda i: (i, 0, 0)),
            pl.BlockSpec(blk, lambda i: (i, 0, 0)),
        ],
        out_specs=pl.BlockSpec((ATT_HEADS, 2, 2 * ATT_BLOCK, ATT_BLOCK), lambda i: (i, 0, 0, 0)),
        out_shape=jax.ShapeDtypeStruct((GH, 2, 2 * ATT_BLOCK, ATT_BLOCK), F32),
        compiler_params=_params(("arbitrary",)),
        name="attn_bias",
    )(rel_bias, jnp.asarray(bm), jnp.asarray(valid))


def _cast_kernel(w_ref, o_ref):
    o_ref[...] = w_ref[...].astype(BF16)


def _to_bf16(w, layer, col_block):
    _, rows, cols = w.shape
    return pl.pallas_call(
        _cast_kernel,
        grid=(cols // col_block,),
        in_specs=[pl.BlockSpec((None, rows, col_block), lambda j: (layer, 0, j))],
        out_specs=pl.BlockSpec((rows, col_block), lambda j: (0, j)),
        out_shape=jax.ShapeDtypeStruct((rows, cols), BF16),
        compiler_params=_params(("parallel",)),
        name="cast_bf16",
    )(w)


def _qkv_kernel(x_ref, g_ref, wq_ref, wk_ref, wv_ref, qg_ref, kg_ref, qt_ref, k_ref, vt_ref,
                slab_ref, bd_ref, *, dil, R, nl):
    rc = pl.program_id(2)
    D = x_ref.shape[2]
    lanes = slab_ref.shape[2]
    first = (pl.program_id(0) == 0) & (pl.program_id(1) == 0) & (rc == 0)

    @pl.when(first)
    def _():
        r = lax.broadcasted_iota(I32, bd_ref.shape, 0)
        c = lax.broadcasted_iota(I32, bd_ref.shape, 1)
        hd_shift = HEAD_DIM.bit_length() - 1
        same_head = lax.shift_right_logical(r, hd_shift) == lax.shift_right_logical(c, hd_shift)
        bd_ref[...] = jnp.where(same_head, 1.0, 0.0).astype(BF16)

    if dil == 1:
        x = x_ref[0]
    else:
        @pl.when(rc == 0)
        def _():
            for c in range(D // lanes):
                slab_ref[c] = x_ref[0, :, c * lanes:(c + 1) * lanes]

        pieces = []
        for j in range(R):
            r = rc * R + j
            cols = [slab_ref[c, pl.ds(r, nl, stride=dil), :] for c in range(D // lanes)]
            pieces.append(jnp.concatenate(cols, axis=1))
        x = jnp.concatenate(pieces, axis=0)
    h = _rms(x, g_ref[...]).astype(BF16)
    nb = bd_ref.shape[0]

    def head_norm(y, gain):
        parts = []
        for c0 in range(0, y.shape[1], nb):
            yc = y[:, c0:c0 + nb]
            ssq = jnp.dot((yc * yc).astype(BF16), bd_ref[...], preferred_element_type=F32)
            parts.append(yc * lax.rsqrt(ssq * (1.0 / HEAD_DIM) + EPS))
        return jnp.concatenate(parts, axis=1) * gain

    q = head_norm(jnp.dot(h, wq_ref[...], preferred_element_type=F32), qg_ref[...])
    q = q * (HEAD_DIM ** -0.5)
    k = head_norm(jnp.dot(h, wk_ref[...], preferred_element_type=F32), kg_ref[...])
    v = jnp.dot(h, wv_ref[...], preferred_element_type=F32)
    blk = ATT_BLOCK
    for j in range(R):
        k_ref[0, j] = k[j * nl:(j + 1) * nl].astype(BF16)
        for c in range(nl // blk):
            rows = slice(j * nl + c * blk, j * nl + (c + 1) * blk)
            for hp in range(q.shape[1] // blk):
                cs = slice(hp * blk, (hp + 1) * blk)
                qt_ref[0, j, cs, c * blk:(c + 1) * blk] = q[rows, cs].T.astype(BF16)
                vt_ref[0, j, cs, c * blk:(c + 1) * blk] = v[rows, cs].T.astype(BF16)


def _qkv_group(x, g, w_bf, q_gain, k_gain, gi, dil):
    B, S, D = x.shape
    HD = ATT_HEADS * HEAD_DIM
    G = len(DILATED_GROUPS)
    L = S // dil
    R = min(dil, QKV_ROWS // ATT_BLOCK)
    nl = QKV_ROWS // R
    lanes = 128
    wspec = lambda s: pl.BlockSpec((D, HD), lambda b, l, r: (0, s * G + gi))
    tspec = pl.BlockSpec((1, R, HD, nl), lambda b, l, r: (b, r, 0, l))
    tshape = jax.ShapeDtypeStruct((B, dil, HD, L), BF16)
    return pl.pallas_call(
        functools.partial(_qkv_kernel, dil=dil, R=R, nl=nl),
        grid=(B, L // nl, dil // R),
        in_specs=[
            pl.BlockSpec((1, nl * dil, D), lambda b, l, r: (b, l, 0)),
            pl.BlockSpec((1, D), lambda b, l, r: (0, 0)),
            wspec(0), wspec(1), wspec(2),
            pl.BlockSpec((1, HD), lambda b, l, r: (0, 0)),
            pl.BlockSpec((1, HD), lambda b, l, r: (0, 0)),
        ],
        out_specs=[tspec, pl.BlockSpec((1, R, nl, HD), lambda b, l, r: (b, r, l, 0)), tspec],
        out_shape=[tshape, jax.ShapeDtypeStruct((B, dil, L, HD), BF16), tshape],
        scratch_shapes=[pltpu.VMEM((D // lanes, nl * dil if dil > 1 else 8, lanes), F32),
                        pltpu.VMEM((256, 256), BF16)],
        compiler_params=_params(("arbitrary", "arbitrary", "arbitrary")),
        name=f"attn_qkv_g{gi}",
    )(x, g.reshape(1, D), w_bf, w_bf, w_bf,
      jnp.tile(q_gain, ATT_HEADS).reshape(1, HD), jnp.tile(k_gain, ATT_HEADS).reshape(1, HD))


def _attn_kernel(*refs, single, nitems):
    if single:
        qt_ref, kc_ref, vtc_ref, bias_ref, o_ref, l_ref, s_ref, p_ref = refs
    else:
        qt_ref, kc_ref, kp_ref, vtc_ref, vtp_ref, bias_ref, o_ref, l_ref, s_ref, p_ref = refs
    blk = ATT_BLOCK
    pair = 2 * HEAD_DIM
    first = jnp.where(pl.program_id(2) == 0, 1, 0)
    if single:
        k0 = blk
        variants = [1] * nitems
        q_of = lambda it, cs: qt_ref[0, it, cs, :]
        k_of = lambda it, cs: kc_ref[0, it, :, cs]
        vt_of = lambda it, cs: vtc_ref[0, it, cs, :]
    else:
        k0 = 0
        variants = [first] + [0] * (nitems - 1)

        def q_of(it, cs):
            return qt_ref[0, 0, cs, it * blk:(it + 1) * blk]

        def k_of(it, cs):
            if it == 0:
                return jnp.concatenate([kp_ref[0, 0, :, cs], kc_ref[0, 0, :blk, cs]], axis=0)
            return kc_ref[0, 0, (it - 1) * blk:(it + 1) * blk, cs]

        def vt_of(it, cs):
            if it == 0:
                return jnp.concatenate([vtp_ref[0, 0, cs, :], vtc_ref[0, 0, cs, :blk]], axis=1)
            return vtc_ref[0, 0, cs, (it - 1) * blk:(it + 1) * blk]

    feat = lax.broadcasted_iota(I32, (pair, blk), 0)
    head_rows = [jnp.where(feat < HEAD_DIM, 1.0, 0.0).astype(BF16),
                 jnp.where(feat < HEAD_DIM, 0.0, 1.0).astype(BF16)]
    invs = {}
    lse_rows = {}

    def scores(it, pr):
        cs = slice(pr * pair, (pr + 1) * pair)
        qt2 = q_of(it, cs)
        k2 = k_of(it, cs)
        for half in range(2):
            qth = qt2 * head_rows[half]
            s_ref[it * ATT_HEADS + 2 * pr + half] = jnp.dot(k2, qth, preferred_element_type=F32)

    def softmax(it, pr):
        for head in (2 * pr, 2 * pr + 1):
            s = s_ref[it * ATT_HEADS + head] + bias_ref[head, variants[it], k0:, :]
            m = jnp.max(s, axis=0, keepdims=True)
            p = jnp.exp(s - m)
            den = jnp.sum(p, axis=0, keepdims=True)
            p_ref[it * ATT_HEADS + head] = p.astype(BF16)
            invs[it, head] = 1.0 / den
            lse_rows[it, head] = m + jnp.log(den)

    def outputs(it, pr):
        cs = slice(pr * pair, (pr + 1) * pair)
        vt2 = vt_of(it, cs)
        halves = []
        for half in range(2):
            head = 2 * pr + half
            vth = vt2[half * HEAD_DIM:(half + 1) * HEAD_DIM]
            ot = jnp.dot(vth, p_ref[it * ATT_HEADS + head], preferred_element_type=F32)
            halves.append(ot * invs[it, head])
        val = jnp.concatenate(halves, axis=0).T
        if single:
            o_ref[0, it, :, cs] = val
        else:
            o_ref[0, 0, it * blk:(it + 1) * blk, cs] = val

    work = [(it, pr) for it in range(nitems) for pr in range(ATT_HEADS // 2)]
    for it, pr in work:
        scores(it, pr)
    for it, pr in work:
        softmax(it, pr)
    for it, pr in work:
        outputs(it, pr)
    lse_row = lax.broadcasted_iota(I32, (blk, blk), 0)
    for it in range(nitems):
        lse_t = jnp.zeros((blk, blk), F32)
        for head in range(ATT_HEADS):
            lse_t = jnp.where(lse_row == head, lse_rows[it, head], lse_t)
        if single:
            l_ref[0, it] = lse_t.T
        else:
            l_ref[0, 0, it * blk:(it + 1) * blk, :] = lse_t.T


def _attn_group(qt, k, vt, bias, gi):
    B, dil, L, HD = k.shape
    blk = ATT_BLOCK
    nb = L // blk
    single = nb == 1
    ni = min(ATT_ITEMS, dil if single else nb)
    bias_spec = pl.BlockSpec((ATT_HEADS, 2, 2 * blk, blk), lambda b, r, n: (gi, 0, 0, 0))
    if single:
        rows = pl.BlockSpec((1, ni, blk, HD), lambda b, r, n: (b, r, 0, 0))
        cols = pl.BlockSpec((1, ni, HD, blk), lambda b, r, n: (b, r, 0, 0))
        grid = (B, dil // ni, 1)
        in_specs = [cols, rows, cols, bias_spec]
        args = (qt, k, vt, bias)
        out_specs = [rows, pl.BlockSpec((1, ni, blk, blk), lambda b, r, n: (b, r, 0, 0))]
        nk = blk
    else:
        rows = pl.BlockSpec((1, 1, ni * blk, HD), lambda b, r, n: (b, r, n, 0))
        cols = pl.BlockSpec((1, 1, HD, ni * blk), lambda b, r, n: (b, r, 0, n))
        prow = pl.BlockSpec((1, 1, blk, HD), lambda b, r, n: (b, r, jnp.maximum(ni * n - 1, 0), 0))
        pcol = pl.BlockSpec((1, 1, HD, blk), lambda b, r, n: (b, r, 0, jnp.maximum(ni * n - 1, 0)))
        grid = (B, dil, nb // ni)
        in_specs = [cols, rows, prow, cols, pcol, bias_spec]
        args = (qt, k, k, vt, vt, bias)
        out_specs = [rows, pl.BlockSpec((1, 1, ni * blk, blk), lambda b, r, n: (b, r, n, 0))]
        nk = 2 * blk
    return pl.pallas_call(
        functools.partial(_attn_kernel, single=single, nitems=ni),
        grid=grid,
        in_specs=in_specs,
        out_specs=out_specs,
        out_shape=[jax.ShapeDtypeStruct((B, dil, L, HD), F32),
                   jax.ShapeDtypeStruct((B, dil, L, blk), F32)],
        scratch_shapes=[pltpu.VMEM((ni * ATT_HEADS, nk, blk), F32),
                        pltpu.VMEM((ni * ATT_HEADS, nk, blk), BF16)],
        compiler_params=_params(("parallel", "parallel", "arbitrary")),
        name=f"attn_core_g{gi}",
    )(*args)


def _attn_out_kernel(x_ref, o0, o1, o2, l0, l1, l2, wo_ref, out_ref, oslab, lslab, ex_ref, *, T):
    lanes = oslab.shape[3]
    HD = o0.shape[3]

    @pl.when((pl.program_id(0) == 0) & (pl.program_id(1) == 0))
    def _():
        h = lax.broadcasted_iota(I32, ex_ref.shape, 0) & (ATT_BLOCK - 1)
        c = lax.broadcasted_iota(I32, ex_ref.shape, 1)
        owner = lax.shift_right_logical(c, HEAD_DIM.bit_length() - 1)
        ex_ref[...] = jnp.where(h == owner, 1.0, 0.0).astype(BF16)

    def token_order(o_ref, l_ref, dil, s):
        if dil == 1:
            return o_ref[0, 0], l_ref[0, 0]
        n = T // dil
        for r in range(dil):
            blk = o_ref[0, r]
            for c in range(HD // lanes):
                oslab[s, c, pl.ds(r, n, stride=dil), :] = blk[:, c * lanes:(c + 1) * lanes]
            lslab[s, pl.ds(r, n, stride=dil), :] = l_ref[0, r]
        o = jnp.concatenate([oslab[s, c] for c in range(HD // lanes)], axis=1)
        return o, lslab[s]

    groups = [token_order(o_ref, l_ref, dil, s)
              for s, ((_, dil), o_ref, l_ref) in enumerate(zip(DILATED_GROUPS, (o0, o1, o2), (l0, l1, l2)))]
    m = jnp.maximum(jnp.maximum(groups[0][1], groups[1][1]), groups[2][1])
    es = [jnp.exp(l - m) for _, l in groups]
    inv = 1.0 / (es[0] + es[1] + es[2])
    att = jnp.zeros((T, HD), F32)
    for (o, _), e in zip(groups, es):
        a = e * inv
        a_hi = a.astype(BF16)
        a_lo = (a - a_hi.astype(F32)).astype(BF16)
        wide = jnp.dot(jnp.concatenate([a_hi, a_lo], axis=1), ex_ref[...], preferred_element_type=F32)
        att = att + wide * o
    out_ref[0] = x_ref[0] + jnp.dot(att.astype(BF16), wo_ref[...], preferred_element_type=F32)


def _attn_out(x, os, ls, wo_bf):
    B, S, D = x.shape
    HD = wo_bf.shape[0]
    T = OUT_TILE
    lanes = 128
    ospec = lambda dil, w: pl.BlockSpec((1, dil, T // dil, w), lambda b, i: (b, 0, i, 0))
    dils = [dil for _, dil in DILATED_GROUPS]
    xspec = pl.BlockSpec((1, T, D), lambda b, i: (b, i, 0))
    return pl.pallas_call(
        functools.partial(_attn_out_kernel, T=T),
        grid=(B, S // T),
        in_specs=[xspec] + [ospec(d, HD) for d in dils] + [ospec(d, ATT_BLOCK) for d in dils]
                 + [pl.BlockSpec((HD, D), lambda b, i: (0, 0))],
        out_specs=xspec,
        out_shape=jax.ShapeDtypeStruct((B, S, D), F32),
        scratch_shapes=[pltpu.VMEM((len(dils), HD // lanes, T, lanes), F32),
                        pltpu.VMEM((len(dils), T, ATT_BLOCK), F32),
                        pltpu.VMEM((2 * ATT_BLOCK, HD), BF16)],
        compiler_params=_params(("arbitrary", "arbitrary")),
        name="attn_out",
    )(x, *os, *ls, wo_bf)


def _attention_layer(x, g, w_qkv, q_gain, k_gain, w_o, rel_bias, layer):
    HD = ATT_HEADS * HEAD_DIM
    bias = _bias_tables(rel_bias)
    wqkv_bf = _to_bf16(w_qkv, layer, HD)
    wo_bf = _to_bf16(w_o, layer, w_o.shape[2])
    os, ls = [], []
    for gi, (_, dil) in enumerate(DILATED_GROUPS):
        qt, k, vt = _qkv_group(x, g, wqkv_bf, q_gain, k_gain, gi, dil)
        o, l = _attn_group(qt, k, vt, bias, gi)
        os.append(o)
        ls.append(l)
    return _attn_out(x, os, ls, wo_bf)


def kernel(x, norm_mix_g, norm_ffn_g, pool_w, pool_scale, attn_w_qkv, attn_q_gain, attn_k_gain,
           attn_w_o, rel_bias, moe_w_router, moe_b_router, moe_w_gate_up, moe_b_gate_up,
           moe_w_down, moe_b_down):
    B, S, D = x.shape
    depth = norm_mix_g.shape[0]
    for i in range(depth):
        j = i // 2
        if i % 2 == 0:
            x = _pool_layer(x, norm_mix_g[i], pool_w[j], pool_scale[j])
        else:
            x = _attention_layer(x, norm_mix_g[i], attn_w_qkv, attn_q_gain[j], attn_k_gain[j],
                                 attn_w_o, rel_bias, j)
        x = _moe_layer(x.reshape(B * S, D), norm_ffn_g[i], moe_w_router[i], moe_b_router[i],
                       moe_w_gate_up, moe_b_gate_up[i], moe_w_down, moe_b_down[i],
                       i).reshape(B, S, D)
    return x
```

```python
import functools
import math

import numpy as np
import jax
import jax.numpy as jnp
from jax import lax
from jax.experimental import pallas as pl
from jax.experimental.pallas import tpu as pltpu

F32 = jnp.float32
BF16 = jnp.bfloat16
I32 = jnp.int32

EPS = 1e-6
POOL_WINDOWS = (2, 4, 8, 16)
POOL_HALO = 16
DILATED_GROUPS = ((128, 1), (512, 4), (2048, 16))
ATT_HEADS = 16
HEAD_DIM = 64
ATT_BLOCK = 128
ATT_ITEMS = 8
N_BUCKETS = 32
REL_MAX_DIST = 2048
NEG_INF = -1e30
N_EXPERTS = 32
TOP_K = 4
SWIGLU_LIMIT = 7.0
SWIGLU_ALPHA = 1.702

V7X_VMEM_LIMIT_BYTES = 56 * 1024 * 1024
MOE_TILE = 512
DISPATCH_TILE = 512
COMBINE_TILE = 256
ROUTER_TILE = 1024
POOL_TILE = 1024
QKV_ROWS = 1024
LANES = 128
OUT_TILE = 512


def _rms(xf, g):
    ms = jnp.mean(xf * xf, axis=-1, keepdims=True)
    return xf * lax.rsqrt(ms + EPS) * g


def _params(sem, vmem=None):
    return pltpu.CompilerParams(dimension_semantics=sem,
                                vmem_limit_bytes=vmem or V7X_VMEM_LIMIT_BYTES)


def _pool_kernel(x_ref, halo_ref, g_ref, w_ref, sc_ref, o_ref, *, ts, dg):
    i = pl.program_id(1)
    x = x_ref[0]
    g = g_ref[...]
    h = _rms(x, g)
    hh = _rms(halo_ref[0], g)
    hh = jnp.where(i == 0, 0.0, hh)
    full = jnp.concatenate([hh, h], axis=0)
    pos = i * ts + lax.broadcasted_iota(I32, (ts, 1), 0)
    outs = []
    for gi, w in enumerate(POOL_WINDOWS):
        s = full[:, gi * dg:(gi + 1) * dg]
        sh = 1
        while sh < w:
            s = s + pltpu.roll(s, sh, 0)
            sh *= 2
        s = s[POOL_HALO:]
        cnt = jnp.minimum(pos + 1, w).astype(F32)
        p = s / cnt - h[:, gi * dg:(gi + 1) * dg]
        outs.append(jnp.dot(p.astype(BF16), w_ref[gi].astype(BF16),
                            preferred_element_type=F32))
    y = jnp.concatenate(outs, axis=1) * sc_ref[...]
    o_ref[0] = x + y


def _pool_layer(x, g, w_groups, scale):
    B, S, D = x.shape
    ts = POOL_TILE
    dg = D // len(POOL_WINDOWS)
    hb = ts // POOL_HALO
    return pl.pallas_call(
        functools.partial(_pool_kernel, ts=ts, dg=dg),
        grid=(B, S // ts),
        in_specs=[
            pl.BlockSpec((1, ts, D), lambda b, i: (b, i, 0)),
            pl.BlockSpec((1, POOL_HALO, D), lambda b, i: (b, jnp.maximum(i * hb - 1, 0), 0)),
            pl.BlockSpec((1, D), lambda b, i: (0, 0)),
            pl.BlockSpec((len(POOL_WINDOWS), dg, dg), lambda b, i: (0, 0, 0)),
            pl.BlockSpec((1, D), lambda b, i: (0, 0)),
        ],
        out_specs=pl.BlockSpec((1, ts, D), lambda b, i: (b, i, 0)),
        out_shape=jax.ShapeDtypeStruct((B, S, D), F32),
        compiler_params=_params(("parallel", "parallel")),
        name="pool_mixer",
    )(x, x, g.reshape(1, D), w_groups, scale.reshape(1, D))


def _router_kernel(x_ref, g_ref, wr_ref, br_ref, e_ref, gate_ref, rank_ref, cnt_ref,
                   tri_ref, carry_ref, wcat_ref, *, tt):
    i = pl.program_id(0)
    E = N_EXPERTS

    @pl.when(i == 0)
    def _():
        r = lax.broadcasted_iota(I32, tri_ref.shape, 0)
        c = lax.broadcasted_iota(I32, tri_ref.shape, 1)
        tri_ref[...] = jnp.where(r < c, 1.0, 0.0).astype(BF16)
        carry_ref[...] = jnp.zeros_like(carry_ref)
        w = wr_ref[...]
        w_hi = w.astype(BF16)
        wcat_ref[:, :LANES] = w_hi
        wcat_ref[:, LANES:] = (w - w_hi.astype(F32)).astype(BF16)

    h = _rms(x_ref[...], g_ref[...])
    h_hi = h.astype(BF16)
    h_lo = (h - h_hi.astype(F32)).astype(BF16)
    both = jnp.dot(h_hi, wcat_ref[...], preferred_element_type=F32)
    cross = jnp.dot(h_lo, wcat_ref[:, :LANES], preferred_element_type=F32)
    logits = both[:, :LANES] + (both[:, LANES:] + cross) + br_ref[...]
    nblk = tt // LANES
    l = jnp.concatenate([logits[c * LANES:(c + 1) * LANES].T for c in range(nblk)], axis=1)[:E]
    row = lax.broadcasted_iota(I32, (E, tt), 0).astype(F32)
    vals, sels, idxs = [], [], []
    for k in range(TOP_K):
        m = jnp.max(l, axis=0, keepdims=True)
        idx = jnp.min(jnp.where(l == m, row, float(E)), axis=0, keepdims=True)
        sel = row == idx
        vals.append(m)
        sels.append(sel)
        idxs.append(idx)
        l = jnp.where(sel, -jnp.inf, l)
    ex = [jnp.exp(v - vals[0]) for v in vals]
    den = ex[0] + ex[1] + ex[2] + ex[3]
    multi = jnp.zeros((E, tt), F32)
    for sel in sels:
        multi = multi + jnp.where(sel, 1.0, 0.0)
    base = carry_ref[:, :1]
    parts = []
    for c in range(nblk):
        mc = multi[:, c * LANES:(c + 1) * LANES]
        parts.append(jnp.dot(mc.astype(BF16), tri_ref[...], preferred_element_type=F32) + base)
        base = base + jnp.sum(mc, axis=1, keepdims=True)
    before = jnp.concatenate(parts, axis=1)
    kk = lax.broadcasted_iota(I32, (TOP_K, tt), 0)
    e_out = jnp.zeros((TOP_K, tt), F32)
    g_out = jnp.zeros((TOP_K, tt), F32)
    r_out = jnp.zeros((TOP_K, tt), F32)
    for k in range(TOP_K):
        e_out = jnp.where(kk == k, idxs[k], e_out)
        g_out = jnp.where(kk == k, ex[k] / den, g_out)
        rk = jnp.sum(jnp.where(sels[k], before, 0.0), axis=0, keepdims=True)
        r_out = jnp.where(kk == k, rk, r_out)
    e_ref[...] = e_out.astype(I32)
    gate_ref[...] = g_out
    rank_ref[...] = r_out.astype(I32)
    carry_ref[...] = jnp.broadcast_to(base, carry_ref.shape)
    cnt_ref[...] = carry_ref[...]


def _router(x, g, w_r, b_r):
    N, D = x.shape
    tt = ROUTER_TILE
    E = N_EXPERTS
    w_pad = jnp.pad(w_r, ((0, 0), (0, LANES - E)))
    b_pad = jnp.pad(b_r, (0, LANES - E)).reshape(1, LANES)
    kspec = pl.BlockSpec((TOP_K, tt), lambda i: (0, i))
    return pl.pallas_call(
        functools.partial(_router_kernel, tt=tt),
        grid=(N // tt,),
        in_specs=[
            pl.BlockSpec((tt, D), lambda i: (i, 0)),
            pl.BlockSpec((1, D), lambda i: (0, 0)),
            pl.BlockSpec((D, LANES), lambda i: (0, 0)),
            pl.BlockSpec((1, LANES), lambda i: (0, 0)),
        ],
        out_specs=[kspec, kspec, kspec, pl.BlockSpec((E, LANES), lambda i: (0, 0))],
        out_shape=[
            jax.ShapeDtypeStruct((TOP_K, N), I32),
            jax.ShapeDtypeStruct((TOP_K, N), F32),
            jax.ShapeDtypeStruct((TOP_K, N), I32),
            jax.ShapeDtypeStruct((E, LANES), F32),
        ],
        scratch_shapes=[pltpu.VMEM((LANES, LANES), BF16), pltpu.VMEM((E, LANES), F32),
                        pltpu.VMEM((D, 2 * LANES), BF16)],
        compiler_params=_params(("arbitrary",)),
        name="moe_router",
    )(x, g.reshape(1, D), w_pad, b_pad)


def _to_tile_rows(ref, val):
    n, D = val.shape
    nsub = D // LANES
    for s in range(nsub):
        ref[pl.ds(s, n, stride=nsub), :] = val[:, s * LANES:(s + 1) * LANES]


def _from_tile_rows(ref, n, nsub):
    return jnp.concatenate([ref[pl.ds(s, n, stride=nsub), :] for s in range(nsub)], axis=1)


def _tile_row(ref, r, nsub, count=1):
    return ref.at[pl.ds(pl.multiple_of(r * nsub, nsub), count * nsub)]


def _row_copy_wait(src, dst, sem, times):
    for _ in range(times):
        pltpu.make_async_copy(src, dst.at[pl.ds(0, src.shape[0])], sem).wait()


def _dispatch_kernel(pend_ref, nu_ref, *refs, tt, tm, nt, nsub):
    dest_refs = refs[:TOP_K]
    x_ref, g_ref, xpad_ref, hbuf, zbuf, sems, zsem = refs[TOP_K:]
    i = pl.program_id(0)
    slot = i % 2

    @pl.when(i == 0)
    def _():
        zbuf[...] = jnp.zeros_like(zbuf)

        def seg_copy(e):
            start = pl.multiple_of(jnp.maximum(pend_ref[e] - tm, 0), tm)
            return pltpu.make_async_copy(zbuf, _tile_row(xpad_ref, start, nsub, tm), zsem)

        def tail_copy(j):
            return pltpu.make_async_copy(zbuf, _tile_row(xpad_ref, (nt - 1 - j) * tm, nsub, tm), zsem)

        for e in range(N_EXPERTS):
            seg_copy(e).start()
        for j in range(N_EXPERTS):
            pl.when(nt - 1 - j >= nu_ref[0])(tail_copy(j).start)
        for e in range(N_EXPERTS):
            seg_copy(e).wait()
        for j in range(N_EXPERTS):
            pl.when(nt - 1 - j >= nu_ref[0])(tail_copy(j).wait)

    hb = hbuf.at[slot]
    _to_tile_rows(hb, _rms(x_ref[...], g_ref[...]))

    def issue(t, carry):
        for k in range(TOP_K):
            d = dest_refs[k][t]
            pltpu.make_async_copy(_tile_row(hb, t, nsub), _tile_row(xpad_ref, d, nsub),
                                  sems.at[slot]).start(priority=k % 2)
        return carry

    lax.fori_loop(0, tt, issue, 0, unroll=64)

    @pl.when(i > 0)
    def _():
        _row_copy_wait(hbuf.at[1 - slot], xpad_ref, sems.at[1 - slot], TOP_K)

    @pl.when(i == pl.num_programs(0) - 1)
    def _():
        _row_copy_wait(hb, xpad_ref, sems.at[slot], TOP_K)


def _dispatch(x, g, dest, pend, n_used, P):
    N, D = x.shape
    tt = DISPATCH_TILE
    tm = MOE_TILE
    nsub = D // LANES
    grid_spec = pltpu.PrefetchScalarGridSpec(
        num_scalar_prefetch=2,
        grid=(N // tt,),
        in_specs=[pl.BlockSpec((tt,), lambda i, pe, nu, k=k: (k * (N // tt) + i,), memory_space=pltpu.SMEM)
                  for k in range(TOP_K)] + [
            pl.BlockSpec((tt, D), lambda i, pe, nu: (i, 0)),
            pl.BlockSpec((1, D), lambda i, pe, nu: (0, 0)),
        ],
        out_specs=pl.BlockSpec(memory_space=pl.ANY),
        scratch_shapes=[pltpu.VMEM((2, tt * nsub, LANES), F32), pltpu.VMEM((tm * nsub, LANES), F32),
                        pltpu.SemaphoreType.DMA((2,)), pltpu.SemaphoreType.DMA(())],
    )
    return pl.pallas_call(
        functools.partial(_dispatch_kernel, tt=tt, tm=tm, nt=P // tm, nsub=nsub),
        grid_spec=grid_spec,
        out_shape=jax.ShapeDtypeStruct((P * nsub, LANES), F32),
        compiler_params=_params(("arbitrary",)),
        name="moe_dispatch",
    )(pend, n_used, *([dest] * TOP_K), x, g.reshape(1, D))


def _expert_kernel(te_ref, nx_ref, nu_ref, x_ref, wgu_hbm, bgu_ref, wdn_hbm, bdn_ref, o_ref,
                   wgu_f32, wdn_f32, wgu_bf, wdn_bf, sems, *, layer, F, tm, nsub):
    i = pl.program_id(0)

    def weight_copies(e):
        return (pltpu.make_async_copy(wgu_hbm.at[layer, e], wgu_f32, sems.at[0]),
                pltpu.make_async_copy(wdn_hbm.at[layer, e], wdn_f32, sems.at[1]))

    @pl.when(i < nu_ref[0])
    def _():
        e = te_ref[i]

        @pl.when(i == 0)
        def _():
            for c in weight_copies(e):
                c.start()

        @pl.when((i == 0) | (e != te_ref[jnp.maximum(i - 1, 0)]))
        def _():
            for c in weight_copies(e):
                c.wait()
            wgu_bf[...] = wgu_f32[...].astype(BF16)
            wdn_bf[...] = wdn_f32[...].astype(BF16)
            nxt = nx_ref[i]

            @pl.when(nxt != e)
            def _():
                for c in weight_copies(nxt):
                    c.start()

        x = _from_tile_rows(x_ref, tm, nsub).astype(BF16)
        gu = jnp.dot(x, wgu_bf[...], preferred_element_type=F32) + bgu_ref[0]
        gate = jnp.minimum(gu[:, :F], SWIGLU_LIMIT)
        up = jnp.clip(gu[:, F:], -SWIGLU_LIMIT, SWIGLU_LIMIT)
        glu = gate * jax.nn.sigmoid(SWIGLU_ALPHA * gate)
        a = ((up + 1.0) * glu).astype(BF16)
        _to_tile_rows(o_ref, jnp.dot(a, wdn_bf[...], preferred_element_type=F32) + bdn_ref[0])

    @pl.when(i >= nu_ref[0])
    def _():
        o_ref[...] = jnp.zeros_like(o_ref)


def _experts(tile_e, next_e, n_used, x_pad, w_gu, b_gu, w_dn, b_dn, layer):
    _, E, D, F2 = w_gu.shape
    F = F2 // 2
    tm = MOE_TILE
    nsub = D // LANES
    P = x_pad.shape[0] // nsub
    row = lambda i, te, nx, nu: (jnp.minimum(i, nu[0] - 1), 0)
    exp3 = lambda i, te, nx, nu: (te[i], 0, 0)
    grid_spec = pltpu.PrefetchScalarGridSpec(
        num_scalar_prefetch=3,
        grid=(P // tm,),
        in_specs=[
            pl.BlockSpec((tm * nsub, LANES), row),
            pl.BlockSpec(memory_space=pl.ANY),
            pl.BlockSpec((1, 1, F2), exp3),
            pl.BlockSpec(memory_space=pl.ANY),
            pl.BlockSpec((1, 1, D), exp3),
        ],
        out_specs=pl.BlockSpec((tm * nsub, LANES), lambda i, te, nx, nu: (i, 0)),
        scratch_shapes=[pltpu.VMEM((D, F2), F32), pltpu.VMEM((F, D), F32),
                        pltpu.VMEM((D, F2), BF16), pltpu.VMEM((F, D), BF16),
                        pltpu.SemaphoreType.DMA((2,))],
    )
    return pl.pallas_call(
        functools.partial(_expert_kernel, layer=layer, F=F, tm=tm, nsub=nsub),
        grid_spec=grid_spec,
        out_shape=jax.ShapeDtypeStruct((P * nsub, LANES), F32),
        compiler_params=_params(("arbitrary",)),
        name="moe_experts",
    )(tile_e, next_e, n_used, x_pad, w_gu, b_gu.reshape(E, 1, F2), w_dn, b_dn.reshape(E, 1, D))


def _combine_kernel(*refs, tt, nsub):
    dest_refs = refs[:TOP_K]
    dnext_refs = refs[TOP_K:2 * TOP_K]
    x_ref, gate_ref, ypad_ref, o_ref, ybuf, sems = refs[2 * TOP_K:]
    i = pl.program_id(0)
    slot = i % 2

    def gather(idx_refs, s):
        def issue(t, carry):
            for k in range(TOP_K):
                d = idx_refs[k][t]
                pltpu.make_async_copy(_tile_row(ypad_ref, d, nsub), _tile_row(ybuf.at[s, k], t, nsub),
                                      sems.at[s]).start(priority=k % 2)
            return carry

        lax.fori_loop(0, tt, issue, 0, unroll=64)

    @pl.when(i == 0)
    def _():
        gather(dest_refs, slot)

    @pl.when(i + 1 < pl.num_programs(0))
    def _():
        gather(dnext_refs, 1 - slot)

    for k in range(TOP_K):
        _row_copy_wait(ybuf.at[slot, k], ypad_ref, sems.at[slot], 1)
    gate = gate_ref[...]
    x = x_ref[...]
    cols = []
    for c in range(nsub):
        acc = x[:, c * LANES:(c + 1) * LANES]
        for k in range(TOP_K):
            acc = acc + ybuf[slot, k, pl.ds(c, tt, stride=nsub), :] * gate[:, k:k + 1]
        cols.append(acc)
    o_ref[...] = jnp.concatenate(cols, axis=1)


def _combine(x, gate, dest, y_pad):
    N, D = x.shape
    tt = COMBINE_TILE
    nsub = D // LANES
    steps = N // tt
    last = steps - 1
    return pl.pallas_call(
        functools.partial(_combine_kernel, tt=tt, nsub=nsub),
        grid=(N // tt,),
        in_specs=[pl.BlockSpec((tt,), lambda i, k=k: (k * steps + i,), memory_space=pltpu.SMEM)
                  for k in range(TOP_K)] + [
            pl.BlockSpec((tt,), lambda i, k=k: (k * steps + jnp.minimum(i + 1, last),),
                         memory_space=pltpu.SMEM) for k in range(TOP_K)] + [
            pl.BlockSpec((tt, D), lambda i: (i, 0)),
            pl.BlockSpec((tt, TOP_K), lambda i: (i, 0)),
            pl.BlockSpec(memory_space=pl.ANY),
        ],
        out_specs=pl.BlockSpec((tt, D), lambda i: (i, 0)),
        out_shape=jax.ShapeDtypeStruct((N, D), F32),
        scratch_shapes=[pltpu.VMEM((2, TOP_K, tt * nsub, LANES), F32), pltpu.SemaphoreType.DMA((2,))],
        compiler_params=_params(("arbitrary",)),
        name="moe_combine",
    )(*([dest] * (2 * TOP_K)), x, gate, y_pad)


def _dest_kernel(pstart_ref, e_ref, rank_ref, o_ref):
    e = e_ref[...]
    seg = jnp.zeros(e.shape, I32)
    for ex in range(N_EXPERTS):
        seg = jnp.where(e == ex, pstart_ref[ex], seg)
    o_ref[...] = seg + rank_ref[...]


def _dest_rows(e_idx, rank, pstart):
    return pl.pallas_call(
        _dest_kernel,
        in_specs=[pl.BlockSpec(memory_space=pltpu.SMEM), pl.BlockSpec(e_idx.shape, lambda: (0, 0)),
                  pl.BlockSpec(rank.shape, lambda: (0, 0))],
        out_specs=pl.BlockSpec(rank.shape, lambda: (0, 0)),
        out_shape=jax.ShapeDtypeStruct(rank.shape, I32),
        name="moe_dest",
    )(pstart, e_idx, rank)


def _moe_layer(x, g, w_r, b_r, w_gu, b_gu, w_dn, b_dn, layer):
    N, D = x.shape
    E = N_EXPERTS
    tm = MOE_TILE
    e_idx, gate_t, rank, counts = _router(x, g, w_r, b_r)
    counts = counts[:, 0].astype(I32)
    padded = ((counts + tm - 1) // tm) * tm
    pend = jnp.cumsum(padded)
    pstart = pend - padded
    dest = _dest_rows(e_idx, rank, pstart.astype(I32)).reshape(-1)
    gate = gate_t.T
    P = ((N * TOP_K + E * (tm - 1) + tm - 1) // tm) * tm
    nt = P // tm
    tiles = jnp.arange(nt, dtype=I32)
    tile_e = jnp.minimum(jnp.sum(tiles[:, None] * tm >= pend[None, :], axis=1), E - 1).astype(I32)
    n_used = (pend[-1] // tm).astype(I32)
    tile_e = jnp.where(tiles < n_used, tile_e, tile_e[n_used - 1])
    experts = jnp.arange(E, dtype=I32)
    later = lax.cummin(jnp.where(counts > 0, experts, E), reverse=True)
    after = jnp.concatenate([later[1:], jnp.full((1,), E, I32)])
    next_e = jnp.where(after < E, after, experts)[tile_e]
    x_pad = _dispatch(x, g, dest, pend.astype(I32), n_used.reshape(1), P)
    y_pad = _experts(tile_e, next_e.astype(I32), n_used.reshape(1), x_pad, w_gu, b_gu, w_dn, b_dn, layer)
    return _combine(x, gate, dest, y_pad)


def _bucket_maps():
    qi = np.arange(ATT_BLOCK)[None, :]
    kj = np.arange(2 * ATT_BLOCK)[:, None]
    delta = qi + ATT_BLOCK - kj
    buckets, valids = [], []
    max_exact = N_BUCKETS // 2
    for win, dil in DILATED_GROUPS:
        n = np.maximum(delta * dil, 0)
        nf = np.maximum(n, 1).astype(np.float32)
        large = max_exact + (np.log(nf / np.float32(max_exact)) / np.float32(math.log(REL_MAX_DIST / max_exact))
                             * np.float32(N_BUCKETS - max_exact)).astype(np.int32)
        large = np.minimum(large, N_BUCKETS - 1)
        buckets.append(np.where(n < max_exact, n, large).astype(np.int32))
        valids.append(((delta >= 0) & (delta <= win // dil)).astype(np.int32))
    return np.stack(buckets), np.stack(valids)


def _bias_kernel(tab_ref, bm_ref, valid_ref, o_ref):
    g = pl.program_id(0)
    bm = bm_ref[0]
    key = lax.broadcasted_iota(I32, bm.shape, 0)

    def one_head(h, carry):
        acc = jnp.zeros(bm.shape, F32)
        for b in range(N_BUCKETS):
            acc = jnp.where(bm == b, tab_ref[b, g * ATT_HEADS + h], acc)
        band = jnp.where(valid_ref[0] > 0, acc, NEG_INF)
        o_ref[h, 0] = band
        o_ref[h, 1] = jnp.where(key >= ATT_BLOCK, band, NEG_INF)
        return carry

    lax.fori_loop(0, ATT_HEADS, one_head, 0)


def _bias_tables(rel_bias):
    bm, valid = _bucket_maps()
    GH = len(DILATED_GROUPS) * ATT_HEADS
    blk = (1, 2 * ATT_BLOCK, ATT_BLOCK)
    return pl.pallas_call(
        _bias_kernel,
        grid=(len(DILATED_GROUPS),),
        in_specs=[
            pl.BlockSpec(memory_space=pltpu.SMEM),
            pl.BlockSpec(blk, lambda i: (i, 0, 0)),
            pl.BlockSpec(blk, lambda i: (i, 0, 0)),
        ],
        out_specs=pl.BlockSpec((ATT_HEADS, 2, 2 * ATT_BLOCK, ATT_BLOCK), lambda i: (i, 0, 0, 0)),
        out_shape=jax.ShapeDtypeStruct((GH, 2, 2 * ATT_BLOCK, ATT_BLOCK), F32),
        compiler_params=_params(("arbitrary",)),
        name="attn_bias",
    )(rel_bias, jnp.asarray(bm), jnp.asarray(valid))


def _cast_kernel(w_ref, o_ref):
    o_ref[...] = w_ref[...].astype(BF16)


def _to_bf16(w, layer, col_block):
    _, rows, cols = w.shape
    return pl.pallas_call(
        _cast_kernel,
        grid=(cols // col_block,),
        in_specs=[pl.BlockSpec((None, rows, col_block), lambda j: (layer, 0, j))],
        out_specs=pl.BlockSpec((rows, col_block), lambda j: (0, j)),
        out_shape=jax.ShapeDtypeStruct((rows, cols), BF16),
        compiler_params=_params(("parallel",)),
        name="cast_bf16",
    )(w)


def _qkv_kernel(x_ref, g_ref, wq_ref, wk_ref, wv_ref, qg_ref, kg_ref, qt_ref, k_ref, vt_ref,
                slab_ref, bd_ref, *, dil, R, nl):
    rc = pl.program_id(2)
    D = x_ref.shape[2]
    lanes = slab_ref.shape[2]
    first = (pl.program_id(0) == 0) & (pl.program_id(1) == 0) & (rc == 0)

    @pl.when(first)
    def _():
        r = lax.broadcasted_iota(I32, bd_ref.shape, 0)
        c = lax.broadcasted_iota(I32, bd_ref.shape, 1)
        hd_shift = HEAD_DIM.bit_length() - 1
        same_head = lax.shift_right_logical(r, hd_shift) == lax.shift_right_logical(c, hd_shift)
        bd_ref[...] = jnp.where(same_head, 1.0, 0.0).astype(BF16)

    if dil == 1:
        x = x_ref[0]
    else:
        @pl.when(rc == 0)
        def _():
            for c in range(D // lanes):
                slab_ref[c] = x_ref[0, :, c * lanes:(c + 1) * lanes]

        pieces = []
        for j in range(R):
            r = rc * R + j
            cols = [slab_ref[c, pl.ds(r, nl, stride=dil), :] for c in range(D // lanes)]
            pieces.append(jnp.concatenate(cols, axis=1))
        x = jnp.concatenate(pieces, axis=0)
    h = _rms(x, g_ref[...]).astype(BF16)
    nb = bd_ref.shape[0]

    def head_norm(y, gain):
        parts = []
        for c0 in range(0, y.shape[1], nb):
            yc = y[:, c0:c0 + nb]
            ssq = jnp.dot((yc * yc).astype(BF16), bd_ref[...], preferred_element_type=F32)
            parts.append(yc * lax.rsqrt(ssq * (1.0 / HEAD_DIM) + EPS))
        return jnp.concatenate(parts, axis=1) * gain

    q = head_norm(jnp.dot(h, wq_ref[...], preferred_element_type=F32), qg_ref[...])
    q = q * (HEAD_DIM ** -0.5)
    k = head_norm(jnp.dot(h, wk_ref[...], preferred_element_type=F32), kg_ref[...])
    v = jnp.dot(h, wv_ref[...], preferred_element_type=F32)
    blk = ATT_BLOCK
    for j in range(R):
        k_ref[0, j] = k[j * nl:(j + 1) * nl].astype(BF16)
        for c in range(nl // blk):
            rows = slice(j * nl + c * blk, j * nl + (c + 1) * blk)
            for hp in range(q.shape[1] // blk):
                cs = slice(hp * blk, (hp + 1) * blk)
                qt_ref[0, j, cs, c * blk:(c + 1) * blk] = q[rows, cs].T.astype(BF16)
                vt_ref[0, j, cs, c * blk:(c + 1) * blk] = v[rows, cs].T.astype(BF16)


def _qkv_group(x, g, w_bf, q_gain, k_gain, gi, dil):
    B, S, D = x.shape
    HD = ATT_HEADS * HEAD_DIM
    G = len(DILATED_GROUPS)
    L = S // dil
    R = min(dil, QKV_ROWS // ATT_BLOCK)
    nl = QKV_ROWS // R
    lanes = 128
    wspec = lambda s: pl.BlockSpec((D, HD), lambda b, l, r: (0, s * G + gi))
    tspec = pl.BlockSpec((1, R, HD, nl), lambda b, l, r: (b, r, 0, l))
    tshape = jax.ShapeDtypeStruct((B, dil, HD, L), BF16)
    return pl.pallas_call(
        functools.partial(_qkv_kernel, dil=dil, R=R, nl=nl),
        grid=(B, L // nl, dil // R),
        in_specs=[
            pl.BlockSpec((1, nl * dil, D), lambda b, l, r: (b, l, 0)),
            pl.BlockSpec((1, D), lambda b, l, r: (0, 0)),
            wspec(0), wspec(1), wspec(2),
            pl.BlockSpec((1, HD), lambda b, l, r: (0, 0)),
            pl.BlockSpec((1, HD), lambda b, l, r: (0, 0)),
        ],
        out_specs=[tspec, pl.BlockSpec((1, R, nl, HD), lambda b, l, r: (b, r, l, 0)), tspec],
        out_shape=[tshape, jax.ShapeDtypeStruct((B, dil, L, HD), BF16), tshape],
        scratch_shapes=[pltpu.VMEM((D // lanes, nl * dil if dil > 1 else 8, lanes), F32),
                        pltpu.VMEM((256, 256), BF16)],
        compiler_params=_params(("arbitrary", "arbitrary", "arbitrary")),
        name=f"attn_qkv_g{gi}",
    )(x, g.reshape(1, D), w_bf, w_bf, w_bf,
      jnp.tile(q_gain, ATT_HEADS).reshape(1, HD), jnp.tile(k_gain, ATT_HEADS).reshape(1, HD))


def _attn_kernel(*refs, single, nitems):
    if single:
        qt_ref, kc_ref, vtc_ref, bias_ref, o_ref, l_ref, s_ref, p_ref = refs
    else:
        qt_ref, kc_ref, kp_ref, vtc_ref, vtp_ref, bias_ref, o_ref, l_ref, s_ref, p_ref = refs
    blk = ATT_BLOCK
    pair = 2 * HEAD_DIM
    first = jnp.where(pl.program_id(2) == 0, 1, 0)
    if single:
        k0 = blk
        variants = [1] * nitems
        q_of = lambda it, cs: qt_ref[0, it, cs, :]
        k_of = lambda it, cs: kc_ref[0, it, :, cs]
        vt_of = lambda it, cs: vtc_ref[0, it, cs, :]
    else:
        k0 = 0
        variants = [first] + [0] * (nitems - 1)

        def q_of(it, cs):
            return qt_ref[0, 0, cs, it * blk:(it + 1) * blk]

        def k_of(it, cs):
            if it == 0:
                return jnp.concatenate([kp_ref[0, 0, :, cs], kc_ref[0, 0, :blk, cs]], axis=0)
            return kc_ref[0, 0, (it - 1) * blk:(it + 1) * blk, cs]

        def vt_of(it, cs):
            if it == 0:
                return jnp.concatenate([vtp_ref[0, 0, cs, :], vtc_ref[0, 0, cs, :blk]], axis=1)
            return vtc_ref[0, 0, cs, (it - 1) * blk:(it + 1) * blk]

    feat = lax.broadcasted_iota(I32, (pair, blk), 0)
    head_rows = [jnp.where(feat < HEAD_DIM, 1.0, 0.0).astype(BF16),
                 jnp.where(feat < HEAD_DIM, 0.0, 1.0).astype(BF16)]
    invs = {}
    lse_rows = {}

    def scores(it, pr):
        cs = slice(pr * pair, (pr + 1) * pair)
        qt2 = q_of(it, cs)
        k2 = k_of(it, cs)
        for half in range(2):
            qth = qt2 * head_rows[half]
            s_ref[it * ATT_HEADS + 2 * pr + half] = jnp.dot(k2, qth, preferred_element_type=F32)

    def softmax(it, pr):
        for head in (2 * pr, 2 * pr + 1):
            s = s_ref[it * ATT_HEADS + head] + bias_ref[head, variants[it], k0:, :]
            m = jnp.max(s, axis=0, keepdims=True)
            p = jnp.exp(s - m)
            den = jnp.sum(p, axis=0, keepdims=True)
            p_ref[it * ATT_HEADS + head] = p.astype(BF16)
            invs[it, head] = 1.0 / den
            lse_rows[it, head] = m + jnp.log(den)

    def outputs(it, pr):
        cs = slice(pr * pair, (pr + 1) * pair)
        vt2 = vt_of(it, cs)
        halves = []
        for half in range(2):
            head = 2 * pr + half
            vth = vt2[half * HEAD_DIM:(half + 1) * HEAD_DIM]
            ot = jnp.dot(vth, p_ref[it * ATT_HEADS + head], preferred_element_type=F32)
            halves.append(ot * invs[it, head])
        val = jnp.concatenate(halves, axis=0).T
        if single:
            o_ref[0, it, :, cs] = val
        else:
            o_ref[0, 0, it * blk:(it + 1) * blk, cs] = val

    work = [(it, pr) for it in range(nitems) for pr in range(ATT_HEADS // 2)]
    for it, pr in work:
        scores(it, pr)
    for it, pr in work:
        softmax(it, pr)
    for it, pr in work:
        outputs(it, pr)
    lse_row = lax.broadcasted_iota(I32, (blk, blk), 0)
    for it in range(nitems):
        lse_t = jnp.zeros((blk, blk), F32)
        for head in range(ATT_HEADS):
            lse_t = jnp.where(lse_row == head, lse_rows[it, head], lse_t)
        if single:
            l_ref[0, it] = lse_t.T
        else:
            l_ref[0, 0, it * blk:(it + 1) * blk, :] = lse_t.T


def _attn_group(qt, k, vt, bias, gi):
    B, dil, L, HD = k.shape
    blk = ATT_BLOCK
    nb = L // blk
    single = nb == 1
    ni = min(ATT_ITEMS, dil if single else nb)
    bias_spec = pl.BlockSpec((ATT_HEADS, 2, 2 * blk, blk), lambda b, r, n: (gi, 0, 0, 0))
    if single:
        rows = pl.BlockSpec((1, ni, blk, HD), lambda b, r, n: (b, r, 0, 0))
        cols = pl.BlockSpec((1, ni, HD, blk), lambda b, r, n: (b, r, 0, 0))
        grid = (B, dil // ni, 1)
        in_specs = [cols, rows, cols, bias_spec]
        args = (qt, k, vt, bias)
        out_specs = [rows, pl.BlockSpec((1, ni, blk, blk), lambda b, r, n: (b, r, 0, 0))]
        nk = blk
    else:
        rows = pl.BlockSpec((1, 1, ni * blk, HD), lambda b, r, n: (b, r, n, 0))
        cols = pl.BlockSpec((1, 1, HD, ni * blk), lambda b, r, n: (b, r, 0, n))
        prow = pl.BlockSpec((1, 1, blk, HD), lambda b, r, n: (b, r, jnp.maximum(ni * n - 1, 0), 0))
        pcol = pl.BlockSpec((1, 1, HD, blk), lambda b, r, n: (b, r, 0, jnp.maximum(ni * n - 1, 0)))
        grid = (B, dil, nb // ni)
        in_specs = [cols, rows, prow, cols, pcol, bias_spec]
        args = (qt, k, k, vt, vt, bias)
        out_specs = [rows, pl.BlockSpec((1, 1, ni * blk, blk), lambda b, r, n: (b, r, n, 0))]
        nk = 2 * blk
    return pl.pallas_call(
        functools.partial(_attn_kernel, single=single, nitems=ni),
        grid=grid,
        in_specs=in_specs,
        out_specs=out_specs,
        out_shape=[jax.ShapeDtypeStruct((B, dil, L, HD), F32),
                   jax.ShapeDtypeStruct((B, dil, L, blk), F32)],
        scratch_shapes=[pltpu.VMEM((ni * ATT_HEADS, nk, blk), F32),
                        pltpu.VMEM((ni * ATT_HEADS, nk, blk), BF16)],
        compiler_params=_params(("parallel", "parallel", "arbitrary")),
        name=f"attn_core_g{gi}",
    )(*args)


def _attn_out_kernel(x_ref, o0, o1, o2, l0, l1, l2, wo_ref, out_ref, oslab, lslab, ex_ref, *, T):
    lanes = oslab.shape[3]
    HD = o0.shape[3]

    @pl.when((pl.program_id(0) == 0) & (pl.program_id(1) == 0))
    def _():
        h = lax.broadcasted_iota(I32, ex_ref.shape, 0) & (ATT_BLOCK - 1)
        c = lax.broadcasted_iota(I32, ex_ref.shape, 1)
        owner = lax.shift_right_logical(c, HEAD_DIM.bit_length() - 1)
        ex_ref[...] = jnp.where(h == owner, 1.0, 0.0).astype(BF16)

    def token_order(o_ref, l_ref, dil, s):
        if dil == 1:
            return o_ref[0, 0], l_ref[0, 0]
        n = T // dil
        for r in range(dil):
            blk = o_ref[0, r]
            for c in range(HD // lanes):
                oslab[s, c, pl.ds(r, n, stride=dil), :] = blk[:, c * lanes:(c + 1) * lanes]
            lslab[s, pl.ds(r, n, stride=dil), :] = l_ref[0, r]
        o = jnp.concatenate([oslab[s, c] for c in range(HD // lanes)], axis=1)
        return o, lslab[s]

    groups = [token_order(o_ref, l_ref, dil, s)
              for s, ((_, dil), o_ref, l_ref) in enumerate(zip(DILATED_GROUPS, (o0, o1, o2), (l0, l1, l2)))]
    m = jnp.maximum(jnp.maximum(groups[0][1], groups[1][1]), groups[2][1])
    es = [jnp.exp(l - m) for _, l in groups]
    inv = 1.0 / (es[0] + es[1] + es[2])
    att = jnp.zeros((T, HD), F32)
    for (o, _), e in zip(groups, es):
        a = e * inv
        a_hi = a.astype(BF16)
        a_lo = (a - a_hi.astype(F32)).astype(BF16)
        wide = jnp.dot(jnp.concatenate([a_hi, a_lo], axis=1), ex_ref[...], preferred_element_type=F32)
        att = att + wide * o
    out_ref[0] = x_ref[0] + jnp.dot(att.astype(BF16), wo_ref[...], preferred_element_type=F32)


def _attn_out(x, os, ls, wo_bf):
    B, S, D = x.shape
    HD = wo_bf.shape[0]
    T = OUT_TILE
    lanes = 128
    ospec = lambda dil, w: pl.BlockSpec((1, dil, T // dil, w), lambda b, i: (b, 0, i, 0))
    dils = [dil for _, dil in DILATED_GROUPS]
    xspec = pl.BlockSpec((1, T, D), lambda b, i: (b, i, 0))
    return pl.pallas_call(
        functools.partial(_attn_out_kernel, T=T),
        grid=(B, S // T),
        in_specs=[xspec] + [ospec(d, HD) for d in dils] + [ospec(d, ATT_BLOCK) for d in dils]
                 + [pl.BlockSpec((HD, D), lambda b, i: (0, 0))],
        out_specs=xspec,
        out_shape=jax.ShapeDtypeStruct((B, S, D), F32),
        scratch_shapes=[pltpu.VMEM((len(dils), HD // lanes, T, lanes), F32),
                        pltpu.VMEM((len(dils), T, ATT_BLOCK), F32),
                        pltpu.VMEM((2 * ATT_BLOCK, HD), BF16)],
        compiler_params=_params(("arbitrary", "arbitrary")),
        name="attn_out",
    )(x, *os, *ls, wo_bf)


def _attention_layer(x, g, w_qkv, q_gain, k_gain, w_o, rel_bias, layer):
    HD = ATT_HEADS * HEAD_DIM
    bias = _bias_tables(rel_bias)
    wqkv_bf = _to_bf16(w_qkv, layer, HD)
    wo_bf = _to_bf16(w_o, layer, w_o.shape[2])
    os, ls = [], []
    for gi, (_, dil) in enumerate(DILATED_GROUPS):
        qt, k, vt = _qkv_group(x, g, wqkv_bf, q_gain, k_gain, gi, dil)
        o, l = _attn_group(qt, k, vt, bias, gi)
        os.append(o)
        ls.append(l)
    return _attn_out(x, os, ls, wo_bf)


def kernel(x, norm_mix_g, norm_ffn_g, pool_w, pool_scale, attn_w_qkv, attn_q_gain, attn_k_gain,
           attn_w_o, rel_bias, moe_w_router, moe_b_router, moe_w_gate_up, moe_b_gate_up,
           moe_w_down, moe_b_down):
    B, S, D = x.shape
    depth = norm_mix_g.shape[0]
    for i in range(depth):
        j = i // 2
        if i % 2 == 0:
            x = _pool_layer(x, norm_mix_g[i], pool_w[j], pool_scale[j])
        else:
            x = _attention_layer(x, norm_mix_g[i], attn_w_qkv, attn_q_gain[j], attn_k_gain[j],
                                 attn_w_o, rel_bias, j)
        x = _moe_layer(x.reshape(B * S, D), norm_ffn_g[i], moe_w_router[i], moe_b_router[i],
                       moe_w_gate_up, moe_b_gate_up[i], moe_w_down, moe_b_down[i],
                       i).reshape(B, S, D)
    return x
```

```python
import functools
import math

import numpy as np
import jax
import jax.numpy as jnp
from jax import lax
from jax.experimental import pallas as pl
from jax.experimental.pallas import tpu as pltpu

F32 = jnp.float32
BF16 = jnp.bfloat16
I32 = jnp.int32

EPS = 1e-6
POOL_WINDOWS = (2, 4, 8, 16)
POOL_HALO = 16
DILATED_GROUPS = ((128, 1), (512, 4), (2048, 16))
ATT_HEADS = 16
HEAD_DIM = 64
ATT_BLOCK = 128
ATT_ITEMS = 8
N_BUCKETS = 32
REL_MAX_DIST = 2048
NEG_INF = -1e30
N_EXPERTS = 32
TOP_K = 4
SWIGLU_LIMIT = 7.0
SWIGLU_ALPHA = 1.702

V7X_VMEM_LIMIT_BYTES = 56 * 1024 * 1024
MOE_TILE = 512
DISPATCH_TILE = 512
COMBINE_TILE = 256
ROUTER_TILE = 1024
POOL_TILE = 1024
QKV_ROWS = 1024
LANES = 128
OUT_TILE = 512


def _rms(xf, g):
    ms = jnp.mean(xf * xf, axis=-1, keepdims=True)
    return xf * lax.rsqrt(ms + EPS) * g


def _params(sem, vmem=None):
    return pltpu.CompilerParams(dimension_semantics=sem,
                                vmem_limit_bytes=vmem or V7X_VMEM_LIMIT_BYTES)


def _pool_kernel(x_ref, halo_ref, g_ref, w_ref, sc_ref, o_ref, *, ts, dg):
    i = pl.program_id(1)
    x = x_ref[0]
    g = g_ref[...]
    h = _rms(x, g)
    hh = _rms(halo_ref[0], g)
    hh = jnp.where(i == 0, 0.0, hh)
    full = jnp.concatenate([hh, h], axis=0)
    pos = i * ts + lax.broadcasted_iota(I32, (ts, 1), 0)
    outs = []
    for gi, w in enumerate(POOL_WINDOWS):
        s = full[:, gi * dg:(gi + 1) * dg]
        sh = 1
        while sh < w:
            s = s + pltpu.roll(s, sh, 0)
            sh *= 2
        s = s[POOL_HALO:]
        cnt = jnp.minimum(pos + 1, w).astype(F32)
        p = s / cnt - h[:, gi * dg:(gi + 1) * dg]
        outs.append(jnp.dot(p.astype(BF16), w_ref[gi].astype(BF16),
                            preferred_element_type=F32))
    y = jnp.concatenate(outs, axis=1) * sc_ref[...]
    o_ref[0] = x + y


def _pool_layer(x, g, w_groups, scale):
    B, S, D = x.shape
    ts = POOL_TILE
    dg = D // len(POOL_WINDOWS)
    hb = ts // POOL_HALO
    return pl.pallas_call(
        functools.partial(_pool_kernel, ts=ts, dg=dg),
        grid=(B, S // ts),
        in_specs=[
            pl.BlockSpec((1, ts, D), lambda b, i: (b, i, 0)),
            pl.BlockSpec((1, POOL_HALO, D), lambda b, i: (b, jnp.maximum(i * hb - 1, 0), 0)),
            pl.BlockSpec((1, D), lambda b, i: (0, 0)),
            pl.BlockSpec((len(POOL_WINDOWS), dg, dg), lambda b, i: (0, 0, 0)),
            pl.BlockSpec((1, D), lambda b, i: (0, 0)),
        ],
        out_specs=pl.BlockSpec((1, ts, D), lambda b, i: (b, i, 0)),
        out_shape=jax.ShapeDtypeStruct((B, S, D), F32),
        compiler_params=_params(("parallel", "parallel")),
        name="pool_mixer",
    )(x, x, g.reshape(1, D), w_groups, scale.reshape(1, D))


def _router_kernel(x_ref, g_ref, wr_ref, br_ref, e_ref, gate_ref, rank_ref, cnt_ref,
                   tri_ref, carry_ref, wcat_ref, *, tt):
    i = pl.program_id(0)
    E = N_EXPERTS

    @pl.when(i == 0)
    def _():
        r = lax.broadcasted_iota(I32, tri_ref.shape, 0)
        c = lax.broadcasted_iota(I32, tri_ref.shape, 1)
        tri_ref[...] = jnp.where(r < c, 1.0, 0.0).astype(BF16)
        carry_ref[...] = jnp.zeros_like(carry_ref)
        w = wr_ref[...]
        w_hi = w.astype(BF16)
        wcat_ref[:, :LANES] = w_hi
        wcat_ref[:, LANES:] = (w - w_hi.astype(F32)).astype(BF16)

    h = _rms(x_ref[...], g_ref[...])
    h_hi = h.astype(BF16)
    h_lo = (h - h_hi.astype(F32)).astype(BF16)
    both = jnp.dot(h_hi, wcat_ref[...], preferred_element_type=F32)
    cross = jnp.dot(h_lo, wcat_ref[:, :LANES], preferred_element_type=F32)
    logits = both[:, :LANES] + (both[:, LANES:] + cross) + br_ref[...]
    nblk = tt // LANES
    l = jnp.concatenate([logits[c * LANES:(c + 1) * LANES].T for c in range(nblk)], axis=1)[:E]
    row = lax.broadcasted_iota(I32, (E, tt), 0).astype(F32)
    vals, sels, idxs = [], [], []
    for k in range(TOP_K):
        m = jnp.max(l, axis=0, keepdims=True)
        idx = jnp.min(jnp.where(l == m, row, float(E)), axis=0, keepdims=True)
        sel = row == idx
        vals.append(m)
        sels.append(sel)
        idxs.append(idx)
        l = jnp.where(sel, -jnp.inf, l)
    ex = [jnp.exp(v - vals[0]) for v in vals]
    den = ex[0] + ex[1] + ex[2] + ex[3]
    multi = jnp.zeros((E, tt), F32)
    for sel in sels:
        multi = multi + jnp.where(sel, 1.0, 0.0)
    base = carry_ref[:, :1]
    parts = []
    for c in range(nblk):
        mc = multi[:, c * LANES:(c + 1) * LANES]
        parts.append(jnp.dot(mc.astype(BF16), tri_ref[...], preferred_element_type=F32) + base)
        base = base + jnp.sum(mc, axis=1, keepdims=True)
    before = jnp.concatenate(parts, axis=1)
    kk = lax.broadcasted_iota(I32, (TOP_K, tt), 0)
    e_out = jnp.zeros((TOP_K, tt), F32)
    g_out = jnp.zeros((TOP_K, tt), F32)
    r_out = jnp.zeros((TOP_K, tt), F32)
    for k in range(TOP_K):
        e_out = jnp.where(kk == k, idxs[k], e_out)
        g_out = jnp.where(kk == k, ex[k] / den, g_out)
        rk = jnp.sum(jnp.where(sels[k], before, 0.0), axis=0, keepdims=True)
        r_out = jnp.where(kk == k, rk, r_out)
    e_ref[...] = e_out.astype(I32)
    gate_ref[...] = g_out
    rank_ref[...] = r_out.astype(I32)
    carry_ref[...] = jnp.broadcast_to(base, carry_ref.shape)
    cnt_ref[...] = carry_ref[...]


def _router(x, g, w_r, b_r):
    N, D = x.shape
    tt = ROUTER_TILE
    E = N_EXPERTS
    w_pad = jnp.pad(w_r, ((0, 0), (0, LANES - E)))
    b_pad = jnp.pad(b_r, (0, LANES - E)).reshape(1, LANES)
    kspec = pl.BlockSpec((TOP_K, tt), lambda i: (0, i))
    return pl.pallas_call(
        functools.partial(_router_kernel, tt=tt),
        grid=(N // tt,),
        in_specs=[
            pl.BlockSpec((tt, D), lambda i: (i, 0)),
            pl.BlockSpec((1, D), lambda i: (0, 0)),
            pl.BlockSpec((D, LANES), lambda i: (0, 0)),
            pl.BlockSpec((1, LANES), lambda i: (0, 0)),
        ],
        out_specs=[kspec, kspec, kspec, pl.BlockSpec((E, LANES), lambda i: (0, 0))],
        out_shape=[
            jax.ShapeDtypeStruct((TOP_K, N), I32),
            jax.ShapeDtypeStruct((TOP_K, N), F32),
            jax.ShapeDtypeStruct((TOP_K, N), I32),
            jax.ShapeDtypeStruct((E, LANES), F32),
        ],
        scratch_shapes=[pltpu.VMEM((LANES, LANES), BF16), pltpu.VMEM((E, LANES), F32),
                        pltpu.VMEM((D, 2 * LANES), BF16)],
        compiler_params=_params(("arbitrary",)),
        name="moe_router",
    )(x, g.reshape(1, D), w_pad, b_pad)


def _to_tile_rows(ref, val):
    n, D = val.shape
    nsub = D // LANES
    for s in range(nsub):
        ref[pl.ds(s, n, stride=nsub), :] = val[:, s * LANES:(s + 1) * LANES]


def _from_tile_rows(ref, n, nsub):
    return jnp.concatenate([ref[pl.ds(s, n, stride=nsub), :] for s in range(nsub)], axis=1)


def _tile_row(ref, r, nsub, count=1):
    return ref.at[pl.ds(pl.multiple_of(r * nsub, nsub), count * nsub)]


def _row_copy_wait(src, dst, sem, times):
    for _ in range(times):
        pltpu.make_async_copy(src, dst.at[pl.ds(0, src.shape[0])], sem).wait()


def _dispatch_kernel(pend_ref, nu_ref, *refs, tt, tm, nt, nsub):
    dest_refs = refs[:TOP_K]
    x_ref, g_ref, xpad_ref, hbuf, zbuf, sems, zsem = refs[TOP_K:]
    i = pl.program_id(0)
    slot = i % 2

    @pl.when(i == 0)
    def _():
        zbuf[...] = jnp.zeros_like(zbuf)

        def seg_copy(e):
            start = pl.multiple_of(jnp.maximum(pend_ref[e] - tm, 0), tm)
            return pltpu.make_async_copy(zbuf, _tile_row(xpad_ref, start, nsub, tm), zsem)

        def tail_copy(j):
            return pltpu.make_async_copy(zbuf, _tile_row(xpad_ref, (nt - 1 - j) * tm, nsub, tm), zsem)

        for e in range(N_EXPERTS):
            seg_copy(e).start()
        for j in range(N_EXPERTS):
            pl.when(nt - 1 - j >= nu_ref[0])(tail_copy(j).start)
        for e in range(N_EXPERTS):
            seg_copy(e).wait()
        for j in range(N_EXPERTS):
            pl.when(nt - 1 - j >= nu_ref[0])(tail_copy(j).wait)

    hb = hbuf.at[slot]
    _to_tile_rows(hb, _rms(x_ref[...], g_ref[...]))

    def issue(t, carry):
        for k in range(TOP_K):
            d = dest_refs[k][t]
            pltpu.make_async_copy(_tile_row(hb, t, nsub), _tile_row(xpad_ref, d, nsub),
                                  sems.at[slot]).start(priority=k % 2)
        return carry

    lax.fori_loop(0, tt, issue, 0, unroll=64)

    @pl.when(i > 0)
    def _():
        _row_copy_wait(hbuf.at[1 - slot], xpad_ref, sems.at[1 - slot], TOP_K)

    @pl.when(i == pl.num_programs(0) - 1)
    def _():
        _row_copy_wait(hb, xpad_ref, sems.at[slot], TOP_K)


def _dispatch(x, g, dest, pend, n_used, P):
    N, D = x.shape
    tt = DISPATCH_TILE
    tm = MOE_TILE
    nsub = D // LANES
    grid_spec = pltpu.PrefetchScalarGridSpec(
        num_scalar_prefetch=2,
        grid=(N // tt,),
        in_specs=[pl.BlockSpec((tt,), lambda i, pe, nu, k=k: (k * (N // tt) + i,), memory_space=pltpu.SMEM)
                  for k in range(TOP_K)] + [
            pl.BlockSpec((tt, D), lambda i, pe, nu: (i, 0)),
            pl.BlockSpec((1, D), lambda i, pe, nu: (0, 0)),
        ],
        out_specs=pl.BlockSpec(memory_space=pl.ANY),
        scratch_shapes=[pltpu.VMEM((2, tt * nsub, LANES), F32), pltpu.VMEM((tm * nsub, LANES), F32),
                        pltpu.SemaphoreType.DMA((2,)), pltpu.SemaphoreType.DMA(())],
    )
    return pl.pallas_call(
        functools.partial(_dispatch_kernel, tt=tt, tm=tm, nt=P // tm, nsub=nsub),
        grid_spec=grid_spec,
        out_shape=jax.ShapeDtypeStruct((P * nsub, LANES), F32),
        compiler_params=_params(("arbitrary",)),
        name="moe_dispatch",
    )(pend, n_used, *([dest] * TOP_K), x, g.reshape(1, D))


def _expert_kernel(te_ref, nx_ref, nu_ref, x_ref, wgu_hbm, bgu_ref, wdn_hbm, bdn_ref, o_ref,
                   wgu_f32, wdn_f32, wgu_bf, wdn_bf, sems, *, layer, F, tm, nsub):
    i = pl.program_id(0)

    def weight_copies(e):
        return (pltpu.make_async_copy(wgu_hbm.at[layer, e], wgu_f32, sems.at[0]),
                pltpu.make_async_copy(wdn_hbm.at[layer, e], wdn_f32, sems.at[1]))

    @pl.when(i < nu_ref[0])
    def _():
        e = te_ref[i]

        @pl.when(i == 0)
        def _():
            for c in weight_copies(e):
                c.start()

        @pl.when((i == 0) | (e != te_ref[jnp.maximum(i - 1, 0)]))
        def _():
            for c in weight_copies(e):
                c.wait()
            wgu_bf[...] = wgu_f32[...].astype(BF16)
            wdn_bf[...] = wdn_f32[...].astype(BF16)
            nxt = nx_ref[i]

            @pl.when(nxt != e)
            def _():
                for c in weight_copies(nxt):
                    c.start()

        x = _from_tile_rows(x_ref, tm, nsub).astype(BF16)
        gu = jnp.dot(x, wgu_bf[...], preferred_element_type=F32) + bgu_ref[0]
        gate = jnp.minimum(gu[:, :F], SWIGLU_LIMIT)
        up = jnp.clip(gu[:, F:], -SWIGLU_LIMIT, SWIGLU_LIMIT)
        glu = gate * jax.nn.sigmoid(SWIGLU_ALPHA * gate)
        a = ((up + 1.0) * glu).astype(BF16)
        _to_tile_rows(o_ref, jnp.dot(a, wdn_bf[...], preferred_element_type=F32) + bdn_ref[0])

    @pl.when(i >= nu_ref[0])
    def _():
        o_ref[...] = jnp.zeros_like(o_ref)


def _experts(tile_e, next_e, n_used, x_pad, w_gu, b_gu, w_dn, b_dn, layer):
    _, E, D, F2 = w_gu.shape
    F = F2 // 2
    tm = MOE_TILE
    nsub = D // LANES
    P = x_pad.shape[0] // nsub
    row = lambda i, te, nx, nu: (jnp.minimum(i, nu[0] - 1), 0)
    exp3 = lambda i, te, nx, nu: (te[i], 0, 0)
    grid_spec = pltpu.PrefetchScalarGridSpec(
        num_scalar_prefetch=3,
        grid=(P // tm,),
        in_specs=[
            pl.BlockSpec((tm * nsub, LANES), row),
            pl.BlockSpec(memory_space=pl.ANY),
            pl.BlockSpec((1, 1, F2), exp3),
            pl.BlockSpec(memory_space=pl.ANY),
            pl.BlockSpec((1, 1, D), exp3),
        ],
        out_specs=pl.BlockSpec((tm * nsub, LANES), lambda i, te, nx, nu: (i, 0)),
        scratch_shapes=[pltpu.VMEM((D, F2), F32), pltpu.VMEM((F, D), F32),
                        pltpu.VMEM((D, F2), BF16), pltpu.VMEM((F, D), BF16),
                        pltpu.SemaphoreType.DMA((2,))],
    )
    return pl.pallas_call(
        functools.partial(_expert_kernel, layer=layer, F=F, tm=tm, nsub=nsub),
        grid_spec=grid_spec,
        out_shape=jax.ShapeDtypeStruct((P * nsub, LANES), F32),
        compiler_params=_params(("arbitrary",)),
        name="moe_experts",
    )(tile_e, next_e, n_used, x_pad, w_gu, b_gu.reshape(E, 1, F2), w_dn, b_dn.reshape(E, 1, D))


def _combine_kernel(*refs, tt, nsub):
    dest_refs = refs[:TOP_K]
    dnext_refs = refs[TOP_K:2 * TOP_K]
    x_ref, gate_ref, ypad_ref, o_ref, ybuf, sems = refs[2 * TOP_K:]
    i = pl.program_id(0)
    slot = i % 2

    def gather(idx_refs, s):
        def issue(t, carry):
            for k in range(TOP_K):
                d = idx_refs[k][t]
                pltpu.make_async_copy(_tile_row(ypad_ref, d, nsub), _tile_row(ybuf.at[s, k], t, nsub),
                                      sems.at[s]).start(priority=k % 2)
            return carry

        lax.fori_loop(0, tt, issue, 0, unroll=64)

    @pl.when(i == 0)
    def _():
        gather(dest_refs, slot)

    @pl.when(i + 1 < pl.num_programs(0))
    def _():
        gather(dnext_refs, 1 - slot)

    for k in range(TOP_K):
        _row_copy_wait(ybuf.at[slot, k], ypad_ref, sems.at[slot], 1)
    gate = gate_ref[...]
    x = x_ref[...]
    cols = []
    for c in range(nsub):
        acc = x[:, c * LANES:(c + 1) * LANES]
        for k in range(TOP_K):
            acc = acc + ybuf[slot, k, pl.ds(c, tt, stride=nsub), :] * gate[:, k:k + 1]
        cols.append(acc)
    o_ref[...] = jnp.concatenate(cols, axis=1)


def _combine(x, gate, dest, y_pad):
    N, D = x.shape
    tt = COMBINE_TILE
    nsub = D // LANES
    steps = N // tt
    last = steps - 1
    return pl.pallas_call(
        functools.partial(_combine_kernel, tt=tt, nsub=nsub),
        grid=(N // tt,),
        in_specs=[pl.BlockSpec((tt,), lambda i, k=k: (k * steps + i,), memory_space=pltpu.SMEM)
                  for k in range(TOP_K)] + [
            pl.BlockSpec((tt,), lambda i, k=k: (k * steps + jnp.minimum(i + 1, last),),
                         memory_space=pltpu.SMEM) for k in range(TOP_K)] + [
            pl.BlockSpec((tt, D), lambda i: (i, 0)),
            pl.BlockSpec((tt, TOP_K), lambda i: (i, 0)),
            pl.BlockSpec(memory_space=pl.ANY),
        ],
        out_specs=pl.BlockSpec((tt, D), lambda i: (i, 0)),
        out_shape=jax.ShapeDtypeStruct((N, D), F32),
        scratch_shapes=[pltpu.VMEM((2, TOP_K, tt * nsub, LANES), F32), pltpu.SemaphoreType.DMA((2,))],
        compiler_params=_params(("arbitrary",)),
        name="moe_combine",
    )(*([dest] * (2 * TOP_K)), x, gate, y_pad)


def _dest_kernel(pstart_ref, e_ref, rank_ref, o_ref):
    e = e_ref[...]
    seg = jnp.zeros(e.shape, I32)
    for ex in range(N_EXPERTS):
        seg = jnp.where(e == ex, pstart_ref[ex], seg)
    o_ref[...] = seg + rank_ref[...]


def _dest_rows(e_idx, rank, pstart):
    return pl.pallas_call(
        _dest_kernel,
        in_specs=[pl.BlockSpec(memory_space=pltpu.SMEM), pl.BlockSpec(e_idx.shape, lambda: (0, 0)),
                  pl.BlockSpec(rank.shape, lambda: (0, 0))],
        out_specs=pl.BlockSpec(rank.shape, lambda: (0, 0)),
        out_shape=jax.ShapeDtypeStruct(rank.shape, I32),
        name="moe_dest",
    )(pstart, e_idx, rank)


def _moe_layer(x, g, w_r, b_r, w_gu, b_gu, w_dn, b_dn, layer):
    N, D = x.shape
    E = N_EXPERTS
    tm = MOE_TILE
    e_idx, gate_t, rank, counts = _router(x, g, w_r, b_r)
    counts = counts[:, 0].astype(I32)
    padded = ((counts + tm - 1) // tm) * tm
    pend = jnp.cumsum(padded)
    pstart = pend - padded
    dest = _dest_rows(e_idx, rank, pstart.astype(I32)).reshape(-1)
    gate = gate_t.T
    P = ((N * TOP_K + E * (tm - 1) + tm - 1) // tm) * tm
    nt = P // tm
    tiles = jnp.arange(nt, dtype=I32)
    tile_e = jnp.minimum(jnp.sum(tiles[:, None] * tm >= pend[None, :], axis=1), E - 1).astype(I32)
    n_used = (pend[-1] // tm).astype(I32)
    experts = jnp.arange(E, dtype=I32)
    used = counts > 0
    tile_e = jnp.where(tiles < n_used, tile_e, jnp.max(jnp.where(used, experts, 0)))
    after = jnp.min(jnp.where((experts[None, :] > experts[:, None]) & used[None, :], experts[None, :], E), axis=1)
    after = jnp.where(after < E, after, experts)
    next_e = jnp.sum(jnp.where(tile_e[:, None] == experts[None, :], after[None, :], 0), axis=1)
    x_pad = _dispatch(x, g, dest, pend.astype(I32), n_used.reshape(1), P)
    y_pad = _experts(tile_e, next_e.astype(I32), n_used.reshape(1), x_pad, w_gu, b_gu, w_dn, b_dn, layer)
    return _combine(x, gate, dest, y_pad)


def _bucket_maps():
    qi = np.arange(ATT_BLOCK)[None, :]
    kj = np.arange(2 * ATT_BLOCK)[:, None]
    delta = qi + ATT_BLOCK - kj
    buckets, valids = [], []
    max_exact = N_BUCKETS // 2
    for win, dil in DILATED_GROUPS:
        n = np.maximum(delta * dil, 0)
        nf = np.maximum(n, 1).astype(np.float32)
        large = max_exact + (np.log(nf / np.float32(max_exact)) / np.float32(math.log(REL_MAX_DIST / max_exact))
                             * np.float32(N_BUCKETS - max_exact)).astype(np.int32)
        large = np.minimum(large, N_BUCKETS - 1)
        buckets.append(np.where(n < max_exact, n, large).astype(np.int32))
        valids.append(((delta >= 0) & (delta <= win // dil)).astype(np.int32))
    return np.stack(buckets), np.stack(valids)


def _bias_kernel(tab_ref, bm_ref, valid_ref, o_ref):
    g = pl.program_id(0)
    bm = bm_ref[0]
    key = lax.broadcasted_iota(I32, bm.shape, 0)

    def one_head(h, carry):
        acc = jnp.zeros(bm.shape, F32)
        for b in range(N_BUCKETS):
            acc = jnp.where(bm == b, tab_ref[b, g * ATT_HEADS + h], acc)
        band = jnp.where(valid_ref[0] > 0, acc, NEG_INF)
        o_ref[h, 0] = band
        o_ref[h, 1] = jnp.where(key >= ATT_BLOCK, band, NEG_INF)
        return carry

    lax.fori_loop(0, ATT_HEADS, one_head, 0)


def _bias_tables(rel_bias):
    bm, valid = _bucket_maps()
    GH = len(DILATED_GROUPS) * ATT_HEADS
    blk = (1, 2 * ATT_BLOCK, ATT_BLOCK)
    return pl.pallas_call(
        _bias_kernel,
        grid=(len(DILATED_GROUPS),),
        in_specs=[
            pl.BlockSpec(memory_space=pltpu.SMEM),
            pl.BlockSpec(blk, lambda i: (i, 0, 0)),
            pl.BlockSpec(blk, lambda i: (i, 0, 0)),
        ],
        out_specs=pl.BlockSpec((ATT_HEADS, 2, 2 * ATT_BLOCK, ATT_BLOCK), lambda i: (i, 0, 0, 0)),
        out_shape=jax.ShapeDtypeStruct((GH, 2, 2 * ATT_BLOCK, ATT_BLOCK), F32),
        compiler_params=_params(("arbitrary",)),
        name="attn_bias",
    )(rel_bias, jnp.asarray(bm), jnp.asarray(valid))


def _cast_kernel(w_ref, o_ref):
    o_ref[...] = w_ref[...].astype(BF16)


def _to_bf16(w, layer, col_block):
    _, rows, cols = w.shape
    return pl.pallas_call(
        _cast_kernel,
        grid=(cols // col_block,),
        in_specs=[pl.BlockSpec((None, rows, col_block), lambda j: (layer, 0, j))],
        out_specs=pl.BlockSpec((rows, col_block), lambda j: (0, j)),
        out_shape=jax.ShapeDtypeStruct((rows, cols), BF16),
        compiler_params=_params(("parallel",)),
        name="cast_bf16",
    )(w)


def _qkv_kernel(x_ref, g_ref, wq_ref, wk_ref, wv_ref, qg_ref, kg_ref, qt_ref, k_ref, vt_ref,
                slab_ref, bd_ref, *, dil, R, nl):
    rc = pl.program_id(2)
    D = x_ref.shape[2]
    lanes = slab_ref.shape[2]
    first = (pl.program_id(0) == 0) & (pl.program_id(1) == 0) & (rc == 0)

    @pl.when(first)
    def _():
        r = lax.broadcasted_iota(I32, bd_ref.shape, 0)
        c = lax.broadcasted_iota(I32, bd_ref.shape, 1)
        hd_shift = HEAD_DIM.bit_length() - 1
        same_head = lax.shift_right_logical(r, hd_shift) == lax.shift_right_logical(c, hd_shift)
        bd_ref[...] = jnp.where(same_head, 1.0, 0.0).astype(BF16)

    if dil == 1:
        x = x_ref[0]
    else:
        @pl.when(rc == 0)
        def _():
            for c in range(D // lanes):
                slab_ref[c] = x_ref[0, :, c * lanes:(c + 1) * lanes]

        pieces = []
        for j in range(R):
            r = rc * R + j
            cols = [slab_ref[c, pl.ds(r, nl, stride=dil), :] for c in range(D // lanes)]
            pieces.append(jnp.concatenate(cols, axis=1))
        x = jnp.concatenate(pieces, axis=0)
    h = _rms(x, g_ref[...]).astype(BF16)
    nb = bd_ref.shape[0]

    def head_norm(y, gain):
        parts = []
        for c0 in range(0, y.shape[1], nb):
            yc = y[:, c0:c0 + nb]
            ssq = jnp.dot((yc * yc).astype(BF16), bd_ref[...], preferred_element_type=F32)
            parts.append(yc * lax.rsqrt(ssq * (1.0 / HEAD_DIM) + EPS))
        return jnp.concatenate(parts, axis=1) * gain

    q = head_norm(jnp.dot(h, wq_ref[...], preferred_element_type=F32), qg_ref[...])
    q = q * (HEAD_DIM ** -0.5)
    k = head_norm(jnp.dot(h, wk_ref[...], preferred_element_type=F32), kg_ref[...])
    v = jnp.dot(h, wv_ref[...], preferred_element_type=F32)
    blk = ATT_BLOCK
    for j in range(R):
        k_ref[0, j] = k[j * nl:(j + 1) * nl].astype(BF16)
        for c in range(nl // blk):
            rows = slice(j * nl + c * blk, j * nl + (c + 1) * blk)
            for hp in range(q.shape[1] // blk):
                cs = slice(hp * blk, (hp + 1) * blk)
                qt_ref[0, j, cs, c * blk:(c + 1) * blk] = q[rows, cs].T.astype(BF16)
                vt_ref[0, j, cs, c * blk:(c + 1) * blk] = v[rows, cs].T.astype(BF16)


def _qkv_group(x, g, w_bf, q_gain, k_gain, gi, dil):
    B, S, D = x.shape
    HD = ATT_HEADS * HEAD_DIM
    G = len(DILATED_GROUPS)
    L = S // dil
    R = min(dil, QKV_ROWS // ATT_BLOCK)
    nl = QKV_ROWS // R
    lanes = 128
    wspec = lambda s: pl.BlockSpec((D, HD), lambda b, l, r: (0, s * G + gi))
    tspec = pl.BlockSpec((1, R, HD, nl), lambda b, l, r: (b, r, 0, l))
    tshape = jax.ShapeDtypeStruct((B, dil, HD, L), BF16)
    return pl.pallas_call(
        functools.partial(_qkv_kernel, dil=dil, R=R, nl=nl),
        grid=(B, L // nl, dil // R),
        in_specs=[
            pl.BlockSpec((1, nl * dil, D), lambda b, l, r: (b, l, 0)),
            pl.BlockSpec((1, D), lambda b, l, r: (0, 0)),
            wspec(0), wspec(1), wspec(2),
            pl.BlockSpec((1, HD), lambda b, l, r: (0, 0)),
            pl.BlockSpec((1, HD), lambda b, l, r: (0, 0)),
        ],
        out_specs=[tspec, pl.BlockSpec((1, R, nl, HD), lambda b, l, r: (b, r, l, 0)), tspec],
        out_shape=[tshape, jax.ShapeDtypeStruct((B, dil, L, HD), BF16), tshape],
        scratch_shapes=[pltpu.VMEM((D // lanes, nl * dil if dil > 1 else 8, lanes), F32),
                        pltpu.VMEM((256, 256), BF16)],
        compiler_params=_params(("arbitrary", "arbitrary", "arbitrary")),
        name=f"attn_qkv_g{gi}",
    )(x, g.reshape(1, D), w_bf, w_bf, w_bf,
      jnp.tile(q_gain, ATT_HEADS).reshape(1, HD), jnp.tile(k_gain, ATT_HEADS).reshape(1, HD))


def _attn_kernel(*refs, single, nitems):
    if single:
        qt_ref, kc_ref, vtc_ref, bias_ref, o_ref, l_ref, s_ref, p_ref = refs
    else:
        qt_ref, kc_ref, kp_ref, vtc_ref, vtp_ref, bias_ref, o_ref, l_ref, s_ref, p_ref = refs
    blk = ATT_BLOCK
    pair = 2 * HEAD_DIM
    first = jnp.where(pl.program_id(2) == 0, 1, 0)
    if single:
        k0 = blk
        variants = [1] * nitems
        q_of = lambda it, cs: qt_ref[0, it, cs, :]
        k_of = lambda it, cs: kc_ref[0, it, :, cs]
        vt_of = lambda it, cs: vtc_ref[0, it, cs, :]
    else:
        k0 = 0
        variants = [first] + [0] * (nitems - 1)

        def q_of(it, cs):
            return qt_ref[0, 0, cs, it * blk:(it + 1) * blk]

        def k_of(it, cs):
            if it == 0:
                return jnp.concatenate([kp_ref[0, 0, :, cs], kc_ref[0, 0, :blk, cs]], axis=0)
            return kc_ref[0, 0, (it - 1) * blk:(it + 1) * blk, cs]

        def vt_of(it, cs):
            if it == 0:
                return jnp.concatenate([vtp_ref[0, 0, cs, :], vtc_ref[0, 0, cs, :blk]], axis=1)
            return vtc_ref[0, 0, cs, (it - 1) * blk:(it + 1) * blk]

    feat = lax.broadcasted_iota(I32, (pair, blk), 0)
    head_rows = [jnp.where(feat < HEAD_DIM, 1.0, 0.0).astype(BF16),
                 jnp.where(feat < HEAD_DIM, 0.0, 1.0).astype(BF16)]
    invs = {}
    lse_rows = {}

    def scores(it, pr):
        cs = slice(pr * pair, (pr + 1) * pair)
        qt2 = q_of(it, cs)
        k2 = k_of(it, cs)
        for half in range(2):
            qth = qt2 * head_rows[half]
            s_ref[it * ATT_HEADS + 2 * pr + half] = jnp.dot(k2, qth, preferred_element_type=F32)

    def softmax(it, pr):
        for head in (2 * pr, 2 * pr + 1):
            s = s_ref[it * ATT_HEADS + head] + bias_ref[head, variants[it], k0:, :]
            m = jnp.max(s, axis=0, keepdims=True)
            p = jnp.exp(s - m)
            den = jnp.sum(p, axis=0, keepdims=True)
            p_ref[it * ATT_HEADS + head] = p.astype(BF16)
            invs[it, head] = 1.0 / den
            lse_rows[it, head] = m + jnp.log(den)

    def outputs(it, pr):
        cs = slice(pr * pair, (pr + 1) * pair)
        vt2 = vt_of(it, cs)
        halves = []
        for half in range(2):
            head = 2 * pr + half
            vth = vt2[half * HEAD_DIM:(half + 1) * HEAD_DIM]
            ot = jnp.dot(vth, p_ref[it * ATT_HEADS + head], preferred_element_type=F32)
            halves.append(ot * invs[it, head])
        val = jnp.concatenate(halves, axis=0).T
        if single:
            o_ref[0, it, :, cs] = val
        else:
            o_ref[0, 0, it * blk:(it + 1) * blk, cs] = val

    work = [(it, pr) for it in range(nitems) for pr in range(ATT_HEADS // 2)]
    for it, pr in work:
        scores(it, pr)
    for it, pr in work:
        softmax(it, pr)
    for it, pr in work:
        outputs(it, pr)
    lse_row = lax.broadcasted_iota(I32, (blk, blk), 0)
    for it in range(nitems):
        lse_t = jnp.zeros((blk, blk), F32)
        for head in range(ATT_HEADS):
            lse_t = jnp.where(lse_row == head, lse_rows[it, head], lse_t)
        if single:
            l_ref[0, it] = lse_t.T
        else:
            l_ref[0, 0, it * blk:(it + 1) * blk, :] = lse_t.T


def _attn_group(qt, k, vt, bias, gi):
    B, dil, L, HD = k.shape
    blk = ATT_BLOCK
    nb = L // blk
    single = nb == 1
    ni = min(ATT_ITEMS, dil if single else nb)
    bias_spec = pl.BlockSpec((ATT_HEADS, 2, 2 * blk, blk), lambda b, r, n: (gi, 0, 0, 0))
    if single:
        rows = pl.BlockSpec((1, ni, blk, HD), lambda b, r, n: (b, r, 0, 0))
        cols = pl.BlockSpec((1, ni, HD, blk), lambda b, r, n: (b, r, 0, 0))
        grid = (B, dil // ni, 1)
        in_specs = [cols, rows, cols, bias_spec]
        args = (qt, k, vt, bias)
        out_specs = [rows, pl.BlockSpec((1, ni, blk, blk), lambda b, r, n: (b, r, 0, 0))]
        nk = blk
    else:
        rows = pl.BlockSpec((1, 1, ni * blk, HD), lambda b, r, n: (b, r, n, 0))
        cols = pl.BlockSpec((1, 1, HD, ni * blk), lambda b, r, n: (b, r, 0, n))
        prow = pl.BlockSpec((1, 1, blk, HD), lambda b, r, n: (b, r, jnp.maximum(ni * n - 1, 0), 0))
        pcol = pl.BlockSpec((1, 1, HD, blk), lambda b, r, n: (b, r, 0, jnp.maximum(ni * n - 1, 0)))
        grid = (B, dil, nb // ni)
        in_specs = [cols, rows, prow, cols, pcol, bias_spec]
        args = (qt, k, k, vt, vt, bias)
        out_specs = [rows, pl.BlockSpec((1, 1, ni * blk, blk), lambda b, r, n: (b, r, n, 0))]
        nk = 2 * blk
    return pl.pallas_call(
        functools.partial(_attn_kernel, single=single, nitems=ni),
        grid=grid,
        in_specs=in_specs,
        out_specs=out_specs,
        out_shape=[jax.ShapeDtypeStruct((B, dil, L, HD), F32),
                   jax.ShapeDtypeStruct((B, dil, L, blk), F32)],
        scratch_shapes=[pltpu.VMEM((ni * ATT_HEADS, nk, blk), F32),
                        pltpu.VMEM((ni * ATT_HEADS, nk, blk), BF16)],
        compiler_params=_params(("parallel", "parallel", "arbitrary")),
        name=f"attn_core_g{gi}",
    )(*args)


def _attn_out_kernel(x_ref, o0, o1, o2, l0, l1, l2, wo_ref, out_ref, oslab, lslab, ex_ref, *, T):
    lanes = oslab.shape[3]
    HD = o0.shape[3]

    @pl.when((pl.program_id(0) == 0) & (pl.program_id(1) == 0))
    def _():
        h = lax.broadcasted_iota(I32, ex_ref.shape, 0) & (ATT_BLOCK - 1)
        c = lax.broadcasted_iota(I32, ex_ref.shape, 1)
        owner = lax.shift_right_logical(c, HEAD_DIM.bit_length() - 1)
        ex_ref[...] = jnp.where(h == owner, 1.0, 0.0).astype(BF16)

    def token_order(o_ref, l_ref, dil, s):
        if dil == 1:
            return o_ref[0, 0], l_ref[0, 0]
        n = T // dil
        for r in range(dil):
            blk = o_ref[0, r]
            for c in range(HD // lanes):
                oslab[s, c, pl.ds(r, n, stride=dil), :] = blk[:, c * lanes:(c + 1) * lanes]
            lslab[s, pl.ds(r, n, stride=dil), :] = l_ref[0, r]
        o = jnp.concatenate([oslab[s, c] for c in range(HD // lanes)], axis=1)
        return o, lslab[s]

    groups = [token_order(o_ref, l_ref, dil, s)
              for s, ((_, dil), o_ref, l_ref) in enumerate(zip(DILATED_GROUPS, (o0, o1, o2), (l0, l1, l2)))]
    m = jnp.maximum(jnp.maximum(groups[0][1], groups[1][1]), groups[2][1])
    es = [jnp.exp(l - m) for _, l in groups]
    inv = 1.0 / (es[0] + es[1] + es[2])
    att = jnp.zeros((T, HD), F32)
    for (o, _), e in zip(groups, es):
        a = e * inv
        a_hi = a.astype(BF16)
        a_lo = (a - a_hi.astype(F32)).astype(BF16)
        wide = jnp.dot(jnp.concatenate([a_hi, a_lo], axis=1), ex_ref[...], preferred_element_type=F32)
        att = att + wide * o
    out_ref[0] = x_ref[0] + jnp.dot(att.astype(BF16), wo_ref[...], preferred_element_type=F32)


def _attn_out(x, os, ls, wo_bf):
    B, S, D = x.shape
    HD = wo_bf.shape[0]
    T = OUT_TILE
    lanes = 128
    ospec = lambda dil, w: pl.BlockSpec((1, dil, T // dil, w), lambda b, i: (b, 0, i, 0))
    dils = [dil for _, dil in DILATED_GROUPS]
    xspec = pl.BlockSpec((1, T, D), lambda b, i: (b, i, 0))
    return pl.pallas_call(
        functools.partial(_attn_out_kernel, T=T),
        grid=(B, S // T),
        in_specs=[xspec] + [ospec(d, HD) for d in dils] + [ospec(d, ATT_BLOCK) for d in dils]
                 + [pl.BlockSpec((HD, D), lambda b, i: (0, 0))],
        out_specs=xspec,
        out_shape=jax.ShapeDtypeStruct((B, S, D), F32),
        scratch_shapes=[pltpu.VMEM((len(dils), HD // lanes, T, lanes), F32),
                        pltpu.VMEM((len(dils), T, ATT_BLOCK), F32),
                        pltpu.VMEM((2 * ATT_BLOCK, HD), BF16)],
        compiler_params=_params(("arbitrary", "arbitrary")),
        name="attn_out",
    )(x, *os, *ls, wo_bf)


def _attention_layer(x, g, w_qkv, q_gain, k_gain, w_o, rel_bias, layer):
    HD = ATT_HEADS * HEAD_DIM
    bias = _bias_tables(rel_bias)
    wqkv_bf = _to_bf16(w_qkv, layer, HD)
    wo_bf = _to_bf16(w_o, layer, w_o.shape[2])
    os, ls = [], []
    for gi, (_, dil) in enumerate(DILATED_GROUPS):
        qt, k, vt = _qkv_group(x, g, wqkv_bf, q_gain, k_gain, gi, dil)
        o, l = _attn_group(qt, k, vt, bias, gi)
        os.append(o)
        ls.append(l)
    return _attn_out(x, os, ls, wo_bf)


def kernel(x, norm_mix_g, norm_ffn_g, pool_w, pool_scale, attn_w_qkv, attn_q_gain, attn_k_gain,
           attn_w_o, rel_bias, moe_w_router, moe_b_router, moe_w_gate_up, moe_b_gate_up,
           moe_w_down, moe_b_down):
    B, S, D = x.shape
    depth = norm_mix_g.shape[0]
    for i in range(depth):
        j = i // 2
        if i % 2 == 0:
            x = _pool_layer(x, norm_mix_g[i], pool_w[j], pool_scale[j])
        else:
            x = _attention_layer(x, norm_mix_g[i], attn_w_qkv, attn_q_gain[j], attn_k_gain[j],
                                 attn_w_o, rel_bias, j)
        x = _moe_layer(x.reshape(B * S, D), norm_ffn_g[i], moe_w_router[i], moe_b_router[i],
                       moe_w_gate_up, moe_b_gate_up[i], moe_w_down, moe_b_down[i],
                       i).reshape(B, S, D)
    return x
```

```python
import functools
import math

import numpy as np
import jax
import jax.numpy as jnp
from jax import lax
from jax.experimental import pallas as pl
from jax.experimental.pallas import tpu as pltpu

F32 = jnp.float32
BF16 = jnp.bfloat16
I32 = jnp.int32

EPS = 1e-6
POOL_WINDOWS = (2, 4, 8, 16)
POOL_HALO = max(POOL_WINDOWS)
DILATED_GROUPS = ((128, 1), (512, 4), (2048, 16))
ATT_HEADS = 16
HEAD_DIM = 64
ATT_BLOCK = 128
ATT_ITEMS = 8
N_BUCKETS = 32
REL_MAX_DIST = 2048
NEG_INF = -1e30
N_EXPERTS = 32
TOP_K = 4
SWIGLU_LIMIT = 7.0
SWIGLU_ALPHA = 1.702

V7X_VMEM_LIMIT_BYTES = 56 * 1024 * 1024
V7X_MXU_DIM = 256
ISSUE_UNROLL = 64
MOE_TILE = 512
DISPATCH_TILE = 512
COMBINE_TILE = 256
ROUTER_TILE = 1024
POOL_TILE = 1024
QKV_ROWS = 1024
LANES = 128
OUT_TILE = 512


def _rms(xf, g):
    ms = jnp.mean(xf * xf, axis=-1, keepdims=True)
    return xf * lax.rsqrt(ms + EPS) * g


def _params(sem):
    return pltpu.CompilerParams(dimension_semantics=sem, vmem_limit_bytes=V7X_VMEM_LIMIT_BYTES)


def _pool_kernel(x_ref, halo_ref, g_ref, w_ref, sc_ref, o_ref, *, ts, dg):
    i = pl.program_id(1)
    x = x_ref[0]
    g = g_ref[...]
    h = _rms(x, g)
    hh = _rms(halo_ref[0], g)
    hh = jnp.where(i == 0, 0.0, hh)
    full = jnp.concatenate([hh, h], axis=0)
    pos = i * ts + lax.broadcasted_iota(I32, (ts, 1), 0)
    outs = []
    for gi, w in enumerate(POOL_WINDOWS):
        s = full[:, gi * dg:(gi + 1) * dg]
        sh = 1
        while sh < w:
            s = s + pltpu.roll(s, sh, 0)
            sh *= 2
        s = s[POOL_HALO:]
        cnt = jnp.minimum(pos + 1, w).astype(F32)
        p = s / cnt - h[:, gi * dg:(gi + 1) * dg]
        outs.append(jnp.dot(p.astype(BF16), w_ref[gi].astype(BF16),
                            preferred_element_type=F32))
    y = jnp.concatenate(outs, axis=1) * sc_ref[...]
    o_ref[0] = x + y


def _pool_layer(x, g, w_groups, scale):
    B, S, D = x.shape
    ts = POOL_TILE
    dg = D // len(POOL_WINDOWS)
    hb = ts // POOL_HALO
    return pl.pallas_call(
        functools.partial(_pool_kernel, ts=ts, dg=dg),
        grid=(B, S // ts),
        in_specs=[
            pl.BlockSpec((1, ts, D), lambda b, i: (b, i, 0)),
            pl.BlockSpec((1, POOL_HALO, D), lambda b, i: (b, jnp.maximum(i * hb - 1, 0), 0)),
            pl.BlockSpec((1, D), lambda b, i: (0, 0)),
            pl.BlockSpec((len(POOL_WINDOWS), dg, dg), lambda b, i: (0, 0, 0)),
            pl.BlockSpec((1, D), lambda b, i: (0, 0)),
        ],
        out_specs=pl.BlockSpec((1, ts, D), lambda b, i: (b, i, 0)),
        out_shape=jax.ShapeDtypeStruct((B, S, D), F32),
        compiler_params=_params(("parallel", "parallel")),
        name="pool_mixer",
    )(x, x, g.reshape(1, D), w_groups, scale.reshape(1, D))


def _router_kernel(x_ref, g_ref, wr_ref, br_ref, e_ref, gate_ref, rank_ref, cnt_ref,
                   tri_ref, carry_ref, wcat_ref, *, tt):
    i = pl.program_id(0)
    E = N_EXPERTS

    @pl.when(i == 0)
    def _():
        r = lax.broadcasted_iota(I32, tri_ref.shape, 0)
        c = lax.broadcasted_iota(I32, tri_ref.shape, 1)
        tri_ref[...] = jnp.where(r < c, 1.0, 0.0).astype(BF16)
        carry_ref[...] = jnp.zeros_like(carry_ref)
        w = wr_ref[...]
        w_hi = w.astype(BF16)
        wcat_ref[:, :LANES] = w_hi
        wcat_ref[:, LANES:] = (w - w_hi.astype(F32)).astype(BF16)

    h = _rms(x_ref[...], g_ref[...])
    h_hi = h.astype(BF16)
    h_lo = (h - h_hi.astype(F32)).astype(BF16)
    both = jnp.dot(h_hi, wcat_ref[...], preferred_element_type=F32)
    cross = jnp.dot(h_lo, wcat_ref[:, :LANES], preferred_element_type=F32)
    logits = both[:, :LANES] + (both[:, LANES:] + cross) + br_ref[...]
    nblk = tt // LANES
    l = jnp.concatenate([logits[c * LANES:(c + 1) * LANES].T for c in range(nblk)], axis=1)[:E]
    row = lax.broadcasted_iota(I32, (E, tt), 0).astype(F32)
    vals, sels, idxs = [], [], []
    for k in range(TOP_K):
        m = jnp.max(l, axis=0, keepdims=True)
        idx = jnp.min(jnp.where(l == m, row, float(E)), axis=0, keepdims=True)
        sel = row == idx
        vals.append(m)
        sels.append(sel)
        idxs.append(idx)
        l = jnp.where(sel, -jnp.inf, l)
    ex = [jnp.exp(v - vals[0]) for v in vals]
    den = ex[0] + ex[1] + ex[2] + ex[3]
    multi = jnp.zeros((E, tt), F32)
    for sel in sels:
        multi = multi + jnp.where(sel, 1.0, 0.0)
    base = carry_ref[:, :1]
    parts = []
    for c in range(nblk):
        mc = multi[:, c * LANES:(c + 1) * LANES]
        parts.append(jnp.dot(mc.astype(BF16), tri_ref[...], preferred_element_type=F32) + base)
        base = base + jnp.sum(mc, axis=1, keepdims=True)
    before = jnp.concatenate(parts, axis=1)
    kk = lax.broadcasted_iota(I32, (TOP_K, tt), 0)
    e_out = jnp.zeros((TOP_K, tt), F32)
    g_out = jnp.zeros((TOP_K, tt), F32)
    r_out = jnp.zeros((TOP_K, tt), F32)
    for k in range(TOP_K):
        e_out = jnp.where(kk == k, idxs[k], e_out)
        g_out = jnp.where(kk == k, ex[k] / den, g_out)
        rk = jnp.sum(jnp.where(sels[k], before, 0.0), axis=0, keepdims=True)
        r_out = jnp.where(kk == k, rk, r_out)
    e_ref[...] = e_out.astype(I32)
    gate_ref[...] = g_out
    rank_ref[...] = r_out.astype(I32)
    carry_ref[...] = jnp.broadcast_to(base, carry_ref.shape)
    cnt_ref[...] = carry_ref[...]


def _router(x, g, w_r, b_r):
    N, D = x.shape
    tt = ROUTER_TILE
    E = N_EXPERTS
    w_pad = jnp.pad(w_r, ((0, 0), (0, LANES - E)))
    b_pad = jnp.pad(b_r, (0, LANES - E)).reshape(1, LANES)
    kspec = pl.BlockSpec((TOP_K, tt), lambda i: (0, i))
    return pl.pallas_call(
        functools.partial(_router_kernel, tt=tt),
        grid=(N // tt,),
        in_specs=[
            pl.BlockSpec((tt, D), lambda i: (i, 0)),
            pl.BlockSpec((1, D), lambda i: (0, 0)),
            pl.BlockSpec((D, LANES), lambda i: (0, 0)),
            pl.BlockSpec((1, LANES), lambda i: (0, 0)),
        ],
        out_specs=[kspec, kspec, kspec, pl.BlockSpec((E, LANES), lambda i: (0, 0))],
        out_shape=[
            jax.ShapeDtypeStruct((TOP_K, N), I32),
            jax.ShapeDtypeStruct((TOP_K, N), F32),
            jax.ShapeDtypeStruct((TOP_K, N), I32),
            jax.ShapeDtypeStruct((E, LANES), F32),
        ],
        scratch_shapes=[pltpu.VMEM((LANES, LANES), BF16), pltpu.VMEM((E, LANES), F32),
                        pltpu.VMEM((D, 2 * LANES), BF16)],
        compiler_params=_params(("arbitrary",)),
        name="moe_router",
    )(x, g.reshape(1, D), w_pad, b_pad)


def _to_tile_rows(ref, val):
    n, D = val.shape
    nsub = D // LANES
    for s in range(nsub):
        ref[pl.ds(s, n, stride=nsub), :] = val[:, s * LANES:(s + 1) * LANES]


def _from_tile_rows(ref, n, nsub):
    return jnp.concatenate([ref[pl.ds(s, n, stride=nsub), :] for s in range(nsub)], axis=1)


def _tile_row(ref, r, nsub, count=1):
    return ref.at[pl.ds(pl.multiple_of(r * nsub, nsub), count * nsub)]


def _row_copy_wait(src, dst, sem, times):
    for _ in range(times):
        pltpu.make_async_copy(src, dst.at[pl.ds(0, src.shape[0])], sem).wait()


def _dispatch_kernel(pend_ref, nu_ref, *refs, tt, tm, nt, nsub):
    dest_refs = refs[:TOP_K]
    x_ref, g_ref, xpad_ref, hbuf, zbuf, sems, zsems = refs[TOP_K:]
    i = pl.program_id(0)
    slot = i % 2
    last = pl.num_programs(0) - 1

    def seg_copy(e):
        start = pl.multiple_of(jnp.maximum(pend_ref[e] - tm, 0), tm)
        return pltpu.make_async_copy(zbuf, _tile_row(xpad_ref, start, nsub, tm), zsems.at[0])

    def tail_copy(j):
        return pltpu.make_async_copy(zbuf, _tile_row(xpad_ref, (nt - 1 - j) * tm, nsub, tm), zsems.at[1])

    @pl.when(i == 0)
    def _():
        zbuf[...] = jnp.zeros_like(zbuf)
        for e in range(N_EXPERTS):
            seg_copy(e).start()
        for j in range(N_EXPERTS):
            pl.when(nt - 1 - j >= nu_ref[0])(tail_copy(j).start)

    hb = hbuf.at[slot]
    _to_tile_rows(hb, _rms(x_ref[...], g_ref[...]))

    @pl.when(i == 0)
    def _():
        for e in range(N_EXPERTS):
            seg_copy(e).wait()

    @pl.when(i == last)
    def _():
        for j in range(N_EXPERTS):
            pl.when(nt - 1 - j >= nu_ref[0])(tail_copy(j).wait)

    def issue(t, carry):
        for k in range(TOP_K):
            d = dest_refs[k][t]
            pltpu.make_async_copy(_tile_row(hb, t, nsub), _tile_row(xpad_ref, d, nsub),
                                  sems.at[slot]).start(priority=k % 2)
        return carry

    lax.fori_loop(0, tt, issue, 0, unroll=ISSUE_UNROLL)

    @pl.when(i > 0)
    def _():
        _row_copy_wait(hbuf.at[1 - slot], xpad_ref, sems.at[1 - slot], TOP_K)

    @pl.when(i == pl.num_programs(0) - 1)
    def _():
        _row_copy_wait(hb, xpad_ref, sems.at[slot], TOP_K)


def _dispatch(x, g, dest, pend, n_used, P):
    N, D = x.shape
    tt = DISPATCH_TILE
    tm = MOE_TILE
    nsub = D // LANES
    grid_spec = pltpu.PrefetchScalarGridSpec(
        num_scalar_prefetch=2,
        grid=(N // tt,),
        in_specs=[pl.BlockSpec((tt,), lambda i, pe, nu, k=k: (k * (N // tt) + i,), memory_space=pltpu.SMEM)
                  for k in range(TOP_K)] + [
            pl.BlockSpec((tt, D), lambda i, pe, nu: (i, 0)),
            pl.BlockSpec((1, D), lambda i, pe, nu: (0, 0)),
        ],
        out_specs=pl.BlockSpec(memory_space=pl.ANY),
        scratch_shapes=[pltpu.VMEM((2, tt * nsub, LANES), F32), pltpu.VMEM((tm * nsub, LANES), F32),
                        pltpu.SemaphoreType.DMA((2,)), pltpu.SemaphoreType.DMA((2,))],
    )
    return pl.pallas_call(
        functools.partial(_dispatch_kernel, tt=tt, tm=tm, nt=P // tm, nsub=nsub),
        grid_spec=grid_spec,
        out_shape=jax.ShapeDtypeStruct((P * nsub, LANES), F32),
        compiler_params=_params(("arbitrary",)),
        name="moe_dispatch",
    )(pend, n_used, *([dest] * TOP_K), x, g.reshape(1, D))


def _expert_kernel(te_ref, nx_ref, nu_ref, x_ref, wgu_hbm, bgu_ref, wdn_hbm, bdn_ref, o_ref,
                   wgu_f32, wdn_f32, wgu_bf, wdn_bf, sems, *, layer, F, tm, nsub):
    i = pl.program_id(0)

    def weight_copies(e):
        return (pltpu.make_async_copy(wgu_hbm.at[layer, e], wgu_f32, sems.at[0]),
                pltpu.make_async_copy(wdn_hbm.at[layer, e], wdn_f32, sems.at[1]))

    @pl.when(i < nu_ref[0])
    def _():
        e = te_ref[i]

        @pl.when(i == 0)
        def _():
            for c in weight_copies(e):
                c.start()

        @pl.when((i == 0) | (e != te_ref[jnp.maximum(i - 1, 0)]))
        def _():
            for c in weight_copies(e):
                c.wait()
            wgu_bf[...] = wgu_f32[...].astype(BF16)
            wdn_bf[...] = wdn_f32[...].astype(BF16)
            nxt = nx_ref[i]

            @pl.when(nxt != e)
            def _():
                for c in weight_copies(nxt):
                    c.start()

        x = _from_tile_rows(x_ref, tm, nsub).astype(BF16)
        gu = jnp.dot(x, wgu_bf[...], preferred_element_type=F32) + bgu_ref[0]
        gate = jnp.minimum(gu[:, :F], SWIGLU_LIMIT)
        up = jnp.clip(gu[:, F:], -SWIGLU_LIMIT, SWIGLU_LIMIT)
        glu = gate * jax.nn.sigmoid(SWIGLU_ALPHA * gate)
        a = ((up + 1.0) * glu).astype(BF16)
        _to_tile_rows(o_ref, jnp.dot(a, wdn_bf[...], preferred_element_type=F32) + bdn_ref[0])

    @pl.when(i >= nu_ref[0])
    def _():
        o_ref[...] = jnp.zeros_like(o_ref)


def _experts(tile_e, next_e, n_used, x_pad, w_gu, b_gu, w_dn, b_dn, layer):
    _, E, D, F2 = w_gu.shape
    F = F2 // 2
    tm = MOE_TILE
    nsub = D // LANES
    P = x_pad.shape[0] // nsub
    row = lambda i, te, nx, nu: (jnp.minimum(i, nu[0] - 1), 0)
    exp3 = lambda i, te, nx, nu: (te[i], 0, 0)
    grid_spec = pltpu.PrefetchScalarGridSpec(
        num_scalar_prefetch=3,
        grid=(P // tm,),
        in_specs=[
            pl.BlockSpec((tm * nsub, LANES), row),
            pl.BlockSpec(memory_space=pl.ANY),
            pl.BlockSpec((1, 1, F2), exp3),
            pl.BlockSpec(memory_space=pl.ANY),
            pl.BlockSpec((1, 1, D), exp3),
        ],
        out_specs=pl.BlockSpec((tm * nsub, LANES), lambda i, te, nx, nu: (i, 0)),
        scratch_shapes=[pltpu.VMEM((D, F2), F32), pltpu.VMEM((F, D), F32),
                        pltpu.VMEM((D, F2), BF16), pltpu.VMEM((F, D), BF16),
                        pltpu.SemaphoreType.DMA((2,))],
    )
    return pl.pallas_call(
        functools.partial(_expert_kernel, layer=layer, F=F, tm=tm, nsub=nsub),
        grid_spec=grid_spec,
        out_shape=jax.ShapeDtypeStruct((P * nsub, LANES), F32),
        compiler_params=_params(("arbitrary",)),
        name="moe_experts",
    )(tile_e, next_e, n_used, x_pad, w_gu, b_gu.reshape(E, 1, F2), w_dn, b_dn.reshape(E, 1, D))


def _combine_kernel(*refs, tt, nsub):
    dest_refs = refs[:TOP_K]
    dnext_refs = refs[TOP_K:2 * TOP_K]
    x_ref, gate_ref, ypad_ref, o_ref, ybuf, sems = refs[2 * TOP_K:]
    i = pl.program_id(0)
    slot = i % 2

    def gather(idx_refs, s):
        def issue(t, carry):
            for k in range(TOP_K):
                d = idx_refs[k][t]
                pltpu.make_async_copy(_tile_row(ypad_ref, d, nsub), _tile_row(ybuf.at[s, k], t, nsub),
                                      sems.at[s]).start(priority=k % 2)
            return carry

        lax.fori_loop(0, tt, issue, 0, unroll=ISSUE_UNROLL)

    @pl.when(i == 0)
    def _():
        gather(dest_refs, slot)

    @pl.when(i + 1 < pl.num_programs(0))
    def _():
        gather(dnext_refs, 1 - slot)

    for k in range(TOP_K):
        _row_copy_wait(ybuf.at[slot, k], ypad_ref, sems.at[slot], 1)
    gate = gate_ref[...]
    x = x_ref[...]
    cols = []
    for c in range(nsub):
        acc = x[:, c * LANES:(c + 1) * LANES]
        for k in range(TOP_K):
            acc = acc + ybuf[slot, k, pl.ds(c, tt, stride=nsub), :] * gate[:, k:k + 1]
        cols.append(acc)
    o_ref[...] = jnp.concatenate(cols, axis=1)


def _combine(x, gate, dest, y_pad):
    N, D = x.shape
    tt = COMBINE_TILE
    nsub = D // LANES
    steps = N // tt
    last = steps - 1
    return pl.pallas_call(
        functools.partial(_combine_kernel, tt=tt, nsub=nsub),
        grid=(N // tt,),
        in_specs=[pl.BlockSpec((tt,), lambda i, k=k: (k * steps + i,), memory_space=pltpu.SMEM)
                  for k in range(TOP_K)] + [
            pl.BlockSpec((tt,), lambda i, k=k: (k * steps + jnp.minimum(i + 1, last),),
                         memory_space=pltpu.SMEM) for k in range(TOP_K)] + [
            pl.BlockSpec((tt, D), lambda i: (i, 0)),
            pl.BlockSpec((tt, TOP_K), lambda i: (i, 0)),
            pl.BlockSpec(memory_space=pl.ANY),
        ],
        out_specs=pl.BlockSpec((tt, D), lambda i: (i, 0)),
        out_shape=jax.ShapeDtypeStruct((N, D), F32),
        scratch_shapes=[pltpu.VMEM((2, TOP_K, tt * nsub, LANES), F32), pltpu.SemaphoreType.DMA((2,))],
        compiler_params=_params(("arbitrary",)),
        name="moe_combine",
    )(*([dest] * (2 * TOP_K)), x, gate, y_pad)


def _dest_kernel(pstart_ref, e_ref, rank_ref, o_ref):
    e = e_ref[...]
    seg = jnp.zeros(e.shape, I32)
    for ex in range(N_EXPERTS):
        seg = jnp.where(e == ex, pstart_ref[ex], seg)
    o_ref[...] = seg + rank_ref[...]


def _dest_rows(e_idx, rank, pstart):
    return pl.pallas_call(
        _dest_kernel,
        in_specs=[pl.BlockSpec(memory_space=pltpu.SMEM), pl.BlockSpec(e_idx.shape, lambda: (0, 0)),
                  pl.BlockSpec(rank.shape, lambda: (0, 0))],
        out_specs=pl.BlockSpec(rank.shape, lambda: (0, 0)),
        out_shape=jax.ShapeDtypeStruct(rank.shape, I32),
        name="moe_dest",
    )(pstart, e_idx, rank)


def _moe_layer(x, g, w_r, b_r, w_gu, b_gu, w_dn, b_dn, layer):
    N, D = x.shape
    E = N_EXPERTS
    tm = MOE_TILE
    e_idx, gate_t, rank, counts = _router(x, g, w_r, b_r)
    counts = counts[:, 0].astype(I32)
    padded = ((counts + tm - 1) // tm) * tm
    pend = jnp.cumsum(padded)
    pstart = pend - padded
    dest = _dest_rows(e_idx, rank, pstart.astype(I32)).reshape(-1)
    gate = gate_t.T
    P = ((N * TOP_K + E * (tm - 1) + tm - 1) // tm) * tm
    nt = P // tm
    tiles = jnp.arange(nt, dtype=I32)
    tile_e = jnp.minimum(jnp.sum(tiles[:, None] * tm >= pend[None, :], axis=1), E - 1).astype(I32)
    n_used = (pend[-1] // tm).astype(I32)
    experts = jnp.arange(E, dtype=I32)
    used = counts > 0
    tile_e = jnp.where(tiles < n_used, tile_e, jnp.max(jnp.where(used, experts, 0)))
    after = jnp.min(jnp.where((experts[None, :] > experts[:, None]) & used[None, :], experts[None, :], E), axis=1)
    after = jnp.where(after < E, after, experts)
    next_e = jnp.sum(jnp.where(tile_e[:, None] == experts[None, :], after[None, :], 0), axis=1)
    x_pad = _dispatch(x, g, dest, pend.astype(I32), n_used.reshape(1), P)
    y_pad = _experts(tile_e, next_e.astype(I32), n_used.reshape(1), x_pad, w_gu, b_gu, w_dn, b_dn, layer)
    return _combine(x, gate, dest, y_pad)


def _bucket_maps():
    qi = np.arange(ATT_BLOCK)[None, :]
    kj = np.arange(2 * ATT_BLOCK)[:, None]
    delta = qi + ATT_BLOCK - kj
    buckets, valids = [], []
    max_exact = N_BUCKETS // 2
    for win, dil in DILATED_GROUPS:
        n = np.maximum(delta * dil, 0)
        nf = np.maximum(n, 1).astype(np.float32)
        large = max_exact + (np.log(nf / np.float32(max_exact)) / np.float32(math.log(REL_MAX_DIST / max_exact))
                             * np.float32(N_BUCKETS - max_exact)).astype(np.int32)
        large = np.minimum(large, N_BUCKETS - 1)
        buckets.append(np.where(n < max_exact, n, large).astype(np.int32))
        valids.append(((delta >= 0) & (delta <= win // dil)).astype(np.int32))
    return np.stack(buckets), np.stack(valids)


def _bias_kernel(tab_ref, bm_ref, valid_ref, o_ref):
    g = pl.program_id(0)
    bm = bm_ref[0]
    key = lax.broadcasted_iota(I32, bm.shape, 0)

    def one_head(h, carry):
        acc = jnp.zeros(bm.shape, F32)
        for b in range(N_BUCKETS):
            acc = jnp.where(bm == b, tab_ref[b, g * ATT_HEADS + h], acc)
        band = jnp.where(valid_ref[0] > 0, acc, NEG_INF)
        o_ref[h, 0] = band
        o_ref[h, 1] = jnp.where(key >= ATT_BLOCK, band, NEG_INF)
        return carry

    lax.fori_loop(0, ATT_HEADS, one_head, 0)


def _bias_tables(rel_bias):
    bm, valid = _bucket_maps()
    GH = len(DILATED_GROUPS) * ATT_HEADS
    blk = (1, 2 * ATT_BLOCK, ATT_BLOCK)
    return pl.pallas_call(
        _bias_kernel,
        grid=(len(DILATED_GROUPS),),
        in_specs=[
            pl.BlockSpec(memory_space=pltpu.SMEM),
            pl.BlockSpec(blk, lambda i: (i, 0, 0)),
            pl.BlockSpec(blk, lambda i: (i, 0, 0)),
        ],
        out_specs=pl.BlockSpec((ATT_HEADS, 2, 2 * ATT_BLOCK, ATT_BLOCK), lambda i: (i, 0, 0, 0)),
        out_shape=jax.ShapeDtypeStruct((GH, 2, 2 * ATT_BLOCK, ATT_BLOCK), F32),
        compiler_params=_params(("arbitrary",)),
        name="attn_bias",
    )(rel_bias, jnp.asarray(bm), jnp.asarray(valid))


def _cast_kernel(w_ref, o_ref):
    o_ref[...] = w_ref[...].astype(BF16)


def _to_bf16(w, layer, col_block):
    _, rows, cols = w.shape
    return pl.pallas_call(
        _cast_kernel,
        grid=(cols // col_block,),
        in_specs=[pl.BlockSpec((None, rows, col_block), lambda j: (layer, 0, j))],
        out_specs=pl.BlockSpec((rows, col_block), lambda j: (0, j)),
        out_shape=jax.ShapeDtypeStruct((rows, cols), BF16),
        compiler_params=_params(("parallel",)),
        name="cast_bf16",
    )(w)


def _qkv_kernel(x_ref, g_ref, wq_ref, wk_ref, wv_ref, qg_ref, kg_ref, qt_ref, k_ref, vt_ref,
                slab_ref, bd_ref, *, dil, R, nl):
    rc = pl.program_id(2)
    D = x_ref.shape[2]
    lanes = slab_ref.shape[2]
    first = (pl.program_id(0) == 0) & (pl.program_id(1) == 0) & (rc == 0)

    @pl.when(first)
    def _():
        r = lax.broadcasted_iota(I32, bd_ref.shape, 0)
        c = lax.broadcasted_iota(I32, bd_ref.shape, 1)
        hd_shift = HEAD_DIM.bit_length() - 1
        same_head = lax.shift_right_logical(r, hd_shift) == lax.shift_right_logical(c, hd_shift)
        bd_ref[...] = jnp.where(same_head, 1.0, 0.0).astype(BF16)

    if dil == 1:
        x = x_ref[0]
    else:
        @pl.when(rc == 0)
        def _():
            for c in range(D // lanes):
                slab_ref[c] = x_ref[0, :, c * lanes:(c + 1) * lanes]

        pieces = []
        for j in range(R):
            r = rc * R + j
            cols = [slab_ref[c, pl.ds(r, nl, stride=dil), :] for c in range(D // lanes)]
            pieces.append(jnp.concatenate(cols, axis=1))
        x = jnp.concatenate(pieces, axis=0)
    h = _rms(x, g_ref[...]).astype(BF16)
    nb = bd_ref.shape[0]

    def head_norm(y, gain):
        parts = []
        for c0 in range(0, y.shape[1], nb):
            yc = y[:, c0:c0 + nb]
            ssq = jnp.dot((yc * yc).astype(BF16), bd_ref[...], preferred_element_type=F32)
            parts.append(yc * lax.rsqrt(ssq * (1.0 / HEAD_DIM) + EPS))
        return jnp.concatenate(parts, axis=1) * gain

    q = head_norm(jnp.dot(h, wq_ref[...], preferred_element_type=F32), qg_ref[...])
    q = q * (HEAD_DIM ** -0.5)
    k = head_norm(jnp.dot(h, wk_ref[...], preferred_element_type=F32), kg_ref[...])
    v = jnp.dot(h, wv_ref[...], preferred_element_type=F32)
    blk = ATT_BLOCK
    for j in range(R):
        k_ref[0, j] = k[j * nl:(j + 1) * nl].astype(BF16)
        for c in range(nl // blk):
            rows = slice(j * nl + c * blk, j * nl + (c + 1) * blk)
            for hp in range(q.shape[1] // blk):
                cs = slice(hp * blk, (hp + 1) * blk)
                qt_ref[0, j, cs, c * blk:(c + 1) * blk] = q[rows, cs].T.astype(BF16)
                vt_ref[0, j, cs, c * blk:(c + 1) * blk] = v[rows, cs].T.astype(BF16)


def _qkv_group(x, g, w_bf, q_gain, k_gain, gi, dil):
    B, S, D = x.shape
    HD = ATT_HEADS * HEAD_DIM
    G = len(DILATED_GROUPS)
    L = S // dil
    R = min(dil, QKV_ROWS // ATT_BLOCK)
    nl = QKV_ROWS // R
    lanes = 128
    wspec = lambda s: pl.BlockSpec((D, HD), lambda b, l, r: (0, s * G + gi))
    tspec = pl.BlockSpec((1, R, HD, nl), lambda b, l, r: (b, r, 0, l))
    tshape = jax.ShapeDtypeStruct((B, dil, HD, L), BF16)
    return pl.pallas_call(
        functools.partial(_qkv_kernel, dil=dil, R=R, nl=nl),
        grid=(B, L // nl, dil // R),
        in_specs=[
            pl.BlockSpec((1, nl * dil, D), lambda b, l, r: (b, l, 0)),
            pl.BlockSpec((1, D), lambda b, l, r: (0, 0)),
            wspec(0), wspec(1), wspec(2),
            pl.BlockSpec((1, HD), lambda b, l, r: (0, 0)),
            pl.BlockSpec((1, HD), lambda b, l, r: (0, 0)),
        ],
        out_specs=[tspec, pl.BlockSpec((1, R, nl, HD), lambda b, l, r: (b, r, l, 0)), tspec],
        out_shape=[tshape, jax.ShapeDtypeStruct((B, dil, L, HD), BF16), tshape],
        scratch_shapes=[pltpu.VMEM((D // lanes, nl * dil if dil > 1 else 8, lanes), F32),
                        pltpu.VMEM((V7X_MXU_DIM, V7X_MXU_DIM), BF16)],
        compiler_params=_params(("arbitrary", "arbitrary", "arbitrary")),
        name=f"attn_qkv_g{gi}",
    )(x, g.reshape(1, D), w_bf, w_bf, w_bf,
      jnp.tile(q_gain, ATT_HEADS).reshape(1, HD), jnp.tile(k_gain, ATT_HEADS).reshape(1, HD))


def _attn_kernel(*refs, single, nitems):
    if single:
        qt_ref, kc_ref, vtc_ref, bias_ref, o_ref, l_ref, s_ref, p_ref = refs
    else:
        qt_ref, kc_ref, kp_ref, vtc_ref, vtp_ref, bias_ref, o_ref, l_ref, s_ref, p_ref = refs
    blk = ATT_BLOCK
    pair = 2 * HEAD_DIM
    first = jnp.where(pl.program_id(2) == 0, 1, 0)
    if single:
        k0 = blk
        variants = [1] * nitems
        q_of = lambda it, cs: qt_ref[0, it, cs, :]
        k_of = lambda it, cs: kc_ref[0, it, :, cs]
        vt_of = lambda it, cs: vtc_ref[0, it, cs, :]
    else:
        k0 = 0
        variants = [first] + [0] * (nitems - 1)

        def q_of(it, cs):
            return qt_ref[0, 0, cs, it * blk:(it + 1) * blk]

        def k_of(it, cs):
            if it == 0:
                return jnp.concatenate([kp_ref[0, 0, :, cs], kc_ref[0, 0, :blk, cs]], axis=0)
            return kc_ref[0, 0, (it - 1) * blk:(it + 1) * blk, cs]

        def vt_of(it, cs):
            if it == 0:
                return jnp.concatenate([vtp_ref[0, 0, cs, :], vtc_ref[0, 0, cs, :blk]], axis=1)
            return vtc_ref[0, 0, cs, (it - 1) * blk:(it + 1) * blk]

    feat = lax.broadcasted_iota(I32, (pair, blk), 0)
    head_rows = [jnp.where(feat < HEAD_DIM, 1.0, 0.0).astype(BF16),
                 jnp.where(feat < HEAD_DIM, 0.0, 1.0).astype(BF16)]
    invs = {}
    lse_rows = {}

    def scores(it, pr):
        cs = slice(pr * pair, (pr + 1) * pair)
        qt2 = q_of(it, cs)
        k2 = k_of(it, cs)
        for half in range(2):
            qth = qt2 * head_rows[half]
            s_ref[it * ATT_HEADS + 2 * pr + half] = jnp.dot(k2, qth, preferred_element_type=F32)

    def softmax(it, pr):
        for head in (2 * pr, 2 * pr + 1):
            s = s_ref[it * ATT_HEADS + head] + bias_ref[head, variants[it], k0:, :]
            m = jnp.max(s, axis=0, keepdims=True)
            p = jnp.exp(s - m)
            den = jnp.sum(p, axis=0, keepdims=True)
            p_ref[it * ATT_HEADS + head] = p.astype(BF16)
            invs[it, head] = 1.0 / den
            lse_rows[it, head] = m + jnp.log(den)

    def outputs(it, pr):
        cs = slice(pr * pair, (pr + 1) * pair)
        vt2 = vt_of(it, cs)
        halves = []
        for half in range(2):
            head = 2 * pr + half
            vth = vt2[half * HEAD_DIM:(half + 1) * HEAD_DIM]
            ot = jnp.dot(vth, p_ref[it * ATT_HEADS + head], preferred_element_type=F32)
            halves.append(ot * invs[it, head])
        val = jnp.concatenate(halves, axis=0).T
        if single:
            o_ref[0, it, :, cs] = val
        else:
            o_ref[0, 0, it * blk:(it + 1) * blk, cs] = val

    work = [(it, pr) for it in range(nitems) for pr in range(ATT_HEADS // 2)]
    for it, pr in work:
        scores(it, pr)
    for it, pr in work:
        softmax(it, pr)
    for it, pr in work:
        outputs(it, pr)
    lse_row = lax.broadcasted_iota(I32, (blk, blk), 0)
    for it in range(nitems):
        lse_t = jnp.zeros((blk, blk), F32)
        for head in range(ATT_HEADS):
            lse_t = jnp.where(lse_row == head, lse_rows[it, head], lse_t)
        if single:
            l_ref[0, it] = lse_t.T
        else:
            l_ref[0, 0, it * blk:(it + 1) * blk, :] = lse_t.T


def _attn_group(qt, k, vt, bias, gi):
    B, dil, L, HD = k.shape
    blk = ATT_BLOCK
    nb = L // blk
    single = nb == 1
    ni = min(ATT_ITEMS, dil if single else nb)
    bias_spec = pl.BlockSpec((ATT_HEADS, 2, 2 * blk, blk), lambda b, r, n: (gi, 0, 0, 0))
    if single:
        rows = pl.BlockSpec((1, ni, blk, HD), lambda b, r, n: (b, r, 0, 0))
        cols = pl.BlockSpec((1, ni, HD, blk), lambda b, r, n: (b, r, 0, 0))
        grid = (B, dil // ni, 1)
        in_specs = [cols, rows, cols, bias_spec]
        args = (qt, k, vt, bias)
        out_specs = [rows, pl.BlockSpec((1, ni, blk, blk), lambda b, r, n: (b, r, 0, 0))]
        nk = blk
    else:
        rows = pl.BlockSpec((1, 1, ni * blk, HD), lambda b, r, n: (b, r, n, 0))
        cols = pl.BlockSpec((1, 1, HD, ni * blk), lambda b, r, n: (b, r, 0, n))
        prow = pl.BlockSpec((1, 1, blk, HD), lambda b, r, n: (b, r, jnp.maximum(ni * n - 1, 0), 0))
        pcol = pl.BlockSpec((1, 1, HD, blk), lambda b, r, n: (b, r, 0, jnp.maximum(ni * n - 1, 0)))
        grid = (B, dil, nb // ni)
        in_specs = [cols, rows, prow, cols, pcol, bias_spec]
        args = (qt, k, k, vt, vt, bias)
        out_specs = [rows, pl.BlockSpec((1, 1, ni * blk, blk), lambda b, r, n: (b, r, n, 0))]
        nk = 2 * blk
    return pl.pallas_call(
        functools.partial(_attn_kernel, single=single, nitems=ni),
        grid=grid,
        in_specs=in_specs,
        out_specs=out_specs,
        out_shape=[jax.ShapeDtypeStruct((B, dil, L, HD), F32),
                   jax.ShapeDtypeStruct((B, dil, L, blk), F32)],
        scratch_shapes=[pltpu.VMEM((ni * ATT_HEADS, nk, blk), F32),
                        pltpu.VMEM((ni * ATT_HEADS, nk, blk), BF16)],
        compiler_params=_params(("parallel", "parallel", "arbitrary")),
        name=f"attn_core_g{gi}",
    )(*args)


def _attn_out_kernel(x_ref, o0, o1, o2, l0, l1, l2, wo_ref, out_ref, oslab, lslab, ex_ref, *, T):
    lanes = oslab.shape[3]
    HD = o0.shape[3]

    @pl.when((pl.program_id(0) == 0) & (pl.program_id(1) == 0))
    def _():
        h = lax.broadcasted_iota(I32, ex_ref.shape, 0) & (ATT_BLOCK - 1)
        c = lax.broadcasted_iota(I32, ex_ref.shape, 1)
        owner = lax.shift_right_logical(c, HEAD_DIM.bit_length() - 1)
        ex_ref[...] = jnp.where(h == owner, 1.0, 0.0).astype(BF16)

    def token_order(o_ref, l_ref, dil, s):
        if dil == 1:
            return o_ref[0, 0], l_ref[0, 0]
        n = T // dil
        for r in range(dil):
            blk = o_ref[0, r]
            for c in range(HD // lanes):
                oslab[s, c, pl.ds(r, n, stride=dil), :] = blk[:, c * lanes:(c + 1) * lanes]
            lslab[s, pl.ds(r, n, stride=dil), :] = l_ref[0, r]
        o = jnp.concatenate([oslab[s, c] for c in range(HD // lanes)], axis=1)
        return o, lslab[s]

    groups = [token_order(o_ref, l_ref, dil, s)
              for s, ((_, dil), o_ref, l_ref) in enumerate(zip(DILATED_GROUPS, (o0, o1, o2), (l0, l1, l2)))]
    m = jnp.maximum(jnp.maximum(groups[0][1], groups[1][1]), groups[2][1])
    es = [jnp.exp(l - m) for _, l in groups]
    inv = 1.0 / (es[0] + es[1] + es[2])
    att = jnp.zeros((T, HD), F32)
    for (o, _), e in zip(groups, es):
        a = e * inv
        a_hi = a.astype(BF16)
        a_lo = (a - a_hi.astype(F32)).astype(BF16)
        wide = jnp.dot(jnp.concatenate([a_hi, a_lo], axis=1), ex_ref[...], preferred_element_type=F32)
        att = att + wide * o
    out_ref[0] = x_ref[0] + jnp.dot(att.astype(BF16), wo_ref[...], preferred_element_type=F32)


def _attn_out(x, os, ls, wo_bf):
    B, S, D = x.shape
    HD = wo_bf.shape[0]
    T = OUT_TILE
    lanes = 128
    ospec = lambda dil, w: pl.BlockSpec((1, dil, T // dil, w), lambda b, i: (b, 0, i, 0))
    dils = [dil for _, dil in DILATED_GROUPS]
    xspec = pl.BlockSpec((1, T, D), lambda b, i: (b, i, 0))
    return pl.pallas_call(
        functools.partial(_attn_out_kernel, T=T),
        grid=(B, S // T),
        in_specs=[xspec] + [ospec(d, HD) for d in dils] + [ospec(d, ATT_BLOCK) for d in dils]
                 + [pl.BlockSpec((HD, D), lambda b, i: (0, 0))],
        out_specs=xspec,
        out_shape=jax.ShapeDtypeStruct((B, S, D), F32),
        scratch_shapes=[pltpu.VMEM((len(dils), HD // lanes, T, lanes), F32),
                        pltpu.VMEM((len(dils), T, ATT_BLOCK), F32),
                        pltpu.VMEM((2 * ATT_BLOCK, HD), BF16)],
        compiler_params=_params(("arbitrary", "arbitrary")),
        name="attn_out",
    )(x, *os, *ls, wo_bf)


def _attention_layer(x, g, w_qkv, q_gain, k_gain, w_o, rel_bias, layer):
    HD = ATT_HEADS * HEAD_DIM
    bias = _bias_tables(rel_bias)
    wqkv_bf = _to_bf16(w_qkv, layer, HD)
    wo_bf = _to_bf16(w_o, layer, w_o.shape[2])
    os, ls = [], []
    for gi, (_, dil) in enumerate(DILATED_GROUPS):
        qt, k, vt = _qkv_group(x, g, wqkv_bf, q_gain, k_gain, gi, dil)
        o, l = _attn_group(qt, k, vt, bias, gi)
        os.append(o)
        ls.append(l)
    return _attn_out(x, os, ls, wo_bf)


def kernel(x, norm_mix_g, norm_ffn_g, pool_w, pool_scale, attn_w_qkv, attn_q_gain, attn_k_gain,
           attn_w_o, rel_bias, moe_w_router, moe_b_router, moe_w_gate_up, moe_b_gate_up,
           moe_w_down, moe_b_down):
    B, S, D = x.shape
    depth = norm_mix_g.shape[0]
    for i in range(depth):
        j = i // 2
        if i % 2 == 0:
            x = _pool_layer(x, norm_mix_g[i], pool_w[j], pool_scale[j])
        else:
            x = _attention_layer(x, norm_mix_g[i], attn_w_qkv, attn_q_gain[j], attn_k_gain[j],
                                 attn_w_o, rel_bias, j)
        x = _moe_layer(x.reshape(B * S, D), norm_ffn_g[i], moe_w_router[i], moe_b_router[i],
                       moe_w_gate_up, moe_b_gate_up[i], moe_w_down, moe_b_down[i],
                       i).reshape(B, S, D)
    return x
```

```python
import functools
import math

import numpy as np
import jax
import jax.numpy as jnp
from jax import lax
from jax.experimental import pallas as pl
from jax.experimental.pallas import tpu as pltpu

F32 = jnp.float32
BF16 = jnp.bfloat16
I32 = jnp.int32

EPS = 1e-6
POOL_WINDOWS = (2, 4, 8, 16)
POOL_HALO = max(POOL_WINDOWS)
DILATED_GROUPS = ((128, 1), (512, 4), (2048, 16))
ATT_HEADS = 16
HEAD_DIM = 64
ATT_BLOCK = 128
ATT_ITEMS = 8
N_BUCKETS = 32
REL_MAX_DIST = 2048
NEG_INF = -1e30
LOG2_E = math.log2(math.e)
N_EXPERTS = 32
TOP_K = 4
SWIGLU_LIMIT = 7.0
SWIGLU_ALPHA = 1.702

V7X_VMEM_LIMIT_BYTES = 56 * 1024 * 1024
V7X_MXU_DIM = 256
ISSUE_UNROLL = 64
MOE_TILE = 512
DISPATCH_TILE = 512
COMBINE_TILE = 256
ROUTER_TILE = 1024
POOL_TILE = 1024
QKV_ROWS = 1024
LANES = 128
OUT_TILE = 512


def _rms(xf, g):
    ms = jnp.mean(xf * xf, axis=-1, keepdims=True)
    return xf * lax.rsqrt(ms + EPS) * g


def _params(sem):
    return pltpu.CompilerParams(dimension_semantics=sem, vmem_limit_bytes=V7X_VMEM_LIMIT_BYTES)


def _pool_kernel(x_ref, halo_ref, g_ref, w_ref, sc_ref, o_ref, *, ts, dg):
    i = pl.program_id(1)
    x = x_ref[0]
    g = g_ref[...]
    h = _rms(x, g)
    hh = _rms(halo_ref[0], g)
    hh = jnp.where(i == 0, 0.0, hh)
    full = jnp.concatenate([hh, h], axis=0)
    pos = i * ts + lax.broadcasted_iota(I32, (ts, 1), 0)
    outs = []
    for gi, w in enumerate(POOL_WINDOWS):
        s = full[:, gi * dg:(gi + 1) * dg]
        sh = 1
        while sh < w:
            s = s + pltpu.roll(s, sh, 0)
            sh *= 2
        s = s[POOL_HALO:]
        cnt = jnp.minimum(pos + 1, w).astype(F32)
        p = s / cnt - h[:, gi * dg:(gi + 1) * dg]
        outs.append(jnp.dot(p.astype(BF16), w_ref[gi].astype(BF16),
                            preferred_element_type=F32))
    y = jnp.concatenate(outs, axis=1) * sc_ref[...]
    o_ref[0] = x + y


def _pool_layer(x, g, w_groups, scale):
    B, S, D = x.shape
    ts = POOL_TILE
    dg = D // len(POOL_WINDOWS)
    hb = ts // POOL_HALO
    return pl.pallas_call(
        functools.partial(_pool_kernel, ts=ts, dg=dg),
        grid=(B, S // ts),
        in_specs=[
            pl.BlockSpec((1, ts, D), lambda b, i: (b, i, 0)),
            pl.BlockSpec((1, POOL_HALO, D), lambda b, i: (b, jnp.maximum(i * hb - 1, 0), 0)),
            pl.BlockSpec((1, D), lambda b, i: (0, 0)),
            pl.BlockSpec((len(POOL_WINDOWS), dg, dg), lambda b, i: (0, 0, 0)),
            pl.BlockSpec((1, D), lambda b, i: (0, 0)),
        ],
        out_specs=pl.BlockSpec((1, ts, D), lambda b, i: (b, i, 0)),
        out_shape=jax.ShapeDtypeStruct((B, S, D), F32),
        compiler_params=_params(("parallel", "parallel")),
        name="pool_mixer",
    )(x, x, g.reshape(1, D), w_groups, scale.reshape(1, D))


def _router_kernel(x_ref, g_ref, wr_ref, br_ref, e_ref, gate_ref, rank_ref, cnt_ref,
                   tri_ref, carry_ref, wcat_ref, *, tt):
    i = pl.program_id(0)
    E = N_EXPERTS

    @pl.when(i == 0)
    def _():
        r = lax.broadcasted_iota(I32, tri_ref.shape, 0)
        c = lax.broadcasted_iota(I32, tri_ref.shape, 1)
        tri_ref[...] = jnp.where(r < c, 1.0, 0.0).astype(BF16)
        carry_ref[...] = jnp.zeros_like(carry_ref)
        w = wr_ref[...]
        w_hi = w.astype(BF16)
        wcat_ref[:, :LANES] = w_hi
        wcat_ref[:, LANES:] = (w - w_hi.astype(F32)).astype(BF16)

    h = _rms(x_ref[...], g_ref[...])
    h_hi = h.astype(BF16)
    h_lo = (h - h_hi.astype(F32)).astype(BF16)
    both = jnp.dot(h_hi, wcat_ref[...], preferred_element_type=F32)
    cross = jnp.dot(h_lo, wcat_ref[:, :LANES], preferred_element_type=F32)
    logits = both[:, :LANES] + (both[:, LANES:] + cross) + br_ref[...]
    nblk = tt // LANES
    l = jnp.concatenate([logits[c * LANES:(c + 1) * LANES].T for c in range(nblk)], axis=1)[:E]
    row = lax.broadcasted_iota(I32, (E, tt), 0).astype(F32)
    vals, sels, idxs = [], [], []
    for k in range(TOP_K):
        m = jnp.max(l, axis=0, keepdims=True)
        idx = jnp.min(jnp.where(l == m, row, float(E)), axis=0, keepdims=True)
        sel = row == idx
        vals.append(m)
        sels.append(sel)
        idxs.append(idx)
        l = jnp.where(sel, -jnp.inf, l)
    ex = [jnp.exp(v - vals[0]) for v in vals]
    den = ex[0] + ex[1] + ex[2] + ex[3]
    multi = jnp.zeros((E, tt), F32)
    for sel in sels:
        multi = multi + jnp.where(sel, 1.0, 0.0)
    base = carry_ref[:, :1]
    parts = []
    for c in range(nblk):
        mc = multi[:, c * LANES:(c + 1) * LANES]
        parts.append(jnp.dot(mc.astype(BF16), tri_ref[...], preferred_element_type=F32) + base)
        base = base + jnp.sum(mc, axis=1, keepdims=True)
    before = jnp.concatenate(parts, axis=1)
    kk = lax.broadcasted_iota(I32, (TOP_K, tt), 0)
    e_out = jnp.zeros((TOP_K, tt), F32)
    g_out = jnp.zeros((TOP_K, tt), F32)
    r_out = jnp.zeros((TOP_K, tt), F32)
    for k in range(TOP_K):
        e_out = jnp.where(kk == k, idxs[k], e_out)
        g_out = jnp.where(kk == k, ex[k] / den, g_out)
        rk = jnp.sum(jnp.where(sels[k], before, 0.0), axis=0, keepdims=True)
        r_out = jnp.where(kk == k, rk, r_out)
    e_ref[...] = e_out.astype(I32)
    gate_ref[...] = g_out
    rank_ref[...] = r_out.astype(I32)
    carry_ref[...] = jnp.broadcast_to(base, carry_ref.shape)
    cnt_ref[...] = carry_ref[...]


def _router(x, g, w_r, b_r):
    N, D = x.shape
    tt = ROUTER_TILE
    E = N_EXPERTS
    w_pad = jnp.pad(w_r, ((0, 0), (0, LANES - E)))
    b_pad = jnp.pad(b_r, (0, LANES - E)).reshape(1, LANES)
    kspec = pl.BlockSpec((TOP_K, tt), lambda i: (0, i))
    return pl.pallas_call(
        functools.partial(_router_kernel, tt=tt),
        grid=(N // tt,),
        in_specs=[
            pl.BlockSpec((tt, D), lambda i: (i, 0)),
            pl.BlockSpec((1, D), lambda i: (0, 0)),
            pl.BlockSpec((D, LANES), lambda i: (0, 0)),
            pl.BlockSpec((1, LANES), lambda i: (0, 0)),
        ],
        out_specs=[kspec, kspec, kspec, pl.BlockSpec((E, LANES), lambda i: (0, 0))],
        out_shape=[
            jax.ShapeDtypeStruct((TOP_K, N), I32),
            jax.ShapeDtypeStruct((TOP_K, N), F32),
            jax.ShapeDtypeStruct((TOP_K, N), I32),
            jax.ShapeDtypeStruct((E, LANES), F32),
        ],
        scratch_shapes=[pltpu.VMEM((LANES, LANES), BF16), pltpu.VMEM((E, LANES), F32),
                        pltpu.VMEM((D, 2 * LANES), BF16)],
        compiler_params=_params(("arbitrary",)),
        name="moe_router",
    )(x, g.reshape(1, D), w_pad, b_pad)


def _to_tile_rows(ref, val):
    n, D = val.shape
    nsub = D // LANES
    for s in range(nsub):
        ref[pl.ds(s, n, stride=nsub), :] = val[:, s * LANES:(s + 1) * LANES]


def _from_tile_rows(ref, n, nsub):
    return jnp.concatenate([ref[pl.ds(s, n, stride=nsub), :] for s in range(nsub)], axis=1)


def _tile_row(ref, r, nsub, count=1):
    return ref.at[pl.ds(pl.multiple_of(r * nsub, nsub), count * nsub)]


def _row_copy_wait(src, dst, sem, times):
    for _ in range(times):
        pltpu.make_async_copy(src, dst.at[pl.ds(0, src.shape[0])], sem).wait()


def _dispatch_kernel(pend_ref, nu_ref, *refs, tt, tm, nt, nsub):
    dest_refs = refs[:TOP_K]
    x_ref, g_ref, xpad_ref, hbuf, zbuf, sems, zsems = refs[TOP_K:]
    i = pl.program_id(0)
    slot = i % 2
    last = pl.num_programs(0) - 1

    def seg_copy(e):
        start = pl.multiple_of(jnp.maximum(pend_ref[e] - tm, 0), tm)
        return pltpu.make_async_copy(zbuf, _tile_row(xpad_ref, start, nsub, tm), zsems.at[0])

    def tail_copy(j):
        return pltpu.make_async_copy(zbuf, _tile_row(xpad_ref, (nt - 1 - j) * tm, nsub, tm), zsems.at[1])

    @pl.when(i == 0)
    def _():
        zbuf[...] = jnp.zeros_like(zbuf)
        for e in range(N_EXPERTS):
            seg_copy(e).start()
        for j in range(N_EXPERTS):
            pl.when(nt - 1 - j >= nu_ref[0])(tail_copy(j).start)

    hb = hbuf.at[slot]
    _to_tile_rows(hb, _rms(x_ref[...], g_ref[...]))

    @pl.when(i == 0)
    def _():
        for e in range(N_EXPERTS):
            seg_copy(e).wait()

    @pl.when(i == last)
    def _():
        for j in range(N_EXPERTS):
            pl.when(nt - 1 - j >= nu_ref[0])(tail_copy(j).wait)

    def issue(t, carry):
        for k in range(TOP_K):
            d = dest_refs[k][t]
            pltpu.make_async_copy(_tile_row(hb, t, nsub), _tile_row(xpad_ref, d, nsub),
                                  sems.at[slot]).start(priority=k % 2)
        return carry

    lax.fori_loop(0, tt, issue, 0, unroll=ISSUE_UNROLL)

    @pl.when(i > 0)
    def _():
        _row_copy_wait(hbuf.at[1 - slot], xpad_ref, sems.at[1 - slot], TOP_K)

    @pl.when(i == pl.num_programs(0) - 1)
    def _():
        _row_copy_wait(hb, xpad_ref, sems.at[slot], TOP_K)


def _dispatch(x, g, dest, pend, n_used, P):
    N, D = x.shape
    tt = DISPATCH_TILE
    tm = MOE_TILE
    nsub = D // LANES
    grid_spec = pltpu.PrefetchScalarGridSpec(
        num_scalar_prefetch=2,
        grid=(N // tt,),
        in_specs=[pl.BlockSpec((tt,), lambda i, pe, nu, k=k: (k * (N // tt) + i,), memory_space=pltpu.SMEM)
                  for k in range(TOP_K)] + [
            pl.BlockSpec((tt, D), lambda i, pe, nu: (i, 0)),
            pl.BlockSpec((1, D), lambda i, pe, nu: (0, 0)),
        ],
        out_specs=pl.BlockSpec(memory_space=pl.ANY),
        scratch_shapes=[pltpu.VMEM((2, tt * nsub, LANES), F32), pltpu.VMEM((tm * nsub, LANES), F32),
                        pltpu.SemaphoreType.DMA((2,)), pltpu.SemaphoreType.DMA((2,))],
    )
    return pl.pallas_call(
        functools.partial(_dispatch_kernel, tt=tt, tm=tm, nt=P // tm, nsub=nsub),
        grid_spec=grid_spec,
        out_shape=jax.ShapeDtypeStruct((P * nsub, LANES), F32),
        compiler_params=_params(("arbitrary",)),
        name="moe_dispatch",
    )(pend, n_used, *([dest] * TOP_K), x, g.reshape(1, D))


def _expert_kernel(te_ref, nx_ref, nu_ref, x_ref, wgu_hbm, bgu_ref, wdn_hbm, bdn_ref, o_ref,
                   wgu_f32, wdn_f32, wgu_bf, wdn_bf, sems, *, layer, F, tm, nsub):
    i = pl.program_id(0)

    def weight_copies(e):
        return (pltpu.make_async_copy(wgu_hbm.at[layer, e], wgu_f32, sems.at[0]),
                pltpu.make_async_copy(wdn_hbm.at[layer, e], wdn_f32, sems.at[1]))

    @pl.when(i < nu_ref[0])
    def _():
        e = te_ref[i]

        @pl.when(i == 0)
        def _():
            for c in weight_copies(e):
                c.start()

        @pl.when((i == 0) | (e != te_ref[jnp.maximum(i - 1, 0)]))
        def _():
            for c in weight_copies(e):
                c.wait()
            wgu_bf[...] = wgu_f32[...].astype(BF16)
            wdn_bf[...] = wdn_f32[...].astype(BF16)
            nxt = nx_ref[i]

            @pl.when(nxt != e)
            def _():
                for c in weight_copies(nxt):
                    c.start()

        x = _from_tile_rows(x_ref, tm, nsub).astype(BF16)
        gu = jnp.dot(x, wgu_bf[...], preferred_element_type=F32) + bgu_ref[0]
        gate = jnp.minimum(gu[:, :F], SWIGLU_LIMIT)
        up = jnp.clip(gu[:, F:], -SWIGLU_LIMIT, SWIGLU_LIMIT)
        glu = gate * jax.nn.sigmoid(SWIGLU_ALPHA * gate)
        a = ((up + 1.0) * glu).astype(BF16)
        _to_tile_rows(o_ref, jnp.dot(a, wdn_bf[...], preferred_element_type=F32) + bdn_ref[0])

    @pl.when(i >= nu_ref[0])
    def _():
        o_ref[...] = jnp.zeros_like(o_ref)


def _experts(tile_e, next_e, n_used, x_pad, w_gu, b_gu, w_dn, b_dn, layer):
    _, E, D, F2 = w_gu.shape
    F = F2 // 2
    tm = MOE_TILE
    nsub = D // LANES
    P = x_pad.shape[0] // nsub
    row = lambda i, te, nx, nu: (jnp.minimum(i, nu[0] - 1), 0)
    exp3 = lambda i, te, nx, nu: (te[i], 0, 0)
    grid_spec = pltpu.PrefetchScalarGridSpec(
        num_scalar_prefetch=3,
        grid=(P // tm,),
        in_specs=[
            pl.BlockSpec((tm * nsub, LANES), row),
            pl.BlockSpec(memory_space=pl.ANY),
            pl.BlockSpec((1, 1, F2), exp3),
            pl.BlockSpec(memory_space=pl.ANY),
            pl.BlockSpec((1, 1, D), exp3),
        ],
        out_specs=pl.BlockSpec((tm * nsub, LANES), lambda i, te, nx, nu: (i, 0)),
        scratch_shapes=[pltpu.VMEM((D, F2), F32), pltpu.VMEM((F, D), F32),
                        pltpu.VMEM((D, F2), BF16), pltpu.VMEM((F, D), BF16),
                        pltpu.SemaphoreType.DMA((2,))],
    )
    return pl.pallas_call(
        functools.partial(_expert_kernel, layer=layer, F=F, tm=tm, nsub=nsub),
        grid_spec=grid_spec,
        out_shape=jax.ShapeDtypeStruct((P * nsub, LANES), F32),
        compiler_params=_params(("arbitrary",)),
        name="moe_experts",
    )(tile_e, next_e, n_used, x_pad, w_gu, b_gu.reshape(E, 1, F2), w_dn, b_dn.reshape(E, 1, D))


def _combine_kernel(*refs, tt, nsub):
    dest_refs = refs[:TOP_K]
    dnext_refs = refs[TOP_K:2 * TOP_K]
    x_ref, gate_ref, ypad_ref, o_ref, ybuf, sems = refs[2 * TOP_K:]
    i = pl.program_id(0)
    slot = i % 2

    def gather(idx_refs, s):
        def issue(t, carry):
            for k in range(TOP_K):
                d = idx_refs[k][t]
                pltpu.make_async_copy(_tile_row(ypad_ref, d, nsub), _tile_row(ybuf.at[s, k], t, nsub),
                                      sems.at[s]).start(priority=k % 2)
            return carry

        lax.fori_loop(0, tt, issue, 0, unroll=ISSUE_UNROLL)

    @pl.when(i == 0)
    def _():
        gather(dest_refs, slot)

    @pl.when(i + 1 < pl.num_programs(0))
    def _():
        gather(dnext_refs, 1 - slot)

    for k in range(TOP_K):
        _row_copy_wait(ybuf.at[slot, k], ypad_ref, sems.at[slot], 1)
    gate = gate_ref[...]
    x = x_ref[...]
    cols = []
    for c in range(nsub):
        acc = x[:, c * LANES:(c + 1) * LANES]
        for k in range(TOP_K):
            acc = acc + ybuf[slot, k, pl.ds(c, tt, stride=nsub), :] * gate[:, k:k + 1]
        cols.append(acc)
    o_ref[...] = jnp.concatenate(cols, axis=1)


def _combine(x, gate, dest, y_pad):
    N, D = x.shape
    tt = COMBINE_TILE
    nsub = D // LANES
    steps = N // tt
    last = steps - 1
    return pl.pallas_call(
        functools.partial(_combine_kernel, tt=tt, nsub=nsub),
        grid=(N // tt,),
        in_specs=[pl.BlockSpec((tt,), lambda i, k=k: (k * steps + i,), memory_space=pltpu.SMEM)
                  for k in range(TOP_K)] + [
            pl.BlockSpec((tt,), lambda i, k=k: (k * steps + jnp.minimum(i + 1, last),),
                         memory_space=pltpu.SMEM) for k in range(TOP_K)] + [
            pl.BlockSpec((tt, D), lambda i: (i, 0)),
            pl.BlockSpec((tt, TOP_K), lambda i: (i, 0)),
            pl.BlockSpec(memory_space=pl.ANY),
        ],
        out_specs=pl.BlockSpec((tt, D), lambda i: (i, 0)),
        out_shape=jax.ShapeDtypeStruct((N, D), F32),
        scratch_shapes=[pltpu.VMEM((2, TOP_K, tt * nsub, LANES), F32), pltpu.SemaphoreType.DMA((2,))],
        compiler_params=_params(("arbitrary",)),
        name="moe_combine",
    )(*([dest] * (2 * TOP_K)), x, gate, y_pad)


def _dest_kernel(pstart_ref, e_ref, rank_ref, o_ref):
    e = e_ref[...]
    seg = jnp.zeros(e.shape, I32)
    for ex in range(N_EXPERTS):
        seg = jnp.where(e == ex, pstart_ref[ex], seg)
    o_ref[...] = seg + rank_ref[...]


def _dest_rows(e_idx, rank, pstart):
    return pl.pallas_call(
        _dest_kernel,
        in_specs=[pl.BlockSpec(memory_space=pltpu.SMEM), pl.BlockSpec(e_idx.shape, lambda: (0, 0)),
                  pl.BlockSpec(rank.shape, lambda: (0, 0))],
        out_specs=pl.BlockSpec(rank.shape, lambda: (0, 0)),
        out_shape=jax.ShapeDtypeStruct(rank.shape, I32),
        name="moe_dest",
    )(pstart, e_idx, rank)


def _moe_layer(x, g, w_r, b_r, w_gu, b_gu, w_dn, b_dn, layer):
    N, D = x.shape
    E = N_EXPERTS
    tm = MOE_TILE
    e_idx, gate_t, rank, counts = _router(x, g, w_r, b_r)
    counts = counts[:, 0].astype(I32)
    padded = ((counts + tm - 1) // tm) * tm
    pend = jnp.cumsum(padded)
    pstart = pend - padded
    dest = _dest_rows(e_idx, rank, pstart.astype(I32)).reshape(-1)
    gate = gate_t.T
    P = ((N * TOP_K + E * (tm - 1) + tm - 1) // tm) * tm
    nt = P // tm
    tiles = jnp.arange(nt, dtype=I32)
    tile_e = jnp.minimum(jnp.sum(tiles[:, None] * tm >= pend[None, :], axis=1), E - 1).astype(I32)
    n_used = (pend[-1] // tm).astype(I32)
    experts = jnp.arange(E, dtype=I32)
    used = counts > 0
    tile_e = jnp.where(tiles < n_used, tile_e, jnp.max(jnp.where(used, experts, 0)))
    after = jnp.min(jnp.where((experts[None, :] > experts[:, None]) & used[None, :], experts[None, :], E), axis=1)
    after = jnp.where(after < E, after, experts)
    next_e = jnp.sum(jnp.where(tile_e[:, None] == experts[None, :], after[None, :], 0), axis=1)
    x_pad = _dispatch(x, g, dest, pend.astype(I32), n_used.reshape(1), P)
    y_pad = _experts(tile_e, next_e.astype(I32), n_used.reshape(1), x_pad, w_gu, b_gu, w_dn, b_dn, layer)
    return _combine(x, gate, dest, y_pad)


def _bucket_maps():
    qi = np.arange(ATT_BLOCK)[None, :]
    kj = np.arange(2 * ATT_BLOCK)[:, None]
    delta = qi + ATT_BLOCK - kj
    buckets, valids = [], []
    max_exact = N_BUCKETS // 2
    for win, dil in DILATED_GROUPS:
        n = np.maximum(delta * dil, 0)
        nf = np.maximum(n, 1).astype(np.float32)
        large = max_exact + (np.log(nf / np.float32(max_exact)) / np.float32(math.log(REL_MAX_DIST / max_exact))
                             * np.float32(N_BUCKETS - max_exact)).astype(np.int32)
        large = np.minimum(large, N_BUCKETS - 1)
        buckets.append(np.where(n < max_exact, n, large).astype(np.int32))
        valids.append(((delta >= 0) & (delta <= win // dil)).astype(np.int32))
    return np.stack(buckets), np.stack(valids)


def _bias_kernel(tab_ref, bm_ref, valid_ref, o_ref):
    g = pl.program_id(0)
    bm = bm_ref[0]
    key = lax.broadcasted_iota(I32, bm.shape, 0)

    def one_head(h, carry):
        acc = jnp.zeros(bm.shape, F32)
        for b in range(N_BUCKETS):
            acc = jnp.where(bm == b, tab_ref[b, g * ATT_HEADS + h], acc)
        band = jnp.where(valid_ref[0] > 0, acc * LOG2_E, NEG_INF)
        o_ref[h, 0] = band
        o_ref[h, 1] = jnp.where(key >= ATT_BLOCK, band, NEG_INF)
        return carry

    lax.fori_loop(0, ATT_HEADS, one_head, 0)


def _bias_tables(rel_bias):
    bm, valid = _bucket_maps()
    GH = len(DILATED_GROUPS) * ATT_HEADS
    blk = (1, 2 * ATT_BLOCK, ATT_BLOCK)
    return pl.pallas_call(
        _bias_kernel,
        grid=(len(DILATED_GROUPS),),
        in_specs=[
            pl.BlockSpec(memory_space=pltpu.SMEM),
            pl.BlockSpec(blk, lambda i: (i, 0, 0)),
            pl.BlockSpec(blk, lambda i: (i, 0, 0)),
        ],
        out_specs=pl.BlockSpec((ATT_HEADS, 2, 2 * ATT_BLOCK, ATT_BLOCK), lambda i: (i, 0, 0, 0)),
        out_shape=jax.ShapeDtypeStruct((GH, 2, 2 * ATT_BLOCK, ATT_BLOCK), F32),
        compiler_params=_params(("arbitrary",)),
        name="attn_bias",
    )(rel_bias, jnp.asarray(bm), jnp.asarray(valid))


def _cast_kernel(w_ref, o_ref):
    o_ref[...] = w_ref[...].astype(BF16)


def _to_bf16(w, layer, col_block):
    _, rows, cols = w.shape
    return pl.pallas_call(
        _cast_kernel,
        grid=(cols // col_block,),
        in_specs=[pl.BlockSpec((None, rows, col_block), lambda j: (layer, 0, j))],
        out_specs=pl.BlockSpec((rows, col_block), lambda j: (0, j)),
        out_shape=jax.ShapeDtypeStruct((rows, cols), BF16),
        compiler_params=_params(("parallel",)),
        name="cast_bf16",
    )(w)


def _qkv_kernel(x_ref, g_ref, wq_ref, wk_ref, wv_ref, qg_ref, kg_ref, qt_ref, k_ref, vt_ref,
                slab_ref, bd_ref, *, dil, R, nl):
    rc = pl.program_id(2)
    D = x_ref.shape[2]
    lanes = slab_ref.shape[2]
    first = (pl.program_id(0) == 0) & (pl.program_id(1) == 0) & (rc == 0)

    @pl.when(first)
    def _():
        r = lax.broadcasted_iota(I32, bd_ref.shape, 0)
        c = lax.broadcasted_iota(I32, bd_ref.shape, 1)
        hd_shift = HEAD_DIM.bit_length() - 1
        same_head = lax.shift_right_logical(r, hd_shift) == lax.shift_right_logical(c, hd_shift)
        bd_ref[...] = jnp.where(same_head, 1.0, 0.0).astype(BF16)

    if dil == 1:
        x = x_ref[0]
    else:
        @pl.when(rc == 0)
        def _():
            for c in range(D // lanes):
                slab_ref[c] = x_ref[0, :, c * lanes:(c + 1) * lanes]

        pieces = []
        for j in range(R):
            r = rc * R + j
            cols = [slab_ref[c, pl.ds(r, nl, stride=dil), :] for c in range(D // lanes)]
            pieces.append(jnp.concatenate(cols, axis=1))
        x = jnp.concatenate(pieces, axis=0)
    h = _rms(x, g_ref[...]).astype(BF16)
    nb = bd_ref.shape[0]

    def head_norm(y, gain):
        parts = []
        for c0 in range(0, y.shape[1], nb):
            yc = y[:, c0:c0 + nb]
            ssq = jnp.dot((yc * yc).astype(BF16), bd_ref[...], preferred_element_type=F32)
            parts.append(yc * lax.rsqrt(ssq * (1.0 / HEAD_DIM) + EPS))
        return jnp.concatenate(parts, axis=1) * gain

    q = head_norm(jnp.dot(h, wq_ref[...], preferred_element_type=F32), qg_ref[...])
    q = q * (HEAD_DIM ** -0.5 * LOG2_E)
    k = head_norm(jnp.dot(h, wk_ref[...], preferred_element_type=F32), kg_ref[...])
    v = jnp.dot(h, wv_ref[...], preferred_element_type=F32)
    blk = ATT_BLOCK
    for j in range(R):
        k_ref[0, j] = k[j * nl:(j + 1) * nl].astype(BF16)
        for c in range(nl // blk):
            rows = slice(j * nl + c * blk, j * nl + (c + 1) * blk)
            for hp in range(q.shape[1] // blk):
                cs = slice(hp * blk, (hp + 1) * blk)
                qt_ref[0, j, cs, c * blk:(c + 1) * blk] = q[rows, cs].T.astype(BF16)
                vt_ref[0, j, cs, c * blk:(c + 1) * blk] = v[rows, cs].T.astype(BF16)


def _qkv_group(x, g, w_bf, q_gain, k_gain, gi, dil):
    B, S, D = x.shape
    HD = ATT_HEADS * HEAD_DIM
    G = len(DILATED_GROUPS)
    L = S // dil
    R = min(dil, QKV_ROWS // ATT_BLOCK)
    nl = QKV_ROWS // R
    lanes = 128
    wspec = lambda s: pl.BlockSpec((D, HD), lambda b, l, r: (0, s * G + gi))
    tspec = pl.BlockSpec((1, R, HD, nl), lambda b, l, r: (b, r, 0, l))
    tshape = jax.ShapeDtypeStruct((B, dil, HD, L), BF16)
    return pl.pallas_call(
        functools.partial(_qkv_kernel, dil=dil, R=R, nl=nl),
        grid=(B, L // nl, dil // R),
        in_specs=[
            pl.BlockSpec((1, nl * dil, D), lambda b, l, r: (b, l, 0)),
            pl.BlockSpec((1, D), lambda b, l, r: (0, 0)),
            wspec(0), wspec(1), wspec(2),
            pl.BlockSpec((1, HD), lambda b, l, r: (0, 0)),
            pl.BlockSpec((1, HD), lambda b, l, r: (0, 0)),
        ],
        out_specs=[tspec, pl.BlockSpec((1, R, nl, HD), lambda b, l, r: (b, r, l, 0)), tspec],
        out_shape=[tshape, jax.ShapeDtypeStruct((B, dil, L, HD), BF16), tshape],
        scratch_shapes=[pltpu.VMEM((D // lanes, nl * dil if dil > 1 else 8, lanes), F32),
                        pltpu.VMEM((V7X_MXU_DIM, V7X_MXU_DIM), BF16)],
        compiler_params=_params(("arbitrary", "arbitrary", "arbitrary")),
        name=f"attn_qkv_g{gi}",
    )(x, g.reshape(1, D), w_bf, w_bf, w_bf,
      jnp.tile(q_gain, ATT_HEADS).reshape(1, HD), jnp.tile(k_gain, ATT_HEADS).reshape(1, HD))


def _attn_kernel(*refs, single, nitems):
    if single:
        qt_ref, kc_ref, vtc_ref, bias_ref, o_ref, l_ref, s_ref, p_ref = refs
    else:
        qt_ref, kc_ref, kp_ref, vtc_ref, vtp_ref, bias_ref, o_ref, l_ref, s_ref, p_ref = refs
    blk = ATT_BLOCK
    pair = 2 * HEAD_DIM
    first = jnp.where(pl.program_id(2) == 0, 1, 0)
    if single:
        k0 = blk
        variants = [1] * nitems
        q_of = lambda it, cs: qt_ref[0, it, cs, :]
        k_of = lambda it, cs: kc_ref[0, it, :, cs]
        vt_of = lambda it, cs: vtc_ref[0, it, cs, :]
    else:
        k0 = 0
        variants = [first] + [0] * (nitems - 1)

        def q_of(it, cs):
            return qt_ref[0, 0, cs, it * blk:(it + 1) * blk]

        def k_of(it, cs):
            if it == 0:
                return jnp.concatenate([kp_ref[0, 0, :, cs], kc_ref[0, 0, :blk, cs]], axis=0)
            return kc_ref[0, 0, (it - 1) * blk:(it + 1) * blk, cs]

        def vt_of(it, cs):
            if it == 0:
                return jnp.concatenate([vtp_ref[0, 0, cs, :], vtc_ref[0, 0, cs, :blk]], axis=1)
            return vtc_ref[0, 0, cs, (it - 1) * blk:(it + 1) * blk]

    feat = lax.broadcasted_iota(I32, (pair, blk), 0)
    head_rows = [jnp.where(feat < HEAD_DIM, 1.0, 0.0).astype(BF16),
                 jnp.where(feat < HEAD_DIM, 0.0, 1.0).astype(BF16)]
    invs = {}
    lse_rows = {}

    def scores(it, pr):
        cs = slice(pr * pair, (pr + 1) * pair)
        qt2 = q_of(it, cs)
        k2 = k_of(it, cs)
        for half in range(2):
            qth = qt2 * head_rows[half]
            s_ref[it * ATT_HEADS + 2 * pr + half] = jnp.dot(k2, qth, preferred_element_type=F32)

    def softmax(it, pr):
        for head in (2 * pr, 2 * pr + 1):
            s = s_ref[it * ATT_HEADS + head] + bias_ref[head, variants[it], k0:, :]
            m = jnp.max(s, axis=0, keepdims=True)
            p = jnp.exp2(s - m)
            den = jnp.sum(p, axis=0, keepdims=True)
            p_ref[it * ATT_HEADS + head] = p.astype(BF16)
            invs[it, head] = 1.0 / den
            lse_rows[it, head] = (m + jnp.log2(den)) * (1.0 / LOG2_E)

    def outputs(it, pr):
        cs = slice(pr * pair, (pr + 1) * pair)
        vt2 = vt_of(it, cs)
        halves = []
        for half in range(2):
            head = 2 * pr + half
            vth = vt2[half * HEAD_DIM:(half + 1) * HEAD_DIM]
            ot = jnp.dot(vth, p_ref[it * ATT_HEADS + head], preferred_element_type=F32)
            halves.append(ot * invs[it, head])
        val = jnp.concatenate(halves, axis=0).T
        if single:
            o_ref[0, it, :, cs] = val
        else:
            o_ref[0, 0, it * blk:(it + 1) * blk, cs] = val

    work = [(it, pr) for it in range(nitems) for pr in range(ATT_HEADS // 2)]
    for it, pr in work:
        scores(it, pr)
    for it, pr in work:
        softmax(it, pr)
    for it, pr in work:
        outputs(it, pr)
    lse_row = lax.broadcasted_iota(I32, (blk, blk), 0)
    for it in range(nitems):
        lse_t = jnp.zeros((blk, blk), F32)
        for head in range(ATT_HEADS):
            lse_t = jnp.where(lse_row == head, lse_rows[it, head], lse_t)
        if single:
            l_ref[0, it] = lse_t.T
        else:
            l_ref[0, 0, it * blk:(it + 1) * blk, :] = lse_t.T


def _attn_group(qt, k, vt, bias, gi):
    B, dil, L, HD = k.shape
    blk = ATT_BLOCK
    nb = L // blk
    single = nb == 1
    ni = min(ATT_ITEMS, dil if single else nb)
    bias_spec = pl.BlockSpec((ATT_HEADS, 2, 2 * blk, blk), lambda b, r, n: (gi, 0, 0, 0))
    if single:
        rows = pl.BlockSpec((1, ni, blk, HD), lambda b, r, n: (b, r, 0, 0))
        cols = pl.BlockSpec((1, ni, HD, blk), lambda b, r, n: (b, r, 0, 0))
        grid = (B, dil // ni, 1)
        in_specs = [cols, rows, cols, bias_spec]
        args = (qt, k, vt, bias)
        out_specs = [rows, pl.BlockSpec((1, ni, blk, blk), lambda b, r, n: (b, r, 0, 0))]
        nk = blk
    else:
        rows = pl.BlockSpec((1, 1, ni * blk, HD), lambda b, r, n: (b, r, n, 0))
        cols = pl.BlockSpec((1, 1, HD, ni * blk), lambda b, r, n: (b, r, 0, n))
        prow = pl.BlockSpec((1, 1, blk, HD), lambda b, r, n: (b, r, jnp.maximum(ni * n - 1, 0), 0))
        pcol = pl.BlockSpec((1, 1, HD, blk), lambda b, r, n: (b, r, 0, jnp.maximum(ni * n - 1, 0)))
        grid = (B, dil, nb // ni)
        in_specs = [cols, rows, prow, cols, pcol, bias_spec]
        args = (qt, k, k, vt, vt, bias)
        out_specs = [rows, pl.BlockSpec((1, 1, ni * blk, blk), lambda b, r, n: (b, r, n, 0))]
        nk = 2 * blk
    return pl.pallas_call(
        functools.partial(_attn_kernel, single=single, nitems=ni),
        grid=grid,
        in_specs=in_specs,
        out_specs=out_specs,
        out_shape=[jax.ShapeDtypeStruct((B, dil, L, HD), F32),
                   jax.ShapeDtypeStruct((B, dil, L, blk), F32)],
        scratch_shapes=[pltpu.VMEM((ni * ATT_HEADS, nk, blk), F32),
                        pltpu.VMEM((ni * ATT_HEADS, nk, blk), BF16)],
        compiler_params=_params(("parallel", "parallel", "arbitrary")),
        name=f"attn_core_g{gi}",
    )(*args)


def _attn_out_kernel(x_ref, o0, o1, o2, l0, l1, l2, wo_ref, out_ref, oslab, lslab, ex_ref, *, T):
    lanes = oslab.shape[3]
    HD = o0.shape[3]

    @pl.when((pl.program_id(0) == 0) & (pl.program_id(1) == 0))
    def _():
        h = lax.broadcasted_iota(I32, ex_ref.shape, 0) & (ATT_BLOCK - 1)
        c = lax.broadcasted_iota(I32, ex_ref.shape, 1)
        owner = lax.shift_right_logical(c, HEAD_DIM.bit_length() - 1)
        ex_ref[...] = jnp.where(h == owner, 1.0, 0.0).astype(BF16)

    def token_order(o_ref, l_ref, dil, s):
        if dil == 1:
            return o_ref[0, 0], l_ref[0, 0]
        n = T // dil
        for r in range(dil):
            blk = o_ref[0, r]
            for c in range(HD // lanes):
                oslab[s, c, pl.ds(r, n, stride=dil), :] = blk[:, c * lanes:(c + 1) * lanes]
            lslab[s, pl.ds(r, n, stride=dil), :] = l_ref[0, r]
        o = jnp.concatenate([oslab[s, c] for c in range(HD // lanes)], axis=1)
        return o, lslab[s]

    groups = [token_order(o_ref, l_ref, dil, s)
              for s, ((_, dil), o_ref, l_ref) in enumerate(zip(DILATED_GROUPS, (o0, o1, o2), (l0, l1, l2)))]
    m = jnp.maximum(jnp.maximum(groups[0][1], groups[1][1]), groups[2][1])
    es = [jnp.exp(l - m) for _, l in groups]
    inv = 1.0 / (es[0] + es[1] + es[2])
    att = jnp.zeros((T, HD), F32)
    for (o, _), e in zip(groups, es):
        a = e * inv
        a_hi = a.astype(BF16)
        a_lo = (a - a_hi.astype(F32)).astype(BF16)
        wide = jnp.dot(jnp.concatenate([a_hi, a_lo], axis=1), ex_ref[...], preferred_element_type=F32)
        att = att + wide * o
    out_ref[0] = x_ref[0] + jnp.dot(att.astype(BF16), wo_ref[...], preferred_element_type=F32)


def _attn_out(x, os, ls, wo_bf):
    B, S, D = x.shape
    HD = wo_bf.shape[0]
    T = OUT_TILE
    lanes = 128
    ospec = lambda dil, w: pl.BlockSpec((1, dil, T // dil, w), lambda b, i: (b, 0, i, 0))
    dils = [dil for _, dil in DILATED_GROUPS]
    xspec = pl.BlockSpec((1, T, D), lambda b, i: (b, i, 0))
    return pl.pallas_call(
        functools.partial(_attn_out_kernel, T=T),
        grid=(B, S // T),
        in_specs=[xspec] + [ospec(d, HD) for d in dils] + [ospec(d, ATT_BLOCK) for d in dils]
                 + [pl.BlockSpec((HD, D), lambda b, i: (0, 0))],
        out_specs=xspec,
        out_shape=jax.ShapeDtypeStruct((B, S, D), F32),
        scratch_shapes=[pltpu.VMEM((len(dils), HD // lanes, T, lanes), F32),
                        pltpu.VMEM((len(dils), T, ATT_BLOCK), F32),
                        pltpu.VMEM((2 * ATT_BLOCK, HD), BF16)],
        compiler_params=_params(("arbitrary", "arbitrary")),
        name="attn_out",
    )(x, *os, *ls, wo_bf)


def _attention_layer(x, g, w_qkv, q_gain, k_gain, w_o, rel_bias, layer):
    HD = ATT_HEADS * HEAD_DIM
    bias = _bias_tables(rel_bias)
    wqkv_bf = _to_bf16(w_qkv, layer, HD)
    wo_bf = _to_bf16(w_o, layer, w_o.shape[2])
    os, ls = [], []
    for gi, (_, dil) in enumerate(DILATED_GROUPS):
        qt, k, vt = _qkv_group(x, g, wqkv_bf, q_gain, k_gain, gi, dil)
        o, l = _attn_group(qt, k, vt, bias, gi)
        os.append(o)
        ls.append(l)
    return _attn_out(x, os, ls, wo_bf)


def kernel(x, norm_mix_g, norm_ffn_g, pool_w, pool_scale, attn_w_qkv, attn_q_gain, attn_k_gain,
           attn_w_o, rel_bias, moe_w_router, moe_b_router, moe_w_gate_up, moe_b_gate_up,
           moe_w_down, moe_b_down):
    B, S, D = x.shape
    depth = norm_mix_g.shape[0]
    for i in range(depth):
        j = i // 2
        if i % 2 == 0:
            x = _pool_layer(x, norm_mix_g[i], pool_w[j], pool_scale[j])
        else:
            x = _attention_layer(x, norm_mix_g[i], attn_w_qkv, attn_q_gain[j], attn_k_gain[j],
                                 attn_w_o, rel_bias, j)
        x = _moe_layer(x.reshape(B * S, D), norm_ffn_g[i], moe_w_router[i], moe_b_router[i],
                       moe_w_gate_up, moe_b_gate_up[i], moe_w_down, moe_b_down[i],
                       i).reshape(B, S, D)
    return x
```

```python
import functools
import math

import numpy as np
import jax
import jax.numpy as jnp
from jax import lax
from jax.experimental import pallas as pl
from jax.experimental.pallas import tpu as pltpu

F32 = jnp.float32
BF16 = jnp.bfloat16
I32 = jnp.int32

EPS = 1e-6
POOL_WINDOWS = (2, 4, 8, 16)
POOL_HALO = max(POOL_WINDOWS)
DILATED_GROUPS = ((128, 1), (512, 4), (2048, 16))
ATT_HEADS = 16
HEAD_DIM = 64
ATT_BLOCK = 128
ATT_ITEMS = 8
N_BUCKETS = 32
REL_MAX_DIST = 2048
NEG_INF = -1e30
LOG2_E = math.log2(math.e)
N_EXPERTS = 32
TOP_K = 4
SWIGLU_LIMIT = 7.0
SWIGLU_ALPHA = 1.702

V7X_VMEM_LIMIT_BYTES = 56 * 1024 * 1024
ISSUE_UNROLL = 64
MOE_TILE = 512
DISPATCH_TILE = 512
COMBINE_TILE = 256
ROUTER_TILE = 1024
POOL_TILE = 1024
QKV_ROWS = 1024
LANES = 128
OUT_TILE = 512


def _rms(xf, g):
    ms = jnp.mean(xf * xf, axis=-1, keepdims=True)
    return xf * lax.rsqrt(ms + EPS) * g


def _params(sem):
    return pltpu.CompilerParams(dimension_semantics=sem, vmem_limit_bytes=V7X_VMEM_LIMIT_BYTES)


def _pool_kernel(x_ref, halo_ref, g_ref, w_ref, sc_ref, o_ref, *, ts, dg):
    i = pl.program_id(1)
    x = x_ref[0]
    g = g_ref[...]
    h = _rms(x, g)
    hh = _rms(halo_ref[0], g)
    hh = jnp.where(i == 0, 0.0, hh)
    full = jnp.concatenate([hh, h], axis=0)
    pos = i * ts + lax.broadcasted_iota(I32, (ts, 1), 0)
    outs = []
    for gi, w in enumerate(POOL_WINDOWS):
        s = full[:, gi * dg:(gi + 1) * dg]
        sh = 1
        while sh < w:
            s = s + pltpu.roll(s, sh, 0)
            sh *= 2
        s = s[POOL_HALO:]
        cnt = jnp.minimum(pos + 1, w).astype(F32)
        p = s / cnt - h[:, gi * dg:(gi + 1) * dg]
        outs.append(jnp.dot(p.astype(BF16), w_ref[gi].astype(BF16),
                            preferred_element_type=F32))
    y = jnp.concatenate(outs, axis=1) * sc_ref[...]
    o_ref[0] = x + y


def _pool_layer(x, g, w_groups, scale):
    B, S, D = x.shape
    ts = POOL_TILE
    dg = D // len(POOL_WINDOWS)
    hb = ts // POOL_HALO
    return pl.pallas_call(
        functools.partial(_pool_kernel, ts=ts, dg=dg),
        grid=(B, S // ts),
        in_specs=[
            pl.BlockSpec((1, ts, D), lambda b, i: (b, i, 0)),
            pl.BlockSpec((1, POOL_HALO, D), lambda b, i: (b, jnp.maximum(i * hb - 1, 0), 0)),
            pl.BlockSpec((1, D), lambda b, i: (0, 0)),
            pl.BlockSpec((len(POOL_WINDOWS), dg, dg), lambda b, i: (0, 0, 0)),
            pl.BlockSpec((1, D), lambda b, i: (0, 0)),
        ],
        out_specs=pl.BlockSpec((1, ts, D), lambda b, i: (b, i, 0)),
        out_shape=jax.ShapeDtypeStruct((B, S, D), F32),
        compiler_params=_params(("parallel", "parallel")),
        name="pool_mixer",
    )(x, x, g.reshape(1, D), w_groups, scale.reshape(1, D))


def _router_kernel(x_ref, g_ref, wr_ref, br_ref, e_ref, gate_ref, rank_ref, cnt_ref,
                   tri_ref, carry_ref, wcat_ref, *, tt):
    i = pl.program_id(0)
    E = N_EXPERTS

    @pl.when(i == 0)
    def _():
        r = lax.broadcasted_iota(I32, tri_ref.shape, 0)
        c = lax.broadcasted_iota(I32, tri_ref.shape, 1)
        tri_ref[...] = jnp.where(r < c, 1.0, 0.0).astype(BF16)
        carry_ref[...] = jnp.zeros_like(carry_ref)
        w = wr_ref[...]
        w_hi = w.astype(BF16)
        wcat_ref[:, :LANES] = w_hi
        wcat_ref[:, LANES:] = (w - w_hi.astype(F32)).astype(BF16)

    h = _rms(x_ref[...], g_ref[...])
    h_hi = h.astype(BF16)
    h_lo = (h - h_hi.astype(F32)).astype(BF16)
    both = jnp.dot(h_hi, wcat_ref[...], preferred_element_type=F32)
    cross = jnp.dot(h_lo, wcat_ref[:, :LANES], preferred_element_type=F32)
    logits = both[:, :LANES] + (both[:, LANES:] + cross) + br_ref[...]
    nblk = tt // LANES
    l = jnp.concatenate([logits[c * LANES:(c + 1) * LANES].T for c in range(nblk)], axis=1)[:E]
    row = lax.broadcasted_iota(I32, (E, tt), 0).astype(F32)
    vals, sels, idxs = [], [], []
    for k in range(TOP_K):
        m = jnp.max(l, axis=0, keepdims=True)
        idx = jnp.min(jnp.where(l == m, row, float(E)), axis=0, keepdims=True)
        sel = row == idx
        vals.append(m)
        sels.append(sel)
        idxs.append(idx)
        l = jnp.where(sel, -jnp.inf, l)
    ex = [jnp.exp(v - vals[0]) for v in vals]
    den = ex[0] + ex[1] + ex[2] + ex[3]
    multi = jnp.zeros((E, tt), F32)
    for sel in sels:
        multi = multi + jnp.where(sel, 1.0, 0.0)
    base = carry_ref[:, :1]
    parts = []
    for c in range(nblk):
        mc = multi[:, c * LANES:(c + 1) * LANES]
        parts.append(jnp.dot(mc.astype(BF16), tri_ref[...], preferred_element_type=F32) + base)
        base = base + jnp.sum(mc, axis=1, keepdims=True)
    before = jnp.concatenate(parts, axis=1)
    kk = lax.broadcasted_iota(I32, (TOP_K, tt), 0)
    e_out = jnp.zeros((TOP_K, tt), F32)
    g_out = jnp.zeros((TOP_K, tt), F32)
    r_out = jnp.zeros((TOP_K, tt), F32)
    for k in range(TOP_K):
        e_out = jnp.where(kk == k, idxs[k], e_out)
        g_out = jnp.where(kk == k, ex[k] / den, g_out)
        rk = jnp.sum(jnp.where(sels[k], before, 0.0), axis=0, keepdims=True)
        r_out = jnp.where(kk == k, rk, r_out)
    e_ref[...] = e_out.astype(I32)
    gate_ref[...] = g_out
    rank_ref[...] = r_out.astype(I32)
    carry_ref[...] = jnp.broadcast_to(base, carry_ref.shape)
    cnt_ref[...] = carry_ref[...]


def _router(x, g, w_r, b_r):
    N, D = x.shape
    tt = ROUTER_TILE
    E = N_EXPERTS
    w_pad = jnp.pad(w_r, ((0, 0), (0, LANES - E)))
    b_pad = jnp.pad(b_r, (0, LANES - E)).reshape(1, LANES)
    kspec = pl.BlockSpec((TOP_K, tt), lambda i: (0, i))
    return pl.pallas_call(
        functools.partial(_router_kernel, tt=tt),
        grid=(N // tt,),
        in_specs=[
            pl.BlockSpec((tt, D), lambda i: (i, 0)),
            pl.BlockSpec((1, D), lambda i: (0, 0)),
            pl.BlockSpec((D, LANES), lambda i: (0, 0)),
            pl.BlockSpec((1, LANES), lambda i: (0, 0)),
        ],
        out_specs=[kspec, kspec, kspec, pl.BlockSpec((E, LANES), lambda i: (0, 0))],
        out_shape=[
            jax.ShapeDtypeStruct((TOP_K, N), I32),
            jax.ShapeDtypeStruct((TOP_K, N), F32),
            jax.ShapeDtypeStruct((TOP_K, N), I32),
            jax.ShapeDtypeStruct((E, LANES), F32),
        ],
        scratch_shapes=[pltpu.VMEM((LANES, LANES), BF16), pltpu.VMEM((E, LANES), F32),
                        pltpu.VMEM((D, 2 * LANES), BF16)],
        compiler_params=_params(("arbitrary",)),
        name="moe_router",
    )(x, g.reshape(1, D), w_pad, b_pad)


def _to_tile_rows(ref, val):
    n, D = val.shape
    nsub = D // LANES
    for s in range(nsub):
        ref[pl.ds(s, n, stride=nsub), :] = val[:, s * LANES:(s + 1) * LANES]


def _from_tile_rows(ref, n, nsub):
    return jnp.concatenate([ref[pl.ds(s, n, stride=nsub), :] for s in range(nsub)], axis=1)


def _tile_row(ref, r, nsub, count=1):
    return ref.at[pl.ds(pl.multiple_of(r * nsub, nsub), count * nsub)]


def _row_copy_wait(src, dst, sem, times):
    for _ in range(times):
        pltpu.make_async_copy(src, dst.at[pl.ds(0, src.shape[0])], sem).wait()


def _dispatch_kernel(pend_ref, nu_ref, *refs, tt, tm, nt, nsub):
    dest_refs = refs[:TOP_K]
    x_ref, g_ref, xpad_ref, hbuf, zbuf, sems, zsems = refs[TOP_K:]
    i = pl.program_id(0)
    slot = i % 2
    last = pl.num_programs(0) - 1

    def seg_copy(e):
        start = pl.multiple_of(jnp.maximum(pend_ref[e] - tm, 0), tm)
        return pltpu.make_async_copy(zbuf, _tile_row(xpad_ref, start, nsub, tm), zsems.at[0])

    def tail_copy(j):
        return pltpu.make_async_copy(zbuf, _tile_row(xpad_ref, (nt - 1 - j) * tm, nsub, tm), zsems.at[1])

    @pl.when(i == 0)
    def _():
        zbuf[...] = jnp.zeros_like(zbuf)
        for e in range(N_EXPERTS):
            seg_copy(e).start()
        for j in range(N_EXPERTS):
            pl.when(nt - 1 - j >= nu_ref[0])(tail_copy(j).start)

    hb = hbuf.at[slot]
    _to_tile_rows(hb, _rms(x_ref[...], g_ref[...]))

    @pl.when(i == 0)
    def _():
        for e in range(N_EXPERTS):
            seg_copy(e).wait()

    @pl.when(i == last)
    def _():
        for j in range(N_EXPERTS):
            pl.when(nt - 1 - j >= nu_ref[0])(tail_copy(j).wait)

    def issue(t, carry):
        for k in range(TOP_K):
            d = dest_refs[k][t]
            pltpu.make_async_copy(_tile_row(hb, t, nsub), _tile_row(xpad_ref, d, nsub),
                                  sems.at[slot]).start(priority=k % 2)
        return carry

    lax.fori_loop(0, tt, issue, 0, unroll=ISSUE_UNROLL)

    @pl.when(i > 0)
    def _():
        _row_copy_wait(hbuf.at[1 - slot], xpad_ref, sems.at[1 - slot], TOP_K)

    @pl.when(i == pl.num_programs(0) - 1)
    def _():
        _row_copy_wait(hb, xpad_ref, sems.at[slot], TOP_K)


def _dispatch(x, g, dest, pend, n_used, P):
    N, D = x.shape
    tt = DISPATCH_TILE
    tm = MOE_TILE
    nsub = D // LANES
    grid_spec = pltpu.PrefetchScalarGridSpec(
        num_scalar_prefetch=2,
        grid=(N // tt,),
        in_specs=[pl.BlockSpec((tt,), lambda i, pe, nu, k=k: (k * (N // tt) + i,), memory_space=pltpu.SMEM)
                  for k in range(TOP_K)] + [
            pl.BlockSpec((tt, D), lambda i, pe, nu: (i, 0)),
            pl.BlockSpec((1, D), lambda i, pe, nu: (0, 0)),
        ],
        out_specs=pl.BlockSpec(memory_space=pl.ANY),
        scratch_shapes=[pltpu.VMEM((2, tt * nsub, LANES), F32), pltpu.VMEM((tm * nsub, LANES), F32),
                        pltpu.SemaphoreType.DMA((2,)), pltpu.SemaphoreType.DMA((2,))],
    )
    return pl.pallas_call(
        functools.partial(_dispatch_kernel, tt=tt, tm=tm, nt=P // tm, nsub=nsub),
        grid_spec=grid_spec,
        out_shape=jax.ShapeDtypeStruct((P * nsub, LANES), F32),
        compiler_params=_params(("arbitrary",)),
        name="moe_dispatch",
    )(pend, n_used, *([dest] * TOP_K), x, g.reshape(1, D))


def _expert_kernel(te_ref, nx_ref, nu_ref, x_ref, wgu_hbm, bgu_ref, wdn_hbm, bdn_ref, o_ref,
                   wgu_f32, wdn_f32, wgu_bf, wdn_bf, sems, *, layer, F, tm, nsub):
    i = pl.program_id(0)

    def weight_copies(e):
        return (pltpu.make_async_copy(wgu_hbm.at[layer, e], wgu_f32, sems.at[0]),
                pltpu.make_async_copy(wdn_hbm.at[layer, e], wdn_f32, sems.at[1]))

    @pl.when(i < nu_ref[0])
    def _():
        e = te_ref[i]

        @pl.when(i == 0)
        def _():
            for c in weight_copies(e):
                c.start()

        @pl.when((i == 0) | (e != te_ref[jnp.maximum(i - 1, 0)]))
        def _():
            for c in weight_copies(e):
                c.wait()
            wgu_bf[...] = wgu_f32[...].astype(BF16)
            wdn_bf[...] = wdn_f32[...].astype(BF16)
            nxt = nx_ref[i]

            @pl.when(nxt != e)
            def _():
                for c in weight_copies(nxt):
                    c.start()

        x = _from_tile_rows(x_ref, tm, nsub).astype(BF16)
        gu = jnp.dot(x, wgu_bf[...], preferred_element_type=F32) + bgu_ref[0]
        gate = jnp.minimum(gu[:, :F], SWIGLU_LIMIT)
        up = jnp.clip(gu[:, F:], -SWIGLU_LIMIT, SWIGLU_LIMIT)
        glu = gate * jax.nn.sigmoid(SWIGLU_ALPHA * gate)
        a = ((up + 1.0) * glu).astype(BF16)
        _to_tile_rows(o_ref, jnp.dot(a, wdn_bf[...], preferred_element_type=F32) + bdn_ref[0])

    @pl.when(i >= nu_ref[0])
    def _():
        o_ref[...] = jnp.zeros_like(o_ref)


def _experts(tile_e, next_e, n_used, x_pad, w_gu, b_gu, w_dn, b_dn, layer):
    _, E, D, F2 = w_gu.shape
    F = F2 // 2
    tm = MOE_TILE
    nsub = D // LANES
    P = x_pad.shape[0] // nsub
    row = lambda i, te, nx, nu: (jnp.minimum(i, nu[0] - 1), 0)
    exp3 = lambda i, te, nx, nu: (te[i], 0, 0)
    grid_spec = pltpu.PrefetchScalarGridSpec(
        num_scalar_prefetch=3,
        grid=(P // tm,),
        in_specs=[
            pl.BlockSpec((tm * nsub, LANES), row),
            pl.BlockSpec(memory_space=pl.ANY),
            pl.BlockSpec((1, 1, F2), exp3),
            pl.BlockSpec(memory_space=pl.ANY),
            pl.BlockSpec((1, 1, D), exp3),
        ],
        out_specs=pl.BlockSpec((tm * nsub, LANES), lambda i, te, nx, nu: (i, 0)),
        scratch_shapes=[pltpu.VMEM((D, F2), F32), pltpu.VMEM((F, D), F32),
                        pltpu.VMEM((D, F2), BF16), pltpu.VMEM((F, D), BF16),
                        pltpu.SemaphoreType.DMA((2,))],
    )
    return pl.pallas_call(
        functools.partial(_expert_kernel, layer=layer, F=F, tm=tm, nsub=nsub),
        grid_spec=grid_spec,
        out_shape=jax.ShapeDtypeStruct((P * nsub, LANES), F32),
        compiler_params=_params(("arbitrary",)),
        name="moe_experts",
    )(tile_e, next_e, n_used, x_pad, w_gu, b_gu.reshape(E, 1, F2), w_dn, b_dn.reshape(E, 1, D))


def _combine_kernel(*refs, tt, nsub):
    dest_refs = refs[:TOP_K]
    dnext_refs = refs[TOP_K:2 * TOP_K]
    x_ref, gate_ref, ypad_ref, o_ref, ybuf, sems = refs[2 * TOP_K:]
    i = pl.program_id(0)
    slot = i % 2

    def gather(idx_refs, s):
        def issue(t, carry):
            for k in range(TOP_K):
                d = idx_refs[k][t]
                pltpu.make_async_copy(_tile_row(ypad_ref, d, nsub), _tile_row(ybuf.at[s, k], t, nsub),
                                      sems.at[s]).start(priority=k % 2)
            return carry

        lax.fori_loop(0, tt, issue, 0, unroll=ISSUE_UNROLL)

    @pl.when(i == 0)
    def _():
        gather(dest_refs, slot)

    @pl.when(i + 1 < pl.num_programs(0))
    def _():
        gather(dnext_refs, 1 - slot)

    for k in range(TOP_K):
        _row_copy_wait(ybuf.at[slot, k], ypad_ref, sems.at[slot], 1)
    gate = gate_ref[...]
    x = x_ref[...]
    cols = []
    for c in range(nsub):
        acc = x[:, c * LANES:(c + 1) * LANES]
        for k in range(TOP_K):
            acc = acc + ybuf[slot, k, pl.ds(c, tt, stride=nsub), :] * gate[:, k:k + 1]
        cols.append(acc)
    o_ref[...] = jnp.concatenate(cols, axis=1)


def _combine(x, gate, dest, y_pad):
    N, D = x.shape
    tt = COMBINE_TILE
    nsub = D // LANES
    steps = N // tt
    last = steps - 1
    return pl.pallas_call(
        functools.partial(_combine_kernel, tt=tt, nsub=nsub),
        grid=(N // tt,),
        in_specs=[pl.BlockSpec((tt,), lambda i, k=k: (k * steps + i,), memory_space=pltpu.SMEM)
                  for k in range(TOP_K)] + [
            pl.BlockSpec((tt,), lambda i, k=k: (k * steps + jnp.minimum(i + 1, last),),
                         memory_space=pltpu.SMEM) for k in range(TOP_K)] + [
            pl.BlockSpec((tt, D), lambda i: (i, 0)),
            pl.BlockSpec((tt, TOP_K), lambda i: (i, 0)),
            pl.BlockSpec(memory_space=pl.ANY),
        ],
        out_specs=pl.BlockSpec((tt, D), lambda i: (i, 0)),
        out_shape=jax.ShapeDtypeStruct((N, D), F32),
        scratch_shapes=[pltpu.VMEM((2, TOP_K, tt * nsub, LANES), F32), pltpu.SemaphoreType.DMA((2,))],
        compiler_params=_params(("arbitrary",)),
        name="moe_combine",
    )(*([dest] * (2 * TOP_K)), x, gate, y_pad)


def _dest_kernel(pstart_ref, e_ref, rank_ref, o_ref):
    e = e_ref[...]
    seg = jnp.zeros(e.shape, I32)
    for ex in range(N_EXPERTS):
        seg = jnp.where(e == ex, pstart_ref[ex], seg)
    o_ref[...] = seg + rank_ref[...]


def _dest_rows(e_idx, rank, pstart):
    return pl.pallas_call(
        _dest_kernel,
        in_specs=[pl.BlockSpec(memory_space=pltpu.SMEM), pl.BlockSpec(e_idx.shape, lambda: (0, 0)),
                  pl.BlockSpec(rank.shape, lambda: (0, 0))],
        out_specs=pl.BlockSpec(rank.shape, lambda: (0, 0)),
        out_shape=jax.ShapeDtypeStruct(rank.shape, I32),
        name="moe_dest",
    )(pstart, e_idx, rank)


def _moe_layer(x, g, w_r, b_r, w_gu, b_gu, w_dn, b_dn, layer):
    N, D = x.shape
    E = N_EXPERTS
    tm = MOE_TILE
    e_idx, gate_t, rank, counts = _router(x, g, w_r, b_r)
    counts = counts[:, 0].astype(I32)
    padded = ((counts + tm - 1) // tm) * tm
    pend = jnp.cumsum(padded)
    pstart = pend - padded
    dest = _dest_rows(e_idx, rank, pstart.astype(I32)).reshape(-1)
    gate = gate_t.T
    P = ((N * TOP_K + E * (tm - 1) + tm - 1) // tm) * tm
    nt = P // tm
    tiles = jnp.arange(nt, dtype=I32)
    tile_e = jnp.minimum(jnp.sum(tiles[:, None] * tm >= pend[None, :], axis=1), E - 1).astype(I32)
    n_used = (pend[-1] // tm).astype(I32)
    experts = jnp.arange(E, dtype=I32)
    used = counts > 0
    tile_e = jnp.where(tiles < n_used, tile_e, jnp.max(jnp.where(used, experts, 0)))
    after = jnp.min(jnp.where((experts[None, :] > experts[:, None]) & used[None, :], experts[None, :], E), axis=1)
    after = jnp.where(after < E, after, experts)
    next_e = jnp.sum(jnp.where(tile_e[:, None] == experts[None, :], after[None, :], 0), axis=1)
    x_pad = _dispatch(x, g, dest, pend.astype(I32), n_used.reshape(1), P)
    y_pad = _experts(tile_e, next_e.astype(I32), n_used.reshape(1), x_pad, w_gu, b_gu, w_dn, b_dn, layer)
    return _combine(x, gate, dest, y_pad)


def _bucket_maps():
    qi = np.arange(ATT_BLOCK)[None, :]
    kj = np.arange(2 * ATT_BLOCK)[:, None]
    delta = qi + ATT_BLOCK - kj
    buckets, valids = [], []
    max_exact = N_BUCKETS // 2
    for win, dil in DILATED_GROUPS:
        n = np.maximum(delta * dil, 0)
        nf = np.maximum(n, 1).astype(np.float32)
        large = max_exact + (np.log(nf / np.float32(max_exact)) / np.float32(math.log(REL_MAX_DIST / max_exact))
                             * np.float32(N_BUCKETS - max_exact)).astype(np.int32)
        large = np.minimum(large, N_BUCKETS - 1)
        buckets.append(np.where(n < max_exact, n, large).astype(np.int32))
        valids.append(((delta >= 0) & (delta <= win // dil)).astype(np.int32))
    return np.stack(buckets), np.stack(valids)


def _bias_kernel(tab_ref, bm_ref, valid_ref, o_ref):
    g = pl.program_id(0)
    bm = bm_ref[0]
    key = lax.broadcasted_iota(I32, bm.shape, 0)

    def one_head(h, carry):
        acc = jnp.zeros(bm.shape, F32)
        for b in range(N_BUCKETS):
            acc = jnp.where(bm == b, tab_ref[b, g * ATT_HEADS + h], acc)
        band = jnp.where(valid_ref[0] > 0, acc * LOG2_E, NEG_INF)
        o_ref[h, 0] = band
        o_ref[h, 1] = jnp.where(key >= ATT_BLOCK, band, NEG_INF)
        return carry

    lax.fori_loop(0, ATT_HEADS, one_head, 0)


def _bias_tables(rel_bias):
    bm, valid = _bucket_maps()
    GH = len(DILATED_GROUPS) * ATT_HEADS
    blk = (1, 2 * ATT_BLOCK, ATT_BLOCK)
    return pl.pallas_call(
        _bias_kernel,
        grid=(len(DILATED_GROUPS),),
        in_specs=[
            pl.BlockSpec(memory_space=pltpu.SMEM),
            pl.BlockSpec(blk, lambda i: (i, 0, 0)),
            pl.BlockSpec(blk, lambda i: (i, 0, 0)),
        ],
        out_specs=pl.BlockSpec((ATT_HEADS, 2, 2 * ATT_BLOCK, ATT_BLOCK), lambda i: (i, 0, 0, 0)),
        out_shape=jax.ShapeDtypeStruct((GH, 2, 2 * ATT_BLOCK, ATT_BLOCK), F32),
        compiler_params=_params(("arbitrary",)),
        name="attn_bias",
    )(rel_bias, jnp.asarray(bm), jnp.asarray(valid))


def _cast_kernel(w_ref, o_ref):
    o_ref[...] = w_ref[...].astype(BF16)


def _to_bf16(w, layer, col_block):
    _, rows, cols = w.shape
    return pl.pallas_call(
        _cast_kernel,
        grid=(cols // col_block,),
        in_specs=[pl.BlockSpec((None, rows, col_block), lambda j: (layer, 0, j))],
        out_specs=pl.BlockSpec((rows, col_block), lambda j: (0, j)),
        out_shape=jax.ShapeDtypeStruct((rows, cols), BF16),
        compiler_params=_params(("parallel",)),
        name="cast_bf16",
    )(w)


def _qkv_kernel(x_ref, g_ref, wq_ref, wk_ref, wv_ref, qg_ref, kg_ref, qt_ref, k_ref, vt_ref,
                slab_ref, *, dil, R, nl):
    rc = pl.program_id(2)
    D = x_ref.shape[2]
    lanes = slab_ref.shape[2]

    if dil == 1:
        x = x_ref[0]
    else:
        @pl.when(rc == 0)
        def _():
            for c in range(D // lanes):
                slab_ref[c] = x_ref[0, :, c * lanes:(c + 1) * lanes]

        pieces = []
        for j in range(R):
            r = rc * R + j
            cols = [slab_ref[c, pl.ds(r, nl, stride=dil), :] for c in range(D // lanes)]
            pieces.append(jnp.concatenate(cols, axis=1))
        x = jnp.concatenate(pieces, axis=0)
    h = _rms(x, g_ref[...]).astype(BF16)
    q = jnp.dot(h, wq_ref[...], preferred_element_type=F32)
    k = jnp.dot(h, wk_ref[...], preferred_element_type=F32)
    v = jnp.dot(h, wv_ref[...], preferred_element_type=F32)
    blk = ATT_BLOCK

    def normed_tile(t, gain_ref, cs):
        heads = []
        for hh in range(t.shape[0] // HEAD_DIM):
            th = t[hh * HEAD_DIM:(hh + 1) * HEAD_DIM]
            ms = jnp.mean(th * th, axis=0, keepdims=True)
            heads.append(th * lax.rsqrt(ms + EPS))
        return jnp.concatenate(heads, axis=0) * gain_ref[cs, :]

    for j in range(R):
        for c in range(nl // blk):
            rows = slice(j * nl + c * blk, j * nl + (c + 1) * blk)
            for hp in range(q.shape[1] // blk):
                cs = slice(hp * blk, (hp + 1) * blk)
                qt_ref[0, j, cs, c * blk:(c + 1) * blk] = normed_tile(q[rows, cs].T, qg_ref, cs).astype(BF16)
                k_ref[0, j, c * blk:(c + 1) * blk, cs] = normed_tile(k[rows, cs].T, kg_ref, cs).T.astype(BF16)
                vt_ref[0, j, cs, c * blk:(c + 1) * blk] = v[rows, cs].T.astype(BF16)


def _qkv_group(x, g, w_bf, q_gain, k_gain, gi, dil):
    B, S, D = x.shape
    HD = ATT_HEADS * HEAD_DIM
    G = len(DILATED_GROUPS)
    L = S // dil
    R = min(dil, QKV_ROWS // ATT_BLOCK)
    nl = QKV_ROWS // R
    lanes = LANES
    rows_of = lambda gain: jnp.broadcast_to(jnp.tile(gain, ATT_HEADS)[:, None], (HD, lanes))
    q_rows = rows_of(q_gain * (HEAD_DIM ** -0.5 * LOG2_E))
    k_rows = rows_of(k_gain)
    wspec = lambda s: pl.BlockSpec((D, HD), lambda b, l, r: (0, s * G + gi))
    tspec = pl.BlockSpec((1, R, HD, nl), lambda b, l, r: (b, r, 0, l))
    tshape = jax.ShapeDtypeStruct((B, dil, HD, L), BF16)
    return pl.pallas_call(
        functools.partial(_qkv_kernel, dil=dil, R=R, nl=nl),
        grid=(B, L // nl, dil // R),
        in_specs=[
            pl.BlockSpec((1, nl * dil, D), lambda b, l, r: (b, l, 0)),
            pl.BlockSpec((1, D), lambda b, l, r: (0, 0)),
            wspec(0), wspec(1), wspec(2),
            pl.BlockSpec((HD, lanes), lambda b, l, r: (0, 0)),
            pl.BlockSpec((HD, lanes), lambda b, l, r: (0, 0)),
        ],
        out_specs=[tspec, pl.BlockSpec((1, R, nl, HD), lambda b, l, r: (b, r, l, 0)), tspec],
        out_shape=[tshape, jax.ShapeDtypeStruct((B, dil, L, HD), BF16), tshape],
        scratch_shapes=[pltpu.VMEM((D // lanes, nl * dil if dil > 1 else 8, lanes), F32)],
        compiler_params=_params(("arbitrary", "arbitrary", "arbitrary")),
        name=f"attn_qkv_g{gi}",
    )(x, g.reshape(1, D), w_bf, w_bf, w_bf, q_rows, k_rows)


def _attn_kernel(*refs, single, nitems):
    if single:
        qt_ref, kc_ref, vtc_ref, bias_ref, o_ref, l_ref, s_ref, p_ref = refs
    else:
        qt_ref, kc_ref, kp_ref, vtc_ref, vtp_ref, bias_ref, o_ref, l_ref, s_ref, p_ref = refs
    blk = ATT_BLOCK
    pair = 2 * HEAD_DIM
    first = jnp.where(pl.program_id(2) == 0, 1, 0)
    if single:
        k0 = blk
        variants = [1] * nitems
        q_of = lambda it, cs: qt_ref[0, it, cs, :]
        k_of = lambda it, cs: kc_ref[0, it, :, cs]
        vt_of = lambda it, cs: vtc_ref[0, it, cs, :]
    else:
        k0 = 0
        variants = [first] + [0] * (nitems - 1)

        def q_of(it, cs):
            return qt_ref[0, 0, cs, it * blk:(it + 1) * blk]

        def k_of(it, cs):
            if it == 0:
                return jnp.concatenate([kp_ref[0, 0, :, cs], kc_ref[0, 0, :blk, cs]], axis=0)
            return kc_ref[0, 0, (it - 1) * blk:(it + 1) * blk, cs]

        def vt_of(it, cs):
            if it == 0:
                return jnp.concatenate([vtp_ref[0, 0, cs, :], vtc_ref[0, 0, cs, :blk]], axis=1)
            return vtc_ref[0, 0, cs, (it - 1) * blk:(it + 1) * blk]

    feat = lax.broadcasted_iota(I32, (pair, blk), 0)
    head_rows = [jnp.where(feat < HEAD_DIM, 1.0, 0.0).astype(BF16),
                 jnp.where(feat < HEAD_DIM, 0.0, 1.0).astype(BF16)]
    invs = {}
    lse_rows = {}

    def scores(it, pr):
        cs = slice(pr * pair, (pr + 1) * pair)
        qt2 = q_of(it, cs)
        k2 = k_of(it, cs)
        for half in range(2):
            qth = qt2 * head_rows[half]
            s_ref[it * ATT_HEADS + 2 * pr + half] = jnp.dot(k2, qth, preferred_element_type=F32)

    def softmax(it, pr):
        for head in (2 * pr, 2 * pr + 1):
            s = s_ref[it * ATT_HEADS + head] + bias_ref[head, variants[it], k0:, :]
            m = jnp.max(s, axis=0, keepdims=True)
            p = jnp.exp2(s - m)
            den = jnp.sum(p, axis=0, keepdims=True)
            p_ref[it * ATT_HEADS + head] = p.astype(BF16)
            invs[it, head] = 1.0 / den
            lse_rows[it, head] = (m + jnp.log2(den)) * (1.0 / LOG2_E)

    def outputs(it, pr):
        cs = slice(pr * pair, (pr + 1) * pair)
        vt2 = vt_of(it, cs)
        halves = []
        for half in range(2):
            head = 2 * pr + half
            vth = vt2[half * HEAD_DIM:(half + 1) * HEAD_DIM]
            ot = jnp.dot(vth, p_ref[it * ATT_HEADS + head], preferred_element_type=F32)
            halves.append(ot * invs[it, head])
        val = jnp.concatenate(halves, axis=0).T
        if single:
            o_ref[0, it, :, cs] = val
        else:
            o_ref[0, 0, it * blk:(it + 1) * blk, cs] = val

    work = [(it, pr) for it in range(nitems) for pr in range(ATT_HEADS // 2)]
    for it, pr in work:
        scores(it, pr)
    for it, pr in work:
        softmax(it, pr)
    for it, pr in work:
        outputs(it, pr)
    lse_row = lax.broadcasted_iota(I32, (blk, blk), 0)
    for it in range(nitems):
        lse_t = jnp.zeros((blk, blk), F32)
        for head in range(ATT_HEADS):
            lse_t = jnp.where(lse_row == head, lse_rows[it, head], lse_t)
        if single:
            l_ref[0, it] = lse_t.T
        else:
            l_ref[0, 0, it * blk:(it + 1) * blk, :] = lse_t.T


def _attn_group(qt, k, vt, bias, gi):
    B, dil, L, HD = k.shape
    blk = ATT_BLOCK
    nb = L // blk
    single = nb == 1
    ni = min(ATT_ITEMS, dil if single else nb)
    bias_spec = pl.BlockSpec((ATT_HEADS, 2, 2 * blk, blk), lambda b, r, n: (gi, 0, 0, 0))
    if single:
        rows = pl.BlockSpec((1, ni, blk, HD), lambda b, r, n: (b, r, 0, 0))
        cols = pl.BlockSpec((1, ni, HD, blk), lambda b, r, n: (b, r, 0, 0))
        grid = (B, dil // ni, 1)
        in_specs = [cols, rows, cols, bias_spec]
        args = (qt, k, vt, bias)
        out_specs = [rows, pl.BlockSpec((1, ni, blk, blk), lambda b, r, n: (b, r, 0, 0))]
        nk = blk
    else:
        rows = pl.BlockSpec((1, 1, ni * blk, HD), lambda b, r, n: (b, r, n, 0))
        cols = pl.BlockSpec((1, 1, HD, ni * blk), lambda b, r, n: (b, r, 0, n))
        prow = pl.BlockSpec((1, 1, blk, HD), lambda b, r, n: (b, r, jnp.maximum(ni * n - 1, 0), 0))
        pcol = pl.BlockSpec((1, 1, HD, blk), lambda b, r, n: (b, r, 0, jnp.maximum(ni * n - 1, 0)))
        grid = (B, dil, nb // ni)
        in_specs = [cols, rows, prow, cols, pcol, bias_spec]
        args = (qt, k, k, vt, vt, bias)
        out_specs = [rows, pl.BlockSpec((1, 1, ni * blk, blk), lambda b, r, n: (b, r, n, 0))]
        nk = 2 * blk
    return pl.pallas_call(
        functools.partial(_attn_kernel, single=single, nitems=ni),
        grid=grid,
        in_specs=in_specs,
        out_specs=out_specs,
        out_shape=[jax.ShapeDtypeStruct((B, dil, L, HD), F32),
                   jax.ShapeDtypeStruct((B, dil, L, blk), F32)],
        scratch_shapes=[pltpu.VMEM((ni * ATT_HEADS, nk, blk), F32),
                        pltpu.VMEM((ni * ATT_HEADS, nk, blk), BF16)],
        compiler_params=_params(("parallel", "parallel", "arbitrary")),
        name=f"attn_core_g{gi}",
    )(*args)


def _attn_out_kernel(x_ref, o0, o1, o2, l0, l1, l2, wo_ref, out_ref, oslab, lslab, ex_ref, *, T):
    lanes = oslab.shape[3]
    HD = o0.shape[3]

    @pl.when((pl.program_id(0) == 0) & (pl.program_id(1) == 0))
    def _():
        h = lax.broadcasted_iota(I32, ex_ref.shape, 0) & (ATT_BLOCK - 1)
        c = lax.broadcasted_iota(I32, ex_ref.shape, 1)
        owner = lax.shift_right_logical(c, HEAD_DIM.bit_length() - 1)
        ex_ref[...] = jnp.where(h == owner, 1.0, 0.0).astype(BF16)

    def token_order(o_ref, l_ref, dil, s):
        if dil == 1:
            return o_ref[0, 0], l_ref[0, 0]
        n = T // dil
        for r in range(dil):
            blk = o_ref[0, r]
            for c in range(HD // lanes):
                oslab[s, c, pl.ds(r, n, stride=dil), :] = blk[:, c * lanes:(c + 1) * lanes]
            lslab[s, pl.ds(r, n, stride=dil), :] = l_ref[0, r]
        o = jnp.concatenate([oslab[s, c] for c in range(HD // lanes)], axis=1)
        return o, lslab[s]

    groups = [token_order(o_ref, l_ref, dil, s)
              for s, ((_, dil), o_ref, l_ref) in enumerate(zip(DILATED_GROUPS, (o0, o1, o2), (l0, l1, l2)))]
    m = jnp.maximum(jnp.maximum(groups[0][1], groups[1][1]), groups[2][1])
    es = [jnp.exp(l - m) for _, l in groups]
    inv = 1.0 / (es[0] + es[1] + es[2])
    att = jnp.zeros((T, HD), F32)
    for (o, _), e in zip(groups, es):
        a = e * inv
        a_hi = a.astype(BF16)
        a_lo = (a - a_hi.astype(F32)).astype(BF16)
        wide = jnp.dot(jnp.concatenate([a_hi, a_lo], axis=1), ex_ref[...], preferred_element_type=F32)
        att = att + wide * o
    out_ref[0] = x_ref[0] + jnp.dot(att.astype(BF16), wo_ref[...], preferred_element_type=F32)


def _attn_out(x, os, ls, wo_bf):
    B, S, D = x.shape
    HD = wo_bf.shape[0]
    T = OUT_TILE
    lanes = LANES
    ospec = lambda dil, w: pl.BlockSpec((1, dil, T // dil, w), lambda b, i: (b, 0, i, 0))
    dils = [dil for _, dil in DILATED_GROUPS]
    xspec = pl.BlockSpec((1, T, D), lambda b, i: (b, i, 0))
    return pl.pallas_call(
        functools.partial(_attn_out_kernel, T=T),
        grid=(B, S // T),
        in_specs=[xspec] + [ospec(d, HD) for d in dils] + [ospec(d, ATT_BLOCK) for d in dils]
                 + [pl.BlockSpec((HD, D), lambda b, i: (0, 0))],
        out_specs=xspec,
        out_shape=jax.ShapeDtypeStruct((B, S, D), F32),
        scratch_shapes=[pltpu.VMEM((len(dils), HD // lanes, T, lanes), F32),
                        pltpu.VMEM((len(dils), T, ATT_BLOCK), F32),
                        pltpu.VMEM((2 * ATT_BLOCK, HD), BF16)],
        compiler_params=_params(("arbitrary", "arbitrary")),
        name="attn_out",
    )(x, *os, *ls, wo_bf)


def _attention_layer(x, g, w_qkv, q_gain, k_gain, w_o, rel_bias, layer):
    HD = ATT_HEADS * HEAD_DIM
    bias = _bias_tables(rel_bias)
    wqkv_bf = _to_bf16(w_qkv, layer, HD)
    wo_bf = _to_bf16(w_o, layer, w_o.shape[2])
    os, ls = [], []
    for gi, (_, dil) in enumerate(DILATED_GROUPS):
        qt, k, vt = _qkv_group(x, g, wqkv_bf, q_gain, k_gain, gi, dil)
        o, l = _attn_group(qt, k, vt, bias, gi)
        os.append(o)
        ls.append(l)
    return _attn_out(x, os, ls, wo_bf)


def kernel(x, norm_mix_g, norm_ffn_g, pool_w, pool_scale, attn_w_qkv, attn_q_gain, attn_k_gain,
           attn_w_o, rel_bias, moe_w_router, moe_b_router, moe_w_gate_up, moe_b_gate_up,
           moe_w_down, moe_b_down):
    B, S, D = x.shape
    depth = norm_mix_g.shape[0]
    for i in range(depth):
        j = i // 2
        if i % 2 == 0:
            x = _pool_layer(x, norm_mix_g[i], pool_w[j], pool_scale[j])
        else:
            x = _attention_layer(x, norm_mix_g[i], attn_w_qkv, attn_q_gain[j], attn_k_gain[j],
                                 attn_w_o, rel_bias, j)
        x = _moe_layer(x.reshape(B * S, D), norm_ffn_g[i], moe_w_router[i], moe_b_router[i],
                       moe_w_gate_up, moe_b_gate_up[i], moe_w_down, moe_b_down[i],
                       i).reshape(B, S, D)
    return x
```

```python
import functools
import math

import numpy as np
import jax
import jax.numpy as jnp
from jax import lax
from jax.experimental import pallas as pl
from jax.experimental.pallas import tpu as pltpu

F32 = jnp.float32
BF16 = jnp.bfloat16
I32 = jnp.int32

EPS = 1e-6
POOL_WINDOWS = (2, 4, 8, 16)
POOL_HALO = max(POOL_WINDOWS)
DILATED_GROUPS = ((128, 1), (512, 4), (2048, 16))
ATT_HEADS = 16
HEAD_DIM = 64
ATT_BLOCK = 128
ATT_ITEMS = 8
N_BUCKETS = 32
REL_MAX_DIST = 2048
NEG_INF = -1e30
LOG2_E = math.log2(math.e)
N_EXPERTS = 32
TOP_K = 4
SWIGLU_LIMIT = 7.0
SWIGLU_ALPHA = 1.702

V7X_VMEM_LIMIT_BYTES = 56 * 1024 * 1024
ISSUE_UNROLL = 64
MOE_TILE = 512
DISPATCH_TILE = 1024
COMBINE_TILE = 256
COMBINE_ROWS = 32
ROUTER_TILE = 1024
POOL_TILE = 1024
QKV_ROWS = 1024
LANES = 128
OUT_TILE = 512


def _rms(xf, g):
    ms = jnp.mean(xf * xf, axis=-1, keepdims=True)
    return xf * lax.rsqrt(ms + EPS) * g


def _params(sem):
    return pltpu.CompilerParams(dimension_semantics=sem, vmem_limit_bytes=V7X_VMEM_LIMIT_BYTES)


def _pool_kernel(x_ref, halo_ref, g_ref, w_ref, sc_ref, o_ref, *, ts, dg):
    i = pl.program_id(1)
    x = x_ref[0]
    g = g_ref[...]
    h = _rms(x, g)
    hh = _rms(halo_ref[0], g)
    hh = jnp.where(i == 0, 0.0, hh)
    full = jnp.concatenate([hh, h], axis=0)
    pos = i * ts + lax.broadcasted_iota(I32, (ts, 1), 0)
    outs = []
    for gi, w in enumerate(POOL_WINDOWS):
        s = full[:, gi * dg:(gi + 1) * dg]
        sh = 1
        while sh < w:
            s = s + pltpu.roll(s, sh, 0)
            sh *= 2
        s = s[POOL_HALO:]
        cnt = jnp.minimum(pos + 1, w).astype(F32)
        p = s / cnt - h[:, gi * dg:(gi + 1) * dg]
        outs.append(jnp.dot(p.astype(BF16), w_ref[gi].astype(BF16),
                            preferred_element_type=F32))
    y = jnp.concatenate(outs, axis=1) * sc_ref[...]
    o_ref[0] = x + y


def _pool_layer(x, g, w_groups, scale):
    B, S, D = x.shape
    ts = POOL_TILE
    dg = D // len(POOL_WINDOWS)
    hb = ts // POOL_HALO
    return pl.pallas_call(
        functools.partial(_pool_kernel, ts=ts, dg=dg),
        grid=(B, S // ts),
        in_specs=[
            pl.BlockSpec((1, ts, D), lambda b, i: (b, i, 0)),
            pl.BlockSpec((1, POOL_HALO, D), lambda b, i: (b, jnp.maximum(i * hb - 1, 0), 0)),
            pl.BlockSpec((1, D), lambda b, i: (0, 0)),
            pl.BlockSpec((len(POOL_WINDOWS), dg, dg), lambda b, i: (0, 0, 0)),
            pl.BlockSpec((1, D), lambda b, i: (0, 0)),
        ],
        out_specs=pl.BlockSpec((1, ts, D), lambda b, i: (b, i, 0)),
        out_shape=jax.ShapeDtypeStruct((B, S, D), F32),
        compiler_params=_params(("parallel", "parallel")),
        name="pool_mixer",
    )(x, x, g.reshape(1, D), w_groups, scale.reshape(1, D))


def _router_kernel(x_ref, g_ref, wr_ref, br_ref, e_ref, gate_ref, rank_ref, cnt_ref,
                   tri_ref, carry_ref, wcat_ref, *, tt):
    i = pl.program_id(0)
    E = N_EXPERTS

    @pl.when(i == 0)
    def _():
        r = lax.broadcasted_iota(I32, tri_ref.shape, 0)
        c = lax.broadcasted_iota(I32, tri_ref.shape, 1)
        tri_ref[...] = jnp.where(r < c, 1.0, 0.0).astype(BF16)
        carry_ref[...] = jnp.zeros_like(carry_ref)
        w = wr_ref[...]
        w_hi = w.astype(BF16)
        wcat_ref[:, :LANES] = w_hi
        wcat_ref[:, LANES:] = (w - w_hi.astype(F32)).astype(BF16)

    h = _rms(x_ref[...], g_ref[...])
    h_hi = h.astype(BF16)
    h_lo = (h - h_hi.astype(F32)).astype(BF16)
    both = jnp.dot(h_hi, wcat_ref[...], preferred_element_type=F32)
    cross = jnp.dot(h_lo, wcat_ref[:, :LANES], preferred_element_type=F32)
    logits = both[:, :LANES] + (both[:, LANES:] + cross) + br_ref[...]
    nblk = tt // LANES
    l = jnp.concatenate([logits[c * LANES:(c + 1) * LANES].T for c in range(nblk)], axis=1)[:E]
    row = lax.broadcasted_iota(I32, (E, tt), 0).astype(F32)
    vals, sels, idxs = [], [], []
    for k in range(TOP_K):
        m = jnp.max(l, axis=0, keepdims=True)
        idx = jnp.min(jnp.where(l == m, row, float(E)), axis=0, keepdims=True)
        sel = row == idx
        vals.append(m)
        sels.append(sel)
        idxs.append(idx)
        l = jnp.where(sel, -jnp.inf, l)
    ex = [jnp.exp(v - vals[0]) for v in vals]
    den = ex[0] + ex[1] + ex[2] + ex[3]
    multi = jnp.zeros((E, tt), F32)
    for sel in sels:
        multi = multi + jnp.where(sel, 1.0, 0.0)
    base = carry_ref[:, :1]
    parts = []
    for c in range(nblk):
        mc = multi[:, c * LANES:(c + 1) * LANES]
        parts.append(jnp.dot(mc.astype(BF16), tri_ref[...], preferred_element_type=F32) + base)
        base = base + jnp.sum(mc, axis=1, keepdims=True)
    before = jnp.concatenate(parts, axis=1)
    kk = lax.broadcasted_iota(I32, (TOP_K, tt), 0)
    e_out = jnp.zeros((TOP_K, tt), F32)
    g_out = jnp.zeros((TOP_K, tt), F32)
    r_out = jnp.zeros((TOP_K, tt), F32)
    for k in range(TOP_K):
        e_out = jnp.where(kk == k, idxs[k], e_out)
        g_out = jnp.where(kk == k, ex[k] / den, g_out)
        rk = jnp.sum(jnp.where(sels[k], before, 0.0), axis=0, keepdims=True)
        r_out = jnp.where(kk == k, rk, r_out)
    e_ref[...] = e_out.astype(I32)
    gate_ref[...] = g_out
    rank_ref[...] = r_out.astype(I32)
    carry_ref[...] = jnp.broadcast_to(base, carry_ref.shape)
    cnt_ref[...] = carry_ref[...]


def _router(x, g, w_r, b_r):
    N, D = x.shape
    tt = ROUTER_TILE
    E = N_EXPERTS
    w_pad = jnp.pad(w_r, ((0, 0), (0, LANES - E)))
    b_pad = jnp.pad(b_r, (0, LANES - E)).reshape(1, LANES)
    kspec = pl.BlockSpec((TOP_K, tt), lambda i: (0, i))
    return pl.pallas_call(
        functools.partial(_router_kernel, tt=tt),
        grid=(N // tt,),
        in_specs=[
            pl.BlockSpec((tt, D), lambda i: (i, 0)),
            pl.BlockSpec((1, D), lambda i: (0, 0)),
            pl.BlockSpec((D, LANES), lambda i: (0, 0)),
            pl.BlockSpec((1, LANES), lambda i: (0, 0)),
        ],
        out_specs=[kspec, kspec, kspec, pl.BlockSpec((E, LANES), lambda i: (0, 0))],
        out_shape=[
            jax.ShapeDtypeStruct((TOP_K, N), I32),
            jax.ShapeDtypeStruct((TOP_K, N), F32),
            jax.ShapeDtypeStruct((TOP_K, N), I32),
            jax.ShapeDtypeStruct((E, LANES), F32),
        ],
        scratch_shapes=[pltpu.VMEM((LANES, LANES), BF16), pltpu.VMEM((E, LANES), F32),
                        pltpu.VMEM((D, 2 * LANES), BF16)],
        compiler_params=_params(("arbitrary",)),
        name="moe_router",
    )(x, g.reshape(1, D), w_pad, b_pad)


def _to_tile_rows(ref, val):
    n, D = val.shape
    nsub = D // LANES
    for s in range(nsub):
        ref[pl.ds(s, n, stride=nsub), :] = val[:, s * LANES:(s + 1) * LANES]


def _from_tile_rows(ref, n, nsub):
    return jnp.concatenate([ref[pl.ds(s, n, stride=nsub), :] for s in range(nsub)], axis=1)


def _tile_row(ref, r, nsub, count=1):
    return ref.at[pl.ds(pl.multiple_of(r * nsub, nsub), count * nsub)]


def _row_copy_wait(src, dst, sem, times):
    for _ in range(times):
        pltpu.make_async_copy(src, dst.at[pl.ds(0, src.shape[0])], sem).wait()


def _dispatch_kernel(pend_ref, nu_ref, *refs, tt, tm, nt, nsub):
    dest_refs = refs[:TOP_K]
    x_ref, g_ref, xpad_ref, hbuf, zbuf, sems, zsems = refs[TOP_K:]
    i = pl.program_id(0)
    slot = i % 2
    last = pl.num_programs(0) - 1

    def seg_copy(e):
        start = pl.multiple_of(jnp.maximum(pend_ref[e] - tm, 0), tm)
        return pltpu.make_async_copy(zbuf, _tile_row(xpad_ref, start, nsub, tm), zsems.at[0])

    def tail_copy(j):
        return pltpu.make_async_copy(zbuf, _tile_row(xpad_ref, (nt - 1 - j) * tm, nsub, tm), zsems.at[1])

    @pl.when(i == 0)
    def _():
        zbuf[...] = jnp.zeros_like(zbuf)
        for e in range(N_EXPERTS):
            seg_copy(e).start()
        for j in range(N_EXPERTS):
            pl.when(nt - 1 - j >= nu_ref[0])(tail_copy(j).start)

    hb = hbuf.at[slot]
    _to_tile_rows(hb, _rms(x_ref[...], g_ref[...]))

    @pl.when(i == 0)
    def _():
        for e in range(N_EXPERTS):
            seg_copy(e).wait()

    @pl.when(i == last)
    def _():
        for j in range(N_EXPERTS):
            pl.when(nt - 1 - j >= nu_ref[0])(tail_copy(j).wait)

    def issue(t, carry):
        for k in range(TOP_K):
            d = dest_refs[k][t]
            pltpu.make_async_copy(_tile_row(hb, t, nsub), _tile_row(xpad_ref, d, nsub),
                                  sems.at[slot]).start(priority=k % 2)
        return carry

    lax.fori_loop(0, tt, issue, 0, unroll=ISSUE_UNROLL)

    @pl.when(i > 0)
    def _():
        _row_copy_wait(hbuf.at[1 - slot], xpad_ref, sems.at[1 - slot], TOP_K)

    @pl.when(i == pl.num_programs(0) - 1)
    def _():
        _row_copy_wait(hb, xpad_ref, sems.at[slot], TOP_K)


def _dispatch(x, g, dest, pend, n_used, P):
    N, D = x.shape
    tt = DISPATCH_TILE
    tm = MOE_TILE
    nsub = D // LANES
    grid_spec = pltpu.PrefetchScalarGridSpec(
        num_scalar_prefetch=2,
        grid=(N // tt,),
        in_specs=[pl.BlockSpec((tt,), lambda i, pe, nu, k=k: (k * (N // tt) + i,), memory_space=pltpu.SMEM)
                  for k in range(TOP_K)] + [
            pl.BlockSpec((tt, D), lambda i, pe, nu: (i, 0)),
            pl.BlockSpec((1, D), lambda i, pe, nu: (0, 0)),
        ],
        out_specs=pl.BlockSpec(memory_space=pl.ANY),
        scratch_shapes=[pltpu.VMEM((2, tt * nsub, LANES), F32), pltpu.VMEM((tm * nsub, LANES), F32),
                        pltpu.SemaphoreType.DMA((2,)), pltpu.SemaphoreType.DMA((2,))],
    )
    return pl.pallas_call(
        functools.partial(_dispatch_kernel, tt=tt, tm=tm, nt=P // tm, nsub=nsub),
        grid_spec=grid_spec,
        out_shape=jax.ShapeDtypeStruct((P * nsub, LANES), F32),
        compiler_params=_params(("arbitrary",)),
        name="moe_dispatch",
    )(pend, n_used, *([dest] * TOP_K), x, g.reshape(1, D))


def _expert_kernel(te_ref, nx_ref, nu_ref, x_ref, wgu_hbm, bgu_ref, wdn_hbm, bdn_ref, o_ref,
                   wgu_f32, wdn_f32, wgu_bf, wdn_bf, sems, *, layer, F, tm, nsub):
    i = pl.program_id(0)

    def weight_copies(e):
        return (pltpu.make_async_copy(wgu_hbm.at[layer, e], wgu_f32, sems.at[0]),
                pltpu.make_async_copy(wdn_hbm.at[layer, e], wdn_f32, sems.at[1]))

    @pl.when(i < nu_ref[0])
    def _():
        e = te_ref[i]

        @pl.when(i == 0)
        def _():
            for c in weight_copies(e):
                c.start()

        @pl.when((i == 0) | (e != te_ref[jnp.maximum(i - 1, 0)]))
        def _():
            for c in weight_copies(e):
                c.wait()
            wgu_bf[...] = wgu_f32[...].astype(BF16)
            wdn_bf[...] = wdn_f32[...].astype(BF16)
            nxt = nx_ref[i]

            @pl.when(nxt != e)
            def _():
                for c in weight_copies(nxt):
                    c.start()

        x = _from_tile_rows(x_ref, tm, nsub).astype(BF16)
        gu = jnp.dot(x, wgu_bf[...], preferred_element_type=F32) + bgu_ref[0]
        gate = jnp.minimum(gu[:, :F], SWIGLU_LIMIT)
        up = jnp.clip(gu[:, F:], -SWIGLU_LIMIT, SWIGLU_LIMIT)
        glu = gate * jax.nn.sigmoid(SWIGLU_ALPHA * gate)
        a = ((up + 1.0) * glu).astype(BF16)
        _to_tile_rows(o_ref, jnp.dot(a, wdn_bf[...], preferred_element_type=F32) + bdn_ref[0])

    @pl.when(i >= nu_ref[0])
    def _():
        o_ref[...] = jnp.zeros_like(o_ref)


def _experts(tile_e, next_e, n_used, x_pad, w_gu, b_gu, w_dn, b_dn, layer):
    _, E, D, F2 = w_gu.shape
    F = F2 // 2
    tm = MOE_TILE
    nsub = D // LANES
    P = x_pad.shape[0] // nsub
    row = lambda i, te, nx, nu: (jnp.minimum(i, nu[0] - 1), 0)
    exp3 = lambda i, te, nx, nu: (te[i], 0, 0)
    grid_spec = pltpu.PrefetchScalarGridSpec(
        num_scalar_prefetch=3,
        grid=(P // tm,),
        in_specs=[
            pl.BlockSpec((tm * nsub, LANES), row),
            pl.BlockSpec(memory_space=pl.ANY),
            pl.BlockSpec((1, 1, F2), exp3),
            pl.BlockSpec(memory_space=pl.ANY),
            pl.BlockSpec((1, 1, D), exp3),
        ],
        out_specs=pl.BlockSpec((tm * nsub, LANES), lambda i, te, nx, nu: (i, 0)),
        scratch_shapes=[pltpu.VMEM((D, F2), F32), pltpu.VMEM((F, D), F32),
                        pltpu.VMEM((D, F2), BF16), pltpu.VMEM((F, D), BF16),
                        pltpu.SemaphoreType.DMA((2,))],
    )
    return pl.pallas_call(
        functools.partial(_expert_kernel, layer=layer, F=F, tm=tm, nsub=nsub),
        grid_spec=grid_spec,
        out_shape=jax.ShapeDtypeStruct((P * nsub, LANES), F32),
        compiler_params=_params(("arbitrary",)),
        name="moe_experts",
    )(tile_e, next_e, n_used, x_pad, w_gu, b_gu.reshape(E, 1, F2), w_dn, b_dn.reshape(E, 1, D))


def _combine_kernel(*refs, tt, nsub):
    dest_refs = refs[:TOP_K]
    dnext_refs = refs[TOP_K:2 * TOP_K]
    x_ref, gate_ref, ypad_ref, o_ref, ybuf, sems = refs[2 * TOP_K:]
    i = pl.program_id(0)
    slot = i % 2

    def gather(idx_refs, s):
        def issue(t, carry):
            for k in range(TOP_K):
                d = idx_refs[k][t]
                pltpu.make_async_copy(_tile_row(ypad_ref, d, nsub), _tile_row(ybuf.at[s, k], t, nsub),
                                      sems.at[s]).start(priority=k % 2)
            return carry

        lax.fori_loop(0, tt, issue, 0, unroll=ISSUE_UNROLL)

    @pl.when(i == 0)
    def _():
        gather(dest_refs, slot)

    @pl.when(i + 1 < pl.num_programs(0))
    def _():
        gather(dnext_refs, 1 - slot)

    for k in range(TOP_K):
        _row_copy_wait(ybuf.at[slot, k], ypad_ref, sems.at[slot], 1)
    rows = COMBINE_ROWS
    for r0 in range(0, tt, rows):
        gates = [jnp.broadcast_to(gate_ref[r0:r0 + rows, k:k + 1], (rows, LANES)) for k in range(TOP_K)]
        for c in range(nsub):
            acc = x_ref[r0:r0 + rows, c * LANES:(c + 1) * LANES]
            for k in range(TOP_K):
                acc = acc + ybuf[slot, k, pl.ds(r0 * nsub + c, rows, stride=nsub), :] * gates[k]
            o_ref[r0:r0 + rows, c * LANES:(c + 1) * LANES] = acc


def _combine(x, gate, dest, y_pad):
    N, D = x.shape
    tt = COMBINE_TILE
    nsub = D // LANES
    steps = N // tt
    last = steps - 1
    return pl.pallas_call(
        functools.partial(_combine_kernel, tt=tt, nsub=nsub),
        grid=(N // tt,),
        in_specs=[pl.BlockSpec((tt,), lambda i, k=k: (k * steps + i,), memory_space=pltpu.SMEM)
                  for k in range(TOP_K)] + [
            pl.BlockSpec((tt,), lambda i, k=k: (k * steps + jnp.minimum(i + 1, last),),
                         memory_space=pltpu.SMEM) for k in range(TOP_K)] + [
            pl.BlockSpec((tt, D), lambda i: (i, 0)),
            pl.BlockSpec((tt, TOP_K), lambda i: (i, 0)),
            pl.BlockSpec(memory_space=pl.ANY),
        ],
        out_specs=pl.BlockSpec((tt, D), lambda i: (i, 0)),
        out_shape=jax.ShapeDtypeStruct((N, D), F32),
        scratch_shapes=[pltpu.VMEM((2, TOP_K, tt * nsub, LANES), F32), pltpu.SemaphoreType.DMA((2,))],
        compiler_params=_params(("arbitrary",)),
        name="moe_combine",
    )(*([dest] * (2 * TOP_K)), x, gate, y_pad)


def _dest_kernel(pstart_ref, e_ref, rank_ref, o_ref):
    e = e_ref[...]
    seg = jnp.zeros(e.shape, I32)
    for ex in range(N_EXPERTS):
        seg = jnp.where(e == ex, pstart_ref[ex], seg)
    o_ref[...] = seg + rank_ref[...]


def _dest_rows(e_idx, rank, pstart):
    return pl.pallas_call(
        _dest_kernel,
        in_specs=[pl.BlockSpec(memory_space=pltpu.SMEM), pl.BlockSpec(e_idx.shape, lambda: (0, 0)),
                  pl.BlockSpec(rank.shape, lambda: (0, 0))],
        out_specs=pl.BlockSpec(rank.shape, lambda: (0, 0)),
        out_shape=jax.ShapeDtypeStruct(rank.shape, I32),
        name="moe_dest",
    )(pstart, e_idx, rank)


def _moe_layer(x, g, w_r, b_r, w_gu, b_gu, w_dn, b_dn, layer):
    N, D = x.shape
    E = N_EXPERTS
    tm = MOE_TILE
    e_idx, gate_t, rank, counts = _router(x, g, w_r, b_r)
    counts = counts[:, 0].astype(I32)
    padded = ((counts + tm - 1) // tm) * tm
    pend = jnp.cumsum(padded)
    pstart = pend - padded
    dest = _dest_rows(e_idx, rank, pstart.astype(I32)).reshape(-1)
    gate = gate_t.T
    P = ((N * TOP_K + E * (tm - 1) + tm - 1) // tm) * tm
    nt = P // tm
    tiles = jnp.arange(nt, dtype=I32)
    tile_e = jnp.minimum(jnp.sum(tiles[:, None] * tm >= pend[None, :], axis=1), E - 1).astype(I32)
    n_used = (pend[-1] // tm).astype(I32)
    experts = jnp.arange(E, dtype=I32)
    used = counts > 0
    tile_e = jnp.where(tiles < n_used, tile_e, jnp.max(jnp.where(used, experts, 0)))
    after = jnp.min(jnp.where((experts[None, :] > experts[:, None]) & used[None, :], experts[None, :], E), axis=1)
    after = jnp.where(after < E, after, experts)
    next_e = jnp.sum(jnp.where(tile_e[:, None] == experts[None, :], after[None, :], 0), axis=1)
    x_pad = _dispatch(x, g, dest, pend.astype(I32), n_used.reshape(1), P)
    y_pad = _experts(tile_e, next_e.astype(I32), n_used.reshape(1), x_pad, w_gu, b_gu, w_dn, b_dn, layer)
    return _combine(x, gate, dest, y_pad)


def _bucket_maps():
    qi = np.arange(ATT_BLOCK)[None, :]
    kj = np.arange(2 * ATT_BLOCK)[:, None]
    delta = qi + ATT_BLOCK - kj
    buckets, valids = [], []
    max_exact = N_BUCKETS // 2
    for win, dil in DILATED_GROUPS:
        n = np.maximum(delta * dil, 0)
        nf = np.maximum(n, 1).astype(np.float32)
        large = max_exact + (np.log(nf / np.float32(max_exact)) / np.float32(math.log(REL_MAX_DIST / max_exact))
                             * np.float32(N_BUCKETS - max_exact)).astype(np.int32)
        large = np.minimum(large, N_BUCKETS - 1)
        buckets.append(np.where(n < max_exact, n, large).astype(np.int32))
        valids.append(((delta >= 0) & (delta <= win // dil)).astype(np.int32))
    return np.stack(buckets), np.stack(valids)


def _bias_kernel(tab_ref, bm_ref, valid_ref, o_ref):
    g = pl.program_id(0)
    bm = bm_ref[0]
    key = lax.broadcasted_iota(I32, bm.shape, 0)

    def one_head(h, carry):
        acc = jnp.zeros(bm.shape, F32)
        for b in range(N_BUCKETS):
            acc = jnp.where(bm == b, tab_ref[b, g * ATT_HEADS + h], acc)
        band = jnp.where(valid_ref[0] > 0, acc * LOG2_E, NEG_INF)
        o_ref[h, 0] = band
        o_ref[h, 1] = jnp.where(key >= ATT_BLOCK, band, NEG_INF)
        return carry

    lax.fori_loop(0, ATT_HEADS, one_head, 0)


def _bias_tables(rel_bias):
    bm, valid = _bucket_maps()
    GH = len(DILATED_GROUPS) * ATT_HEADS
    blk = (1, 2 * ATT_BLOCK, ATT_BLOCK)
    return pl.pallas_call(
        _bias_kernel,
        grid=(len(DILATED_GROUPS),),
        in_specs=[
            pl.BlockSpec(memory_space=pltpu.SMEM),
            pl.BlockSpec(blk, lambda i: (i, 0, 0)),
            pl.BlockSpec(blk, lambda i: (i, 0, 0)),
        ],
        out_specs=pl.BlockSpec((ATT_HEADS, 2, 2 * ATT_BLOCK, ATT_BLOCK), lambda i: (i, 0, 0, 0)),
        out_shape=jax.ShapeDtypeStruct((GH, 2, 2 * ATT_BLOCK, ATT_BLOCK), F32),
        compiler_params=_params(("arbitrary",)),
        name="attn_bias",
    )(rel_bias, jnp.asarray(bm), jnp.asarray(valid))


def _cast_kernel(w_ref, o_ref):
    o_ref[...] = w_ref[...].astype(BF16)


def _to_bf16(w, layer, col_block):
    _, rows, cols = w.shape
    return pl.pallas_call(
        _cast_kernel,
        grid=(cols // col_block,),
        in_specs=[pl.BlockSpec((None, rows, col_block), lambda j: (layer, 0, j))],
        out_specs=pl.BlockSpec((rows, col_block), lambda j: (0, j)),
        out_shape=jax.ShapeDtypeStruct((rows, cols), BF16),
        compiler_params=_params(("parallel",)),
        name="cast_bf16",
    )(w)


def _qkv_kernel(x_ref, g_ref, wq_ref, wk_ref, wv_ref, qg_ref, kg_ref, qt_ref, k_ref, vt_ref,
                slab_ref, *, dil, R, nl):
    rc = pl.program_id(2)
    D = x_ref.shape[2]
    lanes = slab_ref.shape[2]

    if dil == 1:
        x = x_ref[0]
    else:
        @pl.when(rc == 0)
        def _():
            for c in range(D // lanes):
                slab_ref[c] = x_ref[0, :, c * lanes:(c + 1) * lanes]

        pieces = []
        for j in range(R):
            r = rc * R + j
            cols = [slab_ref[c, pl.ds(r, nl, stride=dil), :] for c in range(D // lanes)]
            pieces.append(jnp.concatenate(cols, axis=1))
        x = jnp.concatenate(pieces, axis=0)
    h = _rms(x, g_ref[...]).astype(BF16)
    q = jnp.dot(h, wq_ref[...], preferred_element_type=F32)
    k = jnp.dot(h, wk_ref[...], preferred_element_type=F32)
    v = jnp.dot(h, wv_ref[...], preferred_element_type=F32)
    blk = ATT_BLOCK

    def normed_tile(t, gain_ref, cs):
        heads = []
        for hh in range(t.shape[0] // HEAD_DIM):
            th = t[hh * HEAD_DIM:(hh + 1) * HEAD_DIM]
            ms = jnp.mean(th * th, axis=0, keepdims=True)
            heads.append(th * lax.rsqrt(ms + EPS))
        return jnp.concatenate(heads, axis=0) * gain_ref[cs, :]

    for j in range(R):
        for c in range(nl // blk):
            rows = slice(j * nl + c * blk, j * nl + (c + 1) * blk)
            for hp in range(q.shape[1] // blk):
                cs = slice(hp * blk, (hp + 1) * blk)
                qt_ref[0, j, cs, c * blk:(c + 1) * blk] = normed_tile(q[rows, cs].T, qg_ref, cs).astype(BF16)
                k_ref[0, j, c * blk:(c + 1) * blk, cs] = normed_tile(k[rows, cs].T, kg_ref, cs).T.astype(BF16)
                vt_ref[0, j, cs, c * blk:(c + 1) * blk] = v[rows, cs].T.astype(BF16)


def _qkv_group(x, g, w_bf, q_gain, k_gain, gi, dil):
    B, S, D = x.shape
    HD = ATT_HEADS * HEAD_DIM
    G = len(DILATED_GROUPS)
    L = S // dil
    R = min(dil, QKV_ROWS // ATT_BLOCK)
    nl = QKV_ROWS // R
    lanes = LANES
    rows_of = lambda gain: jnp.broadcast_to(jnp.tile(gain, ATT_HEADS)[:, None], (HD, lanes))
    q_rows = rows_of(q_gain * (HEAD_DIM ** -0.5 * LOG2_E))
    k_rows = rows_of(k_gain)
    wspec = lambda s: pl.BlockSpec((D, HD), lambda b, l, r: (0, s * G + gi))
    tspec = pl.BlockSpec((1, R, HD, nl), lambda b, l, r: (b, r, 0, l))
    tshape = jax.ShapeDtypeStruct((B, dil, HD, L), BF16)
    return pl.pallas_call(
        functools.partial(_qkv_kernel, dil=dil, R=R, nl=nl),
        grid=(B, L // nl, dil // R),
        in_specs=[
            pl.BlockSpec((1, nl * dil, D), lambda b, l, r: (b, l, 0)),
            pl.BlockSpec((1, D), lambda b, l, r: (0, 0)),
            wspec(0), wspec(1), wspec(2),
            pl.BlockSpec((HD, lanes), lambda b, l, r: (0, 0)),
            pl.BlockSpec((HD, lanes), lambda b, l, r: (0, 0)),
        ],
        out_specs=[tspec, pl.BlockSpec((1, R, nl, HD), lambda b, l, r: (b, r, l, 0)), tspec],
        out_shape=[tshape, jax.ShapeDtypeStruct((B, dil, L, HD), BF16), tshape],
        scratch_shapes=[pltpu.VMEM((D // lanes, nl * dil if dil > 1 else 8, lanes), F32)],
        compiler_params=_params(("arbitrary", "arbitrary", "arbitrary")),
        name=f"attn_qkv_g{gi}",
    )(x, g.reshape(1, D), w_bf, w_bf, w_bf, q_rows, k_rows)


def _attn_kernel(*refs, single, nitems):
    if single:
        qt_ref, kc_ref, vtc_ref, bias_ref, o_ref, l_ref, s_ref, p_ref = refs
    else:
        qt_ref, kc_ref, kp_ref, vtc_ref, vtp_ref, bias_ref, o_ref, l_ref, s_ref, p_ref = refs
    blk = ATT_BLOCK
    pair = 2 * HEAD_DIM
    first = jnp.where(pl.program_id(2) == 0, 1, 0)
    if single:
        k0 = blk
        variants = [1] * nitems
        q_of = lambda it, cs: qt_ref[0, it, cs, :]
        k_of = lambda it, cs: kc_ref[0, it, :, cs]
        vt_of = lambda it, cs: vtc_ref[0, it, cs, :]
    else:
        k0 = 0
        variants = [first] + [0] * (nitems - 1)

        def q_of(it, cs):
            return qt_ref[0, 0, cs, it * blk:(it + 1) * blk]

        def k_of(it, cs):
            if it == 0:
                return jnp.concatenate([kp_ref[0, 0, :, cs], kc_ref[0, 0, :blk, cs]], axis=0)
            return kc_ref[0, 0, (it - 1) * blk:(it + 1) * blk, cs]

        def vt_of(it, cs):
            if it == 0:
                return jnp.concatenate([vtp_ref[0, 0, cs, :], vtc_ref[0, 0, cs, :blk]], axis=1)
            return vtc_ref[0, 0, cs, (it - 1) * blk:(it + 1) * blk]

    feat = lax.broadcasted_iota(I32, (pair, blk), 0)
    head_rows = [jnp.where(feat < HEAD_DIM, 1.0, 0.0).astype(BF16),
                 jnp.where(feat < HEAD_DIM, 0.0, 1.0).astype(BF16)]
    invs = {}
    lse_rows = {}

    def scores(it, pr):
        cs = slice(pr * pair, (pr + 1) * pair)
        qt2 = q_of(it, cs)
        k2 = k_of(it, cs)
        for half in range(2):
            qth = qt2 * head_rows[half]
            s_ref[it * ATT_HEADS + 2 * pr + half] = jnp.dot(k2, qth, preferred_element_type=F32)

    def softmax(it, pr):
        for head in (2 * pr, 2 * pr + 1):
            s = s_ref[it * ATT_HEADS + head] + bias_ref[head, variants[it], k0:, :]
            m = jnp.max(s, axis=0, keepdims=True)
            p = jnp.exp2(s - m)
            den = jnp.sum(p, axis=0, keepdims=True)
            p_ref[it * ATT_HEADS + head] = p.astype(BF16)
            invs[it, head] = 1.0 / den
            lse_rows[it, head] = (m + jnp.log2(den)) * (1.0 / LOG2_E)

    def outputs(it, pr):
        cs = slice(pr * pair, (pr + 1) * pair)
        vt2 = vt_of(it, cs)
        halves = []
        for half in range(2):
            head = 2 * pr + half
            vth = vt2[half * HEAD_DIM:(half + 1) * HEAD_DIM]
            ot = jnp.dot(vth, p_ref[it * ATT_HEADS + head], preferred_element_type=F32)
            halves.append(ot * invs[it, head])
        val = jnp.concatenate(halves, axis=0).T
        if single:
            o_ref[0, it, :, cs] = val
        else:
            o_ref[0, 0, it * blk:(it + 1) * blk, cs] = val

    work = [(it, pr) for it in range(nitems) for pr in range(ATT_HEADS // 2)]
    for it, pr in work:
        scores(it, pr)
    for it, pr in work:
        softmax(it, pr)
    for it, pr in work:
        outputs(it, pr)
    lse_row = lax.broadcasted_iota(I32, (blk, blk), 0)
    for it in range(nitems):
        lse_t = jnp.zeros((blk, blk), F32)
        for head in range(ATT_HEADS):
            lse_t = jnp.where(lse_row == head, lse_rows[it, head], lse_t)
        if single:
            l_ref[0, it] = lse_t.T
        else:
            l_ref[0, 0, it * blk:(it + 1) * blk, :] = lse_t.T


def _attn_group(qt, k, vt, bias, gi):
    B, dil, L, HD = k.shape
    blk = ATT_BLOCK
    nb = L // blk
    single = nb == 1
    ni = min(ATT_ITEMS, dil if single else nb)
    bias_spec = pl.BlockSpec((ATT_HEADS, 2, 2 * blk, blk), lambda b, r, n: (gi, 0, 0, 0))
    if single:
        rows = pl.BlockSpec((1, ni, blk, HD), lambda b, r, n: (b, r, 0, 0))
        cols = pl.BlockSpec((1, ni, HD, blk), lambda b, r, n: (b, r, 0, 0))
        grid = (B, dil // ni, 1)
        in_specs = [cols, rows, cols, bias_spec]
        args = (qt, k, vt, bias)
        out_specs = [rows, pl.BlockSpec((1, ni, blk, blk), lambda b, r, n: (b, r, 0, 0))]
        nk = blk
    else:
        rows = pl.BlockSpec((1, 1, ni * blk, HD), lambda b, r, n: (b, r, n, 0))
        cols = pl.BlockSpec((1, 1, HD, ni * blk), lambda b, r, n: (b, r, 0, n))
        prow = pl.BlockSpec((1, 1, blk, HD), lambda b, r, n: (b, r, jnp.maximum(ni * n - 1, 0), 0))
        pcol = pl.BlockSpec((1, 1, HD, blk), lambda b, r, n: (b, r, 0, jnp.maximum(ni * n - 1, 0)))
        grid = (B, dil, nb // ni)
        in_specs = [cols, rows, prow, cols, pcol, bias_spec]
        args = (qt, k, k, vt, vt, bias)
        out_specs = [rows, pl.BlockSpec((1, 1, ni * blk, blk), lambda b, r, n: (b, r, n, 0))]
        nk = 2 * blk
    return pl.pallas_call(
        functools.partial(_attn_kernel, single=single, nitems=ni),
        grid=grid,
        in_specs=in_specs,
        out_specs=out_specs,
        out_shape=[jax.ShapeDtypeStruct((B, dil, L, HD), F32),
                   jax.ShapeDtypeStruct((B, dil, L, blk), F32)],
        scratch_shapes=[pltpu.VMEM((ni * ATT_HEADS, nk, blk), F32),
                        pltpu.VMEM((ni * ATT_HEADS, nk, blk), BF16)],
        compiler_params=_params(("parallel", "parallel", "arbitrary")),
        name=f"attn_core_g{gi}",
    )(*args)


def _attn_out_kernel(x_ref, o0, o1, o2, l0, l1, l2, wo_ref, out_ref, oslab, lslab, ex_ref, *, T):
    lanes = oslab.shape[3]
    HD = o0.shape[3]

    @pl.when((pl.program_id(0) == 0) & (pl.program_id(1) == 0))
    def _():
        h = lax.broadcasted_iota(I32, ex_ref.shape, 0) & (ATT_BLOCK - 1)
        c = lax.broadcasted_iota(I32, ex_ref.shape, 1)
        owner = lax.shift_right_logical(c, HEAD_DIM.bit_length() - 1)
        ex_ref[...] = jnp.where(h == owner, 1.0, 0.0).astype(BF16)

    def token_order(o_ref, l_ref, dil, s):
        if dil == 1:
            return o_ref[0, 0], l_ref[0, 0]
        n = T // dil
        for r in range(dil):
            blk = o_ref[0, r]
            for c in range(HD // lanes):
                oslab[s, c, pl.ds(r, n, stride=dil), :] = blk[:, c * lanes:(c + 1) * lanes]
            lslab[s, pl.ds(r, n, stride=dil), :] = l_ref[0, r]
        o = jnp.concatenate([oslab[s, c] for c in range(HD // lanes)], axis=1)
        return o, lslab[s]

    groups = [token_order(o_ref, l_ref, dil, s)
              for s, ((_, dil), o_ref, l_ref) in enumerate(zip(DILATED_GROUPS, (o0, o1, o2), (l0, l1, l2)))]
    m = jnp.maximum(jnp.maximum(groups[0][1], groups[1][1]), groups[2][1])
    es = [jnp.exp(l - m) for _, l in groups]
    inv = 1.0 / (es[0] + es[1] + es[2])
    att = jnp.zeros((T, HD), F32)
    for (o, _), e in zip(groups, es):
        a = e * inv
        a_hi = a.astype(BF16)
        a_lo = (a - a_hi.astype(F32)).astype(BF16)
        wide = jnp.dot(jnp.concatenate([a_hi, a_lo], axis=1), ex_ref[...], preferred_element_type=F32)
        att = att + wide * o
    out_ref[0] = x_ref[0] + jnp.dot(att.astype(BF16), wo_ref[...], preferred_element_type=F32)


def _attn_out(x, os, ls, wo_bf):
    B, S, D = x.shape
    HD = wo_bf.shape[0]
    T = OUT_TILE
    lanes = LANES
    ospec = lambda dil, w: pl.BlockSpec((1, dil, T // dil, w), lambda b, i: (b, 0, i, 0))
    dils = [dil for _, dil in DILATED_GROUPS]
    xspec = pl.BlockSpec((1, T, D), lambda b, i: (b, i, 0))
    return pl.pallas_call(
        functools.partial(_attn_out_kernel, T=T),
        grid=(B, S // T),
        in_specs=[xspec] + [ospec(d, HD) for d in dils] + [ospec(d, ATT_BLOCK) for d in dils]
                 + [pl.BlockSpec((HD, D), lambda b, i: (0, 0))],
        out_specs=xspec,
        out_shape=jax.ShapeDtypeStruct((B, S, D), F32),
        scratch_shapes=[pltpu.VMEM((len(dils), HD // lanes, T, lanes), F32),
                        pltpu.VMEM((len(dils), T, ATT_BLOCK), F32),
                        pltpu.VMEM((2 * ATT_BLOCK, HD), BF16)],
        compiler_params=_params(("arbitrary", "arbitrary")),
        name="attn_out",
    )(x, *os, *ls, wo_bf)


def _attention_layer(x, g, w_qkv, q_gain, k_gain, w_o, rel_bias, layer):
    HD = ATT_HEADS * HEAD_DIM
    bias = _bias_tables(rel_bias)
    wqkv_bf = _to_bf16(w_qkv, layer, HD)
    wo_bf = _to_bf16(w_o, layer, w_o.shape[2])
    os, ls = [], []
    for gi, (_, dil) in enumerate(DILATED_GROUPS):
        qt, k, vt = _qkv_group(x, g, wqkv_bf, q_gain, k_gain, gi, dil)
        o, l = _attn_group(qt, k, vt, bias, gi)
        os.append(o)
        ls.append(l)
    return _attn_out(x, os, ls, wo_bf)


def kernel(x, norm_mix_g, norm_ffn_g, pool_w, pool_scale, attn_w_qkv, attn_q_gain, attn_k_gain,
           attn_w_o, rel_bias, moe_w_router, moe_b_router, moe_w_gate_up, moe_b_gate_up,
           moe_w_down, moe_b_down):
    B, S, D = x.shape
    depth = norm_mix_g.shape[0]
    for i in range(depth):
        j = i // 2
        if i % 2 == 0:
            x = _pool_layer(x, norm_mix_g[i], pool_w[j], pool_scale[j])
        else:
            x = _attention_layer(x, norm_mix_g[i], attn_w_qkv, attn_q_gain[j], attn_k_gain[j],
                                 attn_w_o, rel_bias, j)
        x = _moe_layer(x.reshape(B * S, D), norm_ffn_g[i], moe_w_router[i], moe_b_router[i],
                       moe_w_gate_up, moe_b_gate_up[i], moe_w_down, moe_b_down[i],
                       i).reshape(B, S, D)
    return x
```

```python
import functools
import math

import numpy as np
import jax
import jax.numpy as jnp
from jax import lax
from jax.experimental import pallas as pl
from jax.experimental.pallas import tpu as pltpu

F32 = jnp.float32
BF16 = jnp.bfloat16
I32 = jnp.int32

EPS = 1e-6
POOL_WINDOWS = (2, 4, 8, 16)
POOL_HALO = max(POOL_WINDOWS)
DILATED_GROUPS = ((128, 1), (512, 4), (2048, 16))
ATT_HEADS = 16
HEAD_DIM = 64
ATT_BLOCK = 128
ATT_ITEMS = 8
N_BUCKETS = 32
REL_MAX_DIST = 2048
NEG_INF = -1e30
LOG2_E = math.log2(math.e)
N_EXPERTS = 32
TOP_K = 4
SWIGLU_LIMIT = 7.0
SWIGLU_ALPHA = 1.702

V7X_VMEM_LIMIT_BYTES = 56 * 1024 * 1024
ISSUE_UNROLL = 64
MOE_TILE = 512
DISPATCH_TILE = 512
COMBINE_TILE = 256
ROUTER_TILE = 1024
POOL_TILE = 1024
QKV_ROWS = 1024
LANES = 128
OUT_TILE = 512


def _rms(xf, g):
    ms = jnp.mean(xf * xf, axis=-1, keepdims=True)
    return xf * lax.rsqrt(ms + EPS) * g


def _params(sem):
    return pltpu.CompilerParams(dimension_semantics=sem, vmem_limit_bytes=V7X_VMEM_LIMIT_BYTES)


def _pool_kernel(x_ref, halo_ref, g_ref, w_ref, sc_ref, o_ref, *, ts, dg):
    i = pl.program_id(1)
    x = x_ref[0]
    g = g_ref[...]
    h = _rms(x, g)
    hh = _rms(halo_ref[0], g)
    hh = jnp.where(i == 0, 0.0, hh)
    full = jnp.concatenate([hh, h], axis=0)
    pos = i * ts + lax.broadcasted_iota(I32, (ts, 1), 0)
    outs = []
    for gi, w in enumerate(POOL_WINDOWS):
        s = full[:, gi * dg:(gi + 1) * dg]
        sh = 1
        while sh < w:
            s = s + pltpu.roll(s, sh, 0)
            sh *= 2
        s = s[POOL_HALO:]
        cnt = jnp.minimum(pos + 1, w).astype(F32)
        p = s / cnt - h[:, gi * dg:(gi + 1) * dg]
        outs.append(jnp.dot(p.astype(BF16), w_ref[gi].astype(BF16),
                            preferred_element_type=F32))
    y = jnp.concatenate(outs, axis=1) * sc_ref[...]
    o_ref[0] = x + y


def _pool_layer(x, g, w_groups, scale):
    B, S, D = x.shape
    ts = POOL_TILE
    dg = D // len(POOL_WINDOWS)
    hb = ts // POOL_HALO
    return pl.pallas_call(
        functools.partial(_pool_kernel, ts=ts, dg=dg),
        grid=(B, S // ts),
        in_specs=[
            pl.BlockSpec((1, ts, D), lambda b, i: (b, i, 0)),
            pl.BlockSpec((1, POOL_HALO, D), lambda b, i: (b, jnp.maximum(i * hb - 1, 0), 0)),
            pl.BlockSpec((1, D), lambda b, i: (0, 0)),
            pl.BlockSpec((len(POOL_WINDOWS), dg, dg), lambda b, i: (0, 0, 0)),
            pl.BlockSpec((1, D), lambda b, i: (0, 0)),
        ],
        out_specs=pl.BlockSpec((1, ts, D), lambda b, i: (b, i, 0)),
        out_shape=jax.ShapeDtypeStruct((B, S, D), F32),
        compiler_params=_params(("parallel", "parallel")),
        name="pool_mixer",
    )(x, x, g.reshape(1, D), w_groups, scale.reshape(1, D))


def _router_kernel(x_ref, g_ref, wr_ref, br_ref, e_ref, gate_ref, rank_ref, cnt_ref,
                   tri_ref, carry_ref, wcat_ref, *, tt):
    i = pl.program_id(0)
    E = N_EXPERTS

    @pl.when(i == 0)
    def _():
        r = lax.broadcasted_iota(I32, tri_ref.shape, 0)
        c = lax.broadcasted_iota(I32, tri_ref.shape, 1)
        tri_ref[...] = jnp.where(r < c, 1.0, 0.0).astype(BF16)
        carry_ref[...] = jnp.zeros_like(carry_ref)
        w = wr_ref[...]
        w_hi = w.astype(BF16)
        wcat_ref[:, :LANES] = w_hi
        wcat_ref[:, LANES:] = (w - w_hi.astype(F32)).astype(BF16)

    h = _rms(x_ref[...], g_ref[...])
    h_hi = h.astype(BF16)
    h_lo = (h - h_hi.astype(F32)).astype(BF16)
    both = jnp.dot(h_hi, wcat_ref[...], preferred_element_type=F32)
    cross = jnp.dot(h_lo, wcat_ref[:, :LANES], preferred_element_type=F32)
    logits = both[:, :LANES] + (both[:, LANES:] + cross) + br_ref[...]
    nblk = tt // LANES
    l = jnp.concatenate([logits[c * LANES:(c + 1) * LANES].T for c in range(nblk)], axis=1)[:E]
    row = lax.broadcasted_iota(I32, (E, tt), 0).astype(F32)
    vals, sels, idxs = [], [], []
    for k in range(TOP_K):
        m = jnp.max(l, axis=0, keepdims=True)
        idx = jnp.min(jnp.where(l == m, row, float(E)), axis=0, keepdims=True)
        sel = row == idx
        vals.append(m)
        sels.append(sel)
        idxs.append(idx)
        l = jnp.where(sel, -jnp.inf, l)
    ex = [jnp.exp(v - vals[0]) for v in vals]
    den = ex[0] + ex[1] + ex[2] + ex[3]
    multi = jnp.zeros((E, tt), F32)
    for sel in sels:
        multi = multi + jnp.where(sel, 1.0, 0.0)
    base = carry_ref[:, :1]
    parts = []
    for c in range(nblk):
        mc = multi[:, c * LANES:(c + 1) * LANES]
        parts.append(jnp.dot(mc.astype(BF16), tri_ref[...], preferred_element_type=F32) + base)
        base = base + jnp.sum(mc, axis=1, keepdims=True)
    before = jnp.concatenate(parts, axis=1)
    kk = lax.broadcasted_iota(I32, (TOP_K, tt), 0)
    e_out = jnp.zeros((TOP_K, tt), F32)
    g_out = jnp.zeros((TOP_K, tt), F32)
    r_out = jnp.zeros((TOP_K, tt), F32)
    for k in range(TOP_K):
        e_out = jnp.where(kk == k, idxs[k], e_out)
        g_out = jnp.where(kk == k, ex[k] / den, g_out)
        rk = jnp.sum(jnp.where(sels[k], before, 0.0), axis=0, keepdims=True)
        r_out = jnp.where(kk == k, rk, r_out)
    e_ref[...] = e_out.astype(I32)
    gate_ref[...] = g_out
    rank_ref[...] = r_out.astype(I32)
    carry_ref[...] = jnp.broadcast_to(base, carry_ref.shape)
    cnt_ref[...] = carry_ref[...]


def _router(x, g, w_r, b_r):
    N, D = x.shape
    tt = ROUTER_TILE
    E = N_EXPERTS
    w_pad = jnp.pad(w_r, ((0, 0), (0, LANES - E)))
    b_pad = jnp.pad(b_r, (0, LANES - E)).reshape(1, LANES)
    kspec = pl.BlockSpec((TOP_K, tt), lambda i: (0, i))
    return pl.pallas_call(
        functools.partial(_router_kernel, tt=tt),
        grid=(N // tt,),
        in_specs=[
            pl.BlockSpec((tt, D), lambda i: (i, 0)),
            pl.BlockSpec((1, D), lambda i: (0, 0)),
            pl.BlockSpec((D, LANES), lambda i: (0, 0)),
            pl.BlockSpec((1, LANES), lambda i: (0, 0)),
        ],
        out_specs=[kspec, kspec, kspec, pl.BlockSpec((E, LANES), lambda i: (0, 0))],
        out_shape=[
            jax.ShapeDtypeStruct((TOP_K, N), I32),
            jax.ShapeDtypeStruct((TOP_K, N), F32),
            jax.ShapeDtypeStruct((TOP_K, N), I32),
            jax.ShapeDtypeStruct((E, LANES), F32),
        ],
        scratch_shapes=[pltpu.VMEM((LANES, LANES), BF16), pltpu.VMEM((E, LANES), F32),
                        pltpu.VMEM((D, 2 * LANES), BF16)],
        compiler_params=_params(("arbitrary",)),
        name="moe_router",
    )(x, g.reshape(1, D), w_pad, b_pad)


def _to_tile_rows(ref, val):
    n, D = val.shape
    nsub = D // LANES
    for s in range(nsub):
        ref[pl.ds(s, n, stride=nsub), :] = val[:, s * LANES:(s + 1) * LANES]


def _from_tile_rows(ref, n, nsub):
    return jnp.concatenate([ref[pl.ds(s, n, stride=nsub), :] for s in range(nsub)], axis=1)


def _tile_row(ref, r, nsub, count=1):
    return ref.at[pl.ds(pl.multiple_of(r * nsub, nsub), count * nsub)]


def _row_copy_wait(src, dst, sem, times):
    for _ in range(times):
        pltpu.make_async_copy(src, dst.at[pl.ds(0, src.shape[0])], sem).wait()


def _dispatch_kernel(pend_ref, nu_ref, *refs, tt, tm, nt, nsub):
    dest_refs = refs[:TOP_K]
    x_ref, g_ref, xpad_ref, hbuf, zbuf, sems, zsems = refs[TOP_K:]
    i = pl.program_id(0)
    slot = i % 2
    last = pl.num_programs(0) - 1

    def seg_copy(e):
        start = pl.multiple_of(jnp.maximum(pend_ref[e] - tm, 0), tm)
        return pltpu.make_async_copy(zbuf, _tile_row(xpad_ref, start, nsub, tm), zsems.at[0])

    def tail_copy(j):
        return pltpu.make_async_copy(zbuf, _tile_row(xpad_ref, (nt - 1 - j) * tm, nsub, tm), zsems.at[1])

    @pl.when(i == 0)
    def _():
        zbuf[...] = jnp.zeros_like(zbuf)
        for e in range(N_EXPERTS):
            seg_copy(e).start()
        for j in range(N_EXPERTS):
            pl.when(nt - 1 - j >= nu_ref[0])(tail_copy(j).start)

    hb = hbuf.at[slot]
    _to_tile_rows(hb, _rms(x_ref[...], g_ref[...]))

    @pl.when(i == 0)
    def _():
        for e in range(N_EXPERTS):
            seg_copy(e).wait()

    @pl.when(i == last)
    def _():
        for j in range(N_EXPERTS):
            pl.when(nt - 1 - j >= nu_ref[0])(tail_copy(j).wait)

    def issue(t, carry):
        for k in range(TOP_K):
            d = dest_refs[k][t]
            pltpu.make_async_copy(_tile_row(hb, t, nsub), _tile_row(xpad_ref, d, nsub),
                                  sems.at[slot]).start(priority=k % 2)
        return carry

    lax.fori_loop(0, tt, issue, 0, unroll=ISSUE_UNROLL)

    @pl.when(i > 0)
    def _():
        _row_copy_wait(hbuf.at[1 - slot], xpad_ref, sems.at[1 - slot], TOP_K)

    @pl.when(i == pl.num_programs(0) - 1)
    def _():
        _row_copy_wait(hb, xpad_ref, sems.at[slot], TOP_K)


def _dispatch(x, g, dest, pend, n_used, P):
    N, D = x.shape
    tt = DISPATCH_TILE
    tm = MOE_TILE
    nsub = D // LANES
    grid_spec = pltpu.PrefetchScalarGridSpec(
        num_scalar_prefetch=2,
        grid=(N // tt,),
        in_specs=[pl.BlockSpec((tt,), lambda i, pe, nu, k=k: (k * (N // tt) + i,), memory_space=pltpu.SMEM)
                  for k in range(TOP_K)] + [
            pl.BlockSpec((tt, D), lambda i, pe, nu: (i, 0)),
            pl.BlockSpec((1, D), lambda i, pe, nu: (0, 0)),
        ],
        out_specs=pl.BlockSpec(memory_space=pl.ANY),
        scratch_shapes=[pltpu.VMEM((2, tt * nsub, LANES), F32), pltpu.VMEM((tm * nsub, LANES), F32),
                        pltpu.SemaphoreType.DMA((2,)), pltpu.SemaphoreType.DMA((2,))],
    )
    return pl.pallas_call(
        functools.partial(_dispatch_kernel, tt=tt, tm=tm, nt=P // tm, nsub=nsub),
        grid_spec=grid_spec,
        out_shape=jax.ShapeDtypeStruct((P * nsub, LANES), F32),
        compiler_params=_params(("arbitrary",)),
        name="moe_dispatch",
    )(pend, n_used, *([dest] * TOP_K), x, g.reshape(1, D))


def _expert_kernel(t0_ref, nt_ref, nu_ref, xpad_hbm, wgu_ref, bgu_ref, wdn_ref, bdn_ref, ypad_hbm,
                   wgu_bf, wdn_bf, xbuf, ybuf, xsems, ysems, *, F, tm, nsub, nt_all):
    e = pl.program_id(0)
    n = nt_ref[e]
    t0 = t0_ref[e]

    def tile_rows(ref, t):
        return _tile_row(ref, t * tm, nsub, tm)

    def x_copy(j, s):
        return pltpu.make_async_copy(tile_rows(xpad_hbm, t0 + j), xbuf.at[s], xsems.at[s])

    def y_copy(t, s):
        return pltpu.make_async_copy(ybuf.at[s], tile_rows(ypad_hbm, t), ysems.at[s])

    @pl.when(n > 0)
    def _():
        x_copy(0, 0).start()
        wgu_bf[...] = wgu_ref[0].astype(BF16)
        wdn_bf[...] = wdn_ref[0].astype(BF16)

        def tile(j, carry):
            s = j % 2
            x_copy(j, s).wait()
            pl.when(j + 1 < n)(x_copy(j + 1, 1 - s).start)
            pl.when(j >= 2)(y_copy(t0 + j - 2, s).wait)
            x = _from_tile_rows(xbuf.at[s], tm, nsub).astype(BF16)
            gu = jnp.dot(x, wgu_bf[...], preferred_element_type=F32) + bgu_ref[0]
            gate = jnp.minimum(gu[:, :F], SWIGLU_LIMIT)
            up = jnp.clip(gu[:, F:], -SWIGLU_LIMIT, SWIGLU_LIMIT)
            glu = gate * jax.nn.sigmoid(SWIGLU_ALPHA * gate)
            a = ((up + 1.0) * glu).astype(BF16)
            _to_tile_rows(ybuf.at[s], jnp.dot(a, wdn_bf[...], preferred_element_type=F32) + bdn_ref[0])
            y_copy(t0 + j, s).start()
            return carry

        lax.fori_loop(0, n, tile, 0)
        pl.when(n >= 2)(y_copy(t0 + n - 2, n % 2).wait)
        y_copy(t0 + n - 1, (n - 1) % 2).wait()

    @pl.when(e == pl.num_programs(0) - 1)
    def _():
        ybuf[0] = jnp.zeros(ybuf.shape[1:], ybuf.dtype)
        lax.fori_loop(nu_ref[0], nt_all, lambda t, c: (y_copy(t, 0).start(), c)[1], 0)
        lax.fori_loop(nu_ref[0], nt_all, lambda t, c: (y_copy(t, 0).wait(), c)[1], 0)


def _experts(first_tile, n_tiles, n_used, x_pad, w_gu, b_gu, w_dn, b_dn, layer):
    _, E, D, F2 = w_gu.shape
    F = F2 // 2
    tm = MOE_TILE
    nsub = D // LANES
    P = x_pad.shape[0] // nsub
    exp4 = lambda e, t0, nt, nu: (layer, e, 0, 0)
    exp3 = lambda e, t0, nt, nu: (e, 0, 0)
    grid_spec = pltpu.PrefetchScalarGridSpec(
        num_scalar_prefetch=3,
        grid=(E,),
        in_specs=[
            pl.BlockSpec(memory_space=pl.ANY),
            pl.BlockSpec((None, 1, D, F2), exp4),
            pl.BlockSpec((1, 1, F2), exp3),
            pl.BlockSpec((None, 1, F, D), exp4),
            pl.BlockSpec((1, 1, D), exp3),
        ],
        out_specs=pl.BlockSpec(memory_space=pl.ANY),
        scratch_shapes=[pltpu.VMEM((D, F2), BF16), pltpu.VMEM((F, D), BF16),
                        pltpu.VMEM((2, tm * nsub, LANES), F32), pltpu.VMEM((2, tm * nsub, LANES), F32),
                        pltpu.SemaphoreType.DMA((2,)), pltpu.SemaphoreType.DMA((2,))],
    )
    return pl.pallas_call(
        functools.partial(_expert_kernel, F=F, tm=tm, nsub=nsub, nt_all=P // tm),
        grid_spec=grid_spec,
        out_shape=jax.ShapeDtypeStruct((P * nsub, LANES), F32),
        compiler_params=_params(("arbitrary",)),
        name="moe_experts",
    )(first_tile, n_tiles, n_used, x_pad, w_gu, b_gu.reshape(E, 1, F2), w_dn, b_dn.reshape(E, 1, D))


def _combine_kernel(*refs, tt, nsub):
    dest_refs = refs[:TOP_K]
    dnext_refs = refs[TOP_K:2 * TOP_K]
    x_ref, gate_ref, ypad_ref, o_ref, ybuf, sems = refs[2 * TOP_K:]
    i = pl.program_id(0)
    slot = i % 2

    def gather(idx_refs, s):
        def issue(t, carry):
            for k in range(TOP_K):
                d = idx_refs[k][t]
                pltpu.make_async_copy(_tile_row(ypad_ref, d, nsub), _tile_row(ybuf.at[s, k], t, nsub),
                                      sems.at[s]).start(priority=k % 2)
            return carry

        lax.fori_loop(0, tt, issue, 0, unroll=ISSUE_UNROLL)

    @pl.when(i == 0)
    def _():
        gather(dest_refs, slot)

    @pl.when(i + 1 < pl.num_programs(0))
    def _():
        gather(dnext_refs, 1 - slot)

    for k in range(TOP_K):
        _row_copy_wait(ybuf.at[slot, k], ypad_ref, sems.at[slot], 1)
    gate = gate_ref[...]
    x = x_ref[...]
    cols = []
    for c in range(nsub):
        acc = x[:, c * LANES:(c + 1) * LANES]
        for k in range(TOP_K):
            acc = acc + ybuf[slot, k, pl.ds(c, tt, stride=nsub), :] * gate[:, k:k + 1]
        cols.append(acc)
    o_ref[...] = jnp.concatenate(cols, axis=1)


def _combine(x, gate, dest, y_pad):
    N, D = x.shape
    tt = COMBINE_TILE
    nsub = D // LANES
    steps = N // tt
    last = steps - 1
    return pl.pallas_call(
        functools.partial(_combine_kernel, tt=tt, nsub=nsub),
        grid=(N // tt,),
        in_specs=[pl.BlockSpec((tt,), lambda i, k=k: (k * steps + i,), memory_space=pltpu.SMEM)
                  for k in range(TOP_K)] + [
            pl.BlockSpec((tt,), lambda i, k=k: (k * steps + jnp.minimum(i + 1, last),),
                         memory_space=pltpu.SMEM) for k in range(TOP_K)] + [
            pl.BlockSpec((tt, D), lambda i: (i, 0)),
            pl.BlockSpec((tt, TOP_K), lambda i: (i, 0)),
            pl.BlockSpec(memory_space=pl.ANY),
        ],
        out_specs=pl.BlockSpec((tt, D), lambda i: (i, 0)),
        out_shape=jax.ShapeDtypeStruct((N, D), F32),
        scratch_shapes=[pltpu.VMEM((2, TOP_K, tt * nsub, LANES), F32), pltpu.SemaphoreType.DMA((2,))],
        compiler_params=_params(("arbitrary",)),
        name="moe_combine",
    )(*([dest] * (2 * TOP_K)), x, gate, y_pad)


def _dest_kernel(pstart_ref, e_ref, rank_ref, o_ref):
    e = e_ref[...]
    seg = jnp.zeros(e.shape, I32)
    for ex in range(N_EXPERTS):
        seg = jnp.where(e == ex, pstart_ref[ex], seg)
    o_ref[...] = seg + rank_ref[...]


def _dest_rows(e_idx, rank, pstart):
    return pl.pallas_call(
        _dest_kernel,
        in_specs=[pl.BlockSpec(memory_space=pltpu.SMEM), pl.BlockSpec(e_idx.shape, lambda: (0, 0)),
                  pl.BlockSpec(rank.shape, lambda: (0, 0))],
        out_specs=pl.BlockSpec(rank.shape, lambda: (0, 0)),
        out_shape=jax.ShapeDtypeStruct(rank.shape, I32),
        name="moe_dest",
    )(pstart, e_idx, rank)


def _moe_layer(x, g, w_r, b_r, w_gu, b_gu, w_dn, b_dn, layer):
    N, D = x.shape
    E = N_EXPERTS
    tm = MOE_TILE
    e_idx, gate_t, rank, counts = _router(x, g, w_r, b_r)
    counts = counts[:, 0].astype(I32)
    padded = ((counts + tm - 1) // tm) * tm
    pend = jnp.cumsum(padded)
    pstart = pend - padded
    dest = _dest_rows(e_idx, rank, pstart.astype(I32)).reshape(-1)
    gate = gate_t.T
    P = ((N * TOP_K + E * (tm - 1) + tm - 1) // tm) * tm
    n_used = (pend[-1] // tm).astype(I32)
    x_pad = _dispatch(x, g, dest, pend.astype(I32), n_used.reshape(1), P)
    y_pad = _experts((pstart // tm).astype(I32), (padded // tm).astype(I32), n_used.reshape(1),
                     x_pad, w_gu, b_gu, w_dn, b_dn, layer)
    return _combine(x, gate, dest, y_pad)


def _bucket_maps():
    qi = np.arange(ATT_BLOCK)[None, :]
    kj = np.arange(2 * ATT_BLOCK)[:, None]
    delta = qi + ATT_BLOCK - kj
    buckets, valids = [], []
    max_exact = N_BUCKETS // 2
    for win, dil in DILATED_GROUPS:
        n = np.maximum(delta * dil, 0)
        nf = np.maximum(n, 1).astype(np.float32)
        large = max_exact + (np.log(nf / np.float32(max_exact)) / np.float32(math.log(REL_MAX_DIST / max_exact))
                             * np.float32(N_BUCKETS - max_exact)).astype(np.int32)
        large = np.minimum(large, N_BUCKETS - 1)
        buckets.append(np.where(n < max_exact, n, large).astype(np.int32))
        valids.append(((delta >= 0) & (delta <= win // dil)).astype(np.int32))
    return np.stack(buckets), np.stack(valids)


def _bias_kernel(tab_ref, bm_ref, valid_ref, o_ref):
    g = pl.program_id(0)
    bm = bm_ref[0]
    key = lax.broadcasted_iota(I32, bm.shape, 0)

    def one_head(h, carry):
        acc = jnp.zeros(bm.shape, F32)
        for b in range(N_BUCKETS):
            acc = jnp.where(bm == b, tab_ref[b, g * ATT_HEADS + h], acc)
        band = jnp.where(valid_ref[0] > 0, acc * LOG2_E, NEG_INF)
        o_ref[h, 0] = band
        o_ref[h, 1] = jnp.where(key >= ATT_BLOCK, band, NEG_INF)
        return carry

    lax.fori_loop(0, ATT_HEADS, one_head, 0)


def _bias_tables(rel_bias):
    bm, valid = _bucket_maps()
    GH = len(DILATED_GROUPS) * ATT_HEADS
    blk = (1, 2 * ATT_BLOCK, ATT_BLOCK)
    return pl.pallas_call(
        _bias_kernel,
        grid=(len(DILATED_GROUPS),),
        in_specs=[
            pl.BlockSpec(memory_space=pltpu.SMEM),
            pl.BlockSpec(blk, lambda i: (i, 0, 0)),
            pl.BlockSpec(blk, lambda i: (i, 0, 0)),
        ],
        out_specs=pl.BlockSpec((ATT_HEADS, 2, 2 * ATT_BLOCK, ATT_BLOCK), lambda i: (i, 0, 0, 0)),
        out_shape=jax.ShapeDtypeStruct((GH, 2, 2 * ATT_BLOCK, ATT_BLOCK), F32),
        compiler_params=_params(("arbitrary",)),
        name="attn_bias",
    )(rel_bias, jnp.asarray(bm), jnp.asarray(valid))


def _cast_kernel(w_ref, o_ref):
    o_ref[...] = w_ref[...].astype(BF16)


def _to_bf16(w, layer, col_block):
    _, rows, cols = w.shape
    return pl.pallas_call(
        _cast_kernel,
        grid=(cols // col_block,),
        in_specs=[pl.BlockSpec((None, rows, col_block), lambda j: (layer, 0, j))],
        out_specs=pl.BlockSpec((rows, col_block), lambda j: (0, j)),
        out_shape=jax.ShapeDtypeStruct((rows, cols), BF16),
        compiler_params=_params(("parallel",)),
        name="cast_bf16",
    )(w)


def _qkv_kernel(x_ref, g_ref, wq_ref, wk_ref, wv_ref, qg_ref, kg_ref, qt_ref, k_ref, vt_ref,
                slab_ref, *, dil, R, nl):
    rc = pl.program_id(2)
    D = x_ref.shape[2]
    lanes = slab_ref.shape[2]

    if dil == 1:
        x = x_ref[0]
    else:
        @pl.when(rc == 0)
        def _():
            for c in range(D // lanes):
                slab_ref[c] = x_ref[0, :, c * lanes:(c + 1) * lanes]

        pieces = []
        for j in range(R):
            r = rc * R + j
            cols = [slab_ref[c, pl.ds(r, nl, stride=dil), :] for c in range(D // lanes)]
            pieces.append(jnp.concatenate(cols, axis=1))
        x = jnp.concatenate(pieces, axis=0)
    h = _rms(x, g_ref[...]).astype(BF16)
    q = jnp.dot(h, wq_ref[...], preferred_element_type=F32)
    k = jnp.dot(h, wk_ref[...], preferred_element_type=F32)
    v = jnp.dot(h, wv_ref[...], preferred_element_type=F32)
    blk = ATT_BLOCK

    def normed_tile(t, gain_ref, cs):
        heads = []
        for hh in range(t.shape[0] // HEAD_DIM):
            th = t[hh * HEAD_DIM:(hh + 1) * HEAD_DIM]
            ms = jnp.mean(th * th, axis=0, keepdims=True)
            heads.append(th * lax.rsqrt(ms + EPS))
        return jnp.concatenate(heads, axis=0) * gain_ref[cs, :]

    for j in range(R):
        for c in range(nl // blk):
            rows = slice(j * nl + c * blk, j * nl + (c + 1) * blk)
            for hp in range(q.shape[1] // blk):
                cs = slice(hp * blk, (hp + 1) * blk)
                qt_ref[0, j, cs, c * blk:(c + 1) * blk] = normed_tile(q[rows, cs].T, qg_ref, cs).astype(BF16)
                k_ref[0, j, c * blk:(c + 1) * blk, cs] = normed_tile(k[rows, cs].T, kg_ref, cs).T.astype(BF16)
                vt_ref[0, j, cs, c * blk:(c + 1) * blk] = v[rows, cs].T.astype(BF16)


def _qkv_group(x, g, w_bf, q_gain, k_gain, gi, dil):
    B, S, D = x.shape
    HD = ATT_HEADS * HEAD_DIM
    G = len(DILATED_GROUPS)
    L = S // dil
    R = min(dil, QKV_ROWS // ATT_BLOCK)
    nl = QKV_ROWS // R
    lanes = LANES
    rows_of = lambda gain: jnp.broadcast_to(jnp.tile(gain, ATT_HEADS)[:, None], (HD, lanes))
    q_rows = rows_of(q_gain * (HEAD_DIM ** -0.5 * LOG2_E))
    k_rows = rows_of(k_gain)
    wspec = lambda s: pl.BlockSpec((D, HD), lambda b, l, r: (0, s * G + gi))
    tspec = pl.BlockSpec((1, R, HD, nl), lambda b, l, r: (b, r, 0, l))
    tshape = jax.ShapeDtypeStruct((B, dil, HD, L), BF16)
    return pl.pallas_call(
        functools.partial(_qkv_kernel, dil=dil, R=R, nl=nl),
        grid=(B, L // nl, dil // R),
        in_specs=[
            pl.BlockSpec((1, nl * dil, D), lambda b, l, r: (b, l, 0)),
            pl.BlockSpec((1, D), lambda b, l, r: (0, 0)),
            wspec(0), wspec(1), wspec(2),
            pl.BlockSpec((HD, lanes), lambda b, l, r: (0, 0)),
            pl.BlockSpec((HD, lanes), lambda b, l, r: (0, 0)),
        ],
        out_specs=[tspec, pl.BlockSpec((1, R, nl, HD), lambda b, l, r: (b, r, l, 0)), tspec],
        out_shape=[tshape, jax.ShapeDtypeStruct((B, dil, L, HD), BF16), tshape],
        scratch_shapes=[pltpu.VMEM((D // lanes, nl * dil if dil > 1 else 8, lanes), F32)],
        compiler_params=_params(("arbitrary", "arbitrary", "arbitrary")),
        name=f"attn_qkv_g{gi}",
    )(x, g.reshape(1, D), w_bf, w_bf, w_bf, q_rows, k_rows)


def _attn_kernel(*refs, single, nitems):
    if single:
        qt_ref, kc_ref, vtc_ref, bias_ref, o_ref, l_ref, s_ref, p_ref = refs
    else:
        qt_ref, kc_ref, kp_ref, vtc_ref, vtp_ref, bias_ref, o_ref, l_ref, s_ref, p_ref = refs
    blk = ATT_BLOCK
    pair = 2 * HEAD_DIM
    first = jnp.where(pl.program_id(2) == 0, 1, 0)
    if single:
        k0 = blk
        variants = [1] * nitems
        q_of = lambda it, cs: qt_ref[0, it, cs, :]
        k_of = lambda it, cs: kc_ref[0, it, :, cs]
        vt_of = lambda it, cs: vtc_ref[0, it, cs, :]
    else:
        k0 = 0
        variants = [first] + [0] * (nitems - 1)

        def q_of(it, cs):
            return qt_ref[0, 0, cs, it * blk:(it + 1) * blk]

        def k_of(it, cs):
            if it == 0:
                return jnp.concatenate([kp_ref[0, 0, :, cs], kc_ref[0, 0, :blk, cs]], axis=0)
            return kc_ref[0, 0, (it - 1) * blk:(it + 1) * blk, cs]

        def vt_of(it, cs):
            if it == 0:
                return jnp.concatenate([vtp_ref[0, 0, cs, :], vtc_ref[0, 0, cs, :blk]], axis=1)
            return vtc_ref[0, 0, cs, (it - 1) * blk:(it + 1) * blk]

    feat = lax.broadcasted_iota(I32, (pair, blk), 0)
    head_rows = [jnp.where(feat < HEAD_DIM, 1.0, 0.0).astype(BF16),
                 jnp.where(feat < HEAD_DIM, 0.0, 1.0).astype(BF16)]
    invs = {}
    lse_rows = {}

    def scores(it, pr):
        cs = slice(pr * pair, (pr + 1) * pair)
        qt2 = q_of(it, cs)
        k2 = k_of(it, cs)
        for half in range(2):
            qth = qt2 * head_rows[half]
            s_ref[it * ATT_HEADS + 2 * pr + half] = jnp.dot(k2, qth, preferred_element_type=F32)

    def softmax(it, pr):
        for head in (2 * pr, 2 * pr + 1):
            s = s_ref[it * ATT_HEADS + head] + bias_ref[head, variants[it], k0:, :]
            m = jnp.max(s, axis=0, keepdims=True)
            p = jnp.exp2(s - m)
            den = jnp.sum(p, axis=0, keepdims=True)
            p_ref[it * ATT_HEADS + head] = p.astype(BF16)
            invs[it, head] = 1.0 / den
            lse_rows[it, head] = (m + jnp.log2(den)) * (1.0 / LOG2_E)

    def outputs(it, pr):
        cs = slice(pr * pair, (pr + 1) * pair)
        vt2 = vt_of(it, cs)
        halves = []
        for half in range(2):
            head = 2 * pr + half
            vth = vt2[half * HEAD_DIM:(half + 1) * HEAD_DIM]
            ot = jnp.dot(vth, p_ref[it * ATT_HEADS + head], preferred_element_type=F32)
            halves.append(ot * invs[it, head])
        val = jnp.concatenate(halves, axis=0).T
        if single:
            o_ref[0, it, :, cs] = val
        else:
            o_ref[0, 0, it * blk:(it + 1) * blk, cs] = val

    work = [(it, pr) for it in range(nitems) for pr in range(ATT_HEADS // 2)]
    for it, pr in work:
        scores(it, pr)
    for it, pr in work:
        softmax(it, pr)
    for it, pr in work:
        outputs(it, pr)
    lse_row = lax.broadcasted_iota(I32, (blk, blk), 0)
    for it in range(nitems):
        lse_t = jnp.zeros((blk, blk), F32)
        for head in range(ATT_HEADS):
            lse_t = jnp.where(lse_row == head, lse_rows[it, head], lse_t)
        if single:
            l_ref[0, it] = lse_t.T
        else:
            l_ref[0, 0, it * blk:(it + 1) * blk, :] = lse_t.T


def _attn_group(qt, k, vt, bias, gi):
    B, dil, L, HD = k.shape
    blk = ATT_BLOCK
    nb = L // blk
    single = nb == 1
    ni = min(ATT_ITEMS, dil if single else nb)
    bias_spec = pl.BlockSpec((ATT_HEADS, 2, 2 * blk, blk), lambda b, r, n: (gi, 0, 0, 0))
    if single:
        rows = pl.BlockSpec((1, ni, blk, HD), lambda b, r, n: (b, r, 0, 0))
        cols = pl.BlockSpec((1, ni, HD, blk), lambda b, r, n: (b, r, 0, 0))
        grid = (B, dil // ni, 1)
        in_specs = [cols, rows, cols, bias_spec]
        args = (qt, k, vt, bias)
        out_specs = [rows, pl.BlockSpec((1, ni, blk, blk), lambda b, r, n: (b, r, 0, 0))]
        nk = blk
    else:
        rows = pl.BlockSpec((1, 1, ni * blk, HD), lambda b, r, n: (b, r, n, 0))
        cols = pl.BlockSpec((1, 1, HD, ni * blk), lambda b, r, n: (b, r, 0, n))
        prow = pl.BlockSpec((1, 1, blk, HD), lambda b, r, n: (b, r, jnp.maximum(ni * n - 1, 0), 0))
        pcol = pl.BlockSpec((1, 1, HD, blk), lambda b, r, n: (b, r, 0, jnp.maximum(ni * n - 1, 0)))
        grid = (B, dil, nb // ni)
        in_specs = [cols, rows, prow, cols, pcol, bias_spec]
        args = (qt, k, k, vt, vt, bias)
        out_specs = [rows, pl.BlockSpec((1, 1, ni * blk, blk), lambda b, r, n: (b, r, n, 0))]
        nk = 2 * blk
    return pl.pallas_call(
        functools.partial(_attn_kernel, single=single, nitems=ni),
        grid=grid,
        in_specs=in_specs,
        out_specs=out_specs,
        out_shape=[jax.ShapeDtypeStruct((B, dil, L, HD), F32),
                   jax.ShapeDtypeStruct((B, dil, L, blk), F32)],
        scratch_shapes=[pltpu.VMEM((ni * ATT_HEADS, nk, blk), F32),
                        pltpu.VMEM((ni * ATT_HEADS, nk, blk), BF16)],
        compiler_params=_params(("parallel", "parallel", "arbitrary")),
        name=f"attn_core_g{gi}",
    )(*args)


def _attn_out_kernel(x_ref, o0, o1, o2, l0, l1, l2, wo_ref, out_ref, oslab, lslab, ex_ref, *, T):
    lanes = oslab.shape[3]
    HD = o0.shape[3]

    @pl.when((pl.program_id(0) == 0) & (pl.program_id(1) == 0))
    def _():
        h = lax.broadcasted_iota(I32, ex_ref.shape, 0) & (ATT_BLOCK - 1)
        c = lax.broadcasted_iota(I32, ex_ref.shape, 1)
        owner = lax.shift_right_logical(c, HEAD_DIM.bit_length() - 1)
        ex_ref[...] = jnp.where(h == owner, 1.0, 0.0).astype(BF16)

    def token_order(o_ref, l_ref, dil, s):
        if dil == 1:
            return o_ref[0, 0], l_ref[0, 0]
        n = T // dil
        for r in range(dil):
            blk = o_ref[0, r]
            for c in range(HD // lanes):
                oslab[s, c, pl.ds(r, n, stride=dil), :] = blk[:, c * lanes:(c + 1) * lanes]
            lslab[s, pl.ds(r, n, stride=dil), :] = l_ref[0, r]
        o = jnp.concatenate([oslab[s, c] for c in range(HD // lanes)], axis=1)
        return o, lslab[s]

    groups = [token_order(o_ref, l_ref, dil, s)
              for s, ((_, dil), o_ref, l_ref) in enumerate(zip(DILATED_GROUPS, (o0, o1, o2), (l0, l1, l2)))]
    m = jnp.maximum(jnp.maximum(groups[0][1], groups[1][1]), groups[2][1])
    es = [jnp.exp(l - m) for _, l in groups]
    inv = 1.0 / (es[0] + es[1] + es[2])
    att = jnp.zeros((T, HD), F32)
    for (o, _), e in zip(groups, es):
        a = e * inv
        a_hi = a.astype(BF16)
        a_lo = (a - a_hi.astype(F32)).astype(BF16)
        wide = jnp.dot(jnp.concatenate([a_hi, a_lo], axis=1), ex_ref[...], preferred_element_type=F32)
        att = att + wide * o
    out_ref[0] = x_ref[0] + jnp.dot(att.astype(BF16), wo_ref[...], preferred_element_type=F32)


def _attn_out(x, os, ls, wo_bf):
    B, S, D = x.shape
    HD = wo_bf.shape[0]
    T = OUT_TILE
    lanes = LANES
    ospec = lambda dil, w: pl.BlockSpec((1, dil, T // dil, w), lambda b, i: (b, 0, i, 0))
    dils = [dil for _, dil in DILATED_GROUPS]
    xspec = pl.BlockSpec((1, T, D), lambda b, i: (b, i, 0))
    return pl.pallas_call(
        functools.partial(_attn_out_kernel, T=T),
        grid=(B, S // T),
        in_specs=[xspec] + [ospec(d, HD) for d in dils] + [ospec(d, ATT_BLOCK) for d in dils]
                 + [pl.BlockSpec((HD, D), lambda b, i: (0, 0))],
        out_specs=xspec,
        out_shape=jax.ShapeDtypeStruct((B, S, D), F32),
        scratch_shapes=[pltpu.VMEM((len(dils), HD // lanes, T, lanes), F32),
                        pltpu.VMEM((len(dils), T, ATT_BLOCK), F32),
                        pltpu.VMEM((2 * ATT_BLOCK, HD), BF16)],
        compiler_params=_params(("arbitrary", "arbitrary")),
        name="attn_out",
    )(x, *os, *ls, wo_bf)


def _attention_layer(x, g, w_qkv, q_gain, k_gain, w_o, rel_bias, layer):
    HD = ATT_HEADS * HEAD_DIM
    bias = _bias_tables(rel_bias)
    wqkv_bf = _to_bf16(w_qkv, layer, HD)
    wo_bf = _to_bf16(w_o, layer, w_o.shape[2])
    os, ls = [], []
    for gi, (_, dil) in enumerate(DILATED_GROUPS):
        qt, k, vt = _qkv_group(x, g, wqkv_bf, q_gain, k_gain, gi, dil)
        o, l = _attn_group(qt, k, vt, bias, gi)
        os.append(o)
        ls.append(l)
    return _attn_out(x, os, ls, wo_bf)


def kernel(x, norm_mix_g, norm_ffn_g, pool_w, pool_scale, attn_w_qkv, attn_q_gain, attn_k_gain,
           attn_w_o, rel_bias, moe_w_router, moe_b_router, moe_w_gate_up, moe_b_gate_up,
           moe_w_down, moe_b_down):
    B, S, D = x.shape
    depth = norm_mix_g.shape[0]
    for i in range(depth):
        j = i // 2
        if i % 2 == 0:
            x = _pool_layer(x, norm_mix_g[i], pool_w[j], pool_scale[j])
        else:
            x = _attention_layer(x, norm_mix_g[i], attn_w_qkv, attn_q_gain[j], attn_k_gain[j],
                                 attn_w_o, rel_bias, j)
        x = _moe_layer(x.reshape(B * S, D), norm_ffn_g[i], moe_w_router[i], moe_b_router[i],
                       moe_w_gate_up, moe_b_gate_up[i], moe_w_down, moe_b_down[i],
                       i).reshape(B, S, D)
    return x
```

```python
import functools
import math

import numpy as np
import jax
import jax.numpy as jnp
from jax import lax
from jax.experimental import pallas as pl
from jax.experimental.pallas import tpu as pltpu

F32 = jnp.float32
BF16 = jnp.bfloat16
I32 = jnp.int32

EPS = 1e-6
POOL_WINDOWS = (2, 4, 8, 16)
POOL_HALO = max(POOL_WINDOWS)
DILATED_GROUPS = ((128, 1), (512, 4), (2048, 16))
ATT_HEADS = 16
HEAD_DIM = 64
ATT_BLOCK = 128
ATT_ITEMS = 8
N_BUCKETS = 32
REL_MAX_DIST = 2048
NEG_INF = -1e30
LOG2_E = math.log2(math.e)
N_EXPERTS = 32
TOP_K = 4
SWIGLU_LIMIT = 7.0
SWIGLU_ALPHA = 1.702

V7X_VMEM_LIMIT_BYTES = 56 * 1024 * 1024
ISSUE_UNROLL = 64
MOE_TILE = 512
DISPATCH_TILE = 512
COMBINE_TILE = 256
ROUTER_TILE = 1024
POOL_TILE = 1024
QKV_ROWS = 1024
LANES = 128
OUT_TILE = 512


def _rms(xf, g):
    ms = jnp.mean(xf * xf, axis=-1, keepdims=True)
    return xf * lax.rsqrt(ms + EPS) * g


def _params(sem):
    return pltpu.CompilerParams(dimension_semantics=sem, vmem_limit_bytes=V7X_VMEM_LIMIT_BYTES)


def _pool_kernel(x_ref, halo_ref, g_ref, w_ref, sc_ref, o_ref, *, ts, dg):
    i = pl.program_id(1)
    x = x_ref[0]
    g = g_ref[...]
    h = _rms(x, g)
    hh = _rms(halo_ref[0], g)
    hh = jnp.where(i == 0, 0.0, hh)
    full = jnp.concatenate([hh, h], axis=0)
    pos = i * ts + lax.broadcasted_iota(I32, (ts, 1), 0)
    outs = []
    for gi, w in enumerate(POOL_WINDOWS):
        s = full[:, gi * dg:(gi + 1) * dg]
        sh = 1
        while sh < w:
            s = s + pltpu.roll(s, sh, 0)
            sh *= 2
        s = s[POOL_HALO:]
        cnt = jnp.minimum(pos + 1, w).astype(F32)
        p = s / cnt - h[:, gi * dg:(gi + 1) * dg]
        outs.append(jnp.dot(p.astype(BF16), w_ref[gi].astype(BF16),
                            preferred_element_type=F32))
    y = jnp.concatenate(outs, axis=1) * sc_ref[...]
    o_ref[0] = x + y


def _pool_layer(x, g, w_groups, scale):
    B, S, D = x.shape
    ts = POOL_TILE
    dg = D // len(POOL_WINDOWS)
    hb = ts // POOL_HALO
    return pl.pallas_call(
        functools.partial(_pool_kernel, ts=ts, dg=dg),
        grid=(B, S // ts),
        in_specs=[
            pl.BlockSpec((1, ts, D), lambda b, i: (b, i, 0)),
            pl.BlockSpec((1, POOL_HALO, D), lambda b, i: (b, jnp.maximum(i * hb - 1, 0), 0)),
            pl.BlockSpec((1, D), lambda b, i: (0, 0)),
            pl.BlockSpec((len(POOL_WINDOWS), dg, dg), lambda b, i: (0, 0, 0)),
            pl.BlockSpec((1, D), lambda b, i: (0, 0)),
        ],
        out_specs=pl.BlockSpec((1, ts, D), lambda b, i: (b, i, 0)),
        out_shape=jax.ShapeDtypeStruct((B, S, D), F32),
        compiler_params=_params(("parallel", "parallel")),
        name="pool_mixer",
    )(x, x, g.reshape(1, D), w_groups, scale.reshape(1, D))


def _router_kernel(x_ref, g_ref, wr_ref, br_ref, e_ref, gate_ref, rank_ref, cnt_ref,
                   tri_ref, carry_ref, wcat_ref, *, tt):
    i = pl.program_id(0)
    E = N_EXPERTS

    @pl.when(i == 0)
    def _():
        r = lax.broadcasted_iota(I32, tri_ref.shape, 0)
        c = lax.broadcasted_iota(I32, tri_ref.shape, 1)
        tri_ref[...] = jnp.where(r < c, 1.0, 0.0).astype(BF16)
        carry_ref[...] = jnp.zeros_like(carry_ref)
        w = wr_ref[...]
        w_hi = w.astype(BF16)
        wcat_ref[:, :LANES] = w_hi
        wcat_ref[:, LANES:] = (w - w_hi.astype(F32)).astype(BF16)

    h = _rms(x_ref[...], g_ref[...])
    h_hi = h.astype(BF16)
    h_lo = (h - h_hi.astype(F32)).astype(BF16)
    both = jnp.dot(h_hi, wcat_ref[...], preferred_element_type=F32)
    cross = jnp.dot(h_lo, wcat_ref[:, :LANES], preferred_element_type=F32)
    logits = both[:, :LANES] + (both[:, LANES:] + cross) + br_ref[...]
    nblk = tt // LANES
    l = jnp.concatenate([logits[c * LANES:(c + 1) * LANES].T for c in range(nblk)], axis=1)[:E]
    row = lax.broadcasted_iota(I32, (E, tt), 0).astype(F32)
    vals, sels, idxs = [], [], []
    for k in range(TOP_K):
        m = jnp.max(l, axis=0, keepdims=True)
        idx = jnp.min(jnp.where(l == m, row, float(E)), axis=0, keepdims=True)
        sel = row == idx
        vals.append(m)
        sels.append(sel)
        idxs.append(idx)
        l = jnp.where(sel, -jnp.inf, l)
    ex = [jnp.exp(v - vals[0]) for v in vals]
    den = ex[0] + ex[1] + ex[2] + ex[3]
    multi = jnp.zeros((E, tt), F32)
    for sel in sels:
        multi = multi + jnp.where(sel, 1.0, 0.0)
    base = carry_ref[:, :1]
    parts = []
    for c in range(nblk):
        mc = multi[:, c * LANES:(c + 1) * LANES]
        parts.append(jnp.dot(mc.astype(BF16), tri_ref[...], preferred_element_type=F32) + base)
        base = base + jnp.sum(mc, axis=1, keepdims=True)
    before = jnp.concatenate(parts, axis=1)
    kk = lax.broadcasted_iota(I32, (TOP_K, tt), 0)
    e_out = jnp.zeros((TOP_K, tt), F32)
    g_out = jnp.zeros((TOP_K, tt), F32)
    r_out = jnp.zeros((TOP_K, tt), F32)
    for k in range(TOP_K):
        e_out = jnp.where(kk == k, idxs[k], e_out)
        g_out = jnp.where(kk == k, ex[k] / den, g_out)
        rk = jnp.sum(jnp.where(sels[k], before, 0.0), axis=0, keepdims=True)
        r_out = jnp.where(kk == k, rk, r_out)
    e_ref[...] = e_out.astype(I32)
    gate_ref[...] = g_out
    rank_ref[...] = r_out.astype(I32)
    carry_ref[...] = jnp.broadcast_to(base, carry_ref.shape)
    cnt_ref[...] = carry_ref[...]


def _router(x, g, w_r, b_r):
    N, D = x.shape
    tt = ROUTER_TILE
    E = N_EXPERTS
    w_pad = jnp.pad(w_r, ((0, 0), (0, LANES - E)))
    b_pad = jnp.pad(b_r, (0, LANES - E)).reshape(1, LANES)
    kspec = pl.BlockSpec((TOP_K, tt), lambda i: (0, i))
    return pl.pallas_call(
        functools.partial(_router_kernel, tt=tt),
        grid=(N // tt,),
        in_specs=[
            pl.BlockSpec((tt, D), lambda i: (i, 0)),
            pl.BlockSpec((1, D), lambda i: (0, 0)),
            pl.BlockSpec((D, LANES), lambda i: (0, 0)),
            pl.BlockSpec((1, LANES), lambda i: (0, 0)),
        ],
        out_specs=[kspec, kspec, kspec, pl.BlockSpec((E, LANES), lambda i: (0, 0))],
        out_shape=[
            jax.ShapeDtypeStruct((TOP_K, N), I32),
            jax.ShapeDtypeStruct((TOP_K, N), F32),
            jax.ShapeDtypeStruct((TOP_K, N), I32),
            jax.ShapeDtypeStruct((E, LANES), F32),
        ],
        scratch_shapes=[pltpu.VMEM((LANES, LANES), BF16), pltpu.VMEM((E, LANES), F32),
                        pltpu.VMEM((D, 2 * LANES), BF16)],
        compiler_params=_params(("arbitrary",)),
        name="moe_router",
    )(x, g.reshape(1, D), w_pad, b_pad)


def _to_tile_rows(ref, val):
    n, D = val.shape
    nsub = D // LANES
    for s in range(nsub):
        ref[pl.ds(s, n, stride=nsub), :] = val[:, s * LANES:(s + 1) * LANES]


def _from_tile_rows(ref, n, nsub):
    return jnp.concatenate([ref[pl.ds(s, n, stride=nsub), :] for s in range(nsub)], axis=1)


def _tile_row(ref, r, nsub, count=1):
    return ref.at[pl.ds(pl.multiple_of(r * nsub, nsub), count * nsub)]


def _row_copy_wait(src, dst, sem, times):
    for _ in range(times):
        pltpu.make_async_copy(src, dst.at[pl.ds(0, src.shape[0])], sem).wait()


def _dispatch_kernel(pend_ref, nu_ref, *refs, tt, tm, nt, nsub):
    dest_refs = refs[:TOP_K]
    x_ref, g_ref, xpad_ref, hbuf, zbuf, sems, zsems = refs[TOP_K:]
    i = pl.program_id(0)
    slot = i % 2
    last = pl.num_programs(0) - 1

    def seg_copy(e):
        start = pl.multiple_of(jnp.maximum(pend_ref[e] - tm, 0), tm)
        return pltpu.make_async_copy(zbuf, _tile_row(xpad_ref, start, nsub, tm), zsems.at[0])

    def tail_copy(j):
        return pltpu.make_async_copy(zbuf, _tile_row(xpad_ref, (nt - 1 - j) * tm, nsub, tm), zsems.at[1])

    @pl.when(i == 0)
    def _():
        zbuf[...] = jnp.zeros_like(zbuf)
        for e in range(N_EXPERTS):
            seg_copy(e).start()
        for j in range(N_EXPERTS):
            pl.when(nt - 1 - j >= nu_ref[0])(tail_copy(j).start)

    hb = hbuf.at[slot]
    _to_tile_rows(hb, _rms(x_ref[...], g_ref[...]))

    @pl.when(i == 0)
    def _():
        for e in range(N_EXPERTS):
            seg_copy(e).wait()

    @pl.when(i == last)
    def _():
        for j in range(N_EXPERTS):
            pl.when(nt - 1 - j >= nu_ref[0])(tail_copy(j).wait)

    def issue(t, carry):
        for k in range(TOP_K):
            d = dest_refs[k][t]
            pltpu.make_async_copy(_tile_row(hb, t, nsub), _tile_row(xpad_ref, d, nsub),
                                  sems.at[slot]).start(priority=k % 2)
        return carry

    lax.fori_loop(0, tt, issue, 0, unroll=ISSUE_UNROLL)

    @pl.when(i > 0)
    def _():
        _row_copy_wait(hbuf.at[1 - slot], xpad_ref, sems.at[1 - slot], TOP_K)

    @pl.when(i == pl.num_programs(0) - 1)
    def _():
        _row_copy_wait(hb, xpad_ref, sems.at[slot], TOP_K)


def _dispatch(x, g, dest, pend, n_used, P):
    N, D = x.shape
    tt = DISPATCH_TILE
    tm = MOE_TILE
    nsub = D // LANES
    grid_spec = pltpu.PrefetchScalarGridSpec(
        num_scalar_prefetch=2,
        grid=(N // tt,),
        in_specs=[pl.BlockSpec((tt,), lambda i, pe, nu, k=k: (k * (N // tt) + i,), memory_space=pltpu.SMEM)
                  for k in range(TOP_K)] + [
            pl.BlockSpec((tt, D), lambda i, pe, nu: (i, 0)),
            pl.BlockSpec((1, D), lambda i, pe, nu: (0, 0)),
        ],
        out_specs=pl.BlockSpec(memory_space=pl.ANY),
        scratch_shapes=[pltpu.VMEM((2, tt * nsub, LANES), F32), pltpu.VMEM((tm * nsub, LANES), F32),
                        pltpu.SemaphoreType.DMA((2,)), pltpu.SemaphoreType.DMA((2,))],
    )
    return pl.pallas_call(
        functools.partial(_dispatch_kernel, tt=tt, tm=tm, nt=P // tm, nsub=nsub),
        grid_spec=grid_spec,
        out_shape=jax.ShapeDtypeStruct((P * nsub, LANES), F32),
        compiler_params=_params(("arbitrary",)),
        name="moe_dispatch",
    )(pend, n_used, *([dest] * TOP_K), x, g.reshape(1, D))


def _expert_kernel(t0_ref, nt_ref, nu_ref, xpad_hbm, wgu_ref, bgu_ref, wdn_ref, bdn_ref, ypad_hbm,
                   wgu_bf, wdn_bf, xbuf, ybuf, xsems, ysems, *, F, tm, nsub, nt_all):
    e = pl.program_id(0)
    n = nt_ref[e]
    t0 = t0_ref[e]

    def tile_rows(ref, t):
        return _tile_row(ref, t * tm, nsub, tm)

    def x_copy(j, s):
        return pltpu.make_async_copy(tile_rows(xpad_hbm, t0 + j), xbuf.at[s], xsems.at[s])

    def y_copy(t, s):
        return pltpu.make_async_copy(ybuf.at[s], tile_rows(ypad_hbm, t), ysems.at[s])

    @pl.when(n > 0)
    def _():
        x_copy(0, 0).start(priority=1)
        wgu_bf[...] = wgu_ref[0].astype(BF16)
        wdn_bf[...] = wdn_ref[0].astype(BF16)

        def tile(j, carry):
            s = j % 2
            x_copy(j, s).wait()
            pl.when(j + 1 < n)(lambda: x_copy(j + 1, 1 - s).start(priority=1))
            pl.when(j >= 2)(y_copy(t0 + j - 2, s).wait)
            x = _from_tile_rows(xbuf.at[s], tm, nsub).astype(BF16)
            gu = jnp.dot(x, wgu_bf[...], preferred_element_type=F32) + bgu_ref[0]
            gate = jnp.minimum(gu[:, :F], SWIGLU_LIMIT)
            up = jnp.clip(gu[:, F:], -SWIGLU_LIMIT, SWIGLU_LIMIT)
            glu = gate * jax.nn.sigmoid(SWIGLU_ALPHA * gate)
            a = ((up + 1.0) * glu).astype(BF16)
            _to_tile_rows(ybuf.at[s], jnp.dot(a, wdn_bf[...], preferred_element_type=F32) + bdn_ref[0])
            y_copy(t0 + j, s).start(priority=1)
            return carry

        lax.fori_loop(0, n, tile, 0)
        pl.when(n >= 2)(y_copy(t0 + n - 2, n % 2).wait)
        y_copy(t0 + n - 1, (n - 1) % 2).wait()

    @pl.when(e == pl.num_programs(0) - 1)
    def _():
        ybuf[0] = jnp.zeros(ybuf.shape[1:], ybuf.dtype)
        lax.fori_loop(nu_ref[0], nt_all, lambda t, c: (y_copy(t, 0).start(), c)[1], 0)
        lax.fori_loop(nu_ref[0], nt_all, lambda t, c: (y_copy(t, 0).wait(), c)[1], 0)


def _experts(first_tile, n_tiles, n_used, x_pad, w_gu, b_gu, w_dn, b_dn, layer):
    _, E, D, F2 = w_gu.shape
    F = F2 // 2
    tm = MOE_TILE
    nsub = D // LANES
    P = x_pad.shape[0] // nsub
    exp4 = lambda e, t0, nt, nu: (layer, e, 0, 0)
    exp3 = lambda e, t0, nt, nu: (e, 0, 0)
    grid_spec = pltpu.PrefetchScalarGridSpec(
        num_scalar_prefetch=3,
        grid=(E,),
        in_specs=[
            pl.BlockSpec(memory_space=pl.ANY),
            pl.BlockSpec((None, 1, D, F2), exp4),
            pl.BlockSpec((1, 1, F2), exp3),
            pl.BlockSpec((None, 1, F, D), exp4),
            pl.BlockSpec((1, 1, D), exp3),
        ],
        out_specs=pl.BlockSpec(memory_space=pl.ANY),
        scratch_shapes=[pltpu.VMEM((D, F2), BF16), pltpu.VMEM((F, D), BF16),
                        pltpu.VMEM((2, tm * nsub, LANES), F32), pltpu.VMEM((2, tm * nsub, LANES), F32),
                        pltpu.SemaphoreType.DMA((2,)), pltpu.SemaphoreType.DMA((2,))],
    )
    return pl.pallas_call(
        functools.partial(_expert_kernel, F=F, tm=tm, nsub=nsub, nt_all=P // tm),
        grid_spec=grid_spec,
        out_shape=jax.ShapeDtypeStruct((P * nsub, LANES), F32),
        compiler_params=_params(("arbitrary",)),
        name="moe_experts",
    )(first_tile, n_tiles, n_used, x_pad, w_gu, b_gu.reshape(E, 1, F2), w_dn, b_dn.reshape(E, 1, D))


def _combine_kernel(*refs, tt, nsub):
    dest_refs = refs[:TOP_K]
    dnext_refs = refs[TOP_K:2 * TOP_K]
    x_ref, gate_ref, ypad_ref, o_ref, ybuf, sems = refs[2 * TOP_K:]
    i = pl.program_id(0)
    slot = i % 2

    def gather(idx_refs, s):
        def issue(t, carry):
            for k in range(TOP_K):
                d = idx_refs[k][t]
                pltpu.make_async_copy(_tile_row(ypad_ref, d, nsub), _tile_row(ybuf.at[s, k], t, nsub),
                                      sems.at[s]).start(priority=k % 2)
            return carry

        lax.fori_loop(0, tt, issue, 0, unroll=ISSUE_UNROLL)

    @pl.when(i == 0)
    def _():
        gather(dest_refs, slot)

    @pl.when(i + 1 < pl.num_programs(0))
    def _():
        gather(dnext_refs, 1 - slot)

    for k in range(TOP_K):
        _row_copy_wait(ybuf.at[slot, k], ypad_ref, sems.at[slot], 1)
    gate = gate_ref[...]
    x = x_ref[...]
    cols = []
    for c in range(nsub):
        acc = x[:, c * LANES:(c + 1) * LANES]
        for k in range(TOP_K):
            acc = acc + ybuf[slot, k, pl.ds(c, tt, stride=nsub), :] * gate[:, k:k + 1]
        cols.append(acc)
    o_ref[...] = jnp.concatenate(cols, axis=1)


def _combine(x, gate, dest, y_pad):
    N, D = x.shape
    tt = COMBINE_TILE
    nsub = D // LANES
    steps = N // tt
    last = steps - 1
    return pl.pallas_call(
        functools.partial(_combine_kernel, tt=tt, nsub=nsub),
        grid=(N // tt,),
        in_specs=[pl.BlockSpec((tt,), lambda i, k=k: (k * steps + i,), memory_space=pltpu.SMEM)
                  for k in range(TOP_K)] + [
            pl.BlockSpec((tt,), lambda i, k=k: (k * steps + jnp.minimum(i + 1, last),),
                         memory_space=pltpu.SMEM) for k in range(TOP_K)] + [
            pl.BlockSpec((tt, D), lambda i: (i, 0)),
            pl.BlockSpec((tt, TOP_K), lambda i: (i, 0)),
            pl.BlockSpec(memory_space=pl.ANY),
        ],
        out_specs=pl.BlockSpec((tt, D), lambda i: (i, 0)),
        out_shape=jax.ShapeDtypeStruct((N, D), F32),
        scratch_shapes=[pltpu.VMEM((2, TOP_K, tt * nsub, LANES), F32), pltpu.SemaphoreType.DMA((2,))],
        compiler_params=_params(("arbitrary",)),
        name="moe_combine",
    )(*([dest] * (2 * TOP_K)), x, gate, y_pad)


def _dest_kernel(pstart_ref, e_ref, rank_ref, o_ref):
    e = e_ref[...]
    seg = jnp.zeros(e.shape, I32)
    for ex in range(N_EXPERTS):
        seg = jnp.where(e == ex, pstart_ref[ex], seg)
    o_ref[...] = seg + rank_ref[...]


def _dest_rows(e_idx, rank, pstart):
    return pl.pallas_call(
        _dest_kernel,
        in_specs=[pl.BlockSpec(memory_space=pltpu.SMEM), pl.BlockSpec(e_idx.shape, lambda: (0, 0)),
                  pl.BlockSpec(rank.shape, lambda: (0, 0))],
        out_specs=pl.BlockSpec(rank.shape, lambda: (0, 0)),
        out_shape=jax.ShapeDtypeStruct(rank.shape, I32),
        name="moe_dest",
    )(pstart, e_idx, rank)


def _moe_layer(x, g, w_r, b_r, w_gu, b_gu, w_dn, b_dn, layer):
    N, D = x.shape
    E = N_EXPERTS
    tm = MOE_TILE
    e_idx, gate_t, rank, counts = _router(x, g, w_r, b_r)
    counts = counts[:, 0].astype(I32)
    padded = ((counts + tm - 1) // tm) * tm
    pend = jnp.cumsum(padded)
    pstart = pend - padded
    dest = _dest_rows(e_idx, rank, pstart.astype(I32)).reshape(-1)
    gate = gate_t.T
    P = ((N * TOP_K + E * (tm - 1) + tm - 1) // tm) * tm
    n_used = (pend[-1] // tm).astype(I32)
    x_pad = _dispatch(x, g, dest, pend.astype(I32), n_used.reshape(1), P)
    y_pad = _experts((pstart // tm).astype(I32), (padded // tm).astype(I32), n_used.reshape(1),
                     x_pad, w_gu, b_gu, w_dn, b_dn, layer)
    return _combine(x, gate, dest, y_pad)


def _bucket_maps():
    qi = np.arange(ATT_BLOCK)[None, :]
    kj = np.arange(2 * ATT_BLOCK)[:, None]
    delta = qi + ATT_BLOCK - kj
    buckets, valids = [], []
    max_exact = N_BUCKETS // 2
    for win, dil in DILATED_GROUPS:
        n = np.maximum(delta * dil, 0)
        nf = np.maximum(n, 1).astype(np.float32)
        large = max_exact + (np.log(nf / np.float32(max_exact)) / np.float32(math.log(REL_MAX_DIST / max_exact))
                             * np.float32(N_BUCKETS - max_exact)).astype(np.int32)
        large = np.minimum(large, N_BUCKETS - 1)
        buckets.append(np.where(n < max_exact, n, large).astype(np.int32))
        valids.append(((delta >= 0) & (delta <= win // dil)).astype(np.int32))
    return np.stack(buckets), np.stack(valids)


def _bias_kernel(tab_ref, bm_ref, valid_ref, o_ref):
    g = pl.program_id(0)
    bm = bm_ref[0]
    key = lax.broadcasted_iota(I32, bm.shape, 0)

    def one_head(h, carry):
        acc = jnp.zeros(bm.shape, F32)
        for b in range(N_BUCKETS):
            acc = jnp.where(bm == b, tab_ref[b, g * ATT_HEADS + h], acc)
        band = jnp.where(valid_ref[0] > 0, acc * LOG2_E, NEG_INF)
        o_ref[h, 0] = band
        o_ref[h, 1] = jnp.where(key >= ATT_BLOCK, band, NEG_INF)
        return carry

    lax.fori_loop(0, ATT_HEADS, one_head, 0)


def _bias_tables(rel_bias):
    bm, valid = _bucket_maps()
    GH = len(DILATED_GROUPS) * ATT_HEADS
    blk = (1, 2 * ATT_BLOCK, ATT_BLOCK)
    return pl.pallas_call(
        _bias_kernel,
        grid=(len(DILATED_GROUPS),),
        in_specs=[
            pl.BlockSpec(memory_space=pltpu.SMEM),
            pl.BlockSpec(blk, lambda i: (i, 0, 0)),
            pl.BlockSpec(blk, lambda i: (i, 0, 0)),
        ],
        out_specs=pl.BlockSpec((ATT_HEADS, 2, 2 * ATT_BLOCK, ATT_BLOCK), lambda i: (i, 0, 0, 0)),
        out_shape=jax.ShapeDtypeStruct((GH, 2, 2 * ATT_BLOCK, ATT_BLOCK), F32),
        compiler_params=_params(("arbitrary",)),
        name="attn_bias",
    )(rel_bias, jnp.asarray(bm), jnp.asarray(valid))


def _cast_kernel(w_ref, o_ref):
    o_ref[...] = w_ref[...].astype(BF16)


def _to_bf16(w, layer, col_block):
    _, rows, cols = w.shape
    return pl.pallas_call(
        _cast_kernel,
        grid=(cols // col_block,),
        in_specs=[pl.BlockSpec((None, rows, col_block), lambda j: (layer, 0, j))],
        out_specs=pl.BlockSpec((rows, col_block), lambda j: (0, j)),
        out_shape=jax.ShapeDtypeStruct((rows, cols), BF16),
        compiler_params=_params(("parallel",)),
        name="cast_bf16",
    )(w)


def _qkv_kernel(x_ref, g_ref, wq_ref, wk_ref, wv_ref, qg_ref, kg_ref, qt_ref, k_ref, vt_ref,
                slab_ref, *, dil, R, nl):
    rc = pl.program_id(2)
    D = x_ref.shape[2]
    lanes = slab_ref.shape[2]

    if dil == 1:
        x = x_ref[0]
    else:
        @pl.when(rc == 0)
        def _():
            for c in range(D // lanes):
                slab_ref[c] = x_ref[0, :, c * lanes:(c + 1) * lanes]

        pieces = []
        for j in range(R):
            r = rc * R + j
            cols = [slab_ref[c, pl.ds(r, nl, stride=dil), :] for c in range(D // lanes)]
            pieces.append(jnp.concatenate(cols, axis=1))
        x = jnp.concatenate(pieces, axis=0)
    h = _rms(x, g_ref[...]).astype(BF16)
    q = jnp.dot(h, wq_ref[...], preferred_element_type=F32)
    k = jnp.dot(h, wk_ref[...], preferred_element_type=F32)
    v = jnp.dot(h, wv_ref[...], preferred_element_type=F32)
    blk = ATT_BLOCK

    def normed_tile(t, gain_ref, cs):
        heads = []
        for hh in range(t.shape[0] // HEAD_DIM):
            th = t[hh * HEAD_DIM:(hh + 1) * HEAD_DIM]
            ms = jnp.mean(th * th, axis=0, keepdims=True)
            heads.append(th * lax.rsqrt(ms + EPS))
        return jnp.concatenate(heads, axis=0) * gain_ref[cs, :]

    for j in range(R):
        for c in range(nl // blk):
            rows = slice(j * nl + c * blk, j * nl + (c + 1) * blk)
            for hp in range(q.shape[1] // blk):
                cs = slice(hp * blk, (hp + 1) * blk)
                qt_ref[0, j, cs, c * blk:(c + 1) * blk] = normed_tile(q[rows, cs].T, qg_ref, cs).astype(BF16)
                k_ref[0, j, c * blk:(c + 1) * blk, cs] = normed_tile(k[rows, cs].T, kg_ref, cs).T.astype(BF16)
                vt_ref[0, j, cs, c * blk:(c + 1) * blk] = v[rows, cs].T.astype(BF16)


def _qkv_group(x, g, w_bf, q_gain, k_gain, gi, dil):
    B, S, D = x.shape
    HD = ATT_HEADS * HEAD_DIM
    G = len(DILATED_GROUPS)
    L = S // dil
    R = min(dil, QKV_ROWS // ATT_BLOCK)
    nl = QKV_ROWS // R
    lanes = LANES
    rows_of = lambda gain: jnp.broadcast_to(jnp.tile(gain, ATT_HEADS)[:, None], (HD, lanes))
    q_rows = rows_of(q_gain * (HEAD_DIM ** -0.5 * LOG2_E))
    k_rows = rows_of(k_gain)
    wspec = lambda s: pl.BlockSpec((D, HD), lambda b, l, r: (0, s * G + gi))
    tspec = pl.BlockSpec((1, R, HD, nl), lambda b, l, r: (b, r, 0, l))
    tshape = jax.ShapeDtypeStruct((B, dil, HD, L), BF16)
    return pl.pallas_call(
        functools.partial(_qkv_kernel, dil=dil, R=R, nl=nl),
        grid=(B, L // nl, dil // R),
        in_specs=[
            pl.BlockSpec((1, nl * dil, D), lambda b, l, r: (b, l, 0)),
            pl.BlockSpec((1, D), lambda b, l, r: (0, 0)),
            wspec(0), wspec(1), wspec(2),
            pl.BlockSpec((HD, lanes), lambda b, l, r: (0, 0)),
            pl.BlockSpec((HD, lanes), lambda b, l, r: (0, 0)),
        ],
        out_specs=[tspec, pl.BlockSpec((1, R, nl, HD), lambda b, l, r: (b, r, l, 0)), tspec],
        out_shape=[tshape, jax.ShapeDtypeStruct((B, dil, L, HD), BF16), tshape],
        scratch_shapes=[pltpu.VMEM((D // lanes, nl * dil if dil > 1 else 8, lanes), F32)],
        compiler_params=_params(("arbitrary", "arbitrary", "arbitrary")),
        name=f"attn_qkv_g{gi}",
    )(x, g.reshape(1, D), w_bf, w_bf, w_bf, q_rows, k_rows)


def _attn_kernel(*refs, single, nitems):
    if single:
        qt_ref, kc_ref, vtc_ref, bias_ref, o_ref, l_ref, s_ref, p_ref = refs
    else:
        qt_ref, kc_ref, kp_ref, vtc_ref, vtp_ref, bias_ref, o_ref, l_ref, s_ref, p_ref = refs
    blk = ATT_BLOCK
    pair = 2 * HEAD_DIM
    first = jnp.where(pl.program_id(2) == 0, 1, 0)
    if single:
        k0 = blk
        variants = [1] * nitems
        q_of = lambda it, cs: qt_ref[0, it, cs, :]
        k_of = lambda it, cs: kc_ref[0, it, :, cs]
        vt_of = lambda it, cs: vtc_ref[0, it, cs, :]
    else:
        k0 = 0
        variants = [first] + [0] * (nitems - 1)

        def q_of(it, cs):
            return qt_ref[0, 0, cs, it * blk:(it + 1) * blk]

        def k_of(it, cs):
            if it == 0:
                return jnp.concatenate([kp_ref[0, 0, :, cs], kc_ref[0, 0, :blk, cs]], axis=0)
            return kc_ref[0, 0, (it - 1) * blk:(it + 1) * blk, cs]

        def vt_of(it, cs):
            if it == 0:
                return jnp.concatenate([vtp_ref[0, 0, cs, :], vtc_ref[0, 0, cs, :blk]], axis=1)
            return vtc_ref[0, 0, cs, (it - 1) * blk:(it + 1) * blk]

    feat = lax.broadcasted_iota(I32, (pair, blk), 0)
    head_rows = [jnp.where(feat < HEAD_DIM, 1.0, 0.0).astype(BF16),
                 jnp.where(feat < HEAD_DIM, 0.0, 1.0).astype(BF16)]
    invs = {}
    lse_rows = {}

    def scores(it, pr):
        cs = slice(pr * pair, (pr + 1) * pair)
        qt2 = q_of(it, cs)
        k2 = k_of(it, cs)
        for half in range(2):
            qth = qt2 * head_rows[half]
            s_ref[it * ATT_HEADS + 2 * pr + half] = jnp.dot(k2, qth, preferred_element_type=F32)

    def softmax(it, pr):
        for head in (2 * pr, 2 * pr + 1):
            s = s_ref[it * ATT_HEADS + head] + bias_ref[head, variants[it], k0:, :]
            m = jnp.max(s, axis=0, keepdims=True)
            p = jnp.exp2(s - m)
            den = jnp.sum(p, axis=0, keepdims=True)
            p_ref[it * ATT_HEADS + head] = p.astype(BF16)
            invs[it, head] = 1.0 / den
            lse_rows[it, head] = (m + jnp.log2(den)) * (1.0 / LOG2_E)

    def outputs(it, pr):
        cs = slice(pr * pair, (pr + 1) * pair)
        vt2 = vt_of(it, cs)
        halves = []
        for half in range(2):
            head = 2 * pr + half
            vth = vt2[half * HEAD_DIM:(half + 1) * HEAD_DIM]
            ot = jnp.dot(vth, p_ref[it * ATT_HEADS + head], preferred_element_type=F32)
            halves.append(ot * invs[it, head])
        val = jnp.concatenate(halves, axis=0).T
        if single:
            o_ref[0, it, :, cs] = val
        else:
            o_ref[0, 0, it * blk:(it + 1) * blk, cs] = val

    work = [(it, pr) for it in range(nitems) for pr in range(ATT_HEADS // 2)]
    for it, pr in work:
        scores(it, pr)
    for it, pr in work:
        softmax(it, pr)
    for it, pr in work:
        outputs(it, pr)
    lse_row = lax.broadcasted_iota(I32, (blk, blk), 0)
    for it in range(nitems):
        lse_t = jnp.zeros((blk, blk), F32)
        for head in range(ATT_HEADS):
            lse_t = jnp.where(lse_row == head, lse_rows[it, head], lse_t)
        if single:
            l_ref[0, it] = lse_t.T
        else:
            l_ref[0, 0, it * blk:(it + 1) * blk, :] = lse_t.T


def _attn_group(qt, k, vt, bias, gi):
    B, dil, L, HD = k.shape
    blk = ATT_BLOCK
    nb = L // blk
    single = nb == 1
    ni = min(ATT_ITEMS, dil if single else nb)
    bias_spec = pl.BlockSpec((ATT_HEADS, 2, 2 * blk, blk), lambda b, r, n: (gi, 0, 0, 0))
    if single:
        rows = pl.BlockSpec((1, ni, blk, HD), lambda b, r, n: (b, r, 0, 0))
        cols = pl.BlockSpec((1, ni, HD, blk), lambda b, r, n: (b, r, 0, 0))
        grid = (B, dil // ni, 1)
        in_specs = [cols, rows, cols, bias_spec]
        args = (qt, k, vt, bias)
        out_specs = [rows, pl.BlockSpec((1, ni, blk, blk), lambda b, r, n: (b, r, 0, 0))]
        nk = blk
    else:
        rows = pl.BlockSpec((1, 1, ni * blk, HD), lambda b, r, n: (b, r, n, 0))
        cols = pl.BlockSpec((1, 1, HD, ni * blk), lambda b, r, n: (b, r, 0, n))
        prow = pl.BlockSpec((1, 1, blk, HD), lambda b, r, n: (b, r, jnp.maximum(ni * n - 1, 0), 0))
        pcol = pl.BlockSpec((1, 1, HD, blk), lambda b, r, n: (b, r, 0, jnp.maximum(ni * n - 1, 0)))
        grid = (B, dil, nb // ni)
        in_specs = [cols, rows, prow, cols, pcol, bias_spec]
        args = (qt, k, k, vt, vt, bias)
        out_specs = [rows, pl.BlockSpec((1, 1, ni * blk, blk), lambda b, r, n: (b, r, n, 0))]
        nk = 2 * blk
    return pl.pallas_call(
        functools.partial(_attn_kernel, single=single, nitems=ni),
        grid=grid,
        in_specs=in_specs,
        out_specs=out_specs,
        out_shape=[jax.ShapeDtypeStruct((B, dil, L, HD), F32),
                   jax.ShapeDtypeStruct((B, dil, L, blk), F32)],
        scratch_shapes=[pltpu.VMEM((ni * ATT_HEADS, nk, blk), F32),
                        pltpu.VMEM((ni * ATT_HEADS, nk, blk), BF16)],
        compiler_params=_params(("parallel", "parallel", "arbitrary")),
        name=f"attn_core_g{gi}",
    )(*args)


def _attn_out_kernel(x_ref, o0, o1, o2, l0, l1, l2, wo_ref, out_ref, oslab, lslab, ex_ref, *, T):
    lanes = oslab.shape[3]
    HD = o0.shape[3]

    @pl.when((pl.program_id(0) == 0) & (pl.program_id(1) == 0))
    def _():
        h = lax.broadcasted_iota(I32, ex_ref.shape, 0) & (ATT_BLOCK - 1)
        c = lax.broadcasted_iota(I32, ex_ref.shape, 1)
        owner = lax.shift_right_logical(c, HEAD_DIM.bit_length() - 1)
        ex_ref[...] = jnp.where(h == owner, 1.0, 0.0).astype(BF16)

    def token_order(o_ref, l_ref, dil, s):
        if dil == 1:
            return o_ref[0, 0], l_ref[0, 0]
        n = T // dil
        for r in range(dil):
            blk = o_ref[0, r]
            for c in range(HD // lanes):
                oslab[s, c, pl.ds(r, n, stride=dil), :] = blk[:, c * lanes:(c + 1) * lanes]
            lslab[s, pl.ds(r, n, stride=dil), :] = l_ref[0, r]
        o = jnp.concatenate([oslab[s, c] for c in range(HD // lanes)], axis=1)
        return o, lslab[s]

    groups = [token_order(o_ref, l_ref, dil, s)
              for s, ((_, dil), o_ref, l_ref) in enumerate(zip(DILATED_GROUPS, (o0, o1, o2), (l0, l1, l2)))]
    m = jnp.maximum(jnp.maximum(groups[0][1], groups[1][1]), groups[2][1])
    es = [jnp.exp(l - m) for _, l in groups]
    inv = 1.0 / (es[0] + es[1] + es[2])
    att = jnp.zeros((T, HD), F32)
    for (o, _), e in zip(groups, es):
        a = e * inv
        a_hi = a.astype(BF16)
        a_lo = (a - a_hi.astype(F32)).astype(BF16)
        wide = jnp.dot(jnp.concatenate([a_hi, a_lo], axis=1), ex_ref[...], preferred_element_type=F32)
        att = att + wide * o
    out_ref[0] = x_ref[0] + jnp.dot(att.astype(BF16), wo_ref[...], preferred_element_type=F32)


def _attn_out(x, os, ls, wo_bf):
    B, S, D = x.shape
    HD = wo_bf.shape[0]
    T = OUT_TILE
    lanes = LANES
    ospec = lambda dil, w: pl.BlockSpec((1, dil, T // dil, w), lambda b, i: (b, 0, i, 0))
    dils = [dil for _, dil in DILATED_GROUPS]
    xspec = pl.BlockSpec((1, T, D), lambda b, i: (b, i, 0))
    return pl.pallas_call(
        functools.partial(_attn_out_kernel, T=T),
        grid=(B, S // T),
        in_specs=[xspec] + [ospec(d, HD) for d in dils] + [ospec(d, ATT_BLOCK) for d in dils]
                 + [pl.BlockSpec((HD, D), lambda b, i: (0, 0))],
        out_specs=xspec,
        out_shape=jax.ShapeDtypeStruct((B, S, D), F32),
        scratch_shapes=[pltpu.VMEM((len(dils), HD // lanes, T, lanes), F32),
                        pltpu.VMEM((len(dils), T, ATT_BLOCK), F32),
                        pltpu.VMEM((2 * ATT_BLOCK, HD), BF16)],
        compiler_params=_params(("arbitrary", "arbitrary")),
        name="attn_out",
    )(x, *os, *ls, wo_bf)


def _attention_layer(x, g, w_qkv, q_gain, k_gain, w_o, rel_bias, layer):
    HD = ATT_HEADS * HEAD_DIM
    bias = _bias_tables(rel_bias)
    wqkv_bf = _to_bf16(w_qkv, layer, HD)
    wo_bf = _to_bf16(w_o, layer, w_o.shape[2])
    os, ls = [], []
    for gi, (_, dil) in enumerate(DILATED_GROUPS):
        qt, k, vt = _qkv_group(x, g, wqkv_bf, q_gain, k_gain, gi, dil)
        o, l = _attn_group(qt, k, vt, bias, gi)
        os.append(o)
        ls.append(l)
    return _attn_out(x, os, ls, wo_bf)


def kernel(x, norm_mix_g, norm_ffn_g, pool_w, pool_scale, attn_w_qkv, attn_q_gain, attn_k_gain,
           attn_w_o, rel_bias, moe_w_router, moe_b_router, moe_w_gate_up, moe_b_gate_up,
           moe_w_down, moe_b_down):
    B, S, D = x.shape
    depth = norm_mix_g.shape[0]
    for i in range(depth):
        j = i // 2
        if i % 2 == 0:
            x = _pool_layer(x, norm_mix_g[i], pool_w[j], pool_scale[j])
        else:
            x = _attention_layer(x, norm_mix_g[i], attn_w_qkv, attn_q_gain[j], attn_k_gain[j],
                                 attn_w_o, rel_bias, j)
        x = _moe_layer(x.reshape(B * S, D), norm_ffn_g[i], moe_w_router[i], moe_b_router[i],
                       moe_w_gate_up, moe_b_gate_up[i], moe_w_down, moe_b_down[i],
                       i).reshape(B, S, D)
    return x
```

```python
import functools
import math

import numpy as np
import jax
import jax.numpy as jnp
from jax import lax
from jax.experimental import pallas as pl
from jax.experimental.pallas import tpu as pltpu

F32 = jnp.float32
BF16 = jnp.bfloat16
I32 = jnp.int32

EPS = 1e-6
POOL_WINDOWS = (2, 4, 8, 16)
POOL_HALO = max(POOL_WINDOWS)
DILATED_GROUPS = ((128, 1), (512, 4), (2048, 16))
ATT_HEADS = 16
HEAD_DIM = 64
ATT_BLOCK = 128
ATT_ITEMS = 8
N_BUCKETS = 32
REL_MAX_DIST = 2048
NEG_INF = -1e30
LOG2_E = math.log2(math.e)
N_EXPERTS = 32
TOP_K = 4
SWIGLU_LIMIT = 7.0
SWIGLU_ALPHA = 1.702

V7X_VMEM_LIMIT_BYTES = 56 * 1024 * 1024
ISSUE_UNROLL = 64
MOE_TILE = 512
DISPATCH_TILE = 512
COMBINE_TILE = 256
ROUTER_TILE = 1024
POOL_TILE = 1024
QKV_ROWS = 1024
LANES = 128
OUT_TILE = 512


def _rms(xf, g):
    ms = jnp.mean(xf * xf, axis=-1, keepdims=True)
    return xf * lax.rsqrt(ms + EPS) * g


def _params(sem):
    return pltpu.CompilerParams(dimension_semantics=sem, vmem_limit_bytes=V7X_VMEM_LIMIT_BYTES)


def _pool_kernel(x_ref, halo_ref, g_ref, w_ref, sc_ref, o_ref, *, ts, dg):
    i = pl.program_id(1)
    x = x_ref[0]
    g = g_ref[...]
    h = _rms(x, g)
    hh = _rms(halo_ref[0], g)
    hh = jnp.where(i == 0, 0.0, hh)
    full = jnp.concatenate([hh, h], axis=0)
    pos = i * ts + lax.broadcasted_iota(I32, (ts, 1), 0)
    outs = []
    for gi, w in enumerate(POOL_WINDOWS):
        s = full[:, gi * dg:(gi + 1) * dg]
        sh = 1
        while sh < w:
            s = s + pltpu.roll(s, sh, 0)
            sh *= 2
        s = s[POOL_HALO:]
        cnt = jnp.minimum(pos + 1, w).astype(F32)
        p = s / cnt - h[:, gi * dg:(gi + 1) * dg]
        outs.append(jnp.dot(p.astype(BF16), w_ref[gi].astype(BF16),
                            preferred_element_type=F32))
    y = jnp.concatenate(outs, axis=1) * sc_ref[...]
    o_ref[0] = x + y


def _pool_layer(x, g, w_groups, scale):
    B, S, D = x.shape
    ts = POOL_TILE
    dg = D // len(POOL_WINDOWS)
    hb = ts // POOL_HALO
    return pl.pallas_call(
        functools.partial(_pool_kernel, ts=ts, dg=dg),
        grid=(B, S // ts),
        in_specs=[
            pl.BlockSpec((1, ts, D), lambda b, i: (b, i, 0)),
            pl.BlockSpec((1, POOL_HALO, D), lambda b, i: (b, jnp.maximum(i * hb - 1, 0), 0)),
            pl.BlockSpec((1, D), lambda b, i: (0, 0)),
            pl.BlockSpec((len(POOL_WINDOWS), dg, dg), lambda b, i: (0, 0, 0)),
            pl.BlockSpec((1, D), lambda b, i: (0, 0)),
        ],
        out_specs=pl.BlockSpec((1, ts, D), lambda b, i: (b, i, 0)),
        out_shape=jax.ShapeDtypeStruct((B, S, D), F32),
        compiler_params=_params(("parallel", "parallel")),
        name="pool_mixer",
    )(x, x, g.reshape(1, D), w_groups, scale.reshape(1, D))


def _router_kernel(x_ref, g_ref, wr_ref, br_ref, e_ref, gate_ref, rank_ref, cnt_ref,
                   tri_ref, carry_ref, wcat_ref, *, tt):
    i = pl.program_id(0)
    E = N_EXPERTS

    @pl.when(i == 0)
    def _():
        r = lax.broadcasted_iota(I32, tri_ref.shape, 0)
        c = lax.broadcasted_iota(I32, tri_ref.shape, 1)
        tri_ref[...] = jnp.where(r < c, 1.0, 0.0).astype(BF16)
        carry_ref[...] = jnp.zeros_like(carry_ref)
        w = wr_ref[...]
        w_hi = w.astype(BF16)
        wcat_ref[:, :LANES] = w_hi
        wcat_ref[:, LANES:] = (w - w_hi.astype(F32)).astype(BF16)

    h = _rms(x_ref[...], g_ref[...])
    h_hi = h.astype(BF16)
    h_lo = (h - h_hi.astype(F32)).astype(BF16)
    both = jnp.dot(h_hi, wcat_ref[...], preferred_element_type=F32)
    cross = jnp.dot(h_lo, wcat_ref[:, :LANES], preferred_element_type=F32)
    logits = both[:, :LANES] + (both[:, LANES:] + cross) + br_ref[...]
    nblk = tt // LANES
    l = jnp.concatenate([logits[c * LANES:(c + 1) * LANES].T for c in range(nblk)], axis=1)[:E]
    row = lax.broadcasted_iota(I32, (E, tt), 0).astype(F32)
    vals, sels, idxs = [], [], []
    for k in range(TOP_K):
        m = jnp.max(l, axis=0, keepdims=True)
        idx = jnp.min(jnp.where(l == m, row, float(E)), axis=0, keepdims=True)
        sel = row == idx
        vals.append(m)
        sels.append(sel)
        idxs.append(idx)
        l = jnp.where(sel, -jnp.inf, l)
    ex = [jnp.exp(v - vals[0]) for v in vals]
    den = ex[0] + ex[1] + ex[2] + ex[3]
    multi = jnp.zeros((E, tt), F32)
    for sel in sels:
        multi = multi + jnp.where(sel, 1.0, 0.0)
    base = carry_ref[:, :1]
    parts = []
    for c in range(nblk):
        mc = multi[:, c * LANES:(c + 1) * LANES]
        parts.append(jnp.dot(mc.astype(BF16), tri_ref[...], preferred_element_type=F32) + base)
        base = base + jnp.sum(mc, axis=1, keepdims=True)
    before = jnp.concatenate(parts, axis=1)
    kk = lax.broadcasted_iota(I32, (TOP_K, tt), 0)
    e_out = jnp.zeros((TOP_K, tt), F32)
    g_out = jnp.zeros((TOP_K, tt), F32)
    r_out = jnp.zeros((TOP_K, tt), F32)
    for k in range(TOP_K):
        e_out = jnp.where(kk == k, idxs[k], e_out)
        g_out = jnp.where(kk == k, ex[k] / den, g_out)
        rk = jnp.sum(jnp.where(sels[k], before, 0.0), axis=0, keepdims=True)
        r_out = jnp.where(kk == k, rk, r_out)
    e_ref[...] = e_out.astype(I32)
    gate_ref[...] = g_out
    rank_ref[...] = r_out.astype(I32)
    carry_ref[...] = jnp.broadcast_to(base, carry_ref.shape)
    cnt_ref[...] = carry_ref[...]


def _router(x, g, w_r, b_r):
    N, D = x.shape
    tt = ROUTER_TILE
    E = N_EXPERTS
    w_pad = jnp.pad(w_r, ((0, 0), (0, LANES - E)))
    b_pad = jnp.pad(b_r, (0, LANES - E)).reshape(1, LANES)
    kspec = pl.BlockSpec((TOP_K, tt), lambda i: (0, i))
    return pl.pallas_call(
        functools.partial(_router_kernel, tt=tt),
        grid=(N // tt,),
        in_specs=[
            pl.BlockSpec((tt, D), lambda i: (i, 0)),
            pl.BlockSpec((1, D), lambda i: (0, 0)),
            pl.BlockSpec((D, LANES), lambda i: (0, 0)),
            pl.BlockSpec((1, LANES), lambda i: (0, 0)),
        ],
        out_specs=[kspec, kspec, kspec, pl.BlockSpec((E, LANES), lambda i: (0, 0))],
        out_shape=[
            jax.ShapeDtypeStruct((TOP_K, N), I32),
            jax.ShapeDtypeStruct((TOP_K, N), F32),
            jax.ShapeDtypeStruct((TOP_K, N), I32),
            jax.ShapeDtypeStruct((E, LANES), F32),
        ],
        scratch_shapes=[pltpu.VMEM((LANES, LANES), BF16), pltpu.VMEM((E, LANES), F32),
                        pltpu.VMEM((D, 2 * LANES), BF16)],
        compiler_params=_params(("arbitrary",)),
        name="moe_router",
    )(x, g.reshape(1, D), w_pad, b_pad)


def _to_tile_rows(ref, val):
    n, D = val.shape
    nsub = D // LANES
    for s in range(nsub):
        ref[pl.ds(s, n, stride=nsub), :] = val[:, s * LANES:(s + 1) * LANES]


def _from_tile_rows(ref, n, nsub):
    return jnp.concatenate([ref[pl.ds(s, n, stride=nsub), :] for s in range(nsub)], axis=1)


def _tile_row(ref, r, nsub, count=1):
    return ref.at[pl.ds(pl.multiple_of(r * nsub, nsub), count * nsub)]


def _row_copy_wait(src, dst, sem, times):
    for _ in range(times):
        pltpu.make_async_copy(src, dst.at[pl.ds(0, src.shape[0])], sem).wait()


def _dispatch_kernel(pend_ref, nu_ref, *refs, tt, tm, nt, nsub):
    dest_refs = refs[:TOP_K]
    x_ref, g_ref, xpad_ref, hbuf, zbuf, sems, zsems = refs[TOP_K:]
    i = pl.program_id(0)
    slot = i % 2
    last = pl.num_programs(0) - 1

    def seg_copy(e):
        start = pl.multiple_of(jnp.maximum(pend_ref[e] - tm, 0), tm)
        return pltpu.make_async_copy(zbuf, _tile_row(xpad_ref, start, nsub, tm), zsems.at[0])

    def tail_copy(j):
        return pltpu.make_async_copy(zbuf, _tile_row(xpad_ref, (nt - 1 - j) * tm, nsub, tm), zsems.at[1])

    @pl.when(i == 0)
    def _():
        zbuf[...] = jnp.zeros_like(zbuf)
        for e in range(N_EXPERTS):
            seg_copy(e).start()
        for j in range(N_EXPERTS):
            pl.when(nt - 1 - j >= nu_ref[0])(tail_copy(j).start)

    hb = hbuf.at[slot]
    _to_tile_rows(hb, _rms(x_ref[...], g_ref[...]))

    @pl.when(i == 0)
    def _():
        for e in range(N_EXPERTS):
            seg_copy(e).wait()

    @pl.when(i == last)
    def _():
        for j in range(N_EXPERTS):
            pl.when(nt - 1 - j >= nu_ref[0])(tail_copy(j).wait)

    def issue(t, carry):
        for k in range(TOP_K):
            d = dest_refs[k][t]
            pltpu.make_async_copy(_tile_row(hb, t, nsub), _tile_row(xpad_ref, d, nsub),
                                  sems.at[slot]).start(priority=k % 2)
        return carry

    lax.fori_loop(0, tt, issue, 0, unroll=ISSUE_UNROLL)

    @pl.when(i > 0)
    def _():
        _row_copy_wait(hbuf.at[1 - slot], xpad_ref, sems.at[1 - slot], TOP_K)

    @pl.when(i == pl.num_programs(0) - 1)
    def _():
        _row_copy_wait(hb, xpad_ref, sems.at[slot], TOP_K)


def _dispatch(x, g, dest, pend, n_used, P):
    N, D = x.shape
    tt = DISPATCH_TILE
    tm = MOE_TILE
    nsub = D // LANES
    grid_spec = pltpu.PrefetchScalarGridSpec(
        num_scalar_prefetch=2,
        grid=(N // tt,),
        in_specs=[pl.BlockSpec((tt,), lambda i, pe, nu, k=k: (k * (N // tt) + i,), memory_space=pltpu.SMEM)
                  for k in range(TOP_K)] + [
            pl.BlockSpec((tt, D), lambda i, pe, nu: (i, 0)),
            pl.BlockSpec((1, D), lambda i, pe, nu: (0, 0)),
        ],
        out_specs=pl.BlockSpec(memory_space=pl.ANY),
        scratch_shapes=[pltpu.VMEM((2, tt * nsub, LANES), F32), pltpu.VMEM((tm * nsub, LANES), F32),
                        pltpu.SemaphoreType.DMA((2,)), pltpu.SemaphoreType.DMA((2,))],
    )
    return pl.pallas_call(
        functools.partial(_dispatch_kernel, tt=tt, tm=tm, nt=P // tm, nsub=nsub),
        grid_spec=grid_spec,
        out_shape=jax.ShapeDtypeStruct((P * nsub, LANES), F32),
        compiler_params=_params(("arbitrary",)),
        name="moe_dispatch",
    )(pend, n_used, *([dest] * TOP_K), x, g.reshape(1, D))


def _expert_kernel(t0_ref, nt_ref, nu_ref, xpad_hbm, wgu_ref, bgu_ref, wdn_ref, bdn_ref, ypad_hbm,
                   wgu_bf, wdn_bf, xbuf, ybuf, xsems, ysems, *, F, tm, nsub, nt_all):
    e = pl.program_id(0)
    n = nt_ref[e]
    t0 = t0_ref[e]

    def tile_rows(ref, t):
        return _tile_row(ref, t * tm, nsub, tm)

    def x_copy(j, s):
        return pltpu.make_async_copy(tile_rows(xpad_hbm, t0 + j), xbuf.at[s], xsems.at[s])

    def y_copy(t, s):
        return pltpu.make_async_copy(ybuf.at[s], tile_rows(ypad_hbm, t), ysems.at[s])

    @pl.when(n > 0)
    def _():
        x_copy(0, 0).start(priority=1)
        wgu_bf[...] = wgu_ref[0].astype(BF16)
        wdn_bf[...] = wdn_ref[0].astype(BF16)

        def tile(j, s):
            x_copy(j, s).wait()
            pl.when(j + 1 < n)(lambda: x_copy(j + 1, 1 - s).start(priority=1))
            pl.when(j >= 2)(y_copy(t0 + j - 2, s).wait)
            x = _from_tile_rows(xbuf.at[s], tm, nsub).astype(BF16)
            gu = jnp.dot(x, wgu_bf[...], preferred_element_type=F32) + bgu_ref[0]
            gate = jnp.minimum(gu[:, :F], SWIGLU_LIMIT)
            up = jnp.clip(gu[:, F:], -SWIGLU_LIMIT, SWIGLU_LIMIT)
            glu = gate * jax.nn.sigmoid(SWIGLU_ALPHA * gate)
            a = ((up + 1.0) * glu).astype(BF16)
            _to_tile_rows(ybuf.at[s], jnp.dot(a, wdn_bf[...], preferred_element_type=F32) + bdn_ref[0])
            y_copy(t0 + j, s).start(priority=1)

        def pair(p, carry):
            tile(2 * p, 0)
            pl.when(2 * p + 1 < n)(lambda: tile(2 * p + 1, 1))
            return carry

        lax.fori_loop(0, (n + 1) // 2, pair, 0)
        pl.when(n >= 2)(y_copy(t0 + n - 2, n % 2).wait)
        y_copy(t0 + n - 1, (n - 1) % 2).wait()

    @pl.when(e == pl.num_programs(0) - 1)
    def _():
        ybuf[0] = jnp.zeros(ybuf.shape[1:], ybuf.dtype)
        lax.fori_loop(nu_ref[0], nt_all, lambda t, c: (y_copy(t, 0).start(), c)[1], 0)
        lax.fori_loop(nu_ref[0], nt_all, lambda t, c: (y_copy(t, 0).wait(), c)[1], 0)


def _experts(first_tile, n_tiles, n_used, x_pad, w_gu, b_gu, w_dn, b_dn, layer):
    _, E, D, F2 = w_gu.shape
    F = F2 // 2
    tm = MOE_TILE
    nsub = D // LANES
    P = x_pad.shape[0] // nsub
    exp4 = lambda e, t0, nt, nu: (layer, e, 0, 0)
    exp3 = lambda e, t0, nt, nu: (e, 0, 0)
    grid_spec = pltpu.PrefetchScalarGridSpec(
        num_scalar_prefetch=3,
        grid=(E,),
        in_specs=[
            pl.BlockSpec(memory_space=pl.ANY),
            pl.BlockSpec((None, 1, D, F2), exp4),
            pl.BlockSpec((1, 1, F2), exp3),
            pl.BlockSpec((None, 1, F, D), exp4),
            pl.BlockSpec((1, 1, D), exp3),
        ],
        out_specs=pl.BlockSpec(memory_space=pl.ANY),
        scratch_shapes=[pltpu.VMEM((D, F2), BF16), pltpu.VMEM((F, D), BF16),
                        pltpu.VMEM((2, tm * nsub, LANES), F32), pltpu.VMEM((2, tm * nsub, LANES), F32),
                        pltpu.SemaphoreType.DMA((2,)), pltpu.SemaphoreType.DMA((2,))],
    )
    return pl.pallas_call(
        functools.partial(_expert_kernel, F=F, tm=tm, nsub=nsub, nt_all=P // tm),
        grid_spec=grid_spec,
        out_shape=jax.ShapeDtypeStruct((P * nsub, LANES), F32),
        compiler_params=_params(("arbitrary",)),
        name="moe_experts",
    )(first_tile, n_tiles, n_used, x_pad, w_gu, b_gu.reshape(E, 1, F2), w_dn, b_dn.reshape(E, 1, D))


def _combine_kernel(*refs, tt, nsub):
    dest_refs = refs[:TOP_K]
    dnext_refs = refs[TOP_K:2 * TOP_K]
    x_ref, gate_ref, ypad_ref, o_ref, ybuf, sems = refs[2 * TOP_K:]
    i = pl.program_id(0)
    slot = i % 2

    def gather(idx_refs, s):
        def issue(t, carry):
            for k in range(TOP_K):
                d = idx_refs[k][t]
                pltpu.make_async_copy(_tile_row(ypad_ref, d, nsub), _tile_row(ybuf.at[s, k], t, nsub),
                                      sems.at[s]).start(priority=k % 2)
            return carry

        lax.fori_loop(0, tt, issue, 0, unroll=ISSUE_UNROLL)

    @pl.when(i == 0)
    def _():
        gather(dest_refs, slot)

    @pl.when(i + 1 < pl.num_programs(0))
    def _():
        gather(dnext_refs, 1 - slot)

    for k in range(TOP_K):
        _row_copy_wait(ybuf.at[slot, k], ypad_ref, sems.at[slot], 1)
    gate = gate_ref[...]
    x = x_ref[...]
    cols = []
    for c in range(nsub):
        acc = x[:, c * LANES:(c + 1) * LANES]
        for k in range(TOP_K):
            acc = acc + ybuf[slot, k, pl.ds(c, tt, stride=nsub), :] * gate[:, k:k + 1]
        cols.append(acc)
    o_ref[...] = jnp.concatenate(cols, axis=1)


def _combine(x, gate, dest, y_pad):
    N, D = x.shape
    tt = COMBINE_TILE
    nsub = D // LANES
    steps = N // tt
    last = steps - 1
    return pl.pallas_call(
        functools.partial(_combine_kernel, tt=tt, nsub=nsub),
        grid=(N // tt,),
        in_specs=[pl.BlockSpec((tt,), lambda i, k=k: (k * steps + i,), memory_space=pltpu.SMEM)
                  for k in range(TOP_K)] + [
            pl.BlockSpec((tt,), lambda i, k=k: (k * steps + jnp.minimum(i + 1, last),),
                         memory_space=pltpu.SMEM) for k in range(TOP_K)] + [
            pl.BlockSpec((tt, D), lambda i: (i, 0)),
            pl.BlockSpec((tt, TOP_K), lambda i: (i, 0)),
            pl.BlockSpec(memory_space=pl.ANY),
        ],
        out_specs=pl.BlockSpec((tt, D), lambda i: (i, 0)),
        out_shape=jax.ShapeDtypeStruct((N, D), F32),
        scratch_shapes=[pltpu.VMEM((2, TOP_K, tt * nsub, LANES), F32), pltpu.SemaphoreType.DMA((2,))],
        compiler_params=_params(("arbitrary",)),
        name="moe_combine",
    )(*([dest] * (2 * TOP_K)), x, gate, y_pad)


def _dest_kernel(pstart_ref, e_ref, rank_ref, o_ref):
    e = e_ref[...]
    seg = jnp.zeros(e.shape, I32)
    for ex in range(N_EXPERTS):
        seg = jnp.where(e == ex, pstart_ref[ex], seg)
    o_ref[...] = seg + rank_ref[...]


def _dest_rows(e_idx, rank, pstart):
    return pl.pallas_call(
        _dest_kernel,
        in_specs=[pl.BlockSpec(memory_space=pltpu.SMEM), pl.BlockSpec(e_idx.shape, lambda: (0, 0)),
                  pl.BlockSpec(rank.shape, lambda: (0, 0))],
        out_specs=pl.BlockSpec(rank.shape, lambda: (0, 0)),
        out_shape=jax.ShapeDtypeStruct(rank.shape, I32),
        name="moe_dest",
    )(pstart, e_idx, rank)


def _moe_layer(x, g, w_r, b_r, w_gu, b_gu, w_dn, b_dn, layer):
    N, D = x.shape
    E = N_EXPERTS
    tm = MOE_TILE
    e_idx, gate_t, rank, counts = _router(x, g, w_r, b_r)
    counts = counts[:, 0].astype(I32)
    padded = ((counts + tm - 1) // tm) * tm
    pend = jnp.cumsum(padded)
    pstart = pend - padded
    dest = _dest_rows(e_idx, rank, pstart.astype(I32)).reshape(-1)
    gate = gate_t.T
    P = ((N * TOP_K + E * (tm - 1) + tm - 1) // tm) * tm
    n_used = (pend[-1] // tm).astype(I32)
    x_pad = _dispatch(x, g, dest, pend.astype(I32), n_used.reshape(1), P)
    y_pad = _experts((pstart // tm).astype(I32), (padded // tm).astype(I32), n_used.reshape(1),
                     x_pad, w_gu, b_gu, w_dn, b_dn, layer)
    return _combine(x, gate, dest, y_pad)


def _bucket_maps():
    qi = np.arange(ATT_BLOCK)[None, :]
    kj = np.arange(2 * ATT_BLOCK)[:, None]
    delta = qi + ATT_BLOCK - kj
    buckets, valids = [], []
    max_exact = N_BUCKETS // 2
    for win, dil in DILATED_GROUPS:
        n = np.maximum(delta * dil, 0)
        nf = np.maximum(n, 1).astype(np.float32)
        large = max_exact + (np.log(nf / np.float32(max_exact)) / np.float32(math.log(REL_MAX_DIST / max_exact))
                             * np.float32(N_BUCKETS - max_exact)).astype(np.int32)
        large = np.minimum(large, N_BUCKETS - 1)
        buckets.append(np.where(n < max_exact, n, large).astype(np.int32))
        valids.append(((delta >= 0) & (delta <= win // dil)).astype(np.int32))
    return np.stack(buckets), np.stack(valids)


def _bias_kernel(tab_ref, bm_ref, valid_ref, o_ref):
    g = pl.program_id(0)
    bm = bm_ref[0]
    key = lax.broadcasted_iota(I32, bm.shape, 0)

    def one_head(h, carry):
        acc = jnp.zeros(bm.shape, F32)
        for b in range(N_BUCKETS):
            acc = jnp.where(bm == b, tab_ref[b, g * ATT_HEADS + h], acc)
        band = jnp.where(valid_ref[0] > 0, acc * LOG2_E, NEG_INF)
        o_ref[h, 0] = band
        o_ref[h, 1] = jnp.where(key >= ATT_BLOCK, band, NEG_INF)
        return carry

    lax.fori_loop(0, ATT_HEADS, one_head, 0)


def _bias_tables(rel_bias):
    bm, valid = _bucket_maps()
    GH = len(DILATED_GROUPS) * ATT_HEADS
    blk = (1, 2 * ATT_BLOCK, ATT_BLOCK)
    return pl.pallas_call(
        _bias_kernel,
        grid=(len(DILATED_GROUPS),),
        in_specs=[
            pl.BlockSpec(memory_space=pltpu.SMEM),
            pl.BlockSpec(blk, lambda i: (i, 0, 0)),
            pl.BlockSpec(blk, lambda i: (i, 0, 0)),
        ],
        out_specs=pl.BlockSpec((ATT_HEADS, 2, 2 * ATT_BLOCK, ATT_BLOCK), lambda i: (i, 0, 0, 0)),
        out_shape=jax.ShapeDtypeStruct((GH, 2, 2 * ATT_BLOCK, ATT_BLOCK), F32),
        compiler_params=_params(("arbitrary",)),
        name="attn_bias",
    )(rel_bias, jnp.asarray(bm), jnp.asarray(valid))


def _cast_kernel(w_ref, o_ref):
    o_ref[...] = w_ref[...].astype(BF16)


def _to_bf16(w, layer, col_block):
    _, rows, cols = w.shape
    return pl.pallas_call(
        _cast_kernel,
        grid=(cols // col_block,),
        in_specs=[pl.BlockSpec((None, rows, col_block), lambda j: (layer, 0, j))],
        out_specs=pl.BlockSpec((rows, col_block), lambda j: (0, j)),
        out_shape=jax.ShapeDtypeStruct((rows, cols), BF16),
        compiler_params=_params(("parallel",)),
        name="cast_bf16",
    )(w)


def _qkv_kernel(x_ref, g_ref, wq_ref, wk_ref, wv_ref, qg_ref, kg_ref, qt_ref, k_ref, vt_ref,
                slab_ref, *, dil, R, nl):
    rc = pl.program_id(2)
    D = x_ref.shape[2]
    lanes = slab_ref.shape[2]

    if dil == 1:
        x = x_ref[0]
    else:
        @pl.when(rc == 0)
        def _():
            for c in range(D // lanes):
                slab_ref[c] = x_ref[0, :, c * lanes:(c + 1) * lanes]

        pieces = []
        for j in range(R):
            r = rc * R + j
            cols = [slab_ref[c, pl.ds(r, nl, stride=dil), :] for c in range(D // lanes)]
            pieces.append(jnp.concatenate(cols, axis=1))
        x = jnp.concatenate(pieces, axis=0)
    h = _rms(x, g_ref[...]).astype(BF16)
    q = jnp.dot(h, wq_ref[...], preferred_element_type=F32)
    k = jnp.dot(h, wk_ref[...], preferred_element_type=F32)
    v = jnp.dot(h, wv_ref[...], preferred_element_type=F32)
    blk = ATT_BLOCK

    def normed_tile(t, gain_ref, cs):
        heads = []
        for hh in range(t.shape[0] // HEAD_DIM):
            th = t[hh * HEAD_DIM:(hh + 1) * HEAD_DIM]
            ms = jnp.mean(th * th, axis=0, keepdims=True)
            heads.append(th * lax.rsqrt(ms + EPS))
        return jnp.concatenate(heads, axis=0) * gain_ref[cs, :]

    for j in range(R):
        for c in range(nl // blk):
            rows = slice(j * nl + c * blk, j * nl + (c + 1) * blk)
            for hp in range(q.shape[1] // blk):
                cs = slice(hp * blk, (hp + 1) * blk)
                qt_ref[0, j, cs, c * blk:(c + 1) * blk] = normed_tile(q[rows, cs].T, qg_ref, cs).astype(BF16)
                k_ref[0, j, c * blk:(c + 1) * blk, cs] = normed_tile(k[rows, cs].T, kg_ref, cs).T.astype(BF16)
                vt_ref[0, j, cs, c * blk:(c + 1) * blk] = v[rows, cs].T.astype(BF16)


def _qkv_group(x, g, w_bf, q_gain, k_gain, gi, dil):
    B, S, D = x.shape
    HD = ATT_HEADS * HEAD_DIM
    G = len(DILATED_GROUPS)
    L = S // dil
    R = min(dil, QKV_ROWS // ATT_BLOCK)
    nl = QKV_ROWS // R
    lanes = LANES
    rows_of = lambda gain: jnp.broadcast_to(jnp.tile(gain, ATT_HEADS)[:, None], (HD, lanes))
    q_rows = rows_of(q_gain * (HEAD_DIM ** -0.5 * LOG2_E))
    k_rows = rows_of(k_gain)
    wspec = lambda s: pl.BlockSpec((D, HD), lambda b, l, r: (0, s * G + gi))
    tspec = pl.BlockSpec((1, R, HD, nl), lambda b, l, r: (b, r, 0, l))
    tshape = jax.ShapeDtypeStruct((B, dil, HD, L), BF16)
    return pl.pallas_call(
        functools.partial(_qkv_kernel, dil=dil, R=R, nl=nl),
        grid=(B, L // nl, dil // R),
        in_specs=[
            pl.BlockSpec((1, nl * dil, D), lambda b, l, r: (b, l, 0)),
            pl.BlockSpec((1, D), lambda b, l, r: (0, 0)),
            wspec(0), wspec(1), wspec(2),
            pl.BlockSpec((HD, lanes), lambda b, l, r: (0, 0)),
            pl.BlockSpec((HD, lanes), lambda b, l, r: (0, 0)),
        ],
        out_specs=[tspec, pl.BlockSpec((1, R, nl, HD), lambda b, l, r: (b, r, l, 0)), tspec],
        out_shape=[tshape, jax.ShapeDtypeStruct((B, dil, L, HD), BF16), tshape],
        scratch_shapes=[pltpu.VMEM((D // lanes, nl * dil if dil > 1 else 8, lanes), F32)],
        compiler_params=_params(("arbitrary", "arbitrary", "arbitrary")),
        name=f"attn_qkv_g{gi}",
    )(x, g.reshape(1, D), w_bf, w_bf, w_bf, q_rows, k_rows)


def _attn_kernel(*refs, single, nitems):
    if single:
        qt_ref, kc_ref, vtc_ref, bias_ref, o_ref, l_ref, s_ref, p_ref = refs
    else:
        qt_ref, kc_ref, kp_ref, vtc_ref, vtp_ref, bias_ref, o_ref, l_ref, s_ref, p_ref = refs
    blk = ATT_BLOCK
    pair = 2 * HEAD_DIM
    first = jnp.where(pl.program_id(2) == 0, 1, 0)
    if single:
        k0 = blk
        variants = [1] * nitems
        q_of = lambda it, cs: qt_ref[0, it, cs, :]
        k_of = lambda it, cs: kc_ref[0, it, :, cs]
        vt_of = lambda it, cs: vtc_ref[0, it, cs, :]
    else:
        k0 = 0
        variants = [first] + [0] * (nitems - 1)

        def q_of(it, cs):
            return qt_ref[0, 0, cs, it * blk:(it + 1) * blk]

        def k_of(it, cs):
            if it == 0:
                return jnp.concatenate([kp_ref[0, 0, :, cs], kc_ref[0, 0, :blk, cs]], axis=0)
            return kc_ref[0, 0, (it - 1) * blk:(it + 1) * blk, cs]

        def vt_of(it, cs):
            if it == 0:
                return jnp.concatenate([vtp_ref[0, 0, cs, :], vtc_ref[0, 0, cs, :blk]], axis=1)
            return vtc_ref[0, 0, cs, (it - 1) * blk:(it + 1) * blk]

    feat = lax.broadcasted_iota(I32, (pair, blk), 0)
    head_rows = [jnp.where(feat < HEAD_DIM, 1.0, 0.0).astype(BF16),
                 jnp.where(feat < HEAD_DIM, 0.0, 1.0).astype(BF16)]
    invs = {}
    lse_rows = {}

    def scores(it, pr):
        cs = slice(pr * pair, (pr + 1) * pair)
        qt2 = q_of(it, cs)
        k2 = k_of(it, cs)
        for half in range(2):
            qth = qt2 * head_rows[half]
            s_ref[it * ATT_HEADS + 2 * pr + half] = jnp.dot(k2, qth, preferred_element_type=F32)

    def softmax(it, pr):
        for head in (2 * pr, 2 * pr + 1):
            s = s_ref[it * ATT_HEADS + head] + bias_ref[head, variants[it], k0:, :]
            m = jnp.max(s, axis=0, keepdims=True)
            p = jnp.exp2(s - m)
            den = jnp.sum(p, axis=0, keepdims=True)
            p_ref[it * ATT_HEADS + head] = p.astype(BF16)
            invs[it, head] = 1.0 / den
            lse_rows[it, head] = (m + jnp.log2(den)) * (1.0 / LOG2_E)

    def outputs(it, pr):
        cs = slice(pr * pair, (pr + 1) * pair)
        vt2 = vt_of(it, cs)
        halves = []
        for half in range(2):
            head = 2 * pr + half
            vth = vt2[half * HEAD_DIM:(half + 1) * HEAD_DIM]
            ot = jnp.dot(vth, p_ref[it * ATT_HEADS + head], preferred_element_type=F32)
            halves.append(ot * invs[it, head])
        val = jnp.concatenate(halves, axis=0).T
        if single:
            o_ref[0, it, :, cs] = val
        else:
            o_ref[0, 0, it * blk:(it + 1) * blk, cs] = val

    work = [(it, pr) for it in range(nitems) for pr in range(ATT_HEADS // 2)]
    for it, pr in work:
        scores(it, pr)
    for it, pr in work:
        softmax(it, pr)
    for it, pr in work:
        outputs(it, pr)
    lse_row = lax.broadcasted_iota(I32, (blk, blk), 0)
    for it in range(nitems):
        lse_t = jnp.zeros((blk, blk), F32)
        for head in range(ATT_HEADS):
            lse_t = jnp.where(lse_row == head, lse_rows[it, head], lse_t)
        if single:
            l_ref[0, it] = lse_t.T
        else:
            l_ref[0, 0, it * blk:(it + 1) * blk, :] = lse_t.T


def _attn_group(qt, k, vt, bias, gi):
    B, dil, L, HD = k.shape
    blk = ATT_BLOCK
    nb = L // blk
    single = nb == 1
    ni = min(ATT_ITEMS, dil if single else nb)
    bias_spec = pl.BlockSpec((ATT_HEADS, 2, 2 * blk, blk), lambda b, r, n: (gi, 0, 0, 0))
    if single:
        rows = pl.BlockSpec((1, ni, blk, HD), lambda b, r, n: (b, r, 0, 0))
        cols = pl.BlockSpec((1, ni, HD, blk), lambda b, r, n: (b, r, 0, 0))
        grid = (B, dil // ni, 1)
        in_specs = [cols, rows, cols, bias_spec]
        args = (qt, k, vt, bias)
        out_specs = [rows, pl.BlockSpec((1, ni, blk, blk), lambda b, r, n: (b, r, 0, 0))]
        nk = blk
    else:
        rows = pl.BlockSpec((1, 1, ni * blk, HD), lambda b, r, n: (b, r, n, 0))
        cols = pl.BlockSpec((1, 1, HD, ni * blk), lambda b, r, n: (b, r, 0, n))
        prow = pl.BlockSpec((1, 1, blk, HD), lambda b, r, n: (b, r, jnp.maximum(ni * n - 1, 0), 0))
        pcol = pl.BlockSpec((1, 1, HD, blk), lambda b, r, n: (b, r, 0, jnp.maximum(ni * n - 1, 0)))
        grid = (B, dil, nb // ni)
        in_specs = [cols, rows, prow, cols, pcol, bias_spec]
        args = (qt, k, k, vt, vt, bias)
        out_specs = [rows, pl.BlockSpec((1, 1, ni * blk, blk), lambda b, r, n: (b, r, n, 0))]
        nk = 2 * blk
    return pl.pallas_call(
        functools.partial(_attn_kernel, single=single, nitems=ni),
        grid=grid,
        in_specs=in_specs,
        out_specs=out_specs,
        out_shape=[jax.ShapeDtypeStruct((B, dil, L, HD), F32),
                   jax.ShapeDtypeStruct((B, dil, L, blk), F32)],
        scratch_shapes=[pltpu.VMEM((ni * ATT_HEADS, nk, blk), F32),
                        pltpu.VMEM((ni * ATT_HEADS, nk, blk), BF16)],
        compiler_params=_params(("parallel", "parallel", "arbitrary")),
        name=f"attn_core_g{gi}",
    )(*args)


def _attn_out_kernel(x_ref, o0, o1, o2, l0, l1, l2, wo_ref, out_ref, oslab, lslab, ex_ref, *, T):
    lanes = oslab.shape[3]
    HD = o0.shape[3]

    @pl.when((pl.program_id(0) == 0) & (pl.program_id(1) == 0))
    def _():
        h = lax.broadcasted_iota(I32, ex_ref.shape, 0) & (ATT_BLOCK - 1)
        c = lax.broadcasted_iota(I32, ex_ref.shape, 1)
        owner = lax.shift_right_logical(c, HEAD_DIM.bit_length() - 1)
        ex_ref[...] = jnp.where(h == owner, 1.0, 0.0).astype(BF16)

    def token_order(o_ref, l_ref, dil, s):
        if dil == 1:
            return o_ref[0, 0], l_ref[0, 0]
        n = T // dil
        for r in range(dil):
            blk = o_ref[0, r]
            for c in range(HD // lanes):
                oslab[s, c, pl.ds(r, n, stride=dil), :] = blk[:, c * lanes:(c + 1) * lanes]
            lslab[s, pl.ds(r, n, stride=dil), :] = l_ref[0, r]
        o = jnp.concatenate([oslab[s, c] for c in range(HD // lanes)], axis=1)
        return o, lslab[s]

    groups = [token_order(o_ref, l_ref, dil, s)
              for s, ((_, dil), o_ref, l_ref) in enumerate(zip(DILATED_GROUPS, (o0, o1, o2), (l0, l1, l2)))]
    m = jnp.maximum(jnp.maximum(groups[0][1], groups[1][1]), groups[2][1])
    es = [jnp.exp(l - m) for _, l in groups]
    inv = 1.0 / (es[0] + es[1] + es[2])
    att = jnp.zeros((T, HD), F32)
    for (o, _), e in zip(groups, es):
        a = e * inv
        a_hi = a.astype(BF16)
        a_lo = (a - a_hi.astype(F32)).astype(BF16)
        wide = jnp.dot(jnp.concatenate([a_hi, a_lo], axis=1), ex_ref[...], preferred_element_type=F32)
        att = att + wide * o
    out_ref[0] = x_ref[0] + jnp.dot(att.astype(BF16), wo_ref[...], preferred_element_type=F32)


def _attn_out(x, os, ls, wo_bf):
    B, S, D = x.shape
    HD = wo_bf.shape[0]
    T = OUT_TILE
    lanes = LANES
    ospec = lambda dil, w: pl.BlockSpec((1, dil, T // dil, w), lambda b, i: (b, 0, i, 0))
    dils = [dil for _, dil in DILATED_GROUPS]
    xspec = pl.BlockSpec((1, T, D), lambda b, i: (b, i, 0))
    return pl.pallas_call(
        functools.partial(_attn_out_kernel, T=T),
        grid=(B, S // T),
        in_specs=[xspec] + [ospec(d, HD) for d in dils] + [ospec(d, ATT_BLOCK) for d in dils]
                 + [pl.BlockSpec((HD, D), lambda b, i: (0, 0))],
        out_specs=xspec,
        out_shape=jax.ShapeDtypeStruct((B, S, D), F32),
        scratch_shapes=[pltpu.VMEM((len(dils), HD // lanes, T, lanes), F32),
                        pltpu.VMEM((len(dils), T, ATT_BLOCK), F32),
                        pltpu.VMEM((2 * ATT_BLOCK, HD), BF16)],
        compiler_params=_params(("arbitrary", "arbitrary")),
        name="attn_out",
    )(x, *os, *ls, wo_bf)


def _attention_layer(x, g, w_qkv, q_gain, k_gain, w_o, rel_bias, layer):
    HD = ATT_HEADS * HEAD_DIM
    bias = _bias_tables(rel_bias)
    wqkv_bf = _to_bf16(w_qkv, layer, HD)
    wo_bf = _to_bf16(w_o, layer, w_o.shape[2])
    os, ls = [], []
    for gi, (_, dil) in enumerate(DILATED_GROUPS):
        qt, k, vt = _qkv_group(x, g, wqkv_bf, q_gain, k_gain, gi, dil)
        o, l = _attn_group(qt, k, vt, bias, gi)
        os.append(o)
        ls.append(l)
    return _attn_out(x, os, ls, wo_bf)


def kernel(x, norm_mix_g, norm_ffn_g, pool_w, pool_scale, attn_w_qkv, attn_q_gain, attn_k_gain,
           attn_w_o, rel_bias, moe_w_router, moe_b_router, moe_w_gate_up, moe_b_gate_up,
           moe_w_down, moe_b_down):
    B, S, D = x.shape
    depth = norm_mix_g.shape[0]
    for i in range(depth):
        j = i // 2
        if i % 2 == 0:
            x = _pool_layer(x, norm_mix_g[i], pool_w[j], pool_scale[j])
        else:
            x = _attention_layer(x, norm_mix_g[i], attn_w_qkv, attn_q_gain[j], attn_k_gain[j],
                                 attn_w_o, rel_bias, j)
        x = _moe_layer(x.reshape(B * S, D), norm_ffn_g[i], moe_w_router[i], moe_b_router[i],
                       moe_w_gate_up, moe_b_gate_up[i], moe_w_down, moe_b_down[i],
                       i).reshape(B, S, D)
    return x
```
